```python
import math
import jax, jax.numpy as jnp
from jax import lax
import numpy as np

D_MODEL = 1024
BATCH = 8
SEQ = 2048
DEPTH = 2
DEC_BATCH = 128
DEC_SEQ = 8
PAST_LEN = 16384
PAGE_SIZE = 128

D_MIX = D_MODEL
HGRN_WIDTH = 3 * D_MODEL // 8
HGRN_HEAD_DIM = 64
HGRN_HEADS = HGRN_WIDTH // HGRN_HEAD_DIM
S5_WIDTH = D_MODEL // 4
S5_GROUP_SIZE = 16
S5_GROUPS = S5_WIDTH // S5_GROUP_SIZE
S5_STATE = 64
SSD_WIDTH = D_MIX - HGRN_WIDTH - S5_WIDTH
SSD_HEAD_DIM = 64
SSD_HEADS = SSD_WIDTH // SSD_HEAD_DIM
SSD_NGROUPS = 2
SSD_STATE = 64
SSD_CONV_WIDTH = 4
SSD_CONV_DIM = SSD_WIDTH + 2 * SSD_NGROUPS * SSD_STATE
IN_SIZES = (HGRN_WIDTH, HGRN_WIDTH, HGRN_WIDTH, HGRN_WIDTH, S5_WIDTH, SSD_WIDTH, SSD_CONV_DIM, SSD_HEADS)
N_IN = sum(IN_SIZES)
D_FF = 2816
N_EXPERTS = 8
TOP_K = 2
D_FF_EXPERT = 2816
N_DENSE = (DEPTH + 1) // 2
N_MOE = DEPTH // 2
CHUNK = 64
DEEPNORM_ALPHA = (2 * DEPTH) ** 0.25
DEEPNORM_BETA = (8 * DEPTH) ** -0.25
LN_EPS = 1e-5
RMS_EPS = 1e-6

kernel_name = 'hymba_style_hgrn2_s5_ssd_decoder_step'

F32 = jnp.float32


def _layer_norm(x, g, b):
    xf = x.astype(F32)
    mu = jnp.mean(xf, -1, keepdims=True)
    var = jnp.mean(jnp.square(xf - mu), -1, keepdims=True)
    return ((xf - mu) * lax.rsqrt(var + LN_EPS) * g.astype(F32) + b.astype(F32)).astype(x.dtype)


def _rms(x):
    return x * lax.rsqrt(jnp.mean(jnp.square(x), -1, keepdims=True) + RMS_EPS)


def _chunked_gated_scan(q, k, v, log_f, s0, chunk):
    B_, L, H, _ = q.shape
    n = L // chunk

    def split(t):
        return t.reshape(B_, n, chunk, H, t.shape[-1]).swapaxes(0, 1)

    mask = jnp.tril(jnp.ones((chunk, chunk), bool))[None, :, :, None, None]

    def step(S, inp):
        qc, kc, vc, fc = inp
        b = jnp.cumsum(fc, axis=1)
        diff = b[:, :, None] - b[:, None, :]
        decay = jnp.exp(jnp.where(mask, diff, -jnp.inf))
        scores = jnp.sum(qc[:, :, None] * kc[:, None] * decay, -1)
        o = (jnp.einsum('btsh,bshv->bthv', scores, vc)
             + jnp.einsum('bthk,bhkv->bthv', qc * jnp.exp(b), S))
        b_last = b[:, -1]
        kd = kc * jnp.exp(b_last[:, None] - b)
        S = jnp.exp(b_last)[..., None] * S + jnp.einsum('bshk,bshv->bhkv', kd, vc)
        return S, o

    S, o = lax.scan(step, s0, (split(q), split(k), split(v), split(log_f)))
    return o.swapaxes(0, 1).reshape(B_, L, H, v.shape[-1]), S


def _hgrn2(q_raw, f_raw, i_raw, g_raw, lb, norm_g, s0, chunk):
    B_, L, _ = q_raw.shape
    zf = f_raw.astype(F32)
    log_f = jnp.logaddexp(jnp.log(lb), jnp.log1p(-lb) + jax.nn.log_sigmoid(zf))
    k = (1.0 - lb) * jax.nn.sigmoid(-zf)
    q = jax.nn.silu(q_raw.astype(F32))
    v = i_raw.astype(F32)

    def heads(t):
        return t.reshape(B_, L, HGRN_HEADS, HGRN_HEAD_DIM)

    o, s = _chunked_gated_scan(heads(q), heads(k), heads(v), heads(log_f), s0.astype(F32), chunk)
    o = _rms(o).reshape(B_, L, HGRN_WIDTH) * norm_g.astype(F32) * jax.nn.silu(g_raw.astype(F32))
    return o, s


def _complex_affine_combine(e1, e2):
    a1r, a1i, b1r, b1i = e1
    a2r, a2i, b2r, b2i = e2
    return (a2r * a1r - a2i * a1i, a2r * a1i + a2i * a1r,
            a2r * b1r - a2i * b1i + b2r, a2r * b1i + a2i * b1r + b2i)


def _s5(u, a_re, a_im, log_dt, b_re, b_im, c_re, c_im, d, w_glu, b_glu, s0):
    B_, L, _ = u.shape
    uflat = u.astype(F32)
    ug = uflat.reshape(B_, L, S5_GROUPS, S5_GROUP_SIZE)
    ar, ai = a_re.astype(F32), a_im.astype(F32)
    dt = jnp.exp(log_dt.astype(F32))[:, None]
    mag = jnp.exp(ar * dt)
    lam_re, lam_im = mag * jnp.cos(ai * dt), mag * jnp.sin(ai * dt)
    den = ar * ar + ai * ai
    nr, ni = lam_re - 1.0, lam_im
    zr = (nr * ar + ni * ai) / den
    zi = (ni * ar - nr * ai) / den
    br, bi = b_re.astype(F32), b_im.astype(F32)
    bbar_re = zr[..., None] * br - zi[..., None] * bi
    bbar_im = zr[..., None] * bi + zi[..., None] * br
    bu_re = jnp.einsum('blgh,gph->blgp', ug, bbar_re)
    bu_im = jnp.einsum('blgh,gph->blgp', ug, bbar_im)
    h0r, h0i = s0[:, 0].astype(F32), s0[:, 1].astype(F32)
    bu_re = bu_re.at[:, 0].add(lam_re * h0r - lam_im * h0i)
    bu_im = bu_im.at[:, 0].add(lam_re * h0i + lam_im * h0r)
    shp = bu_re.shape
    elems = (jnp.broadcast_to(lam_re, shp), jnp.broadcast_to(lam_im, shp), bu_re, bu_im)
    _, _, h_re, h_im = lax.associative_scan(_complex_affine_combine, elems, axis=1)
    y = (jnp.einsum('blgp,ghp->blgh', h_re, c_re.astype(F32))
         - jnp.einsum('blgp,ghp->blgh', h_im, c_im.astype(F32)))
    y = y.reshape(B_, L, S5_WIDTH) + d.astype(F32) * uflat
    y = jax.nn.gelu(y)
    out = y * jax.nn.sigmoid(y @ w_glu.astype(F32) + b_glu.astype(F32))
    s_new = jnp.stack([h_re[:, -1], h_im[:, -1]], axis=1)
    return out, s_new


def _ssd(z, xbc, dt_raw, conv_w, conv_b, dt_bias, a_log, d, norm_g, s0, conv0, chunk):
    B_, L, _ = z.shape
    buf = jnp.concatenate([conv0.astype(xbc.dtype), xbc], axis=1)
    conv_new = buf[:, -(SSD_CONV_WIDTH - 1):]
    xc = lax.conv_general_dilated(buf.astype(F32), conv_w.astype(F32)[:, None, :], (1,), 'VALID',
                                  dimension_numbers=('NWC', 'WIO', 'NWC'),
                                  feature_group_count=SSD_CONV_DIM) + conv_b.astype(F32)
    xc = jax.nn.silu(xc)
    xs, bm, cm = jnp.split(xc, [SSD_WIDTH, SSD_WIDTH + SSD_NGROUPS * SSD_STATE], axis=-1)
    dt = jax.nn.softplus(dt_raw.astype(F32) + dt_bias.astype(F32))
    log_a = dt * (-jnp.exp(a_log.astype(F32)))
    rep = SSD_HEADS // SSD_NGROUPS
    bh = jnp.repeat(bm.reshape(B_, L, SSD_NGROUPS, SSD_STATE), rep, axis=2)
    ch = jnp.repeat(cm.reshape(B_, L, SSD_NGROUPS, SSD_STATE), rep, axis=2)
    xh = xs.reshape(B_, L, SSD_HEADS, SSD_HEAD_DIM)
    o, s = _chunked_gated_scan(ch, bh * dt[..., None], xh, log_a[..., None], s0.astype(F32), chunk)
    y = (o + d.astype(F32)[:, None] * xh).reshape(B_, L, SSD_WIDTH) * jax.nn.silu(z.astype(F32))
    y = _rms(y.reshape(B_, L, SSD_NGROUPS, SSD_WIDTH // SSD_NGROUPS)).reshape(B_, L, SSD_WIDTH)
    return y * norm_g.astype(F32), s, conv_new


def _swiglu(h, wg, wu, wd):
    return (jax.nn.silu(h @ wg) * (h @ wu)) @ wd


def _moe(h, w_router, b_router, wg, wu, wd):
    B_, L, D = h.shape
    t = h.reshape(B_ * L, D)
    logits = (t @ w_router).astype(F32) + b_router.astype(F32)
    top_v, top_i = lax.top_k(logits, TOP_K)
    top_w = jax.nn.softmax(top_v, axis=-1)
    comb = jnp.sum(jax.nn.one_hot(top_i, N_EXPERTS, dtype=F32) * top_w[..., None], axis=1).astype(h.dtype)
    out = jnp.zeros_like(t)
    for e in range(N_EXPERTS):
        out = out + comb[:, e:e + 1] * _swiglu(t, wg[e], wu[e], wd[e])
    return out.reshape(B_, L, D)


def _trunk(x, c, s_hgrn, s_s5, s_ssd, s_conv, p):
    L = x.shape[1]
    chunk = math.gcd(L, CHUNK)
    lb_all = jnp.cumsum(jax.nn.softmax(p['hgrn_lb_logits'].astype(F32), axis=0), axis=0)
    lb_all = lb_all - lb_all[0]
    offs = []
    acc = 0
    for sz in IN_SIZES[:-1]:
        acc += sz
        offs.append(acc)
    c_act = jax.nn.silu(c)
    new_h, new_s5, new_m, new_c = [], [], [], []
    for l in range(DEPTH):
        ada = (c_act @ p['w_ada'][l] + p['b_ada'][l])[:, None, :]
        sh1, sc1, g1, sh2, sc2, g2 = jnp.split(ada, 6, axis=-1)
        h = x * (1 + sc1) + sh1
        proj = h @ p['w_in'][l]
        q_r, f_r, i_r, g_r, u, z, xbc, dt_r = jnp.split(proj, offs, axis=-1)
        o_h, sh_new = _hgrn2(q_r, f_r, i_r, g_r, lb_all[l], p['hgrn_norm_g'][l], s_hgrn[l], chunk)
        o_s, ss_new = _s5(u, p['s5_a_re'][l], p['s5_a_im'][l], p['s5_log_dt'][l], p['s5_b_re'][l],
                          p['s5_b_im'][l], p['s5_c_re'][l], p['s5_c_im'][l], p['s5_d'][l],
                          p['s5_w_glu'][l], p['s5_b_glu'][l], s_s5[l])
        o_m, sm_new, sc_new = _ssd(z, xbc, dt_r, p['ssd_conv_w'][l], p['ssd_conv_b'][l], p['ssd_dt_bias'][l],
                                   p['ssd_a_log'][l], p['ssd_d'][l], p['ssd_norm_g'][l], s_ssd[l], s_conv[l], chunk)
        mix = jnp.concatenate([o_h, o_s, o_m], axis=-1).astype(x.dtype) @ p['w_out'][l]
        x = _layer_norm(DEEPNORM_ALPHA * x + g1 * mix, p['ln_g'][l, 0], p['ln_b'][l, 0])
        h = x * (1 + sc2) + sh2
        j = l // 2
        if l % 2 == 0:
            f = _swiglu(h, p['ffn_w_gate'][j], p['ffn_w_up'][j], p['ffn_w_down'][j])
        else:
            f = _moe(h, p['moe_w_router'][j], p['moe_b_router'][j], p['moe_w_gate'][j],
                     p['moe_w_up'][j], p['moe_w_down'][j])
        x = _layer_norm(DEEPNORM_ALPHA * x + g2 * f, p['ln_g'][l, 1], p['ln_b'][l, 1])
        new_h.append(sh_new)
        new_s5.append(ss_new)
        new_m.append(sm_new)
        new_c.append(sc_new)
    return x, jnp.stack(new_h), jnp.stack(new_s5), jnp.stack(new_m), jnp.stack(new_c)


def setup_inputs(seed: int = 0) -> dict:
    key = jax.random.key(seed)
    keys = list(jax.random.split(key, 64))

    def nrm(shape, scale):
        return scale * jax.random.normal(keys.pop(), shape, jnp.float32)

    def unif(shape, lo, hi):
        return jax.random.uniform(keys.pop(), shape, jnp.float32, lo, hi)

    D = D_MODEL
    dt0 = jnp.exp(unif((DEPTH, SSD_HEADS), math.log(1e-3), math.log(1e-1)))
    return {
        'x_prompt': nrm((BATCH, SEQ, D), 1.0),
        'x_sample': nrm((DEC_BATCH, DEC_SEQ, D), 1.0),
        'c_prompt': nrm((BATCH, D), 1.0),
        'c_sample': nrm((DEC_BATCH, D), 1.0),
        'state_hgrn': nrm((DEPTH, DEC_BATCH, HGRN_HEADS, HGRN_HEAD_DIM, HGRN_HEAD_DIM), 0.5),
        'state_s5': nrm((DEPTH, DEC_BATCH, 2, S5_GROUPS, S5_STATE), 0.5),
        'state_ssd': nrm((DEPTH, DEC_BATCH, SSD_HEADS, SSD_STATE, SSD_HEAD_DIM), 0.5),
        'state_ssd_conv': nrm((DEPTH, DEC_BATCH, SSD_CONV_WIDTH - 1, SSD_CONV_DIM), 1.0),
        'w_ada': nrm((DEPTH, D, 6 * D), 0.5 * D ** -0.5),
        'b_ada': nrm((DEPTH, 6 * D), 0.02),
        'ln_g': 1.0 + nrm((DEPTH, 2, D), 0.02),
        'ln_b': nrm((DEPTH, 2, D), 0.02),
        'w_in': nrm((DEPTH, D, N_IN), D ** -0.5),
        'w_out': nrm((DEPTH, D_MIX, D), DEEPNORM_BETA * D_MIX ** -0.5),
        'hgrn_lb_logits': nrm((DEPTH, HGRN_WIDTH), 0.1),
        'hgrn_norm_g': 1.0 + nrm((DEPTH, HGRN_WIDTH), 0.02),
        's5_a_re': -0.5 + nrm((DEPTH, S5_GROUPS, S5_STATE), 0.01),
        's5_a_im': math.pi * jnp.arange(S5_STATE, dtype=jnp.float32) + nrm((DEPTH, S5_GROUPS, S5_STATE), 0.01),
        's5_log_dt': unif((DEPTH, S5_GROUPS), math.log(1e-3), math.log(1e-1)),
        's5_b_re': nrm((DEPTH, S5_GROUPS, S5_STATE, S5_GROUP_SIZE), (2 * S5_GROUP_SIZE) ** -0.5),
        's5_b_im': nrm((DEPTH, S5_GROUPS, S5_STATE, S5_GROUP_SIZE), (2 * S5_GROUP_SIZE) ** -0.5),
        's5_c_re': nrm((DEPTH, S5_GROUPS, S5_GROUP_SIZE, S5_STATE), S5_STATE ** -0.5),
        's5_c_im': nrm((DEPTH, S5_GROUPS, S5_GROUP_SIZE, S5_STATE), S5_STATE ** -0.5),
        's5_d': nrm((DEPTH, S5_WIDTH), 1.0),
        's5_w_glu': nrm((DEPTH, S5_WIDTH, S5_WIDTH), S5_WIDTH ** -0.5),
        's5_b_glu': nrm((DEPTH, S5_WIDTH), 0.02),
        'ssd_conv_w': nrm((DEPTH, SSD_CONV_WIDTH, SSD_CONV_DIM), SSD_CONV_WIDTH ** -0.5),
        'ssd_conv_b': nrm((DEPTH, SSD_CONV_DIM), 0.02),
        'ssd_dt_bias': dt0 + jnp.log(-jnp.expm1(-dt0)),
        'ssd_a_log': jnp.log(unif((DEPTH, SSD_HEADS), 1.0, 16.0)),
        'ssd_d': 1.0 + nrm((DEPTH, SSD_HEADS), 0.1),
        'ssd_norm_g': 1.0 + nrm((DEPTH, SSD_WIDTH), 0.02),
        'ffn_w_gate': nrm((N_DENSE, D, D_FF), D ** -0.5),
        'ffn_w_up': nrm((N_DENSE, D, D_FF), D ** -0.5),
        'ffn_w_down': nrm((N_DENSE, D_FF, D), DEEPNORM_BETA * D_FF ** -0.5),
        'moe_w_router': nrm((N_MOE, D, N_EXPERTS), D ** -0.5),
        'moe_b_router': nrm((N_MOE, N_EXPERTS), 0.01),
        'moe_w_gate': nrm((N_MOE, N_EXPERTS, D, D_FF_EXPERT), D ** -0.5),
        'moe_w_up': nrm((N_MOE, N_EXPERTS, D, D_FF_EXPERT), D ** -0.5),
        'moe_w_down': nrm((N_MOE, N_EXPERTS, D_FF_EXPERT, D), DEEPNORM_BETA * D_FF_EXPERT ** -0.5),
    }


def reference(x_prompt, x_sample, c_prompt, c_sample, state_hgrn, state_s5, state_ssd, state_ssd_conv,
              w_ada, b_ada, ln_g, ln_b, w_in, w_out, hgrn_lb_logits, hgrn_norm_g,
              s5_a_re, s5_a_im, s5_log_dt, s5_b_re, s5_b_im, s5_c_re, s5_c_im, s5_d, s5_w_glu, s5_b_glu,
              ssd_conv_w, ssd_conv_b, ssd_dt_bias, ssd_a_log, ssd_d, ssd_norm_g,
              ffn_w_gate, ffn_w_up, ffn_w_down,
              moe_w_router, moe_b_router, moe_w_gate, moe_w_up, moe_w_down):
    p = dict(w_ada=w_ada, b_ada=b_ada, ln_g=ln_g, ln_b=ln_b, w_in=w_in, w_out=w_out,
             hgrn_lb_logits=hgrn_lb_logits, hgrn_norm_g=hgrn_norm_g,
             s5_a_re=s5_a_re, s5_a_im=s5_a_im, s5_log_dt=s5_log_dt, s5_b_re=s5_b_re, s5_b_im=s5_b_im,
             s5_c_re=s5_c_re, s5_c_im=s5_c_im, s5_d=s5_d, s5_w_glu=s5_w_glu, s5_b_glu=s5_b_glu,
             ssd_conv_w=ssd_conv_w, ssd_conv_b=ssd_conv_b, ssd_dt_bias=ssd_dt_bias, ssd_a_log=ssd_a_log,
             ssd_d=ssd_d, ssd_norm_g=ssd_norm_g,
             ffn_w_gate=ffn_w_gate, ffn_w_up=ffn_w_up, ffn_w_down=ffn_w_down,
             moe_w_router=moe_w_router, moe_b_router=moe_b_router, moe_w_gate=moe_w_gate,
             moe_w_up=moe_w_up, moe_w_down=moe_w_down)
    nb = x_prompt.shape[0]
    z_hgrn = jnp.zeros((DEPTH, nb, HGRN_HEADS, HGRN_HEAD_DIM, HGRN_HEAD_DIM), F32)
    z_s5 = jnp.zeros((DEPTH, nb, 2, S5_GROUPS, S5_STATE), F32)
    z_ssd = jnp.zeros((DEPTH, nb, SSD_HEADS, SSD_STATE, SSD_HEAD_DIM), F32)
    z_conv = jnp.zeros((DEPTH, nb, SSD_CONV_WIDTH - 1, SSD_CONV_DIM), x_prompt.dtype)
    y_prompt, hgrn_p, s5_p, ssd_p, conv_p = _trunk(x_prompt, c_prompt, z_hgrn, z_s5, z_ssd, z_conv, p)
    y_sample, hgrn_s, s5_s, ssd_s, conv_s = _trunk(x_sample, c_sample, state_hgrn, state_s5, state_ssd,
                                                   state_ssd_conv, p)
    return (y_prompt, y_sample, hgrn_p, s5_p, ssd_p, conv_p, hgrn_s, s5_s, ssd_s, conv_s)
```

```python
import functools
import math

import jax
import jax.numpy as jnp
from jax import lax
from jax.experimental import pallas as pl
from jax.experimental.pallas import tpu as pltpu

F32 = jnp.float32
BF16 = jnp.bfloat16

D = 1024
NB_P, L_P = 8, 2048
NB_S, L_S = 128, 8
ROWS_P = NB_P * L_P
ROWS_S = NB_S * L_S
ROWS = ROWS_P + ROWS_S
DEPTH = 2
HW = 384
S5W = 256
SSW = 384
NH = 6
HD = 64
S5N = 1024
CONVC = 640
N_IN = 2822
N_IN_PAD = 2944
P_HG = 1536
P_SS = 1152
DFF = 2816
NEXP = 8
ALPHA = (2 * DEPTH) ** 0.25
LN_EPS = 1e-5
RMS_EPS = 1e-6

TM = 256
NPT = ROWS_P // TM
NST = ROWS_S // TM
NT = NPT + NST
SEQ_PER_TILE = TM // L_S
TILES_PER_SEQ = L_P // TM

CB = 128
NPC = ROWS_P // CB
NSC = ROWS_S // CB
CH_PER_SEQ = L_P // CB
SEQ_PER_CB = CB // L_S

TMF = 512
TF = 1408
VMEM_LIMIT = 56 * 1024 * 1024


def _cparams(sem):
    return pltpu.CompilerParams(dimension_semantics=sem, vmem_limit_bytes=VMEM_LIMIT)


def _bdot(a, b):
    return jnp.dot(a.astype(BF16), b.astype(BF16), preferred_element_type=F32)


def _bdot_nt(a, b):
    return lax.dot_general(a.astype(BF16), b.astype(BF16), (((1,), (1,)), ((), ())),
                           preferred_element_type=F32)


def _split_dot(x, e, passes):
    acc = None
    r = x
    for _ in range(passes):
        hi = r.astype(BF16)
        d = jnp.dot(hi, e, preferred_element_type=F32)
        acc = d if acc is None else acc + d
        r = r - hi.astype(F32)
    return acc


def _split_dot_l(e, x, passes):
    acc = None
    r = x
    for _ in range(passes):
        hi = r.astype(BF16)
        d = jnp.dot(e, hi, preferred_element_type=F32)
        acc = d if acc is None else acc + d
        r = r - hi.astype(F32)
    return acc


def _silu(x):
    return x * jax.nn.sigmoid(x)


def _layer_norm(x, g, b):
    mu = jnp.mean(x, -1, keepdims=True)
    xc = x - mu
    var = jnp.mean(xc * xc, -1, keepdims=True)
    return xc * lax.rsqrt(var + LN_EPS) * g + b


def _rowmod(i, p_ref, s_ref):
    s = jnp.broadcast_to(s_ref[...], (SEQ_PER_TILE, L_S, D)).reshape(TM, D)
    return jnp.where(i < NPT, p_ref[0], s)


ADA_TN = 1536


def _ada_body(c_ref, w_ref, b_ref, o_ref):
    o_ref[...] = _bdot(_silu(c_ref[...]), w_ref[...]) + b_ref[...]


def _ada(c_all, w_ada, b_ada):
    nc = c_all.shape[0]
    return pl.pallas_call(
        _ada_body,
        grid=(DEPTH, 6 * D // ADA_TN),
        in_specs=[
            pl.BlockSpec((nc, D), lambda l, j: (0, 0)),
            pl.BlockSpec((None, D, ADA_TN), lambda l, j: (l, 0, j)),
            pl.BlockSpec((None, 1, ADA_TN), lambda l, j: (l, 0, j)),
        ],
        out_specs=pl.BlockSpec((None, nc, ADA_TN), lambda l, j: (l, 0, j)),
        out_shape=jax.ShapeDtypeStruct((DEPTH, nc, 6 * D), F32),
        compiler_params=_cparams(("parallel", "parallel")),
        name="ada",
    )(c_all, w_ada, b_ada.reshape(DEPTH, 1, 6 * D))


def _mod_specs(layer, k):
    ps = pl.BlockSpec((None, 1, 1, D),
                      lambda i: (layer, NB_S + jnp.minimum(i // TILES_PER_SEQ, NB_P - 1), 0, k))
    ss = pl.BlockSpec((None, SEQ_PER_TILE, 1, D),
                      lambda i: (layer, jnp.clip(i - NPT, 0, NST - 1), 0, k))
    return [ps, ss]


def _row_spec(width):
    return pl.BlockSpec((TM, width), lambda i: (i, 0))


def _const_spec(shape):
    nd = len(shape)
    return pl.BlockSpec(shape, lambda *_: (0,) * nd)


def _proj_out(x, i, scp, scs, shp, shs, w_ref, ph_ref, ps_ref, pm_ref):
    h = x * (1.0 + _rowmod(i, scp, scs)) + _rowmod(i, shp, shs)
    proj = jnp.dot(h.astype(BF16), w_ref[...], preferred_element_type=F32)
    ph_ref[...] = proj[:, 0:P_HG]
    ps_ref[...] = proj[:, P_HG:P_HG + S5W]
    pm_ref[...] = proj[:, P_HG + S5W:N_IN_PAD]


def _a0_body(xp_ref, xs_ref, scp, scs, shp, shs, w_ref, x_ref, ph_ref, ps_ref, pm_ref):
    i = pl.program_id(0)
    x = jnp.where(i < NPT, xp_ref[...], xs_ref[...])
    x_ref[...] = x
    _proj_out(x, i, scp, scs, shp, shs, w_ref, ph_ref, ps_ref, pm_ref)


def _a1_body(x1_ref, f_ref, gp, gs, lng_ref, lnb_ref, scp, scs, shp, shs, w_ref,
             x_ref, ph_ref, ps_ref, pm_ref):
    i = pl.program_id(0)
    x = _layer_norm(ALPHA * x1_ref[...] + _rowmod(i, gp, gs) * f_ref[...], lng_ref[...], lnb_ref[...])
    x_ref[...] = x
    _proj_out(x, i, scp, scs, shp, shs, w_ref, ph_ref, ps_ref, pm_ref)


def _a_out():
    specs = [_row_spec(D), _row_spec(P_HG), _row_spec(S5W), _row_spec(P_SS)]
    shapes = [jax.ShapeDtypeStruct((ROWS, w), F32) for w in (D, P_HG, S5W, P_SS)]
    return specs, shapes


def _stage_a0(xp, xs, ada4, w_in_b):
    out_specs, out_shape = _a_out()
    return pl.pallas_call(
        _a0_body,
        grid=(NT,),
        in_specs=[
            pl.BlockSpec((TM, D), lambda i: (jnp.minimum(i, NPT - 1), 0)),
            pl.BlockSpec((TM, D), lambda i: (jnp.clip(i - NPT, 0, NST - 1), 0)),
            *_mod_specs(0, 1), *_mod_specs(0, 0),
            _const_spec((D, N_IN_PAD)),
        ],
        out_specs=out_specs, out_shape=out_shape,
        compiler_params=_cparams(("parallel",)),
        name="stage_a0",
    )(xp, xs, ada4, ada4, ada4, ada4, w_in_b)


def _stage_a1(layer, x1, f, ada4, ln_g, ln_b, w_in_b):
    out_specs, out_shape = _a_out()
    return pl.pallas_call(
        _a1_body,
        grid=(NT,),
        in_specs=[
            _row_spec(D), _row_spec(D),
            *_mod_specs(layer - 1, 5),
            _const_spec((1, D)), _const_spec((1, D)),
            *_mod_specs(layer, 1), *_mod_specs(layer, 0),
            _const_spec((D, N_IN_PAD)),
        ],
        out_specs=out_specs, out_shape=out_shape,
        compiler_params=_cparams(("parallel",)),
        name="stage_a1",
    )(x1, f, ada4, ada4, ln_g, ln_b, ada4, ada4, ada4, ada4, w_in_b)


def _mixer_blk(width):
    return pl.BlockSpec((CB, width), lambda i: (i, 0))


def _pstate_spec(shape):
    nd = len(shape)
    return pl.BlockSpec((1,) + shape, lambda i: (jnp.minimum(i // CH_PER_SEQ, NB_P - 1),) + (0,) * nd)


def _sstate_spec(shape):
    nd = len(shape)
    return pl.BlockSpec((SEQ_PER_CB,) + shape, lambda i: (jnp.clip(i - NPC, 0, NSC - 1),) + (0,) * nd)


def _seq_last_rows(x):
    w = x.shape[-1]
    x3 = x.reshape(SEQ_PER_CB, L_S, w)
    return jnp.broadcast_to(x3[:, L_S - 1:L_S, :], (SEQ_PER_CB, L_S, w)).reshape(CB, w)


def _concat_heads(parts):
    return jnp.concatenate(parts, axis=1)


def _stack_select(shape, row_div, lane_div):
    r = lax.broadcasted_iota(jnp.int32, shape, 0) // row_div
    c = lax.broadcasted_iota(jnp.int32, shape, 1) // lane_div
    return r == c


def _seq_expand_lanes(qh):
    q2 = jnp.concatenate([qh, qh], axis=1)
    q16 = jnp.concatenate([q2] * (SEQ_PER_CB // 2), axis=1)
    return jnp.where(_stack_select((CB, SEQ_PER_CB * HD), L_S, HD), q16, 0.0)


def _seq_expand_rows(xt):
    t = jnp.broadcast_to(xt[None], (SEQ_PER_CB, HD, CB)).reshape(SEQ_PER_CB * HD, CB)
    return jnp.where(_stack_select((SEQ_PER_CB * HD, CB), HD, L_S), t, 0.0)


def _fold_seq_lanes(full):
    acc = full[:, 0:128]
    for j in range(1, SEQ_PER_CB * HD // 128):
        acc = acc + full[:, 128 * j:128 * (j + 1)]
    return acc[:, 0:HD] + acc[:, HD:2 * HD]


def _hgrn_block(prompt, p_ref, lb_ref, hg_ref, e64_ref, tril_ref, oh_ref, st_scr, i,
                stp_out=None, sts_in=None, sts_out=None):
    lb = lb_ref[...]
    qr = p_ref[:, 0:HW]
    fr = p_ref[:, HW:2 * HW]
    v = p_ref[:, 2 * HW:3 * HW]
    gr = p_ref[:, 3 * HW:4 * HW]
    ls = jnp.minimum(fr, 0.0) - jnp.log1p(jnp.exp(-jnp.abs(fr)))
    a = jnp.log(lb)
    bb = jnp.log1p(-lb) + ls
    lf = jnp.maximum(a, bb) + jnp.log1p(jnp.exp(-jnp.abs(a - bb)))
    kk = (1.0 - lb) * jax.nn.sigmoid(-fr)
    q = _silu(qr)
    b = _split_dot_l(tril_ref[...], lf, 3)

    nsub = CB // 8
    b3 = b.reshape(nsub, 8, HW)
    q3 = q.reshape(nsub, 8, HW)
    k3 = kk.reshape(nsub, 8, HW)
    v3 = v.reshape(nsub, 8, HW)
    r3 = lax.broadcasted_iota(jnp.int32, (nsub, 8, HW), 1)
    e64 = e64_ref[...]
    o = jnp.zeros((CB, HW), F32)
    for s in range(8):
        dlt = jnp.minimum(b3 - b3[:, s:s + 1, :], 0.0)
        w = jnp.where(r3 >= s, jnp.exp(dlt), 0.0) * q3 * k3[:, s:s + 1, :]
        hsum = jnp.dot(w.reshape(CB, HW).astype(BF16), e64, preferred_element_type=F32)
        o = o + hsum * jnp.broadcast_to(v3[:, s:s + 1, :], (nsub, 8, HW)).reshape(CB, HW)

    qt = q * jnp.exp(b)
    if prompt:
        ti = lax.broadcasted_iota(jnp.int32, (CB, CB), 0)
        si = lax.broadcasted_iota(jnp.int32, (CB, CB), 1)
        levels = []
        m = 8
        while m < CB:
            nb = CB // (2 * m)
            b4 = b.reshape(nb, 2 * m, HW)
            bmid = b4[:, m - 1:m, :]
            pos = lax.broadcasted_iota(jnp.int32, (nb, 2 * m, HW), 1)
            qq = jnp.where(pos >= m, q.reshape(nb, 2 * m, HW) * jnp.exp(jnp.minimum(b4 - bmid, 0.0)), 0.0)
            kq = jnp.where(pos < m, kk.reshape(nb, 2 * m, HW) * jnp.exp(jnp.minimum(bmid - b4, 0.0)), 0.0)
            same = (ti // (2 * m)) == (si // (2 * m))
            levels.append((qq.reshape(CB, HW), kq.reshape(CB, HW), same))
            m *= 2
        blast = b[CB - 1:CB, :]
        kd = kk * jnp.exp(blast - b)
        vt = v.T
        parts = []
        for h in range(NH):
            sl = slice(HD * h, HD * (h + 1))
            sc = None
            for qq, kq, same in levels:
                t = jnp.where(same, _bdot_nt(qq[:, sl], kq[:, sl]), 0.0)
                sc = t if sc is None else sc + t
            st = st_scr[h]
            oh = o[:, sl] + _bdot(sc, v[:, sl]) + _bdot_nt(qt[:, sl], st)
            st_new = st * jnp.exp(blast[:, sl]) + _bdot(vt[sl, :], kd[:, sl])
            st_scr[h] = st_new
            parts.append(oh)

        @pl.when(i % CH_PER_SEQ == CH_PER_SEQ - 1)
        def _():
            stp_out[0] = st_scr[...]
    else:
        blast = _seq_last_rows(b)
        kd = kk * jnp.exp(blast - b)
        dec = jnp.exp(blast)
        vt = v.T
        parts = []
        for h in range(NH):
            sl = slice(HD * h, HD * (h + 1))
            sts = sts_in[:, h].reshape(SEQ_PER_CB * HD, HD)
            full = _bdot_nt(qt[:, sl], sts)
            sel = jnp.where(_stack_select((CB, SEQ_PER_CB * HD), L_S, HD), full, 0.0)
            parts.append(o[:, sl] + _fold_seq_lanes(sel))
            dec3 = dec[:, sl].reshape(SEQ_PER_CB, L_S, HD)[:, L_S - 1:L_S, :]
            dec_rows = jnp.broadcast_to(dec3, (SEQ_PER_CB, HD, HD)).reshape(SEQ_PER_CB * HD, HD)
            upd = _bdot(_seq_expand_rows(vt[sl, :]), kd[:, sl])
            sts_out[:, h] = (sts * dec_rows + upd).reshape(SEQ_PER_CB, HD, HD)
    oall = _concat_heads(parts)
    ms = _split_dot(oall * oall, e64, 2) * (1.0 / HD)
    oh_ref[...] = oall * lax.rsqrt(ms + RMS_EPS) * hg_ref[...] * _silu(gr)


def _hgrn_body(p_ref, lb_ref, hg_ref, e64_ref, trilp_ref, trils_ref, sts_in,
               oh_ref, stp_out, sts_out, st_scr):
    i = pl.program_id(0)

    @pl.when((i < NPC) & (i % CH_PER_SEQ == 0))
    def _():
        st_scr[...] = jnp.zeros_like(st_scr)

    @pl.when(i < NPC)
    def _():
        _hgrn_block(True, p_ref, lb_ref, hg_ref, e64_ref, trilp_ref, oh_ref, st_scr, i, stp_out=stp_out)

    @pl.when(i >= NPC)
    def _():
        _hgrn_block(False, p_ref, lb_ref, hg_ref, e64_ref, trils_ref, oh_ref, st_scr, i,
                    sts_in=sts_in, sts_out=sts_out)


def _hgrn(ph, lb, hg, consts, st_t):
    return pl.pallas_call(
        _hgrn_body,
        grid=(NPC + NSC,),
        in_specs=[
            _mixer_blk(P_HG), _const_spec((1, HW)), _const_spec((1, HW)),
            _const_spec((HW, HW)), _const_spec((CB, CB)), _const_spec((CB, CB)),
            _sstate_spec((NH, HD, HD)),
        ],
        out_specs=[_mixer_blk(HW), _pstate_spec((NH, HD, HD)), _sstate_spec((NH, HD, HD))],
        out_shape=[
            jax.ShapeDtypeStruct((ROWS, HW), F32),
            jax.ShapeDtypeStruct((NB_P, NH, HD, HD), F32),
            jax.ShapeDtypeStruct((NB_S, NH, HD, HD), F32),
        ],
        scratch_shapes=[pltpu.VMEM((NH, HD, HD), F32)],
        compiler_params=_cparams(("arbitrary",)),
        name="hgrn",
    )(ph, lb, hg, consts["e64"], consts["trilp"], consts["trils"], st_t)


def _cmul_add(hr, hi, lr, li, sr, si):
    return hr + lr * sr - li * si, hi + lr * si + li * sr


def _s5_block(prompt, p_ref, tab_ref, bblk_ref, ccat_ref, d_ref, wglu_ref, bglu_ref, os_ref,
              carry_scr, h_scr, i, s5p_out=None, s5s_in=None, s5s_out=None):
    u = p_ref[...]
    u_hi = u.astype(BF16)
    u_lo = (u - u_hi.astype(F32)).astype(BF16)
    bu = (jnp.dot(u_hi, bblk_ref[0], preferred_element_type=F32)
          + jnp.dot(u_lo, bblk_ref[0], preferred_element_type=F32)
          + jnp.dot(u_hi, bblk_ref[1], preferred_element_type=F32))
    hr = bu[:, 0:S5N]
    hi = bu[:, S5N:2 * S5N]
    nsub = CB // 8
    for idx, dsh in enumerate((1, 2, 4)):
        sr = pltpu.roll(hr, dsh, 0).reshape(nsub, 8, S5N)
        si = pltpu.roll(hi, dsh, 0).reshape(nsub, 8, S5N)
        lr = tab_ref[idx, :, 0:S5N][None]
        li = tab_ref[idx, :, S5N:2 * S5N][None]
        nr, ni = _cmul_add(hr.reshape(nsub, 8, S5N), hi.reshape(nsub, 8, S5N), lr, li, sr, si)
        hr = nr.reshape(CB, S5N)
        hi = ni.reshape(CB, S5N)
    tcr = tab_ref[3, :, 0:S5N]
    tci = tab_ref[3, :, S5N:2 * S5N]
    if prompt:
        cr = carry_scr[:, 0:S5N]
        ci = carry_scr[:, S5N:2 * S5N]
        for j in range(nsub):
            tr, tim = _cmul_add(hr[8 * j:8 * j + 8], hi[8 * j:8 * j + 8], tcr, tci, cr, ci)
            h_scr[8 * j:8 * j + 8, 0:S5N] = tr
            h_scr[8 * j:8 * j + 8, S5N:2 * S5N] = tim
            cr = tr[7:8]
            ci = tim[7:8]
        carry_scr[:, 0:S5N] = cr
        carry_scr[:, S5N:2 * S5N] = ci

        @pl.when(i % CH_PER_SEQ == CH_PER_SEQ - 1)
        def _():
            s5p_out[0] = carry_scr[...]
    else:
        cr = s5s_in[:, :, 0:S5N]
        ci = s5s_in[:, :, S5N:2 * S5N]
        tr, tim = _cmul_add(hr.reshape(nsub, 8, S5N), hi.reshape(nsub, 8, S5N), tcr[None], tci[None], cr, ci)
        h_scr[:, 0:S5N] = tr.reshape(CB, S5N)
        h_scr[:, S5N:2 * S5N] = tim.reshape(CB, S5N)
        sb = lax.broadcasted_iota(jnp.int32, (SEQ_PER_CB, CB), 0)
        st = lax.broadcasted_iota(jnp.int32, (SEQ_PER_CB, CB), 1)
        sel = (st == L_S * sb + (L_S - 1)).astype(BF16)
        s5s_out[...] = _split_dot_l(sel, h_scr[...], 3)
    y = _bdot(h_scr[...], ccat_ref[...]) + d_ref[...] * u
    c0 = math.sqrt(2.0 / math.pi)
    y = y * (0.5 * (1.0 + jnp.tanh(c0 * (y + 0.044715 * (y * y * y)))))
    os_ref[...] = y * jax.nn.sigmoid(_bdot(y, wglu_ref[...]) + bglu_ref[...])


def _s5_body(p_ref, tab_ref, bblk_ref, ccat_ref, d_ref, wglu_ref, bglu_ref, s5s_in,
             os_ref, s5p_out, s5s_out, carry_scr, h_scr):
    i = pl.program_id(0)
    args = (p_ref, tab_ref, bblk_ref, ccat_ref, d_ref, wglu_ref, bglu_ref, os_ref, carry_scr, h_scr, i)

    @pl.when((i < NPC) & (i % CH_PER_SEQ == 0))
    def _():
        carry_scr[...] = jnp.zeros_like(carry_scr)

    @pl.when(i < NPC)
    def _():
        _s5_block(True, *args, s5p_out=s5p_out)

    @pl.when(i >= NPC)
    def _():
        _s5_block(False, *args, s5s_in=s5s_in, s5s_out=s5s_out)


def _s5(ps, prm, st):
    return pl.pallas_call(
        _s5_body,
        grid=(NPC + NSC,),
        in_specs=[
            _mixer_blk(S5W), _const_spec((4, 8, 2 * S5N)), _const_spec((2, S5W, 2 * S5N)),
            _const_spec((2 * S5N, S5W)), _const_spec((1, S5W)), _const_spec((S5W, S5W)),
            _const_spec((1, S5W)), _sstate_spec((1, 2 * S5N)),
        ],
        out_specs=[_mixer_blk(S5W), _pstate_spec((1, 2 * S5N)), _sstate_spec((2 * S5N,))],
        out_shape=[
            jax.ShapeDtypeStruct((ROWS, S5W), F32),
            jax.ShapeDtypeStruct((NB_P, 1, 2 * S5N), F32),
            jax.ShapeDtypeStruct((NB_S, 2 * S5N), F32),
        ],
        scratch_shapes=[pltpu.VMEM((1, 2 * S5N), F32), pltpu.VMEM((CB, 2 * S5N), F32)],
        compiler_params=_cparams(("arbitrary",)),
        name="s5",
    )(ps, prm["tab"], prm["bblk"], prm["ccat"], prm["d"], prm["wglu"], prm["bglu"], st)


def _ssd_block(prompt, p_ref, cw_ref, cb_ref, dtb_ref, aneg_ref, dx_ref, sg_ref, e64x_ref, e128x_ref,
               g192_ref, tril_ref, om_ref, cbuf, st_scr, i, ssp_out=None, hist_ref=None,
               sss_in=None, sss_out=None):
    z = p_ref[:, 0:SSW]
    xbc = p_ref[:, SSW:SSW + CONVC]
    dtr = p_ref[:, SSW + CONVC:P_SS]
    acc = cb_ref[...] + cw_ref[3:4, :] * xbc
    if prompt:
        cbuf[8:8 + CB, :] = xbc
        for k in (1, 2, 3):
            acc = acc + cw_ref[3 - k:4 - k, :] * cbuf[8 - k:8 - k + CB, :]
        cbuf[0:8, :] = cbuf[CB:CB + 8, :]
    else:
        tl = lax.broadcasted_iota(jnp.int32, (CB, CONVC), 0) % L_S
        hist = hist_ref[...]
        for k in (1, 2, 3):
            hk = hist if k == 3 else pltpu.roll(hist, CB - (3 - k), 0)
            sh = jnp.where(tl >= k, pltpu.roll(xbc, k, 0), hk)
            acc = acc + cw_ref[3 - k:4 - k, :] * sh
    xc = _silu(acc)
    xs = xc[:, 0:SSW]
    bm = xc[:, SSW:SSW + 2 * HD]
    cm = xc[:, SSW + 2 * HD:CONVC]
    xdt = dtr + dtb_ref[...]
    dt = jnp.maximum(xdt, 0.0) + jnp.log1p(jnp.exp(-jnp.abs(xdt)))
    la = dt * aneg_ref[...]
    tril = tril_ref[...]
    b6 = _split_dot_l(tril, la, 3)
    bx = _split_dot(b6, e64x_ref[...], 3)
    bxw = _split_dot(b6, e128x_ref[...], 3)
    dtx = _split_dot(dt, e64x_ref[...], 2)
    bm_rep = _concat_heads([bm[:, 0:HD]] * 3 + [bm[:, HD:2 * HD]] * 3)
    cm_rep = _concat_heads([cm[:, 0:HD]] * 3 + [cm[:, HD:2 * HD]] * 3)
    kh = bm_rep * dtx
    qt = cm_rep * jnp.exp(bx)
    blast = bx[CB - 1:CB, :] if prompt else _seq_last_rows(bx)
    kd = kh * jnp.exp(blast - bx)
    kdt = kd.T
    mask = tril > 0
    parts = []
    for h in range(NH):
        sl = slice(HD * h, HD * (h + 1))
        bcol = bxw[:, 128 * h:128 * (h + 1)]
        decay = jnp.exp(jnp.where(mask, bcol - bcol.T, -1e30))
        sc = _bdot_nt(cm_rep[:, sl], kh[:, sl]) * decay
        oh = _bdot(sc, xs[:, sl])
        if prompt:
            st = st_scr[h]
            oh = oh + _bdot(qt[:, sl], st)
            st_scr[h] = st * jnp.exp(blast[:, sl]) + _bdot(kdt[sl, :], xs[:, sl])
        else:
            sts = sss_in[:, h].reshape(SEQ_PER_CB * HD, HD)
            oh = oh + _bdot(_seq_expand_lanes(qt[:, sl]), sts)
            dec3 = jnp.exp(blast[:, sl]).reshape(SEQ_PER_CB, L_S, HD)[:, L_S - 1:L_S, :]
            dec_rows = jnp.broadcast_to(dec3, (SEQ_PER_CB, HD, HD)).reshape(SEQ_PER_CB * HD, HD)
            upd = _bdot(_seq_expand_rows(kdt[sl, :]), xs[:, sl])
            sss_out[:, h] = (sts * dec_rows + upd).reshape(SEQ_PER_CB, HD, HD)
        parts.append(oh)
    if prompt:
        @pl.when(i % CH_PER_SEQ == CH_PER_SEQ - 1)
        def _():
            ssp_out[0] = st_scr[...]
    y = (_concat_heads(parts) + dx_ref[...] * xs) * _silu(z)
    ms = _split_dot(y * y, g192_ref[...], 2) * (1.0 / (SSW // 2))
    om_ref[...] = y * lax.rsqrt(ms + RMS_EPS) * sg_ref[...]


def _ssd_body(p_ref, cw_ref, cb_ref, dtb_ref, aneg_ref, dx_ref, sg_ref, e64x_ref, e128x_ref, g192_ref,
              trilp_ref, trils_ref, hist_ref, sss_in, om_ref, ssp_out, sss_out, cbuf, st_scr):
    i = pl.program_id(0)
    common = (p_ref, cw_ref, cb_ref, dtb_ref, aneg_ref, dx_ref, sg_ref, e64x_ref, e128x_ref, g192_ref)

    @pl.when((i < NPC) & (i % CH_PER_SEQ == 0))
    def _():
        st_scr[...] = jnp.zeros_like(st_scr)
        cbuf[...] = jnp.zeros_like(cbuf)

    @pl.when(i < NPC)
    def _():
        _ssd_block(True, *common, trilp_ref, om_ref, cbuf, st_scr, i, ssp_out=ssp_out)

    @pl.when(i >= NPC)
    def _():
        _ssd_block(False, *common, trils_ref, om_ref, cbuf, st_scr, i, hist_ref=hist_ref,
                   sss_in=sss_in, sss_out=sss_out)


def _ssd(pm, prm, consts, hist, st):
    return pl.pallas_call(
        _ssd_body,
        grid=(NPC + NSC,),
        in_specs=[
            _mixer_blk(P_SS), _const_spec((4, CONVC)), _const_spec((1, CONVC)),
            _const_spec((1, 128)), _const_spec((1, 128)), _const_spec((1, SSW)), _const_spec((1, SSW)),
            _const_spec((128, SSW)), _const_spec((128, 2 * SSW)), _const_spec((SSW, SSW)),
            _const_spec((CB, CB)), _const_spec((CB, CB)),
            pl.BlockSpec((CB, CONVC), lambda i: (jnp.clip(i - NPC, 0, NSC - 1), 0)),
            _sstate_spec((NH, HD, HD)),
        ],
        out_specs=[_mixer_blk(SSW), _pstate_spec((NH, HD, HD)), _sstate_spec((NH, HD, HD))],
        out_shape=[
            jax.ShapeDtypeStruct((ROWS, SSW), F32),
            jax.ShapeDtypeStruct((NB_P, NH, HD, HD), F32),
            jax.ShapeDtypeStruct((NB_S, NH, HD, HD), F32),
        ],
        scratch_shapes=[pltpu.VMEM((CB + 8, CONVC), F32), pltpu.VMEM((NH, HD, HD), F32)],
        compiler_params=_cparams(("arbitrary",)),
        name="ssd",
    )(pm, prm["cw"], prm["cb"], prm["dtb"], prm["aneg"], prm["dx"], prm["sg"],
      consts["e64x"], consts["e128x"], consts["g192"], consts["trilp"], consts["trils"], hist, st)


def _o_core(i, oh_ref, os_ref, om_ref, x_ref, wo_ref, gp, gs, lng_ref, lnb_ref, scp, scs, shp, shs):
    mix = (_bdot(oh_ref[...], wo_ref[0:HW, :]) + _bdot(os_ref[...], wo_ref[HW:HW + S5W, :])
           + _bdot(om_ref[...], wo_ref[HW + S5W:D, :]))
    x1 = _layer_norm(ALPHA * x_ref[...] + _rowmod(i, gp, gs) * mix, lng_ref[...], lnb_ref[...])
    h2 = x1 * (1.0 + _rowmod(i, scp, scs)) + _rowmod(i, shp, shs)
    return x1, h2


def _o_body(oh_ref, os_ref, om_ref, x_ref, wo_ref, gp, gs, lng_ref, lnb_ref, scp, scs, shp, shs,
            x1_ref, h2_ref):
    i = pl.program_id(0)
    x1, h2 = _o_core(i, oh_ref, os_ref, om_ref, x_ref, wo_ref, gp, gs, lng_ref, lnb_ref, scp, scs, shp, shs)
    x1_ref[...] = x1
    h2_ref[...] = h2.astype(BF16)


def _o_router_body(oh_ref, os_ref, om_ref, x_ref, wo_ref, gp, gs, lng_ref, lnb_ref, scp, scs, shp, shs,
                   wr_ref, br_ref, x1_ref, h2_ref, route_ref):
    i = pl.program_id(0)
    x1, h2 = _o_core(i, oh_ref, os_ref, om_ref, x_ref, wo_ref, gp, gs, lng_ref, lnb_ref, scp, scs, shp, shs)
    x1_ref[...] = x1
    h2_ref[...] = h2.astype(BF16)
    logits = jnp.dot(h2, wr_ref[...], preferred_element_type=F32, precision=lax.Precision.HIGHEST) + br_ref[...]
    lane = lax.broadcasted_iota(jnp.int32, (TM, 128), 1).astype(F32)
    neg = -jnp.inf
    lg = jnp.where(lane < NEXP, logits, neg)
    m1 = jnp.max(lg, axis=-1, keepdims=True)
    i1 = jnp.min(jnp.where(lg == m1, lane, 128.0), axis=-1, keepdims=True)
    lg2 = jnp.where(lane == i1, neg, lg)
    m2 = jnp.max(lg2, axis=-1, keepdims=True)
    i2 = jnp.min(jnp.where(lg2 == m2, lane, 128.0), axis=-1, keepdims=True)
    e2 = jnp.exp(m2 - m1)
    den = 1.0 + e2
    route_ref[...] = jnp.where(lane == 0.0, i1, jnp.where(lane == 1.0, i2,
                               jnp.where(lane == 2.0, 1.0 / den, jnp.where(lane == 3.0, e2 / den, 0.0))))


def _stage_o(layer, oh, os_, om, x, wo_b, ada4, ln_g, ln_b, router=None):
    in_specs = [
        _row_spec(HW), _row_spec(S5W), _row_spec(SSW), _row_spec(D), _const_spec((D, D)),
        *_mod_specs(layer, 2), _const_spec((1, D)), _const_spec((1, D)),
        *_mod_specs(layer, 4), *_mod_specs(layer, 3),
    ]
    args = [oh, os_, om, x, wo_b, ada4, ada4, ln_g, ln_b, ada4, ada4, ada4, ada4]
    out_specs = [_row_spec(D), _row_spec(D)]
    out_shape = [jax.ShapeDtypeStruct((ROWS, D), F32), jax.ShapeDtypeStruct((ROWS, D), BF16)]
    body = _o_body
    if router is not None:
        in_specs += [_const_spec((D, 128)), _const_spec((1, 128))]
        args += list(router)
        out_specs.append(_row_spec(128))
        out_shape.append(jax.ShapeDtypeStruct((ROWS, 128), F32))
        body = _o_router_body
    return pl.pallas_call(
        body, grid=(NT,), in_specs=in_specs, out_specs=out_specs, out_shape=out_shape,
        compiler_params=_cparams(("parallel",)),
        name="stage_o_router" if router is not None else "stage_o",
    )(*args)


def _ffn_body(te_ref, h_ref, wg_ref, wu_ref, wd_ref, o_ref, acc_ref):
    j = pl.program_id(1)

    @pl.when(j == 0)
    def _():
        acc_ref[...] = jnp.zeros_like(acc_ref)

    h = h_ref[...]
    g = jnp.dot(h, wg_ref[...], preferred_element_type=F32)
    u = jnp.dot(h, wu_ref[...], preferred_element_type=F32)
    act = (_silu(g) * u).astype(BF16)
    acc_ref[...] += jnp.dot(act, wd_ref[...], preferred_element_type=F32)

    @pl.when(j == pl.num_programs(1) - 1)
    def _():
        o_ref[...] = acc_ref[...]


def _ffn(tile_expert, h, wg, wu, wd):
    rows = h.shape[0]
    grid_spec = pltpu.PrefetchScalarGridSpec(
        num_scalar_prefetch=1,
        grid=(rows // TMF, DFF // TF),
        in_specs=[
            pl.BlockSpec((TMF, D), lambda i, j, te: (i, 0)),
            pl.BlockSpec((None, D, TF), lambda i, j, te: (te[i], 0, j)),
            pl.BlockSpec((None, D, TF), lambda i, j, te: (te[i], 0, j)),
            pl.BlockSpec((None, TF, D), lambda i, j, te: (te[i], j, 0)),
        ],
        out_specs=pl.BlockSpec((TMF, D), lambda i, j, te: (i, 0)),
        scratch_shapes=[pltpu.VMEM((TMF, D), F32)],
    )
    return pl.pallas_call(
        _ffn_body, grid_spec=grid_spec,
        out_shape=jax.ShapeDtypeStruct((rows, D), F32),
        compiler_params=_cparams(("parallel", "arbitrary")),
        name="ffn",
    )(tile_expert, h, wg, wu, wd)


def _final_body(x1_ref, ya_ref, yb_ref, route_ref, gp, gs, lng_ref, lnb_ref, yp_ref, ys_ref):
    i = pl.program_id(0)
    f = route_ref[:, 2:3] * ya_ref[...] + route_ref[:, 3:4] * yb_ref[...]
    y = _layer_norm(ALPHA * x1_ref[...] + _rowmod(i, gp, gs) * f, lng_ref[...], lnb_ref[...])

    @pl.when(i < NPT)
    def _():
        yp_ref[...] = y

    @pl.when(i >= NPT)
    def _():
        ys_ref[...] = y


def _stage_final(layer, x1, ya, yb, route, ada4, ln_g, ln_b):
    return pl.pallas_call(
        _final_body,
        grid=(NT,),
        in_specs=[
            _row_spec(D), _row_spec(D), _row_spec(D), _row_spec(128),
            *_mod_specs(layer, 5), _const_spec((1, D)), _const_spec((1, D)),
        ],
        out_specs=[
            pl.BlockSpec((TM, D), lambda i: (jnp.minimum(i, NPT - 1), 0)),
            pl.BlockSpec((TM, D), lambda i: (jnp.clip(i - NPT, 0, NST - 1), 0)),
        ],
        out_shape=[jax.ShapeDtypeStruct((ROWS_P, D), F32), jax.ShapeDtypeStruct((ROWS_S, D), F32)],
        compiler_params=_cparams(("arbitrary",)),
        name="stage_final",
    )(x1, ya, yb, route, ada4, ada4, ln_g, ln_b)


def _block_ones(n, blk):
    r = jnp.arange(n) // blk
    return (r[:, None] == r[None, :]).astype(BF16)


def _consts():
    t = jnp.arange(CB)
    causal = t[:, None] >= t[None, :]
    same_seq = (t[:, None] // L_S) == (t[None, :] // L_S)
    h = jnp.arange(128)
    return {
        "e64": _block_ones(HW, HD),
        "g192": _block_ones(SSW, SSW // 2),
        "trilp": causal.astype(BF16),
        "trils": (causal & same_seq).astype(BF16),
        "e64x": (h[:, None] == (jnp.arange(SSW)[None, :] // HD)).astype(BF16),
        "e128x": (h[:, None] == (jnp.arange(2 * SSW)[None, :] // 128)).astype(BF16),
    }


def _s5_params(a_re, a_im, log_dt, b_re, b_im, c_re, c_im, d, w_glu, b_glu):
    dt = jnp.exp(log_dt)[:, None]
    mag = jnp.exp(a_re * dt)
    lam_re, lam_im = mag * jnp.cos(a_im * dt), mag * jnp.sin(a_im * dt)
    den = a_re * a_re + a_im * a_im
    nr, ni = lam_re - 1.0, lam_im
    zr = (nr * a_re + ni * a_im) / den
    zi = (ni * a_re - nr * a_im) / den
    bbar_re = zr[..., None] * b_re - zi[..., None] * b_im
    bbar_im = zr[..., None] * b_im + zi[..., None] * b_re
    eye = jnp.eye(16, dtype=F32)
    blk = lambda bb: jnp.einsum('gph,gk->ghkp', bb, eye).reshape(S5W, S5N)
    bblk = jnp.concatenate([blk(bbar_re), blk(bbar_im)], axis=1)
    bblk_hi = bblk.astype(BF16)
    bblk_lo = (bblk - bblk_hi.astype(F32)).astype(BF16)
    cblk = lambda cc: jnp.einsum('ghp,gk->gpkh', cc, eye).reshape(S5N, S5W)
    ccat = jnp.concatenate([cblk(c_re), -cblk(c_im)], axis=0)
    lr, li = lam_re.reshape(-1), lam_im.reshape(-1)
    pows = [(jnp.ones_like(lr), jnp.zeros_like(li))]
    for _ in range(8):
        pr, pi = pows[-1]
        pows.append((pr * lr - pi * li, pr * li + pi * lr))
    rows = jnp.arange(8)[:, None]
    tabs = []
    for dsh in (1, 2, 4):
        pr, pi = pows[dsh]
        tabs.append(jnp.where(rows >= dsh, jnp.concatenate([pr, pi])[None, :], 0.0))
    tabs.append(jnp.stack([jnp.concatenate(pows[r + 1]) for r in range(8)]))
    return {
        "tab": jnp.stack(tabs), "bblk": jnp.stack([bblk_hi, bblk_lo]), "ccat": ccat.astype(BF16),
        "d": d.reshape(1, S5W), "wglu": w_glu.astype(BF16), "bglu": b_glu.reshape(1, S5W),
    }


def _pad_lanes(v, n=128):
    return jnp.pad(v, (0, n - v.shape[0])).reshape(1, n)


def kernel(x_prompt, x_sample, c_prompt, c_sample, state_hgrn, state_s5, state_ssd, state_ssd_conv, w_ada, b_ada, ln_g, ln_b, w_in, w_out, hgrn_lb_logits, hgrn_norm_g, s5_a_re, s5_a_im, s5_log_dt, s5_b_re, s5_b_im, s5_c_re, s5_c_im, s5_d, s5_w_glu, s5_b_glu, ssd_conv_w, ssd_conv_b, ssd_dt_bias, ssd_a_log, ssd_d, ssd_norm_g, ffn_w_gate, ffn_w_up, ffn_w_down, moe_w_router, moe_b_router, moe_w_gate, moe_w_up, moe_w_down):
    consts = _consts()
    c_all = jnp.concatenate([c_sample, c_prompt], axis=0)
    ada4 = _ada(c_all, w_ada, b_ada).reshape(DEPTH, NB_S + NB_P, 1, 6 * D)

    lb_all = jnp.cumsum(jax.nn.softmax(hgrn_lb_logits, axis=0), axis=0)
    lb_all = lb_all - lb_all[0]

    xp = x_prompt.reshape(ROWS_P, D)
    xs = x_sample.reshape(ROWS_S, D)
    new_h, new_s5, new_m, new_c = [], [], [], []
    x1 = f = route = None
    for l in range(DEPTH):
        w_in_b = jnp.pad(w_in[l], ((0, 0), (0, N_IN_PAD - N_IN))).astype(BF16)
        if l == 0:
            x, ph, ps, pm = _stage_a0(xp, xs, ada4, w_in_b)
        else:
            x, ph, ps, pm = _stage_a1(l, x1, f, ada4, ln_g[l - 1, 1].reshape(1, D), ln_b[l - 1, 1].reshape(1, D),
                                      w_in_b)
        oh, hg_p, hg_s = _hgrn(ph, lb_all[l].reshape(1, HW), hgrn_norm_g[l].reshape(1, HW), consts,
                               jnp.swapaxes(state_hgrn[l], -1, -2))
        s5p = _s5_params(s5_a_re[l], s5_a_im[l], s5_log_dt[l], s5_b_re[l], s5_b_im[l], s5_c_re[l], s5_c_im[l],
                         s5_d[l], s5_w_glu[l], s5_b_glu[l])
        os_, s5_p, s5_s = _s5(ps, s5p, state_s5[l].reshape(NB_S, 1, 2 * S5N))
        ssd_prm = {
            "cw": ssd_conv_w[l], "cb": ssd_conv_b[l].reshape(1, CONVC),
            "dtb": _pad_lanes(ssd_dt_bias[l]), "aneg": _pad_lanes(-jnp.exp(ssd_a_log[l])),
            "dx": jnp.repeat(ssd_d[l], HD).reshape(1, SSW), "sg": ssd_norm_g[l].reshape(1, SSW),
        }
        hist = jnp.pad(state_ssd_conv[l], ((0, 0), (0, L_S - 3), (0, 0))).reshape(ROWS_S, CONVC)
        om, ss_p, ss_s = _ssd(pm, ssd_prm, consts, hist, state_ssd[l])
        xbc = pm[:, SSW:SSW + CONVC]
        conv_p = xbc[:ROWS_P].reshape(NB_P, L_P, CONVC)[:, L_P - 3:]
        conv_s = xbc[ROWS_P:].reshape(NB_S, L_S, CONVC)[:, L_S - 3:]
        new_h.append((jnp.swapaxes(hg_p, -1, -2), jnp.swapaxes(hg_s, -1, -2)))
        new_s5.append((s5_p.reshape(NB_P, 2, 16, 64), s5_s.reshape(NB_S, 2, 16, 64)))
        new_m.append((ss_p, ss_s))
        new_c.append((conv_p, conv_s))

        wo_b = w_out[l].astype(BF16)
        lg, lbb = ln_g[l, 0].reshape(1, D), ln_b[l, 0].reshape(1, D)
        j = l // 2
        if l % 2 == 0:
            x1, h2 = _stage_o(l, oh, os_, om, x, wo_b, ada4, lg, lbb)
            f = _ffn(jnp.zeros((ROWS // TMF,), jnp.int32), h2, ffn_w_gate[j:j + 1].astype(BF16),
                     ffn_w_up[j:j + 1].astype(BF16), ffn_w_down[j:j + 1].astype(BF16))
        else:
            wr = jnp.pad(moe_w_router[j], ((0, 0), (0, 128 - NEXP)))
            br = _pad_lanes(moe_b_router[j])
            x1, h2, route = _stage_o(l, oh, os_, om, x, wo_b, ada4, lg, lbb, router=(wr, br))
            flat_e = route[:, 0:2].astype(jnp.int32).reshape(-1)
            onehot = (flat_e[:, None] == jnp.arange(NEXP)[None, :]).astype(jnp.int32)
            csum = jnp.cumsum(onehot, axis=0)
            counts = csum[-1]
            rank = jnp.take_along_axis(csum, flat_e[:, None], axis=1)[:, 0] - 1
            padded = ((counts + TMF - 1) // TMF) * TMF
            pend = jnp.cumsum(padded)
            pstart = pend - padded
            dest = pstart[flat_e] + rank
            n_pad = 2 * ROWS + NEXP * TMF
            src_tok = jnp.zeros((n_pad,), jnp.int32).at[dest].set(jnp.arange(2 * ROWS, dtype=jnp.int32) // 2)
            tile_e = jnp.minimum(jnp.searchsorted(pend, jnp.arange(n_pad // TMF) * TMF, side='right'),
                                 NEXP - 1).astype(jnp.int32)
            y_sorted = _ffn(tile_e, h2[src_tok], moe_w_gate[j].astype(BF16), moe_w_up[j].astype(BF16),
                            moe_w_down[j].astype(BF16))
            pos = dest.reshape(ROWS, 2)
            ya = y_sorted[pos[:, 0]]
            yb = y_sorted[pos[:, 1]]
    y_p, y_s = _stage_final(DEPTH - 1, x1, ya, yb, route, ada4, ln_g[DEPTH - 1, 1].reshape(1, D),
                            ln_b[DEPTH - 1, 1].reshape(1, D))
    stack = lambda lst, k: jnp.stack([t[k] for t in lst])
    return (y_p.reshape(NB_P, L_P, D), y_s.reshape(NB_S, L_S, D),
            stack(new_h, 0), stack(new_s5, 0), stack(new_m, 0), stack(new_c, 0),
            stack(new_h, 1), stack(new_s5, 1), stack(new_m, 1), stack(new_c, 1))
```

```python
import functools
import math

import jax
import jax.numpy as jnp
from jax import lax
from jax.experimental import pallas as pl
from jax.experimental.pallas import tpu as pltpu

F32 = jnp.float32
BF16 = jnp.bfloat16

D = 1024
NB_P, L_P = 8, 2048
NB_S, L_S = 128, 8
ROWS_P = NB_P * L_P
ROWS_S = NB_S * L_S
ROWS = ROWS_P + ROWS_S
DEPTH = 2
HW = 384
S5W = 256
SSW = 384
NH = 6
HD = 64
S5N = 1024
CONVC = 640
N_IN = 2822
N_IN_PAD = 2944
P_HG = 1536
P_SS = 1152
DFF = 2816
NEXP = 8
ALPHA = (2 * DEPTH) ** 0.25
LN_EPS = 1e-5
RMS_EPS = 1e-6

TM = 256
NPT = ROWS_P // TM
NST = ROWS_S // TM
NT = NPT + NST
SEQ_PER_TILE = TM // L_S
TILES_PER_SEQ = L_P // TM

CB = 128
NPC = ROWS_P // CB
NSC = ROWS_S // CB
CH_PER_SEQ = L_P // CB
SEQ_PER_CB = CB // L_S

TMF = 512
TF = 1408
VMEM_LIMIT = 56 * 1024 * 1024


def _cparams(sem):
    return pltpu.CompilerParams(dimension_semantics=sem, vmem_limit_bytes=VMEM_LIMIT)


def _bdot(a, b):
    return jnp.dot(a.astype(BF16), b.astype(BF16), preferred_element_type=F32)


def _bdot_nt(a, b):
    return lax.dot_general(a.astype(BF16), b.astype(BF16), (((1,), (1,)), ((), ())),
                           preferred_element_type=F32)


def _split_dot(x, e, passes):
    acc = None
    r = x
    for _ in range(passes):
        hi = r.astype(BF16)
        d = jnp.dot(hi, e, preferred_element_type=F32)
        acc = d if acc is None else acc + d
        r = r - hi.astype(F32)
    return acc


def _split_dot_l(e, x, passes):
    acc = None
    r = x
    for _ in range(passes):
        hi = r.astype(BF16)
        d = jnp.dot(e, hi, preferred_element_type=F32)
        acc = d if acc is None else acc + d
        r = r - hi.astype(F32)
    return acc


def _silu(x):
    return x * jax.nn.sigmoid(x)


def _layer_norm(x, g, b):
    mu = jnp.mean(x, -1, keepdims=True)
    xc = x - mu
    var = jnp.mean(xc * xc, -1, keepdims=True)
    return xc * lax.rsqrt(var + LN_EPS) * g + b


def _rowmod(i, p_ref, s_ref):
    s = jnp.broadcast_to(s_ref[...], (SEQ_PER_TILE, L_S, D)).reshape(TM, D)
    return jnp.where(i < NPT, p_ref[0], s)


ADA_TN = 1536


def _ada_body(c_ref, w_ref, b_ref, o_ref):
    o_ref[...] = _bdot(_silu(c_ref[...]), w_ref[...]) + b_ref[...]


def _ada(c_all, w_ada, b_ada):
    nc = c_all.shape[0]
    return pl.pallas_call(
        _ada_body,
        grid=(DEPTH, 6 * D // ADA_TN),
        in_specs=[
            pl.BlockSpec((nc, D), lambda l, j: (0, 0)),
            pl.BlockSpec((None, D, ADA_TN), lambda l, j: (l, 0, j)),
            pl.BlockSpec((None, 1, ADA_TN), lambda l, j: (l, 0, j)),
        ],
        out_specs=pl.BlockSpec((None, nc, ADA_TN), lambda l, j: (l, 0, j)),
        out_shape=jax.ShapeDtypeStruct((DEPTH, nc, 6 * D), F32),
        compiler_params=_cparams(("parallel", "parallel")),
        name="ada",
    )(c_all, w_ada, b_ada.reshape(DEPTH, 1, 6 * D))


def _mod_specs(layer, k):
    ps = pl.BlockSpec((None, 1, 1, D),
                      lambda i: (layer, NB_S + jnp.minimum(i // TILES_PER_SEQ, NB_P - 1), 0, k))
    ss = pl.BlockSpec((None, SEQ_PER_TILE, 1, D),
                      lambda i: (layer, jnp.clip(i - NPT, 0, NST - 1), 0, k))
    return [ps, ss]


def _row_spec(width):
    return pl.BlockSpec((TM, width), lambda i: (i, 0))


def _const_spec(shape):
    nd = len(shape)
    return pl.BlockSpec(shape, lambda *_: (0,) * nd)


def _proj_out(x, i, scp, scs, shp, shs, w_ref, ph_ref, ps_ref, pm_ref):
    h = x * (1.0 + _rowmod(i, scp, scs)) + _rowmod(i, shp, shs)
    proj = jnp.dot(h.astype(BF16), w_ref[...], preferred_element_type=F32)
    ph_ref[...] = proj[:, 0:P_HG]
    ps_ref[...] = proj[:, P_HG:P_HG + S5W]
    pm_ref[...] = proj[:, P_HG + S5W:N_IN_PAD]


def _a0_body(xp_ref, xs_ref, scp, scs, shp, shs, w_ref, x_ref, ph_ref, ps_ref, pm_ref):
    i = pl.program_id(0)
    x = jnp.where(i < NPT, xp_ref[...], xs_ref[...])
    x_ref[...] = x
    _proj_out(x, i, scp, scs, shp, shs, w_ref, ph_ref, ps_ref, pm_ref)


def _a1_body(x1_ref, f_ref, gp, gs, lng_ref, lnb_ref, scp, scs, shp, shs, w_ref,
             x_ref, ph_ref, ps_ref, pm_ref):
    i = pl.program_id(0)
    x = _layer_norm(ALPHA * x1_ref[...] + _rowmod(i, gp, gs) * f_ref[...], lng_ref[...], lnb_ref[...])
    x_ref[...] = x
    _proj_out(x, i, scp, scs, shp, shs, w_ref, ph_ref, ps_ref, pm_ref)


def _a_out():
    specs = [_row_spec(D), _row_spec(P_HG), _row_spec(S5W), _row_spec(P_SS)]
    shapes = [jax.ShapeDtypeStruct((ROWS, w), F32) for w in (D, P_HG, S5W, P_SS)]
    return specs, shapes


def _stage_a0(xp, xs, ada4, w_in_b):
    out_specs, out_shape = _a_out()
    return pl.pallas_call(
        _a0_body,
        grid=(NT,),
        in_specs=[
            pl.BlockSpec((TM, D), lambda i: (jnp.minimum(i, NPT - 1), 0)),
            pl.BlockSpec((TM, D), lambda i: (jnp.clip(i - NPT, 0, NST - 1), 0)),
            *_mod_specs(0, 1), *_mod_specs(0, 0),
            _const_spec((D, N_IN_PAD)),
        ],
        out_specs=out_specs, out_shape=out_shape,
        compiler_params=_cparams(("parallel",)),
        name="stage_a0",
    )(xp, xs, ada4, ada4, ada4, ada4, w_in_b)


def _stage_a1(layer, x1, f, ada4, ln_g, ln_b, w_in_b):
    out_specs, out_shape = _a_out()
    return pl.pallas_call(
        _a1_body,
        grid=(NT,),
        in_specs=[
            _row_spec(D), _row_spec(D),
            *_mod_specs(layer - 1, 5),
            _const_spec((1, D)), _const_spec((1, D)),
            *_mod_specs(layer, 1), *_mod_specs(layer, 0),
            _const_spec((D, N_IN_PAD)),
        ],
        out_specs=out_specs, out_shape=out_shape,
        compiler_params=_cparams(("parallel",)),
        name="stage_a1",
    )(x1, f, ada4, ada4, ln_g, ln_b, ada4, ada4, ada4, ada4, w_in_b)


def _mixer_blk(width):
    return pl.BlockSpec((CB, width), lambda i: (i, 0))


def _pstate_spec(shape):
    nd = len(shape)
    return pl.BlockSpec((1,) + shape, lambda i: (jnp.minimum(i // CH_PER_SEQ, NB_P - 1),) + (0,) * nd)


def _sstate_spec(shape):
    nd = len(shape)
    return pl.BlockSpec((SEQ_PER_CB,) + shape, lambda i: (jnp.clip(i - NPC, 0, NSC - 1),) + (0,) * nd)


def _seq_last_rows(x):
    w = x.shape[-1]
    x3 = x.reshape(SEQ_PER_CB, L_S, w)
    return jnp.broadcast_to(x3[:, L_S - 1:L_S, :], (SEQ_PER_CB, L_S, w)).reshape(CB, w)


def _concat_heads(parts):
    return jnp.concatenate(parts, axis=1)


def _stack_select(shape, row_div, lane_div):
    r = lax.broadcasted_iota(jnp.int32, shape, 0) // row_div
    c = lax.broadcasted_iota(jnp.int32, shape, 1) // lane_div
    return r == c


def _seq_expand_lanes(qh):
    q2 = jnp.concatenate([qh, qh], axis=1)
    q16 = jnp.concatenate([q2] * (SEQ_PER_CB // 2), axis=1)
    return jnp.where(_stack_select((CB, SEQ_PER_CB * HD), L_S, HD), q16, 0.0)


def _seq_expand_rows(xt):
    t = jnp.broadcast_to(xt[None], (SEQ_PER_CB, HD, CB)).reshape(SEQ_PER_CB * HD, CB)
    return jnp.where(_stack_select((SEQ_PER_CB * HD, CB), HD, L_S), t, 0.0)


def _fold_seq_lanes(full):
    acc = full[:, 0:128]
    for j in range(1, SEQ_PER_CB * HD // 128):
        acc = acc + full[:, 128 * j:128 * (j + 1)]
    return acc[:, 0:HD] + acc[:, HD:2 * HD]


def _hgrn_block(prompt, p_ref, lb_ref, hg_ref, e64_ref, tril_ref, oh_ref, st_scr, i,
                stp_out=None, sts_in=None, sts_out=None):
    lb = lb_ref[...]
    qr = p_ref[:, 0:HW]
    fr = p_ref[:, HW:2 * HW]
    v = p_ref[:, 2 * HW:3 * HW]
    gr = p_ref[:, 3 * HW:4 * HW]
    ls = jnp.minimum(fr, 0.0) - jnp.log1p(jnp.exp(-jnp.abs(fr)))
    a = jnp.log(lb)
    bb = jnp.log1p(-lb) + ls
    lf = jnp.maximum(a, bb) + jnp.log1p(jnp.exp(-jnp.abs(a - bb)))
    kk = (1.0 - lb) * jax.nn.sigmoid(-fr)
    q = _silu(qr)
    b = _split_dot_l(tril_ref[...], lf, 3)

    nsub = CB // 8
    b3 = b.reshape(nsub, 8, HW)
    q3 = q.reshape(nsub, 8, HW)
    k3 = kk.reshape(nsub, 8, HW)
    v3 = v.reshape(nsub, 8, HW)
    r3 = lax.broadcasted_iota(jnp.int32, (nsub, 8, HW), 1)
    e64 = e64_ref[...]
    o = jnp.zeros((CB, HW), F32)
    for s in range(8):
        dlt = jnp.minimum(b3 - b3[:, s:s + 1, :], 0.0)
        w = jnp.where(r3 >= s, jnp.exp(dlt), 0.0) * q3 * k3[:, s:s + 1, :]
        hsum = jnp.dot(w.reshape(CB, HW).astype(BF16), e64, preferred_element_type=F32)
        o = o + hsum * jnp.broadcast_to(v3[:, s:s + 1, :], (nsub, 8, HW)).reshape(CB, HW)

    qt = q * jnp.exp(b)
    if prompt:
        ti = lax.broadcasted_iota(jnp.int32, (CB, CB), 0)
        si = lax.broadcasted_iota(jnp.int32, (CB, CB), 1)
        levels = []
        m = 8
        while m < CB:
            nb = CB // (2 * m)
            b4 = b.reshape(nb, 2 * m, HW)
            bmid = b4[:, m - 1:m, :]
            pos = lax.broadcasted_iota(jnp.int32, (nb, 2 * m, HW), 1)
            qq = jnp.where(pos >= m, q.reshape(nb, 2 * m, HW) * jnp.exp(jnp.minimum(b4 - bmid, 0.0)), 0.0)
            kq = jnp.where(pos < m, kk.reshape(nb, 2 * m, HW) * jnp.exp(jnp.minimum(bmid - b4, 0.0)), 0.0)
            same = (ti // (2 * m)) == (si // (2 * m))
            levels.append((qq.reshape(CB, HW), kq.reshape(CB, HW), same))
            m *= 2
        blast = b[CB - 1:CB, :]
        kd = kk * jnp.exp(blast - b)
        vt = v.T
        parts = []
        for h in range(NH):
            sl = slice(HD * h, HD * (h + 1))
            sc = None
            for qq, kq, same in levels:
                t = jnp.where(same, _bdot_nt(qq[:, sl], kq[:, sl]), 0.0)
                sc = t if sc is None else sc + t
            st = st_scr[h]
            oh = o[:, sl] + _bdot(sc, v[:, sl]) + _bdot_nt(qt[:, sl], st)
            st_new = st * jnp.exp(blast[:, sl]) + _bdot(vt[sl, :], kd[:, sl])
            st_scr[h] = st_new
            parts.append(oh)

        @pl.when(i % CH_PER_SEQ == CH_PER_SEQ - 1)
        def _():
            stp_out[0] = st_scr[...]
    else:
        blast = _seq_last_rows(b)
        kd = kk * jnp.exp(blast - b)
        dec = jnp.exp(blast)
        vt = v.T
        parts = []
        for h in range(NH):
            sl = slice(HD * h, HD * (h + 1))
            sts = sts_in[:, h].reshape(SEQ_PER_CB * HD, HD)
            full = _bdot_nt(qt[:, sl], sts)
            sel = jnp.where(_stack_select((CB, SEQ_PER_CB * HD), L_S, HD), full, 0.0)
            parts.append(o[:, sl] + _fold_seq_lanes(sel))
            dec3 = dec[:, sl].reshape(SEQ_PER_CB, L_S, HD)[:, L_S - 1:L_S, :]
            dec_rows = jnp.broadcast_to(dec3, (SEQ_PER_CB, HD, HD)).reshape(SEQ_PER_CB * HD, HD)
            upd = _bdot(_seq_expand_rows(vt[sl, :]), kd[:, sl])
            sts_out[:, h] = (sts * dec_rows + upd).reshape(SEQ_PER_CB, HD, HD)
    oall = _concat_heads(parts)
    ms = _split_dot(oall * oall, e64, 2) * (1.0 / HD)
    oh_ref[...] = oall * lax.rsqrt(ms + RMS_EPS) * hg_ref[...] * _silu(gr)


def _hgrn_body(p_ref, lb_ref, hg_ref, e64_ref, trilp_ref, trils_ref, sts_in,
               oh_ref, stp_out, sts_out, st_scr):
    i = pl.program_id(0)

    @pl.when((i < NPC) & (i % CH_PER_SEQ == 0))
    def _():
        st_scr[...] = jnp.zeros_like(st_scr)

    @pl.when(i < NPC)
    def _():
        _hgrn_block(True, p_ref, lb_ref, hg_ref, e64_ref, trilp_ref, oh_ref, st_scr, i, stp_out=stp_out)

    @pl.when(i >= NPC)
    def _():
        _hgrn_block(False, p_ref, lb_ref, hg_ref, e64_ref, trils_ref, oh_ref, st_scr, i,
                    sts_in=sts_in, sts_out=sts_out)


def _hgrn(ph, lb, hg, consts, st_t):
    return pl.pallas_call(
        _hgrn_body,
        grid=(NPC + NSC,),
        in_specs=[
            _mixer_blk(P_HG), _const_spec((1, HW)), _const_spec((1, HW)),
            _const_spec((HW, HW)), _const_spec((CB, CB)), _const_spec((CB, CB)),
            _sstate_spec((NH, HD, HD)),
        ],
        out_specs=[_mixer_blk(HW), _pstate_spec((NH, HD, HD)), _sstate_spec((NH, HD, HD))],
        out_shape=[
            jax.ShapeDtypeStruct((ROWS, HW), F32),
            jax.ShapeDtypeStruct((NB_P, NH, HD, HD), F32),
            jax.ShapeDtypeStruct((NB_S, NH, HD, HD), F32),
        ],
        scratch_shapes=[pltpu.VMEM((NH, HD, HD), F32)],
        compiler_params=_cparams(("arbitrary",)),
        name="hgrn",
    )(ph, lb, hg, consts["e64"], consts["trilp"], consts["trils"], st_t)


def _cmul_add(hr, hi, lr, li, sr, si):
    return hr + lr * sr - li * si, hi + lr * si + li * sr


def _s5_block(prompt, p_ref, tab_ref, bblk_ref, ccat_ref, d_ref, wglu_ref, bglu_ref, os_ref,
              carry_scr, h_scr, i, s5p_out=None, s5s_in=None, s5s_out=None):
    u = p_ref[...]
    u_hi = u.astype(BF16)
    u_lo = (u - u_hi.astype(F32)).astype(BF16)
    bu = (jnp.dot(u_hi, bblk_ref[0], preferred_element_type=F32)
          + jnp.dot(u_lo, bblk_ref[0], preferred_element_type=F32)
          + jnp.dot(u_hi, bblk_ref[1], preferred_element_type=F32))
    hr = bu[:, 0:S5N]
    hi = bu[:, S5N:2 * S5N]
    nsub = CB // 8
    for idx, dsh in enumerate((1, 2, 4)):
        sr = pltpu.roll(hr, dsh, 0).reshape(nsub, 8, S5N)
        si = pltpu.roll(hi, dsh, 0).reshape(nsub, 8, S5N)
        lr = tab_ref[idx, :, 0:S5N][None]
        li = tab_ref[idx, :, S5N:2 * S5N][None]
        nr, ni = _cmul_add(hr.reshape(nsub, 8, S5N), hi.reshape(nsub, 8, S5N), lr, li, sr, si)
        hr = nr.reshape(CB, S5N)
        hi = ni.reshape(CB, S5N)
    tcr = tab_ref[3, :, 0:S5N]
    tci = tab_ref[3, :, S5N:2 * S5N]
    if prompt:
        cr = carry_scr[:, 0:S5N]
        ci = carry_scr[:, S5N:2 * S5N]
        for j in range(nsub):
            tr, tim = _cmul_add(hr[8 * j:8 * j + 8], hi[8 * j:8 * j + 8], tcr, tci, cr, ci)
            h_scr[8 * j:8 * j + 8, 0:S5N] = tr
            h_scr[8 * j:8 * j + 8, S5N:2 * S5N] = tim
            cr = tr[7:8]
            ci = tim[7:8]
        carry_scr[:, 0:S5N] = cr
        carry_scr[:, S5N:2 * S5N] = ci

        @pl.when(i % CH_PER_SEQ == CH_PER_SEQ - 1)
        def _():
            s5p_out[0] = carry_scr[...]
    else:
        cr = s5s_in[:, :, 0:S5N]
        ci = s5s_in[:, :, S5N:2 * S5N]
        tr, tim = _cmul_add(hr.reshape(nsub, 8, S5N), hi.reshape(nsub, 8, S5N), tcr[None], tci[None], cr, ci)
        h_scr[:, 0:S5N] = tr.reshape(CB, S5N)
        h_scr[:, S5N:2 * S5N] = tim.reshape(CB, S5N)
        sb = lax.broadcasted_iota(jnp.int32, (SEQ_PER_CB, CB), 0)
        st = lax.broadcasted_iota(jnp.int32, (SEQ_PER_CB, CB), 1)
        sel = (st == L_S * sb + (L_S - 1)).astype(BF16)
        s5s_out[...] = _split_dot_l(sel, h_scr[...], 3)
    y = _bdot(h_scr[...], ccat_ref[...]) + d_ref[...] * u
    c0 = math.sqrt(2.0 / math.pi)
    y = y * (0.5 * (1.0 + jnp.tanh(c0 * (y + 0.044715 * (y * y * y)))))
    os_ref[...] = y * jax.nn.sigmoid(_bdot(y, wglu_ref[...]) + bglu_ref[...])


def _s5_body(p_ref, tab_ref, bblk_ref, ccat_ref, d_ref, wglu_ref, bglu_ref, s5s_in,
             os_ref, s5p_out, s5s_out, carry_scr, h_scr):
    i = pl.program_id(0)
    args = (p_ref, tab_ref, bblk_ref, ccat_ref, d_ref, wglu_ref, bglu_ref, os_ref, carry_scr, h_scr, i)

    @pl.when((i < NPC) & (i % CH_PER_SEQ == 0))
    def _():
        carry_scr[...] = jnp.zeros_like(carry_scr)

    @pl.when(i < NPC)
    def _():
        _s5_block(True, *args, s5p_out=s5p_out)

    @pl.when(i >= NPC)
    def _():
        _s5_block(False, *args, s5s_in=s5s_in, s5s_out=s5s_out)


def _s5(ps, prm, st):
    return pl.pallas_call(
        _s5_body,
        grid=(NPC + NSC,),
        in_specs=[
            _mixer_blk(S5W), _const_spec((4, 8, 2 * S5N)), _const_spec((2, S5W, 2 * S5N)),
            _const_spec((2 * S5N, S5W)), _const_spec((1, S5W)), _const_spec((S5W, S5W)),
            _const_spec((1, S5W)), _sstate_spec((1, 2 * S5N)),
        ],
        out_specs=[_mixer_blk(S5W), _pstate_spec((1, 2 * S5N)), _sstate_spec((2 * S5N,))],
        out_shape=[
            jax.ShapeDtypeStruct((ROWS, S5W), F32),
            jax.ShapeDtypeStruct((NB_P, 1, 2 * S5N), F32),
            jax.ShapeDtypeStruct((NB_S, 2 * S5N), F32),
        ],
        scratch_shapes=[pltpu.VMEM((1, 2 * S5N), F32), pltpu.VMEM((CB, 2 * S5N), F32)],
        compiler_params=_cparams(("arbitrary",)),
        name="s5",
    )(ps, prm["tab"], prm["bblk"], prm["ccat"], prm["d"], prm["wglu"], prm["bglu"], st)


def _ssd_block(prompt, p_ref, cw_ref, cb_ref, dtb_ref, aneg_ref, dx_ref, sg_ref, e64x_ref, e128x_ref,
               g192_ref, tril_ref, om_ref, cbuf, st_scr, i, ssp_out=None, hist_ref=None,
               sss_in=None, sss_out=None):
    z = p_ref[:, 0:SSW]
    xbc = p_ref[:, SSW:SSW + CONVC]
    dtr = p_ref[:, SSW + CONVC:P_SS]
    acc = cb_ref[...] + cw_ref[3:4, :] * xbc
    if prompt:
        cbuf[8:8 + CB, :] = xbc
        for k in (1, 2, 3):
            acc = acc + cw_ref[3 - k:4 - k, :] * cbuf[8 - k:8 - k + CB, :]
        cbuf[0:8, :] = cbuf[CB:CB + 8, :]
    else:
        tl = lax.broadcasted_iota(jnp.int32, (CB, CONVC), 0) % L_S
        hist = hist_ref[...]
        for k in (1, 2, 3):
            hk = hist if k == 3 else pltpu.roll(hist, CB - (3 - k), 0)
            sh = jnp.where(tl >= k, pltpu.roll(xbc, k, 0), hk)
            acc = acc + cw_ref[3 - k:4 - k, :] * sh
    xc = _silu(acc)
    xs = xc[:, 0:SSW]
    bm = xc[:, SSW:SSW + 2 * HD]
    cm = xc[:, SSW + 2 * HD:CONVC]
    xdt = dtr + dtb_ref[...]
    dt = jnp.maximum(xdt, 0.0) + jnp.log1p(jnp.exp(-jnp.abs(xdt)))
    la = dt * aneg_ref[...]
    tril = tril_ref[...]
    b6 = _split_dot_l(tril, la, 3)
    bx = _split_dot(b6, e64x_ref[...], 3)
    bxw = _split_dot(b6, e128x_ref[...], 3)
    dtx = _split_dot(dt, e64x_ref[...], 2)
    bm_rep = _concat_heads([bm[:, 0:HD]] * 3 + [bm[:, HD:2 * HD]] * 3)
    cm_rep = _concat_heads([cm[:, 0:HD]] * 3 + [cm[:, HD:2 * HD]] * 3)
    kh = bm_rep * dtx
    qt = cm_rep * jnp.exp(bx)
    blast = bx[CB - 1:CB, :] if prompt else _seq_last_rows(bx)
    kd = kh * jnp.exp(blast - bx)
    kdt = kd.T
    mask = tril > 0
    parts = []
    for h in range(NH):
        sl = slice(HD * h, HD * (h + 1))
        bcol = bxw[:, 128 * h:128 * (h + 1)]
        decay = jnp.exp(jnp.where(mask, bcol - bcol.T, -1e30))
        sc = _bdot_nt(cm_rep[:, sl], kh[:, sl]) * decay
        oh = _bdot(sc, xs[:, sl])
        if prompt:
            st = st_scr[h]
            oh = oh + _bdot(qt[:, sl], st)
            st_scr[h] = st * jnp.exp(blast[:, sl]) + _bdot(kdt[sl, :], xs[:, sl])
        else:
            sts = sss_in[:, h].reshape(SEQ_PER_CB * HD, HD)
            oh = oh + _bdot(_seq_expand_lanes(qt[:, sl]), sts)
            dec3 = jnp.exp(blast[:, sl]).reshape(SEQ_PER_CB, L_S, HD)[:, L_S - 1:L_S, :]
            dec_rows = jnp.broadcast_to(dec3, (SEQ_PER_CB, HD, HD)).reshape(SEQ_PER_CB * HD, HD)
            upd = _bdot(_seq_expand_rows(kdt[sl, :]), xs[:, sl])
            sss_out[:, h] = (sts * dec_rows + upd).reshape(SEQ_PER_CB, HD, HD)
        parts.append(oh)
    if prompt:
        @pl.when(i % CH_PER_SEQ == CH_PER_SEQ - 1)
        def _():
            ssp_out[0] = st_scr[...]
    y = (_concat_heads(parts) + dx_ref[...] * xs) * _silu(z)
    ms = _split_dot(y * y, g192_ref[...], 2) * (1.0 / (SSW // 2))
    om_ref[...] = y * lax.rsqrt(ms + RMS_EPS) * sg_ref[...]


def _ssd_body(p_ref, cw_ref, cb_ref, dtb_ref, aneg_ref, dx_ref, sg_ref, e64x_ref, e128x_ref, g192_ref,
              trilp_ref, trils_ref, hist_ref, sss_in, om_ref, ssp_out, sss_out, cbuf, st_scr):
    i = pl.program_id(0)
    common = (p_ref, cw_ref, cb_ref, dtb_ref, aneg_ref, dx_ref, sg_ref, e64x_ref, e128x_ref, g192_ref)

    @pl.when((i < NPC) & (i % CH_PER_SEQ == 0))
    def _():
        st_scr[...] = jnp.zeros_like(st_scr)
        cbuf[...] = jnp.zeros_like(cbuf)

    @pl.when(i < NPC)
    def _():
        _ssd_block(True, *common, trilp_ref, om_ref, cbuf, st_scr, i, ssp_out=ssp_out)

    @pl.when(i >= NPC)
    def _():
        _ssd_block(False, *common, trils_ref, om_ref, cbuf, st_scr, i, hist_ref=hist_ref,
                   sss_in=sss_in, sss_out=sss_out)


def _ssd(pm, prm, consts, hist, st):
    return pl.pallas_call(
        _ssd_body,
        grid=(NPC + NSC,),
        in_specs=[
            _mixer_blk(P_SS), _const_spec((4, CONVC)), _const_spec((1, CONVC)),
            _const_spec((1, 128)), _const_spec((1, 128)), _const_spec((1, SSW)), _const_spec((1, SSW)),
            _const_spec((128, SSW)), _const_spec((128, 2 * SSW)), _const_spec((SSW, SSW)),
            _const_spec((CB, CB)), _const_spec((CB, CB)),
            pl.BlockSpec((CB, CONVC), lambda i: (jnp.clip(i - NPC, 0, NSC - 1), 0)),
            _sstate_spec((NH, HD, HD)),
        ],
        out_specs=[_mixer_blk(SSW), _pstate_spec((NH, HD, HD)), _sstate_spec((NH, HD, HD))],
        out_shape=[
            jax.ShapeDtypeStruct((ROWS, SSW), F32),
            jax.ShapeDtypeStruct((NB_P, NH, HD, HD), F32),
            jax.ShapeDtypeStruct((NB_S, NH, HD, HD), F32),
        ],
        scratch_shapes=[pltpu.VMEM((CB + 8, CONVC), F32), pltpu.VMEM((NH, HD, HD), F32)],
        compiler_params=_cparams(("arbitrary",)),
        name="ssd",
    )(pm, prm["cw"], prm["cb"], prm["dtb"], prm["aneg"], prm["dx"], prm["sg"],
      consts["e64x"], consts["e128x"], consts["g192"], consts["trilp"], consts["trils"], hist, st)


def _o_core(i, oh_ref, os_ref, om_ref, x_ref, wo_ref, gp, gs, lng_ref, lnb_ref, scp, scs, shp, shs):
    mix = (_bdot(oh_ref[...], wo_ref[0:HW, :]) + _bdot(os_ref[...], wo_ref[HW:HW + S5W, :])
           + _bdot(om_ref[...], wo_ref[HW + S5W:D, :]))
    x1 = _layer_norm(ALPHA * x_ref[...] + _rowmod(i, gp, gs) * mix, lng_ref[...], lnb_ref[...])
    h2 = x1 * (1.0 + _rowmod(i, scp, scs)) + _rowmod(i, shp, shs)
    return x1, h2


def _o_body(oh_ref, os_ref, om_ref, x_ref, wo_ref, gp, gs, lng_ref, lnb_ref, scp, scs, shp, shs,
            x1_ref, h2_ref):
    i = pl.program_id(0)
    x1, h2 = _o_core(i, oh_ref, os_ref, om_ref, x_ref, wo_ref, gp, gs, lng_ref, lnb_ref, scp, scs, shp, shs)
    x1_ref[...] = x1
    h2_ref[...] = h2.astype(BF16)


def _o_router_body(oh_ref, os_ref, om_ref, x_ref, wo_ref, gp, gs, lng_ref, lnb_ref, scp, scs, shp, shs,
                   wr_ref, br_ref, x1_ref, h2_ref, route_ref):
    i = pl.program_id(0)
    x1, h2 = _o_core(i, oh_ref, os_ref, om_ref, x_ref, wo_ref, gp, gs, lng_ref, lnb_ref, scp, scs, shp, shs)
    x1_ref[...] = x1
    h2_ref[...] = h2
    h_hi = h2.astype(BF16)
    h_lo = (h2 - h_hi.astype(F32)).astype(BF16)
    logits = (jnp.dot(h_hi, wr_ref[0], preferred_element_type=F32)
              + jnp.dot(h_lo, wr_ref[0], preferred_element_type=F32)
              + jnp.dot(h_hi, wr_ref[1], preferred_element_type=F32)) + br_ref[...]
    lane = lax.broadcasted_iota(jnp.int32, (TM, 128), 1).astype(F32)
    neg = -jnp.inf
    lg = jnp.where(lane < NEXP, logits, neg)
    m1 = jnp.max(lg, axis=-1, keepdims=True)
    i1 = jnp.min(jnp.where(lg == m1, lane, 128.0), axis=-1, keepdims=True)
    lg2 = jnp.where(lane == i1, neg, lg)
    m2 = jnp.max(lg2, axis=-1, keepdims=True)
    i2 = jnp.min(jnp.where(lg2 == m2, lane, 128.0), axis=-1, keepdims=True)
    e2 = jnp.exp(m2 - m1)
    den = 1.0 + e2
    route_ref[...] = jnp.where(lane == 0.0, i1, jnp.where(lane == 1.0, i2,
                               jnp.where(lane == 2.0, 1.0 / den, jnp.where(lane == 3.0, e2 / den, 0.0))))


def _stage_o(layer, oh, os_, om, x, wo_b, ada4, ln_g, ln_b, router=None):
    in_specs = [
        _row_spec(HW), _row_spec(S5W), _row_spec(SSW), _row_spec(D), _const_spec((D, D)),
        *_mod_specs(layer, 2), _const_spec((1, D)), _const_spec((1, D)),
        *_mod_specs(layer, 4), *_mod_specs(layer, 3),
    ]
    args = [oh, os_, om, x, wo_b, ada4, ada4, ln_g, ln_b, ada4, ada4, ada4, ada4]
    out_specs = [_row_spec(D), _row_spec(D)]
    out_shape = [jax.ShapeDtypeStruct((ROWS, D), F32), jax.ShapeDtypeStruct((ROWS, D), BF16)]
    body = _o_body
    if router is not None:
        in_specs += [_const_spec((2, D, 128)), _const_spec((1, 128))]
        args += list(router)
        out_specs.append(_row_spec(128))
        out_shape[1] = jax.ShapeDtypeStruct((ROWS, D), F32)
        out_shape.append(jax.ShapeDtypeStruct((ROWS, 128), F32))
        body = _o_router_body
    return pl.pallas_call(
        body, grid=(NT,), in_specs=in_specs, out_specs=out_specs, out_shape=out_shape,
        compiler_params=_cparams(("parallel",)),
        name="stage_o_router" if router is not None else "stage_o",
    )(*args)


def _ffn_body(te_ref, h_ref, wg_ref, wu_ref, wd_ref, o_ref, acc_ref):
    j = pl.program_id(1)

    @pl.when(j == 0)
    def _():
        acc_ref[...] = jnp.zeros_like(acc_ref)

    h = h_ref[...].astype(BF16)
    g = jnp.dot(h, wg_ref[...], preferred_element_type=F32)
    u = jnp.dot(h, wu_ref[...], preferred_element_type=F32)
    act = (_silu(g) * u).astype(BF16)
    acc_ref[...] += jnp.dot(act, wd_ref[...], preferred_element_type=F32)

    @pl.when(j == pl.num_programs(1) - 1)
    def _():
        o_ref[...] = acc_ref[...]


def _ffn(tile_expert, h, wg, wu, wd):
    rows = h.shape[0]
    grid_spec = pltpu.PrefetchScalarGridSpec(
        num_scalar_prefetch=1,
        grid=(rows // TMF, DFF // TF),
        in_specs=[
            pl.BlockSpec((TMF, D), lambda i, j, te: (i, 0)),
            pl.BlockSpec((None, D, TF), lambda i, j, te: (te[i], 0, j)),
            pl.BlockSpec((None, D, TF), lambda i, j, te: (te[i], 0, j)),
            pl.BlockSpec((None, TF, D), lambda i, j, te: (te[i], j, 0)),
        ],
        out_specs=pl.BlockSpec((TMF, D), lambda i, j, te: (i, 0)),
        scratch_shapes=[pltpu.VMEM((TMF, D), F32)],
    )
    return pl.pallas_call(
        _ffn_body, grid_spec=grid_spec,
        out_shape=jax.ShapeDtypeStruct((rows, D), F32),
        compiler_params=_cparams(("parallel", "arbitrary")),
        name="ffn",
    )(tile_expert, h, wg, wu, wd)


def _final_body(x1_ref, ya_ref, yb_ref, route_ref, gp, gs, lng_ref, lnb_ref, yp_ref, ys_ref):
    i = pl.program_id(0)
    f = route_ref[:, 2:3] * ya_ref[...] + route_ref[:, 3:4] * yb_ref[...]
    y = _layer_norm(ALPHA * x1_ref[...] + _rowmod(i, gp, gs) * f, lng_ref[...], lnb_ref[...])

    @pl.when(i < NPT)
    def _():
        yp_ref[...] = y

    @pl.when(i >= NPT)
    def _():
        ys_ref[...] = y


def _stage_final(layer, x1, ya, yb, route, ada4, ln_g, ln_b):
    return pl.pallas_call(
        _final_body,
        grid=(NT,),
        in_specs=[
            _row_spec(D), _row_spec(D), _row_spec(D), _row_spec(128),
            *_mod_specs(layer, 5), _const_spec((1, D)), _const_spec((1, D)),
        ],
        out_specs=[
            pl.BlockSpec((TM, D), lambda i: (jnp.minimum(i, NPT - 1), 0)),
            pl.BlockSpec((TM, D), lambda i: (jnp.clip(i - NPT, 0, NST - 1), 0)),
        ],
        out_shape=[jax.ShapeDtypeStruct((ROWS_P, D), F32), jax.ShapeDtypeStruct((ROWS_S, D), F32)],
        compiler_params=_cparams(("arbitrary",)),
        name="stage_final",
    )(x1, ya, yb, route, ada4, ada4, ln_g, ln_b)


def _block_ones(n, blk):
    r = jnp.arange(n) // blk
    return (r[:, None] == r[None, :]).astype(BF16)


def _consts():
    t = jnp.arange(CB)
    causal = t[:, None] >= t[None, :]
    same_seq = (t[:, None] // L_S) == (t[None, :] // L_S)
    h = jnp.arange(128)
    return {
        "e64": _block_ones(HW, HD),
        "g192": _block_ones(SSW, SSW // 2),
        "trilp": causal.astype(BF16),
        "trils": (causal & same_seq).astype(BF16),
        "e64x": (h[:, None] == (jnp.arange(SSW)[None, :] // HD)).astype(BF16),
        "e128x": (h[:, None] == (jnp.arange(2 * SSW)[None, :] // 128)).astype(BF16),
    }


def _s5_params(a_re, a_im, log_dt, b_re, b_im, c_re, c_im, d, w_glu, b_glu):
    dt = jnp.exp(log_dt)[:, None]
    mag = jnp.exp(a_re * dt)
    lam_re, lam_im = mag * jnp.cos(a_im * dt), mag * jnp.sin(a_im * dt)
    den = a_re * a_re + a_im * a_im
    nr, ni = lam_re - 1.0, lam_im
    zr = (nr * a_re + ni * a_im) / den
    zi = (ni * a_re - nr * a_im) / den
    bbar_re = zr[..., None] * b_re - zi[..., None] * b_im
    bbar_im = zr[..., None] * b_im + zi[..., None] * b_re
    eye = jnp.eye(16, dtype=F32)
    blk = lambda bb: jnp.einsum('gph,gk->ghkp', bb, eye).reshape(S5W, S5N)
    bblk = jnp.concatenate([blk(bbar_re), blk(bbar_im)], axis=1)
    bblk_hi = bblk.astype(BF16)
    bblk_lo = (bblk - bblk_hi.astype(F32)).astype(BF16)
    cblk = lambda cc: jnp.einsum('ghp,gk->gpkh', cc, eye).reshape(S5N, S5W)
    ccat = jnp.concatenate([cblk(c_re), -cblk(c_im)], axis=0)
    lr, li = lam_re.reshape(-1), lam_im.reshape(-1)
    pows = [(jnp.ones_like(lr), jnp.zeros_like(li))]
    for _ in range(8):
        pr, pi = pows[-1]
        pows.append((pr * lr - pi * li, pr * li + pi * lr))
    rows = jnp.arange(8)[:, None]
    tabs = []
    for dsh in (1, 2, 4):
        pr, pi = pows[dsh]
        tabs.append(jnp.where(rows >= dsh, jnp.concatenate([pr, pi])[None, :], 0.0))
    tabs.append(jnp.stack([jnp.concatenate(pows[r + 1]) for r in range(8)]))
    return {
        "tab": jnp.stack(tabs), "bblk": jnp.stack([bblk_hi, bblk_lo]), "ccat": ccat.astype(BF16),
        "d": d.reshape(1, S5W), "wglu": w_glu.astype(BF16), "bglu": b_glu.reshape(1, S5W),
    }


def _pad_lanes(v, n=128):
    return jnp.pad(v, (0, n - v.shape[0])).reshape(1, n)


def kernel(x_prompt, x_sample, c_prompt, c_sample, state_hgrn, state_s5, state_ssd, state_ssd_conv, w_ada, b_ada, ln_g, ln_b, w_in, w_out, hgrn_lb_logits, hgrn_norm_g, s5_a_re, s5_a_im, s5_log_dt, s5_b_re, s5_b_im, s5_c_re, s5_c_im, s5_d, s5_w_glu, s5_b_glu, ssd_conv_w, ssd_conv_b, ssd_dt_bias, ssd_a_log, ssd_d, ssd_norm_g, ffn_w_gate, ffn_w_up, ffn_w_down, moe_w_router, moe_b_router, moe_w_gate, moe_w_up, moe_w_down):
    consts = _consts()
    c_all = jnp.concatenate([c_sample, c_prompt], axis=0)
    ada4 = _ada(c_all, w_ada, b_ada).reshape(DEPTH, NB_S + NB_P, 1, 6 * D)

    lb_all = jnp.cumsum(jax.nn.softmax(hgrn_lb_logits, axis=0), axis=0)
    lb_all = lb_all - lb_all[0]

    xp = x_prompt.reshape(ROWS_P, D)
    xs = x_sample.reshape(ROWS_S, D)
    new_h, new_s5, new_m, new_c = [], [], [], []
    x1 = f = route = None
    for l in range(DEPTH):
        w_in_b = jnp.pad(w_in[l], ((0, 0), (0, N_IN_PAD - N_IN))).astype(BF16)
        if l == 0:
            x, ph, ps, pm = _stage_a0(xp, xs, ada4, w_in_b)
        else:
            x, ph, ps, pm = _stage_a1(l, x1, f, ada4, ln_g[l - 1, 1].reshape(1, D), ln_b[l - 1, 1].reshape(1, D),
                                      w_in_b)
        oh, hg_p, hg_s = _hgrn(ph, lb_all[l].reshape(1, HW), hgrn_norm_g[l].reshape(1, HW), consts,
                               jnp.swapaxes(state_hgrn[l], -1, -2))
        s5p = _s5_params(s5_a_re[l], s5_a_im[l], s5_log_dt[l], s5_b_re[l], s5_b_im[l], s5_c_re[l], s5_c_im[l],
                         s5_d[l], s5_w_glu[l], s5_b_glu[l])
        os_, s5_p, s5_s = _s5(ps, s5p, state_s5[l].reshape(NB_S, 1, 2 * S5N))
        ssd_prm = {
            "cw": ssd_conv_w[l], "cb": ssd_conv_b[l].reshape(1, CONVC),
            "dtb": _pad_lanes(ssd_dt_bias[l]), "aneg": _pad_lanes(-jnp.exp(ssd_a_log[l])),
            "dx": jnp.repeat(ssd_d[l], HD).reshape(1, SSW), "sg": ssd_norm_g[l].reshape(1, SSW),
        }
        hist = jnp.pad(state_ssd_conv[l], ((0, 0), (0, L_S - 3), (0, 0))).reshape(ROWS_S, CONVC)
        om, ss_p, ss_s = _ssd(pm, ssd_prm, consts, hist, state_ssd[l])
        xbc = pm[:, SSW:SSW + CONVC]
        conv_p = xbc[:ROWS_P].reshape(NB_P, L_P, CONVC)[:, L_P - 3:]
        conv_s = xbc[ROWS_P:].reshape(NB_S, L_S, CONVC)[:, L_S - 3:]
        new_h.append((jnp.swapaxes(hg_p, -1, -2), jnp.swapaxes(hg_s, -1, -2)))
        new_s5.append((s5_p.reshape(NB_P, 2, 16, 64), s5_s.reshape(NB_S, 2, 16, 64)))
        new_m.append((ss_p, ss_s))
        new_c.append((conv_p, conv_s))

        wo_b = w_out[l].astype(BF16)
        lg, lbb = ln_g[l, 0].reshape(1, D), ln_b[l, 0].reshape(1, D)
        j = l // 2
        if l % 2 == 0:
            x1, h2 = _stage_o(l, oh, os_, om, x, wo_b, ada4, lg, lbb)
            f = _ffn(jnp.zeros((ROWS // TMF,), jnp.int32), h2, ffn_w_gate[j:j + 1].astype(BF16),
                     ffn_w_up[j:j + 1].astype(BF16), ffn_w_down[j:j + 1].astype(BF16))
        else:
            wr = jnp.pad(moe_w_router[j], ((0, 0), (0, 128 - NEXP)))
            wr_hi = wr.astype(BF16)
            wr = jnp.stack([wr_hi, (wr - wr_hi.astype(F32)).astype(BF16)])
            br = _pad_lanes(moe_b_router[j])
            x1, h2, route = _stage_o(l, oh, os_, om, x, wo_b, ada4, lg, lbb, router=(wr, br))
            flat_e = route[:, 0:2].astype(jnp.int32).reshape(-1)
            onehot = (flat_e[:, None] == jnp.arange(NEXP)[None, :]).astype(jnp.int32)
            csum = jnp.cumsum(onehot, axis=0)
            counts = csum[-1]
            rank = jnp.take_along_axis(csum, flat_e[:, None], axis=1)[:, 0] - 1
            padded = ((counts + TMF - 1) // TMF) * TMF
            pend = jnp.cumsum(padded)
            pstart = pend - padded
            dest = pstart[flat_e] + rank
            n_pad = 2 * ROWS + NEXP * TMF
            src_tok = jnp.zeros((n_pad,), jnp.int32).at[dest].set(jnp.arange(2 * ROWS, dtype=jnp.int32) // 2)
            tile_start = jnp.arange(n_pad // TMF, dtype=jnp.int32) * TMF
            tile_e = jnp.minimum(jnp.sum((pend[None, :] <= tile_start[:, None]).astype(jnp.int32), axis=1),
                                 NEXP - 1)
            y_sorted = _ffn(tile_e, h2[src_tok], moe_w_gate[j].astype(BF16), moe_w_up[j].astype(BF16),
                            moe_w_down[j].astype(BF16))
            pos = dest.reshape(ROWS, 2)
            ya = y_sorted[pos[:, 0]]
            yb = y_sorted[pos[:, 1]]
    y_p, y_s = _stage_final(DEPTH - 1, x1, ya, yb, route, ada4, ln_g[DEPTH - 1, 1].reshape(1, D),
                            ln_b[DEPTH - 1, 1].reshape(1, D))
    stack = lambda lst, k: jnp.stack([t[k] for t in lst])
    return (y_p.reshape(NB_P, L_P, D), y_s.reshape(NB_S, L_S, D),
            stack(new_h, 0), stack(new_s5, 0), stack(new_m, 0), stack(new_c, 0),
            stack(new_h, 1), stack(new_s5, 1), stack(new_m, 1), stack(new_c, 1))
```

```python
import functools
import math

import jax
import jax.numpy as jnp
from jax import lax
from jax.experimental import pallas as pl
from jax.experimental.pallas import tpu as pltpu

F32 = jnp.float32
BF16 = jnp.bfloat16

D = 1024
NB_P, L_P = 8, 2048
NB_S, L_S = 128, 8
ROWS_P = NB_P * L_P
ROWS_S = NB_S * L_S
ROWS = ROWS_P + ROWS_S
DEPTH = 2
HW = 384
S5W = 256
SSW = 384
NH = 6
HD = 64
S5N = 1024
CONVC = 640
N_IN = 2822
N_IN_PAD = 2944
P_HG = 1536
P_SS = 1152
DFF = 2816
NEXP = 8
ALPHA = (2 * DEPTH) ** 0.25
LN_EPS = 1e-5
RMS_EPS = 1e-6

TM = 256
NPT = ROWS_P // TM
NST = ROWS_S // TM
NT = NPT + NST
SEQ_PER_TILE = TM // L_S
TILES_PER_SEQ = L_P // TM

CB = 128
NPC = ROWS_P // CB
NSC = ROWS_S // CB
CH_PER_SEQ = L_P // CB
SEQ_PER_CB = CB // L_S

TMF = 512
TF = 1408
VMEM_LIMIT = 56 * 1024 * 1024


def _cparams(sem):
    return pltpu.CompilerParams(dimension_semantics=sem, vmem_limit_bytes=VMEM_LIMIT)


def _bdot(a, b):
    return jnp.dot(a.astype(BF16), b.astype(BF16), preferred_element_type=F32)


def _bdot_nt(a, b):
    return lax.dot_general(a.astype(BF16), b.astype(BF16), (((1,), (1,)), ((), ())),
                           preferred_element_type=F32)


def _split_dot(x, e, passes):
    acc = None
    r = x
    for _ in range(passes):
        hi = r.astype(BF16)
        d = jnp.dot(hi, e, preferred_element_type=F32)
        acc = d if acc is None else acc + d
        r = r - hi.astype(F32)
    return acc


def _split_dot_l(e, x, passes):
    acc = None
    r = x
    for _ in range(passes):
        hi = r.astype(BF16)
        d = jnp.dot(e, hi, preferred_element_type=F32)
        acc = d if acc is None else acc + d
        r = r - hi.astype(F32)
    return acc


def _silu(x):
    return x * jax.nn.sigmoid(x)


def _layer_norm(x, g, b):
    mu = jnp.mean(x, -1, keepdims=True)
    xc = x - mu
    var = jnp.mean(xc * xc, -1, keepdims=True)
    return xc * lax.rsqrt(var + LN_EPS) * g + b


def _rowmod(i, p_ref, s_ref):
    s = jnp.broadcast_to(s_ref[...], (SEQ_PER_TILE, L_S, D)).reshape(TM, D)
    return jnp.where(i < NPT, p_ref[0], s)


ADA_TN = 1536


def _ada_body(c_ref, w_ref, b_ref, o_ref):
    o_ref[...] = _bdot(_silu(c_ref[...]), w_ref[...]) + b_ref[...]


def _ada(c_all, w_ada, b_ada):
    nc = c_all.shape[0]
    return pl.pallas_call(
        _ada_body,
        grid=(DEPTH, 6 * D // ADA_TN),
        in_specs=[
            pl.BlockSpec((nc, D), lambda l, j: (0, 0)),
            pl.BlockSpec((None, D, ADA_TN), lambda l, j: (l, 0, j)),
            pl.BlockSpec((None, 1, ADA_TN), lambda l, j: (l, 0, j)),
        ],
        out_specs=pl.BlockSpec((None, nc, ADA_TN), lambda l, j: (l, 0, j)),
        out_shape=jax.ShapeDtypeStruct((DEPTH, nc, 6 * D), F32),
        compiler_params=_cparams(("parallel", "parallel")),
        name="ada",
    )(c_all, w_ada, b_ada.reshape(DEPTH, 1, 6 * D))


def _mod_specs(layer, k):
    ps = pl.BlockSpec((None, 1, 1, D),
                      lambda i: (layer, NB_S + jnp.minimum(i // TILES_PER_SEQ, NB_P - 1), 0, k))
    ss = pl.BlockSpec((None, SEQ_PER_TILE, 1, D),
                      lambda i: (layer, jnp.clip(i - NPT, 0, NST - 1), 0, k))
    return [ps, ss]


def _row_spec(width):
    return pl.BlockSpec((TM, width), lambda i: (i, 0))


def _const_spec(shape):
    nd = len(shape)
    return pl.BlockSpec(shape, lambda *_: (0,) * nd)


def _proj_out(x, i, scp, scs, shp, shs, w_ref, ph_ref, ps_ref, pm_ref):
    h = x * (1.0 + _rowmod(i, scp, scs)) + _rowmod(i, shp, shs)
    proj = jnp.dot(h.astype(BF16), w_ref[...], preferred_element_type=F32)
    ph_ref[...] = proj[:, 0:P_HG]
    ps_ref[...] = proj[:, P_HG:P_HG + S5W]
    pm_ref[...] = proj[:, P_HG + S5W:N_IN_PAD]


def _a0_body(xp_ref, xs_ref, scp, scs, shp, shs, w_ref, x_ref, ph_ref, ps_ref, pm_ref):
    i = pl.program_id(0)
    x = jnp.where(i < NPT, xp_ref[...], xs_ref[...])
    x_ref[...] = x
    _proj_out(x, i, scp, scs, shp, shs, w_ref, ph_ref, ps_ref, pm_ref)


def _a1_body(x1_ref, f_ref, gp, gs, lng_ref, lnb_ref, scp, scs, shp, shs, w_ref,
             x_ref, ph_ref, ps_ref, pm_ref):
    i = pl.program_id(0)
    x = _layer_norm(ALPHA * x1_ref[...] + _rowmod(i, gp, gs) * f_ref[...], lng_ref[...], lnb_ref[...])
    x_ref[...] = x
    _proj_out(x, i, scp, scs, shp, shs, w_ref, ph_ref, ps_ref, pm_ref)


def _a_out():
    specs = [_row_spec(D), _row_spec(P_HG), _row_spec(S5W), _row_spec(P_SS)]
    shapes = [jax.ShapeDtypeStruct((ROWS, w), F32) for w in (D, P_HG, S5W, P_SS)]
    return specs, shapes


def _stage_a0(xp, xs, ada4, w_in_b):
    out_specs, out_shape = _a_out()
    return pl.pallas_call(
        _a0_body,
        grid=(NT,),
        in_specs=[
            pl.BlockSpec((TM, D), lambda i: (jnp.minimum(i, NPT - 1), 0)),
            pl.BlockSpec((TM, D), lambda i: (jnp.clip(i - NPT, 0, NST - 1), 0)),
            *_mod_specs(0, 1), *_mod_specs(0, 0),
            _const_spec((D, N_IN_PAD)),
        ],
        out_specs=out_specs, out_shape=out_shape,
        compiler_params=_cparams(("parallel",)),
        name="stage_a0",
    )(xp, xs, ada4, ada4, ada4, ada4, w_in_b)


def _stage_a1(layer, x1, f, ada4, ln_g, ln_b, w_in_b):
    out_specs, out_shape = _a_out()
    return pl.pallas_call(
        _a1_body,
        grid=(NT,),
        in_specs=[
            _row_spec(D), _row_spec(D),
            *_mod_specs(layer - 1, 5),
            _const_spec((1, D)), _const_spec((1, D)),
            *_mod_specs(layer, 1), *_mod_specs(layer, 0),
            _const_spec((D, N_IN_PAD)),
        ],
        out_specs=out_specs, out_shape=out_shape,
        compiler_params=_cparams(("parallel",)),
        name="stage_a1",
    )(x1, f, ada4, ada4, ln_g, ln_b, ada4, ada4, ada4, ada4, w_in_b)


def _mixer_blk(width):
    return pl.BlockSpec((CB, width), lambda i: (i, 0))


def _pstate_spec(shape):
    nd = len(shape)
    return pl.BlockSpec((1,) + shape, lambda i: (jnp.minimum(i // CH_PER_SEQ, NB_P - 1),) + (0,) * nd)


def _sstate_spec(shape):
    nd = len(shape)
    return pl.BlockSpec((SEQ_PER_CB,) + shape, lambda i: (jnp.clip(i - NPC, 0, NSC - 1),) + (0,) * nd)


def _seq_last_rows(x):
    w = x.shape[-1]
    x3 = x.reshape(SEQ_PER_CB, L_S, w)
    return jnp.broadcast_to(x3[:, L_S - 1:L_S, :], (SEQ_PER_CB, L_S, w)).reshape(CB, w)


def _concat_heads(parts):
    return jnp.concatenate(parts, axis=1)


def _stack_select(shape, row_div, lane_div):
    r = lax.broadcasted_iota(jnp.int32, shape, 0) // row_div
    c = lax.broadcasted_iota(jnp.int32, shape, 1) // lane_div
    return r == c


def _seq_expand_lanes(qh):
    q2 = jnp.concatenate([qh, qh], axis=1)
    q16 = jnp.concatenate([q2] * (SEQ_PER_CB // 2), axis=1)
    return jnp.where(_stack_select((CB, SEQ_PER_CB * HD), L_S, HD), q16, 0.0)


def _seq_expand_rows(xt):
    t = jnp.broadcast_to(xt[None], (SEQ_PER_CB, HD, CB)).reshape(SEQ_PER_CB * HD, CB)
    return jnp.where(_stack_select((SEQ_PER_CB * HD, CB), HD, L_S), t, 0.0)


def _fold_seq_lanes(full):
    acc = full[:, 0:128]
    for j in range(1, SEQ_PER_CB * HD // 128):
        acc = acc + full[:, 128 * j:128 * (j + 1)]
    return acc[:, 0:HD] + acc[:, HD:2 * HD]


HGRN_BASE = 32
EXP_RANGE_MAX = 80.0


def _hgrn_block(prompt, p_ref, lb_ref, hg_ref, e64_ref, tril_ref, oh_ref, st_scr, o_scr, i,
                stp_out=None, sts_in=None, sts_out=None):
    lb = lb_ref[...]
    qr = p_ref[:, 0:HW]
    fr = p_ref[:, HW:2 * HW]
    v = p_ref[:, 2 * HW:3 * HW]
    gr = p_ref[:, 3 * HW:4 * HW]
    ls = jnp.minimum(fr, 0.0) - jnp.log1p(jnp.exp(-jnp.abs(fr)))
    a = jnp.log(lb)
    bb = jnp.log1p(-lb) + ls
    lf = jnp.maximum(a, bb) + jnp.log1p(jnp.exp(-jnp.abs(a - bb)))
    kk = (1.0 - lb) * jax.nn.sigmoid(-fr)
    q = _silu(qr)
    b = _split_dot_l(tril_ref[...], lf, 3)

    e64 = e64_ref[...]
    ti = lax.broadcasted_iota(jnp.int32, (CB, CB), 0)
    si = lax.broadcasted_iota(jnp.int32, (CB, CB), 1)

    def diag8():
        nsub = CB // 8
        b3 = b.reshape(nsub, 8, HW)
        q3 = q.reshape(nsub, 8, HW)
        k3 = kk.reshape(nsub, 8, HW)
        v3 = v.reshape(nsub, 8, HW)
        r3 = lax.broadcasted_iota(jnp.int32, (nsub, 8, HW), 1)
        o = jnp.zeros((CB, HW), F32)
        for s in range(8):
            dlt = jnp.minimum(b3 - b3[:, s:s + 1, :], 0.0)
            w = jnp.where(r3 >= s, jnp.exp(dlt), 0.0) * q3 * k3[:, s:s + 1, :]
            hsum = jnp.dot(w.reshape(CB, HW).astype(BF16), e64, preferred_element_type=F32)
            o = o + hsum * jnp.broadcast_to(v3[:, s:s + 1, :], (nsub, 8, HW)).reshape(CB, HW)
        return o

    def level_terms(m):
        terms = []
        while m < CB:
            nb = CB // (2 * m)
            b4 = b.reshape(nb, 2 * m, HW)
            bmid = b4[:, m - 1:m, :]
            pos = lax.broadcasted_iota(jnp.int32, (nb, 2 * m, HW), 1)
            qq = jnp.where(pos >= m, q.reshape(nb, 2 * m, HW) * jnp.exp(jnp.minimum(b4 - bmid, 0.0)), 0.0)
            kq = jnp.where(pos < m, kk.reshape(nb, 2 * m, HW) * jnp.exp(jnp.minimum(bmid - b4, 0.0)), 0.0)
            terms.append((qq.reshape(CB, HW), kq.reshape(CB, HW), (ti // (2 * m)) == (si // (2 * m))))
            m *= 2
        return terms

    def scores_times_v(terms):
        parts = []
        for h in range(NH):
            sl = slice(HD * h, HD * (h + 1))
            sc = None
            for qq, kq, keep in terms:
                t = jnp.where(keep, _bdot_nt(qq[:, sl], kq[:, sl]), 0.0)
                sc = t if sc is None else sc + t
            parts.append(_bdot(sc, v[:, sl]))
        return _concat_heads(parts)

    base = HGRN_BASE if prompt else L_S
    nbase = CB // base
    bb3 = b.reshape(nbase, base, HW)
    top = bb3[:, 0:1, :] - lf.reshape(nbase, base, HW)[:, 0:1, :]
    decay_range = jnp.max(top - bb3[:, base - 1:base, :])
    fast = decay_range <= EXP_RANGE_MAX

    @pl.when(fast)
    def _():
        qf = (q.reshape(nbase, base, HW) * jnp.exp(bb3 - top)).reshape(CB, HW)
        kf = (kk.reshape(nbase, base, HW) * jnp.exp(top - bb3)).reshape(CB, HW)
        keep = ((ti // base) == (si // base)) & (si <= ti)
        o_scr[...] = scores_times_v([(qf, kf, keep)] + (level_terms(base) if prompt else []))

    @pl.when(jnp.logical_not(fast))
    def _():
        o_scr[...] = diag8() + (scores_times_v(level_terms(8)) if prompt else 0.0)

    o = o_scr[...]
    qt = q * jnp.exp(b)
    if prompt:
        blast = b[CB - 1:CB, :]
        kd = kk * jnp.exp(blast - b)
        vt = v.T
        parts = []
        for h in range(NH):
            sl = slice(HD * h, HD * (h + 1))
            st = st_scr[h]
            oh = o[:, sl] + _bdot_nt(qt[:, sl], st)
            st_new = st * jnp.exp(blast[:, sl]) + _bdot(vt[sl, :], kd[:, sl])
            st_scr[h] = st_new
            parts.append(oh)

        @pl.when(i % CH_PER_SEQ == CH_PER_SEQ - 1)
        def _():
            stp_out[0] = st_scr[...]
    else:
        blast = _seq_last_rows(b)
        kd = kk * jnp.exp(blast - b)
        dec = jnp.exp(blast)
        vt = v.T
        parts = []
        for h in range(NH):
            sl = slice(HD * h, HD * (h + 1))
            sts = sts_in[:, h].reshape(SEQ_PER_CB * HD, HD)
            full = _bdot_nt(qt[:, sl], sts)
            sel = jnp.where(_stack_select((CB, SEQ_PER_CB * HD), L_S, HD), full, 0.0)
            parts.append(o[:, sl] + _fold_seq_lanes(sel))
            dec3 = dec[:, sl].reshape(SEQ_PER_CB, L_S, HD)[:, L_S - 1:L_S, :]
            dec_rows = jnp.broadcast_to(dec3, (SEQ_PER_CB, HD, HD)).reshape(SEQ_PER_CB * HD, HD)
            upd = _bdot(_seq_expand_rows(vt[sl, :]), kd[:, sl])
            sts_out[:, h] = (sts * dec_rows + upd).reshape(SEQ_PER_CB, HD, HD)
    oall = _concat_heads(parts)
    ms = _split_dot(oall * oall, e64, 2) * (1.0 / HD)
    oh_ref[...] = oall * lax.rsqrt(ms + RMS_EPS) * hg_ref[...] * _silu(gr)


def _hgrn_body(p_ref, lb_ref, hg_ref, e64_ref, trilp_ref, trils_ref, sts_in,
               oh_ref, stp_out, sts_out, st_scr, o_scr):
    i = pl.program_id(0)

    @pl.when((i < NPC) & (i % CH_PER_SEQ == 0))
    def _():
        st_scr[...] = jnp.zeros_like(st_scr)

    @pl.when(i < NPC)
    def _():
        _hgrn_block(True, p_ref, lb_ref, hg_ref, e64_ref, trilp_ref, oh_ref, st_scr, o_scr, i,
                    stp_out=stp_out)

    @pl.when(i >= NPC)
    def _():
        _hgrn_block(False, p_ref, lb_ref, hg_ref, e64_ref, trils_ref, oh_ref, st_scr, o_scr, i,
                    sts_in=sts_in, sts_out=sts_out)


def _hgrn(ph, lb, hg, consts, st_t):
    return pl.pallas_call(
        _hgrn_body,
        grid=(NPC + NSC,),
        in_specs=[
            _mixer_blk(P_HG), _const_spec((1, HW)), _const_spec((1, HW)),
            _const_spec((HW, HW)), _const_spec((CB, CB)), _const_spec((CB, CB)),
            _sstate_spec((NH, HD, HD)),
        ],
        out_specs=[_mixer_blk(HW), _pstate_spec((NH, HD, HD)), _sstate_spec((NH, HD, HD))],
        out_shape=[
            jax.ShapeDtypeStruct((ROWS, HW), F32),
            jax.ShapeDtypeStruct((NB_P, NH, HD, HD), F32),
            jax.ShapeDtypeStruct((NB_S, NH, HD, HD), F32),
        ],
        scratch_shapes=[pltpu.VMEM((NH, HD, HD), F32), pltpu.VMEM((CB, HW), F32)],
        compiler_params=_cparams(("arbitrary",)),
        name="hgrn",
    )(ph, lb, hg, consts["e64"], consts["trilp"], consts["trils"], st_t)


def _cmul_add(hr, hi, lr, li, sr, si):
    return hr + lr * sr - li * si, hi + lr * si + li * sr


def _s5_block(prompt, p_ref, tab_ref, bblk_ref, ccat_ref, d_ref, wglu_ref, bglu_ref, os_ref,
              carry_scr, h_scr, i, s5p_out=None, s5s_in=None, s5s_out=None):
    u = p_ref[...]
    ub = u.astype(BF16)
    halves = [ub[:, (S5W // 2) * j:(S5W // 2) * (j + 1)] for j in range(2)]
    hr = jnp.concatenate([jnp.dot(halves[j], bblk_ref[0, j], preferred_element_type=F32) for j in range(2)], axis=1)
    hi = jnp.concatenate([jnp.dot(halves[j], bblk_ref[1, j], preferred_element_type=F32) for j in range(2)], axis=1)
    nsub = CB // 8
    for idx, dsh in enumerate((1, 2, 4)):
        sr = pltpu.roll(hr, dsh, 0).reshape(nsub, 8, S5N)
        si = pltpu.roll(hi, dsh, 0).reshape(nsub, 8, S5N)
        lr = tab_ref[idx, :, 0:S5N][None]
        li = tab_ref[idx, :, S5N:2 * S5N][None]
        nr, ni = _cmul_add(hr.reshape(nsub, 8, S5N), hi.reshape(nsub, 8, S5N), lr, li, sr, si)
        hr = nr.reshape(CB, S5N)
        hi = ni.reshape(CB, S5N)
    tcr = tab_ref[3, :, 0:S5N]
    tci = tab_ref[3, :, S5N:2 * S5N]
    if prompt:
        cr = carry_scr[:, 0:S5N]
        ci = carry_scr[:, S5N:2 * S5N]
        for j in range(nsub):
            tr, tim = _cmul_add(hr[8 * j:8 * j + 8], hi[8 * j:8 * j + 8], tcr, tci, cr, ci)
            h_scr[8 * j:8 * j + 8, 0:S5N] = tr
            h_scr[8 * j:8 * j + 8, S5N:2 * S5N] = tim
            cr = tr[7:8]
            ci = tim[7:8]
        carry_scr[:, 0:S5N] = cr
        carry_scr[:, S5N:2 * S5N] = ci

        @pl.when(i % CH_PER_SEQ == CH_PER_SEQ - 1)
        def _():
            s5p_out[0] = carry_scr[...]
    else:
        cr = s5s_in[:, :, 0:S5N]
        ci = s5s_in[:, :, S5N:2 * S5N]
        tr, tim = _cmul_add(hr.reshape(nsub, 8, S5N), hi.reshape(nsub, 8, S5N), tcr[None], tci[None], cr, ci)
        h_scr[:, 0:S5N] = tr.reshape(CB, S5N)
        h_scr[:, S5N:2 * S5N] = tim.reshape(CB, S5N)
        sb = lax.broadcasted_iota(jnp.int32, (SEQ_PER_CB, CB), 0)
        st = lax.broadcasted_iota(jnp.int32, (SEQ_PER_CB, CB), 1)
        sel = (st == L_S * sb + (L_S - 1)).astype(BF16)
        s5s_out[...] = _split_dot_l(sel, h_scr[...], 3)
    hs = S5N // 2
    ch = [_bdot(h_scr[:, hs * j:hs * (j + 1)], ccat_ref[j, 0])
          + _bdot(h_scr[:, S5N + hs * j:S5N + hs * (j + 1)], ccat_ref[j, 1]) for j in range(2)]
    y = jnp.concatenate(ch, axis=1) + d_ref[...] * u
    c0 = math.sqrt(2.0 / math.pi)
    y = y * (0.5 * (1.0 + jnp.tanh(c0 * (y + 0.044715 * (y * y * y)))))
    os_ref[...] = y * jax.nn.sigmoid(_bdot(y, wglu_ref[...]) + bglu_ref[...])


def _s5_body(p_ref, tab_ref, bblk_ref, ccat_ref, d_ref, wglu_ref, bglu_ref, s5s_in,
             os_ref, s5p_out, s5s_out, carry_scr, h_scr):
    i = pl.program_id(0)
    args = (p_ref, tab_ref, bblk_ref, ccat_ref, d_ref, wglu_ref, bglu_ref, os_ref, carry_scr, h_scr, i)

    @pl.when((i < NPC) & (i % CH_PER_SEQ == 0))
    def _():
        carry_scr[...] = jnp.zeros_like(carry_scr)

    @pl.when(i < NPC)
    def _():
        _s5_block(True, *args, s5p_out=s5p_out)

    @pl.when(i >= NPC)
    def _():
        _s5_block(False, *args, s5s_in=s5s_in, s5s_out=s5s_out)


def _s5(ps, prm, st):
    return pl.pallas_call(
        _s5_body,
        grid=(NPC + NSC,),
        in_specs=[
            _mixer_blk(S5W), _const_spec((4, 8, 2 * S5N)), _const_spec((2, 2, S5W // 2, S5N // 2)),
            _const_spec((2, 2, S5N // 2, S5W // 2)), _const_spec((1, S5W)), _const_spec((S5W, S5W)),
            _const_spec((1, S5W)), _sstate_spec((1, 2 * S5N)),
        ],
        out_specs=[_mixer_blk(S5W), _pstate_spec((1, 2 * S5N)), _sstate_spec((2 * S5N,))],
        out_shape=[
            jax.ShapeDtypeStruct((ROWS, S5W), F32),
            jax.ShapeDtypeStruct((NB_P, 1, 2 * S5N), F32),
            jax.ShapeDtypeStruct((NB_S, 2 * S5N), F32),
        ],
        scratch_shapes=[pltpu.VMEM((1, 2 * S5N), F32), pltpu.VMEM((CB, 2 * S5N), F32)],
        compiler_params=_cparams(("arbitrary",)),
        name="s5",
    )(ps, prm["tab"], prm["bblk"], prm["ccat"], prm["d"], prm["wglu"], prm["bglu"], st)


def _ssd_block(prompt, p_ref, cw_ref, cb_ref, dtb_ref, aneg_ref, dx_ref, sg_ref, e64x_ref, e128x_ref,
               g192_ref, tril_ref, om_ref, cbuf, st_scr, i, ssp_out=None, hist_ref=None,
               sss_in=None, sss_out=None):
    z = p_ref[:, 0:SSW]
    xbc = p_ref[:, SSW:SSW + CONVC]
    dtr = p_ref[:, SSW + CONVC:P_SS]
    acc = cb_ref[...] + cw_ref[3:4, :] * xbc
    if prompt:
        cbuf[8:8 + CB, :] = xbc
        for k in (1, 2, 3):
            acc = acc + cw_ref[3 - k:4 - k, :] * cbuf[8 - k:8 - k + CB, :]
        cbuf[0:8, :] = cbuf[CB:CB + 8, :]
    else:
        tl = lax.broadcasted_iota(jnp.int32, (CB, CONVC), 0) % L_S
        hist = hist_ref[...]
        for k in (1, 2, 3):
            hk = hist if k == 3 else pltpu.roll(hist, CB - (3 - k), 0)
            sh = jnp.where(tl >= k, pltpu.roll(xbc, k, 0), hk)
            acc = acc + cw_ref[3 - k:4 - k, :] * sh
    xc = _silu(acc)
    xs = xc[:, 0:SSW]
    bm = xc[:, SSW:SSW + 2 * HD]
    cm = xc[:, SSW + 2 * HD:CONVC]
    xdt = dtr + dtb_ref[...]
    dt = jnp.maximum(xdt, 0.0) + jnp.log1p(jnp.exp(-jnp.abs(xdt)))
    la = dt * aneg_ref[...]
    tril = tril_ref[...]
    b6 = _split_dot_l(tril, la, 3)
    bx = _split_dot(b6, e64x_ref[...], 3)
    bxw = _split_dot(b6, e128x_ref[...], 3)
    dtx = _split_dot(dt, e64x_ref[...], 2)
    bm_rep = _concat_heads([bm[:, 0:HD]] * 3 + [bm[:, HD:2 * HD]] * 3)
    cm_rep = _concat_heads([cm[:, 0:HD]] * 3 + [cm[:, HD:2 * HD]] * 3)
    kh = bm_rep * dtx
    qt = cm_rep * jnp.exp(bx)
    blast = bx[CB - 1:CB, :] if prompt else _seq_last_rows(bx)
    kd = kh * jnp.exp(blast - bx)
    kdt = kd.T
    mask = tril > 0
    parts = []
    for h in range(NH):
        sl = slice(HD * h, HD * (h + 1))
        bcol = bxw[:, 128 * h:128 * (h + 1)]
        decay = jnp.exp(jnp.where(mask, bcol - bcol.T, -1e30))
        sc = _bdot_nt(cm_rep[:, sl], kh[:, sl]) * decay
        oh = _bdot(sc, xs[:, sl])
        if prompt:
            st = st_scr[h]
            oh = oh + _bdot(qt[:, sl], st)
            st_scr[h] = st * jnp.exp(blast[:, sl]) + _bdot(kdt[sl, :], xs[:, sl])
        else:
            sts = sss_in[:, h].reshape(SEQ_PER_CB * HD, HD)
            oh = oh + _bdot(_seq_expand_lanes(qt[:, sl]), sts)
            dec3 = jnp.exp(blast[:, sl]).reshape(SEQ_PER_CB, L_S, HD)[:, L_S - 1:L_S, :]
            dec_rows = jnp.broadcast_to(dec3, (SEQ_PER_CB, HD, HD)).reshape(SEQ_PER_CB * HD, HD)
            upd = _bdot(_seq_expand_rows(kdt[sl, :]), xs[:, sl])
            sss_out[:, h] = (sts * dec_rows + upd).reshape(SEQ_PER_CB, HD, HD)
        parts.append(oh)
    if prompt:
        @pl.when(i % CH_PER_SEQ == CH_PER_SEQ - 1)
        def _():
            ssp_out[0] = st_scr[...]
    y = (_concat_heads(parts) + dx_ref[...] * xs) * _silu(z)
    ms = _split_dot(y * y, g192_ref[...], 2) * (1.0 / (SSW // 2))
    om_ref[...] = y * lax.rsqrt(ms + RMS_EPS) * sg_ref[...]


def _ssd_body(p_ref, cw_ref, cb_ref, dtb_ref, aneg_ref, dx_ref, sg_ref, e64x_ref, e128x_ref, g192_ref,
              trilp_ref, trils_ref, hist_ref, sss_in, om_ref, ssp_out, sss_out, cbuf, st_scr):
    i = pl.program_id(0)
    common = (p_ref, cw_ref, cb_ref, dtb_ref, aneg_ref, dx_ref, sg_ref, e64x_ref, e128x_ref, g192_ref)

    @pl.when((i < NPC) & (i % CH_PER_SEQ == 0))
    def _():
        st_scr[...] = jnp.zeros_like(st_scr)
        cbuf[...] = jnp.zeros_like(cbuf)

    @pl.when(i < NPC)
    def _():
        _ssd_block(True, *common, trilp_ref, om_ref, cbuf, st_scr, i, ssp_out=ssp_out)

    @pl.when(i >= NPC)
    def _():
        _ssd_block(False, *common, trils_ref, om_ref, cbuf, st_scr, i, hist_ref=hist_ref,
                   sss_in=sss_in, sss_out=sss_out)


def _ssd(pm, prm, consts, hist, st):
    return pl.pallas_call(
        _ssd_body,
        grid=(NPC + NSC,),
        in_specs=[
            _mixer_blk(P_SS), _const_spec((4, CONVC)), _const_spec((1, CONVC)),
            _const_spec((1, 128)), _const_spec((1, 128)), _const_spec((1, SSW)), _const_spec((1, SSW)),
            _const_spec((128, SSW)), _const_spec((128, 2 * SSW)), _const_spec((SSW, SSW)),
            _const_spec((CB, CB)), _const_spec((CB, CB)),
            pl.BlockSpec((CB, CONVC), lambda i: (jnp.clip(i - NPC, 0, NSC - 1), 0)),
            _sstate_spec((NH, HD, HD)),
        ],
        out_specs=[_mixer_blk(SSW), _pstate_spec((NH, HD, HD)), _sstate_spec((NH, HD, HD))],
        out_shape=[
            jax.ShapeDtypeStruct((ROWS, SSW), F32),
            jax.ShapeDtypeStruct((NB_P, NH, HD, HD), F32),
            jax.ShapeDtypeStruct((NB_S, NH, HD, HD), F32),
        ],
        scratch_shapes=[pltpu.VMEM((CB + 8, CONVC), F32), pltpu.VMEM((NH, HD, HD), F32)],
        compiler_params=_cparams(("arbitrary",)),
        name="ssd",
    )(pm, prm["cw"], prm["cb"], prm["dtb"], prm["aneg"], prm["dx"], prm["sg"],
      consts["e64x"], consts["e128x"], consts["g192"], consts["trilp"], consts["trils"], hist, st)


def _o_core(i, oh_ref, os_ref, om_ref, x_ref, wo_ref, gp, gs, lng_ref, lnb_ref, scp, scs, shp, shs):
    mix = (_bdot(oh_ref[...], wo_ref[0:HW, :]) + _bdot(os_ref[...], wo_ref[HW:HW + S5W, :])
           + _bdot(om_ref[...], wo_ref[HW + S5W:D, :]))
    x1 = _layer_norm(ALPHA * x_ref[...] + _rowmod(i, gp, gs) * mix, lng_ref[...], lnb_ref[...])
    h2 = x1 * (1.0 + _rowmod(i, scp, scs)) + _rowmod(i, shp, shs)
    return x1, h2


def _o_body(oh_ref, os_ref, om_ref, x_ref, wo_ref, gp, gs, lng_ref, lnb_ref, scp, scs, shp, shs,
            x1_ref, h2_ref):
    i = pl.program_id(0)
    x1, h2 = _o_core(i, oh_ref, os_ref, om_ref, x_ref, wo_ref, gp, gs, lng_ref, lnb_ref, scp, scs, shp, shs)
    x1_ref[...] = x1
    h2_ref[...] = h2.astype(BF16)


def _o_router_body(oh_ref, os_ref, om_ref, x_ref, wo_ref, gp, gs, lng_ref, lnb_ref, scp, scs, shp, shs,
                   wr_ref, br_ref, x1_ref, h2_ref, route_ref):
    i = pl.program_id(0)
    x1, h2 = _o_core(i, oh_ref, os_ref, om_ref, x_ref, wo_ref, gp, gs, lng_ref, lnb_ref, scp, scs, shp, shs)
    x1_ref[...] = x1
    h2_ref[...] = h2
    h_hi = h2.astype(BF16)
    h_lo = (h2 - h_hi.astype(F32)).astype(BF16)
    logits = (jnp.dot(h_hi, wr_ref[0], preferred_element_type=F32)
              + jnp.dot(h_lo, wr_ref[0], preferred_element_type=F32)
              + jnp.dot(h_hi, wr_ref[1], preferred_element_type=F32)) + br_ref[...]
    lane = lax.broadcasted_iota(jnp.int32, (TM, 128), 1).astype(F32)
    neg = -jnp.inf
    lg = jnp.where(lane < NEXP, logits, neg)
    m1 = jnp.max(lg, axis=-1, keepdims=True)
    i1 = jnp.min(jnp.where(lg == m1, lane, 128.0), axis=-1, keepdims=True)
    lg2 = jnp.where(lane == i1, neg, lg)
    m2 = jnp.max(lg2, axis=-1, keepdims=True)
    i2 = jnp.min(jnp.where(lg2 == m2, lane, 128.0), axis=-1, keepdims=True)
    e2 = jnp.exp(m2 - m1)
    den = 1.0 + e2
    route_ref[...] = jnp.where(lane == 0.0, i1, jnp.where(lane == 1.0, i2,
                               jnp.where(lane == 2.0, 1.0 / den, jnp.where(lane == 3.0, e2 / den, 0.0))))


def _stage_o(layer, oh, os_, om, x, wo_b, ada4, ln_g, ln_b, router=None):
    in_specs = [
        _row_spec(HW), _row_spec(S5W), _row_spec(SSW), _row_spec(D), _const_spec((D, D)),
        *_mod_specs(layer, 2), _const_spec((1, D)), _const_spec((1, D)),
        *_mod_specs(layer, 4), *_mod_specs(layer, 3),
    ]
    args = [oh, os_, om, x, wo_b, ada4, ada4, ln_g, ln_b, ada4, ada4, ada4, ada4]
    out_specs = [_row_spec(D), _row_spec(D)]
    out_shape = [jax.ShapeDtypeStruct((ROWS, D), F32), jax.ShapeDtypeStruct((ROWS, D), BF16)]
    body = _o_body
    if router is not None:
        in_specs += [_const_spec((2, D, 128)), _const_spec((1, 128))]
        args += list(router)
        out_specs.append(_row_spec(128))
        out_shape[1] = jax.ShapeDtypeStruct((ROWS, D), F32)
        out_shape.append(jax.ShapeDtypeStruct((ROWS, 128), F32))
        body = _o_router_body
    return pl.pallas_call(
        body, grid=(NT,), in_specs=in_specs, out_specs=out_specs, out_shape=out_shape,
        compiler_params=_cparams(("parallel",)),
        name="stage_o_router" if router is not None else "stage_o",
    )(*args)


def _ffn_body(te_ref, h_ref, wg_ref, wu_ref, wd_ref, o_ref, acc_ref):
    j = pl.program_id(1)

    @pl.when(j == 0)
    def _():
        acc_ref[...] = jnp.zeros_like(acc_ref)

    h = h_ref[...].astype(BF16)
    g = jnp.dot(h, wg_ref[...], preferred_element_type=F32)
    u = jnp.dot(h, wu_ref[...], preferred_element_type=F32)
    act = (_silu(g) * u).astype(BF16)
    acc_ref[...] += jnp.dot(act, wd_ref[...], preferred_element_type=F32)

    @pl.when(j == pl.num_programs(1) - 1)
    def _():
        o_ref[...] = acc_ref[...]


def _ffn(tile_expert, h, wg, wu, wd):
    rows = h.shape[0]
    grid_spec = pltpu.PrefetchScalarGridSpec(
        num_scalar_prefetch=1,
        grid=(rows // TMF, DFF // TF),
        in_specs=[
            pl.BlockSpec((TMF, D), lambda i, j, te: (i, 0)),
            pl.BlockSpec((None, D, TF), lambda i, j, te: (te[i], 0, j)),
            pl.BlockSpec((None, D, TF), lambda i, j, te: (te[i], 0, j)),
            pl.BlockSpec((None, TF, D), lambda i, j, te: (te[i], j, 0)),
        ],
        out_specs=pl.BlockSpec((TMF, D), lambda i, j, te: (i, 0)),
        scratch_shapes=[pltpu.VMEM((TMF, D), F32)],
    )
    return pl.pallas_call(
        _ffn_body, grid_spec=grid_spec,
        out_shape=jax.ShapeDtypeStruct((rows, D), F32),
        compiler_params=_cparams(("parallel", "arbitrary")),
        name="ffn",
    )(tile_expert, h, wg, wu, wd)


def _final_body(x1_ref, ya_ref, yb_ref, route_ref, gp, gs, lng_ref, lnb_ref, yp_ref, ys_ref):
    i = pl.program_id(0)
    f = route_ref[:, 2:3] * ya_ref[...] + route_ref[:, 3:4] * yb_ref[...]
    y = _layer_norm(ALPHA * x1_ref[...] + _rowmod(i, gp, gs) * f, lng_ref[...], lnb_ref[...])

    @pl.when(i < NPT)
    def _():
        yp_ref[...] = y

    @pl.when(i >= NPT)
    def _():
        ys_ref[...] = y


def _stage_final(layer, x1, ya, yb, route, ada4, ln_g, ln_b):
    return pl.pallas_call(
        _final_body,
        grid=(NT,),
        in_specs=[
            _row_spec(D), _row_spec(D), _row_spec(D), _row_spec(128),
            *_mod_specs(layer, 5), _const_spec((1, D)), _const_spec((1, D)),
        ],
        out_specs=[
            pl.BlockSpec((TM, D), lambda i: (jnp.minimum(i, NPT - 1), 0)),
            pl.BlockSpec((TM, D), lambda i: (jnp.clip(i - NPT, 0, NST - 1), 0)),
        ],
        out_shape=[jax.ShapeDtypeStruct((ROWS_P, D), F32), jax.ShapeDtypeStruct((ROWS_S, D), F32)],
        compiler_params=_cparams(("arbitrary",)),
        name="stage_final",
    )(x1, ya, yb, route, ada4, ada4, ln_g, ln_b)


def _block_ones(n, blk):
    r = jnp.arange(n) // blk
    return (r[:, None] == r[None, :]).astype(BF16)


def _consts():
    t = jnp.arange(CB)
    causal = t[:, None] >= t[None, :]
    same_seq = (t[:, None] // L_S) == (t[None, :] // L_S)
    h = jnp.arange(128)
    return {
        "e64": _block_ones(HW, HD),
        "g192": _block_ones(SSW, SSW // 2),
        "trilp": causal.astype(BF16),
        "trils": (causal & same_seq).astype(BF16),
        "e64x": (h[:, None] == (jnp.arange(SSW)[None, :] // HD)).astype(BF16),
        "e128x": (h[:, None] == (jnp.arange(2 * SSW)[None, :] // 128)).astype(BF16),
    }


def _s5_params(a_re, a_im, log_dt, b_re, b_im, c_re, c_im, d, w_glu, b_glu):
    dt = jnp.exp(log_dt)[:, None]
    mag = jnp.exp(a_re * dt)
    lam_re, lam_im = mag * jnp.cos(a_im * dt), mag * jnp.sin(a_im * dt)
    den = a_re * a_re + a_im * a_im
    nr, ni = lam_re - 1.0, lam_im
    zr = (nr * a_re + ni * a_im) / den
    zi = (ni * a_re - nr * a_im) / den
    bbar_re = zr[..., None] * b_re - zi[..., None] * b_im
    bbar_im = zr[..., None] * b_im + zi[..., None] * b_re
    eye = jnp.eye(16, dtype=F32)
    blk = lambda bb: jnp.einsum('gph,gk->ghkp', bb, eye).reshape(S5W, S5N)
    hu, hs = S5W // 2, S5N // 2
    bblk = jnp.stack([jnp.stack([blk(bb)[hu * j:hu * (j + 1), hs * j:hs * (j + 1)] for j in range(2)])
                      for bb in (bbar_re, bbar_im)]).astype(BF16)
    cblk = lambda cc: jnp.einsum('ghp,gk->gpkh', cc, eye).reshape(S5N, S5W)
    ccat = jnp.stack([jnp.stack([cblk(cc)[hs * j:hs * (j + 1), hu * j:hu * (j + 1)] for cc in (c_re, -c_im)])
                      for j in range(2)]).astype(BF16)
    lr, li = lam_re.reshape(-1), lam_im.reshape(-1)
    pows = [(jnp.ones_like(lr), jnp.zeros_like(li))]
    for _ in range(8):
        pr, pi = pows[-1]
        pows.append((pr * lr - pi * li, pr * li + pi * lr))
    rows = jnp.arange(8)[:, None]
    tabs = []
    for dsh in (1, 2, 4):
        pr, pi = pows[dsh]
        tabs.append(jnp.where(rows >= dsh, jnp.concatenate([pr, pi])[None, :], 0.0))
    tabs.append(jnp.stack([jnp.concatenate(pows[r + 1]) for r in range(8)]))
    return {
        "tab": jnp.stack(tabs), "bblk": bblk, "ccat": ccat,
        "d": d.reshape(1, S5W), "wglu": w_glu.astype(BF16), "bglu": b_glu.reshape(1, S5W),
    }


def _pad_lanes(v, n=128):
    return jnp.pad(v, (0, n - v.shape[0])).reshape(1, n)


def kernel(x_prompt, x_sample, c_prompt, c_sample, state_hgrn, state_s5, state_ssd, state_ssd_conv, w_ada, b_ada, ln_g, ln_b, w_in, w_out, hgrn_lb_logits, hgrn_norm_g, s5_a_re, s5_a_im, s5_log_dt, s5_b_re, s5_b_im, s5_c_re, s5_c_im, s5_d, s5_w_glu, s5_b_glu, ssd_conv_w, ssd_conv_b, ssd_dt_bias, ssd_a_log, ssd_d, ssd_norm_g, ffn_w_gate, ffn_w_up, ffn_w_down, moe_w_router, moe_b_router, moe_w_gate, moe_w_up, moe_w_down):
    consts = _consts()
    c_all = jnp.concatenate([c_sample, c_prompt], axis=0)
    ada4 = _ada(c_all, w_ada, b_ada).reshape(DEPTH, NB_S + NB_P, 1, 6 * D)

    lb_all = jnp.cumsum(jax.nn.softmax(hgrn_lb_logits, axis=0), axis=0)
    lb_all = lb_all - lb_all[0]

    xp = x_prompt.reshape(ROWS_P, D)
    xs = x_sample.reshape(ROWS_S, D)
    new_h, new_s5, new_m, new_c = [], [], [], []
    x1 = f = route = None
    for l in range(DEPTH):
        w_in_b = jnp.pad(w_in[l], ((0, 0), (0, N_IN_PAD - N_IN))).astype(BF16)
        if l == 0:
            x, ph, ps, pm = _stage_a0(xp, xs, ada4, w_in_b)
        else:
            x, ph, ps, pm = _stage_a1(l, x1, f, ada4, ln_g[l - 1, 1].reshape(1, D), ln_b[l - 1, 1].reshape(1, D),
                                      w_in_b)
        oh, hg_p, hg_s = _hgrn(ph, lb_all[l].reshape(1, HW), hgrn_norm_g[l].reshape(1, HW), consts,
                               jnp.swapaxes(state_hgrn[l], -1, -2))
        s5p = _s5_params(s5_a_re[l], s5_a_im[l], s5_log_dt[l], s5_b_re[l], s5_b_im[l], s5_c_re[l], s5_c_im[l],
                         s5_d[l], s5_w_glu[l], s5_b_glu[l])
        os_, s5_p, s5_s = _s5(ps, s5p, state_s5[l].reshape(NB_S, 1, 2 * S5N))
        ssd_prm = {
            "cw": ssd_conv_w[l], "cb": ssd_conv_b[l].reshape(1, CONVC),
            "dtb": _pad_lanes(ssd_dt_bias[l]), "aneg": _pad_lanes(-jnp.exp(ssd_a_log[l])),
            "dx": jnp.repeat(ssd_d[l], HD).reshape(1, SSW), "sg": ssd_norm_g[l].reshape(1, SSW),
        }
        hist = jnp.pad(state_ssd_conv[l], ((0, 0), (0, L_S - 3), (0, 0))).reshape(ROWS_S, CONVC)
        om, ss_p, ss_s = _ssd(pm, ssd_prm, consts, hist, state_ssd[l])
        conv_p = pm[:ROWS_P].reshape(NB_P, L_P, P_SS)[:, L_P - 3:, SSW:SSW + CONVC]
        conv_s = pm[ROWS_P:].reshape(NB_S, L_S, P_SS)[:, L_S - 3:, SSW:SSW + CONVC]
        new_h.append((jnp.swapaxes(hg_p, -1, -2), jnp.swapaxes(hg_s, -1, -2)))
        new_s5.append((s5_p.reshape(NB_P, 2, 16, 64), s5_s.reshape(NB_S, 2, 16, 64)))
        new_m.append((ss_p, ss_s))
        new_c.append((conv_p, conv_s))

        wo_b = w_out[l].astype(BF16)
        lg, lbb = ln_g[l, 0].reshape(1, D), ln_b[l, 0].reshape(1, D)
        j = l // 2
        if l % 2 == 0:
            x1, h2 = _stage_o(l, oh, os_, om, x, wo_b, ada4, lg, lbb)
            f = _ffn(jnp.zeros((ROWS // TMF,), jnp.int32), h2, ffn_w_gate[j:j + 1].astype(BF16),
                     ffn_w_up[j:j + 1].astype(BF16), ffn_w_down[j:j + 1].astype(BF16))
        else:
            wr = jnp.pad(moe_w_router[j], ((0, 0), (0, 128 - NEXP)))
            wr_hi = wr.astype(BF16)
            wr = jnp.stack([wr_hi, (wr - wr_hi.astype(F32)).astype(BF16)])
            br = _pad_lanes(moe_b_router[j])
            x1, h2, route = _stage_o(l, oh, os_, om, x, wo_b, ada4, lg, lbb, router=(wr, br))
            flat_e = route[:, 0:2].astype(jnp.int32).reshape(-1)
            onehot = (flat_e[:, None] == jnp.arange(NEXP)[None, :]).astype(jnp.int32)
            csum = jnp.cumsum(onehot, axis=0)
            counts = csum[-1]
            rank = jnp.take_along_axis(csum, flat_e[:, None], axis=1)[:, 0] - 1
            padded = ((counts + TMF - 1) // TMF) * TMF
            pend = jnp.cumsum(padded)
            pstart = pend - padded
            dest = pstart[flat_e] + rank
            n_pad = 2 * ROWS + NEXP * TMF
            src_tok = jnp.zeros((n_pad,), jnp.int32).at[dest].set(jnp.arange(2 * ROWS, dtype=jnp.int32) // 2)
            tile_start = jnp.arange(n_pad // TMF, dtype=jnp.int32) * TMF
            tile_e = jnp.minimum(jnp.sum((pend[None, :] <= tile_start[:, None]).astype(jnp.int32), axis=1),
                                 NEXP - 1)
            y_sorted = _ffn(tile_e, h2[src_tok], moe_w_gate[j].astype(BF16), moe_w_up[j].astype(BF16),
                            moe_w_down[j].astype(BF16))
            pos = dest.reshape(ROWS, 2)
            ya = y_sorted[pos[:, 0]]
            yb = y_sorted[pos[:, 1]]
    y_p, y_s = _stage_final(DEPTH - 1, x1, ya, yb, route, ada4, ln_g[DEPTH - 1, 1].reshape(1, D),
                            ln_b[DEPTH - 1, 1].reshape(1, D))
    stack = lambda lst, k: jnp.stack([t[k] for t in lst])
    return (y_p.reshape(NB_P, L_P, D), y_s.reshape(NB_S, L_S, D),
            stack(new_h, 0), stack(new_s5, 0), stack(new_m, 0), stack(new_c, 0),
            stack(new_h, 1), stack(new_s5, 1), stack(new_m, 1), stack(new_c, 1))
```

```python
import functools
import math

import jax
import jax.numpy as jnp
from jax import lax
from jax.experimental import pallas as pl
from jax.experimental.pallas import tpu as pltpu

F32 = jnp.float32
BF16 = jnp.bfloat16

D = 1024
NB_P, L_P = 8, 2048
NB_S, L_S = 128, 8
ROWS_P = NB_P * L_P
ROWS_S = NB_S * L_S
ROWS = ROWS_P + ROWS_S
DEPTH = 2
HW = 384
S5W = 256
SSW = 384
NH = 6
HD = 64
S5N = 1024
CONVC = 640
N_IN = 2822
N_IN_PAD = 2944
P_HG = 1536
P_SS = 1152
DFF = 2816
NEXP = 8
ALPHA = (2 * DEPTH) ** 0.25
LN_EPS = 1e-5
RMS_EPS = 1e-6

TM = 256
NPT = ROWS_P // TM
NST = ROWS_S // TM
NT = NPT + NST
SEQ_PER_TILE = TM // L_S
TILES_PER_SEQ = L_P // TM

CB = 128
NPC = ROWS_P // CB
NSC = ROWS_S // CB
CH_PER_SEQ = L_P // CB
SEQ_PER_CB = CB // L_S

TMF = 512
TF = 1408
VMEM_LIMIT = 56 * 1024 * 1024


def _cparams(sem):
    return pltpu.CompilerParams(dimension_semantics=sem, vmem_limit_bytes=VMEM_LIMIT)


def _bdot(a, b):
    return jnp.dot(a.astype(BF16), b.astype(BF16), preferred_element_type=F32)


def _bdot_nt(a, b):
    return lax.dot_general(a.astype(BF16), b.astype(BF16), (((1,), (1,)), ((), ())),
                           preferred_element_type=F32)


def _split_dot(x, e, passes):
    acc = None
    r = x
    for _ in range(passes):
        hi = r.astype(BF16)
        d = jnp.dot(hi, e, preferred_element_type=F32)
        acc = d if acc is None else acc + d
        r = r - hi.astype(F32)
    return acc


def _split_dot_l(e, x, passes):
    acc = None
    r = x
    for _ in range(passes):
        hi = r.astype(BF16)
        d = jnp.dot(e, hi, preferred_element_type=F32)
        acc = d if acc is None else acc + d
        r = r - hi.astype(F32)
    return acc


def _silu(x):
    return x * jax.nn.sigmoid(x)


def _layer_norm(x, g, b):
    mu = jnp.mean(x, -1, keepdims=True)
    xc = x - mu
    var = jnp.mean(xc * xc, -1, keepdims=True)
    return xc * lax.rsqrt(var + LN_EPS) * g + b


def _rowmod(i, p_ref, s_ref):
    s = jnp.broadcast_to(s_ref[...], (SEQ_PER_TILE, L_S, D)).reshape(TM, D)
    return jnp.where(i < NPT, p_ref[0], s)


ADA_TN = 1536


def _ada_body(c_ref, w_ref, b_ref, o_ref):
    o_ref[...] = _bdot(_silu(c_ref[...]), w_ref[...]) + b_ref[...]


def _ada(c_all, w_ada, b_ada):
    nc = c_all.shape[0]
    return pl.pallas_call(
        _ada_body,
        grid=(DEPTH, 6 * D // ADA_TN),
        in_specs=[
            pl.BlockSpec((nc, D), lambda l, j: (0, 0)),
            pl.BlockSpec((None, D, ADA_TN), lambda l, j: (l, 0, j)),
            pl.BlockSpec((None, 1, ADA_TN), lambda l, j: (l, 0, j)),
        ],
        out_specs=pl.BlockSpec((None, nc, ADA_TN), lambda l, j: (l, 0, j)),
        out_shape=jax.ShapeDtypeStruct((DEPTH, nc, 6 * D), F32),
        compiler_params=_cparams(("parallel", "parallel")),
        name="ada",
    )(c_all, w_ada, b_ada.reshape(DEPTH, 1, 6 * D))


def _mod_specs(layer, k):
    ps = pl.BlockSpec((None, 1, 1, D),
                      lambda i: (layer, NB_S + jnp.minimum(i // TILES_PER_SEQ, NB_P - 1), 0, k))
    ss = pl.BlockSpec((None, SEQ_PER_TILE, 1, D),
                      lambda i: (layer, jnp.clip(i - NPT, 0, NST - 1), 0, k))
    return [ps, ss]


def _row_spec(width):
    return pl.BlockSpec((TM, width), lambda i: (i, 0))


def _const_spec(shape):
    nd = len(shape)
    return pl.BlockSpec(shape, lambda *_: (0,) * nd)


def _proj_out(x, i, scp, scs, shp, shs, w_ref, ph_ref, ps_ref, pm_ref):
    h = x * (1.0 + _rowmod(i, scp, scs)) + _rowmod(i, shp, shs)
    proj = jnp.dot(h.astype(BF16), w_ref[...], preferred_element_type=F32)
    ph_ref[...] = proj[:, 0:P_HG]
    ps_ref[...] = proj[:, P_HG:P_HG + S5W]
    pm_ref[...] = proj[:, P_HG + S5W:N_IN_PAD]


def _a0_body(xp_ref, xs_ref, scp, scs, shp, shs, w_ref, x_ref, ph_ref, ps_ref, pm_ref):
    i = pl.program_id(0)
    x = jnp.where(i < NPT, xp_ref[...], xs_ref[...])
    x_ref[...] = x
    _proj_out(x, i, scp, scs, shp, shs, w_ref, ph_ref, ps_ref, pm_ref)


def _a1_body(x1_ref, f_ref, gp, gs, lng_ref, lnb_ref, scp, scs, shp, shs, w_ref,
             x_ref, ph_ref, ps_ref, pm_ref):
    i = pl.program_id(0)
    x = _layer_norm(ALPHA * x1_ref[...] + _rowmod(i, gp, gs) * f_ref[...], lng_ref[...], lnb_ref[...])
    x_ref[...] = x
    _proj_out(x, i, scp, scs, shp, shs, w_ref, ph_ref, ps_ref, pm_ref)


def _a_out():
    specs = [_row_spec(D), _row_spec(P_HG), _row_spec(S5W), _row_spec(P_SS)]
    shapes = [jax.ShapeDtypeStruct((ROWS, w), F32) for w in (D, P_HG, S5W, P_SS)]
    return specs, shapes


def _stage_a0(xp, xs, ada4, w_in_b):
    out_specs, out_shape = _a_out()
    return pl.pallas_call(
        _a0_body,
        grid=(NT,),
        in_specs=[
            pl.BlockSpec((TM, D), lambda i: (jnp.minimum(i, NPT - 1), 0)),
            pl.BlockSpec((TM, D), lambda i: (jnp.clip(i - NPT, 0, NST - 1), 0)),
            *_mod_specs(0, 1), *_mod_specs(0, 0),
            _const_spec((D, N_IN_PAD)),
        ],
        out_specs=out_specs, out_shape=out_shape,
        compiler_params=_cparams(("parallel",)),
        name="stage_a0",
    )(xp, xs, ada4, ada4, ada4, ada4, w_in_b)


def _stage_a1(layer, x1, f, ada4, ln_g, ln_b, w_in_b):
    out_specs, out_shape = _a_out()
    return pl.pallas_call(
        _a1_body,
        grid=(NT,),
        in_specs=[
            _row_spec(D), _row_spec(D),
            *_mod_specs(layer - 1, 5),
            _const_spec((1, D)), _const_spec((1, D)),
            *_mod_specs(layer, 1), *_mod_specs(layer, 0),
            _const_spec((D, N_IN_PAD)),
        ],
        out_specs=out_specs, out_shape=out_shape,
        compiler_params=_cparams(("parallel",)),
        name="stage_a1",
    )(x1, f, ada4, ada4, ln_g, ln_b, ada4, ada4, ada4, ada4, w_in_b)


def _mixer_blk(width):
    return pl.BlockSpec((CB, width), lambda i: (i, 0))


def _pstate_spec(shape):
    nd = len(shape)
    return pl.BlockSpec((1,) + shape, lambda i: (jnp.minimum(i // CH_PER_SEQ, NB_P - 1),) + (0,) * nd)


def _sstate_spec(shape):
    nd = len(shape)
    return pl.BlockSpec((SEQ_PER_CB,) + shape, lambda i: (jnp.clip(i - NPC, 0, NSC - 1),) + (0,) * nd)


def _seq_last_rows(x):
    w = x.shape[-1]
    x3 = x.reshape(SEQ_PER_CB, L_S, w)
    return jnp.broadcast_to(x3[:, L_S - 1:L_S, :], (SEQ_PER_CB, L_S, w)).reshape(CB, w)


def _concat_heads(parts):
    return jnp.concatenate(parts, axis=1)


def _stack_select(shape, row_div, lane_div):
    r = lax.broadcasted_iota(jnp.int32, shape, 0) // row_div
    c = lax.broadcasted_iota(jnp.int32, shape, 1) // lane_div
    return r == c


def _seq_expand_lanes(qh):
    q2 = jnp.concatenate([qh, qh], axis=1)
    q16 = jnp.concatenate([q2] * (SEQ_PER_CB // 2), axis=1)
    return jnp.where(_stack_select((CB, SEQ_PER_CB * HD), L_S, HD), q16, 0.0)


def _seq_expand_rows(xt):
    t = jnp.broadcast_to(xt[None], (SEQ_PER_CB, HD, CB)).reshape(SEQ_PER_CB * HD, CB)
    return jnp.where(_stack_select((SEQ_PER_CB * HD, CB), HD, L_S), t, 0.0)


def _fold_seq_lanes(full):
    acc = full[:, 0:128]
    for j in range(1, SEQ_PER_CB * HD // 128):
        acc = acc + full[:, 128 * j:128 * (j + 1)]
    return acc[:, 0:HD] + acc[:, HD:2 * HD]


HGRN_BASE = 32
EXP_RANGE_MAX = 80.0


def _hgrn_block(prompt, p_ref, lb_ref, hg_ref, e64_ref, tril_ref, oh_ref, st_scr, o_scr, i,
                stp_out=None, sts_in=None, sts_out=None):
    lb = lb_ref[...]
    qr = p_ref[:, 0:HW]
    fr = p_ref[:, HW:2 * HW]
    v = p_ref[:, 2 * HW:3 * HW]
    gr = p_ref[:, 3 * HW:4 * HW]
    ls = jnp.minimum(fr, 0.0) - jnp.log1p(jnp.exp(-jnp.abs(fr)))
    a = jnp.log(lb)
    bb = jnp.log1p(-lb) + ls
    lf = jnp.maximum(a, bb) + jnp.log1p(jnp.exp(-jnp.abs(a - bb)))
    kk = (1.0 - lb) * jax.nn.sigmoid(-fr)
    q = _silu(qr)
    b = _split_dot_l(tril_ref[...], lf, 3)

    e64 = e64_ref[...]
    ti = lax.broadcasted_iota(jnp.int32, (CB, CB), 0)
    si = lax.broadcasted_iota(jnp.int32, (CB, CB), 1)

    def diag8():
        nsub = CB // 8
        b3 = b.reshape(nsub, 8, HW)
        q3 = q.reshape(nsub, 8, HW)
        k3 = kk.reshape(nsub, 8, HW)
        v3 = v.reshape(nsub, 8, HW)
        r3 = lax.broadcasted_iota(jnp.int32, (nsub, 8, HW), 1)
        o = jnp.zeros((CB, HW), F32)
        for s in range(8):
            dlt = jnp.minimum(b3 - b3[:, s:s + 1, :], 0.0)
            w = jnp.where(r3 >= s, jnp.exp(dlt), 0.0) * q3 * k3[:, s:s + 1, :]
            hsum = jnp.dot(w.reshape(CB, HW).astype(BF16), e64, preferred_element_type=F32)
            o = o + hsum * jnp.broadcast_to(v3[:, s:s + 1, :], (nsub, 8, HW)).reshape(CB, HW)
        return o

    def level_terms(m):
        terms = []
        while m < CB:
            nb = CB // (2 * m)
            b4 = b.reshape(nb, 2 * m, HW)
            bmid = b4[:, m - 1:m, :]
            pos = lax.broadcasted_iota(jnp.int32, (nb, 2 * m, HW), 1)
            qq = jnp.where(pos >= m, q.reshape(nb, 2 * m, HW) * jnp.exp(jnp.minimum(b4 - bmid, 0.0)), 0.0)
            kq = jnp.where(pos < m, kk.reshape(nb, 2 * m, HW) * jnp.exp(jnp.minimum(bmid - b4, 0.0)), 0.0)
            terms.append((qq.reshape(CB, HW), kq.reshape(CB, HW), (ti // (2 * m)) == (si // (2 * m))))
            m *= 2
        return terms

    def scores_times_v(terms):
        parts = []
        for h in range(NH):
            sl = slice(HD * h, HD * (h + 1))
            sc = None
            for qq, kq, keep in terms:
                t = jnp.where(keep, _bdot_nt(qq[:, sl], kq[:, sl]), 0.0)
                sc = t if sc is None else sc + t
            parts.append(_bdot(sc, v[:, sl]))
        return _concat_heads(parts)

    base = HGRN_BASE if prompt else L_S
    nbase = CB // base
    bb3 = b.reshape(nbase, base, HW)
    top = bb3[:, 0:1, :] - lf.reshape(nbase, base, HW)[:, 0:1, :]
    decay_range = jnp.max(top - bb3[:, base - 1:base, :])
    fast = decay_range <= EXP_RANGE_MAX

    @pl.when(jnp.logical_not(fast))
    def _():
        o_scr[...] = diag8() + (scores_times_v(level_terms(8)) if prompt else 0.0)

    qf = (q.reshape(nbase, base, HW) * jnp.exp(bb3 - top)).reshape(CB, HW)
    kf = (kk.reshape(nbase, base, HW) * jnp.exp(top - bb3)).reshape(CB, HW)
    keep = ((ti // base) == (si // base)) & (si <= ti)
    o_fast = scores_times_v([(qf, kf, keep)] + (level_terms(base) if prompt else []))
    o = jnp.where(fast, o_fast, o_scr[...])
    qt = q * jnp.exp(b)
    if prompt:
        blast = b[CB - 1:CB, :]
        kd = kk * jnp.exp(blast - b)
        vt = v.T
        parts = []
        for h in range(NH):
            sl = slice(HD * h, HD * (h + 1))
            st = st_scr[h]
            oh = o[:, sl] + _bdot_nt(qt[:, sl], st)
            st_new = st * jnp.exp(blast[:, sl]) + _bdot(vt[sl, :], kd[:, sl])
            st_scr[h] = st_new
            parts.append(oh)

        @pl.when(i % CH_PER_SEQ == CH_PER_SEQ - 1)
        def _():
            stp_out[0] = st_scr[...]
    else:
        blast = _seq_last_rows(b)
        kd = kk * jnp.exp(blast - b)
        dec = jnp.exp(blast)
        vt = v.T
        parts = []
        for h in range(NH):
            sl = slice(HD * h, HD * (h + 1))
            sts = sts_in[:, h].reshape(SEQ_PER_CB * HD, HD)
            full = _bdot_nt(qt[:, sl], sts)
            sel = jnp.where(_stack_select((CB, SEQ_PER_CB * HD), L_S, HD), full, 0.0)
            parts.append(o[:, sl] + _fold_seq_lanes(sel))
            dec3 = dec[:, sl].reshape(SEQ_PER_CB, L_S, HD)[:, L_S - 1:L_S, :]
            dec_rows = jnp.broadcast_to(dec3, (SEQ_PER_CB, HD, HD)).reshape(SEQ_PER_CB * HD, HD)
            upd = _bdot(_seq_expand_rows(vt[sl, :]), kd[:, sl])
            sts_out[:, h] = (sts * dec_rows + upd).reshape(SEQ_PER_CB, HD, HD)
    oall = _concat_heads(parts)
    ms = _split_dot(oall * oall, e64, 2) * (1.0 / HD)
    oh_ref[...] = oall * lax.rsqrt(ms + RMS_EPS) * hg_ref[...] * _silu(gr)


def _hgrn_body(p_ref, lb_ref, hg_ref, e64_ref, trilp_ref, trils_ref, sts_in,
               oh_ref, stp_out, sts_out, st_scr, o_scr):
    i = pl.program_id(0)

    @pl.when(i == 0)
    def _():
        o_scr[...] = jnp.zeros_like(o_scr)

    @pl.when((i < NPC) & (i % CH_PER_SEQ == 0))
    def _():
        st_scr[...] = jnp.zeros_like(st_scr)

    @pl.when(i < NPC)
    def _():
        _hgrn_block(True, p_ref, lb_ref, hg_ref, e64_ref, trilp_ref, oh_ref, st_scr, o_scr, i,
                    stp_out=stp_out)

    @pl.when(i >= NPC)
    def _():
        _hgrn_block(False, p_ref, lb_ref, hg_ref, e64_ref, trils_ref, oh_ref, st_scr, o_scr, i,
                    sts_in=sts_in, sts_out=sts_out)


def _hgrn(ph, lb, hg, consts, st_t):
    return pl.pallas_call(
        _hgrn_body,
        grid=(NPC + NSC,),
        in_specs=[
            _mixer_blk(P_HG), _const_spec((1, HW)), _const_spec((1, HW)),
            _const_spec((HW, HW)), _const_spec((CB, CB)), _const_spec((CB, CB)),
            _sstate_spec((NH, HD, HD)),
        ],
        out_specs=[_mixer_blk(HW), _pstate_spec((NH, HD, HD)), _sstate_spec((NH, HD, HD))],
        out_shape=[
            jax.ShapeDtypeStruct((ROWS, HW), F32),
            jax.ShapeDtypeStruct((NB_P, NH, HD, HD), F32),
            jax.ShapeDtypeStruct((NB_S, NH, HD, HD), F32),
        ],
        scratch_shapes=[pltpu.VMEM((NH, HD, HD), F32), pltpu.VMEM((CB, HW), F32)],
        compiler_params=_cparams(("arbitrary",)),
        name="hgrn",
    )(ph, lb, hg, consts["e64"], consts["trilp"], consts["trils"], st_t)


def _cmul_add(hr, hi, lr, li, sr, si):
    return hr + lr * sr - li * si, hi + lr * si + li * sr


def _s5_project(ub, bblk_ref):
    halves = [ub[:, (S5W // 2) * j:(S5W // 2) * (j + 1)] for j in range(2)]
    hr = jnp.concatenate([jnp.dot(halves[j], bblk_ref[0, j], preferred_element_type=F32) for j in range(2)], axis=1)
    hi = jnp.concatenate([jnp.dot(halves[j], bblk_ref[1, j], preferred_element_type=F32) for j in range(2)], axis=1)
    return hr, hi


def _s5_readout(h_scr, u, ccat_ref, d_ref, wglu_ref, bglu_ref):
    hs = S5N // 2
    ch = [_bdot(h_scr[:, hs * j:hs * (j + 1)], ccat_ref[j, 0])
          + _bdot(h_scr[:, S5N + hs * j:S5N + hs * (j + 1)], ccat_ref[j, 1]) for j in range(2)]
    y = jnp.concatenate(ch, axis=1) + d_ref[...] * u
    c0 = math.sqrt(2.0 / math.pi)
    y = y * (0.5 * (1.0 + jnp.tanh(c0 * (y + 0.044715 * (y * y * y)))))
    return y * jax.nn.sigmoid(_bdot(y, wglu_ref[...]) + bglu_ref[...])


S5_TB = 32
S5_ROWS = NB_P * S5_TB


def _s5_prompt_body(p_ref, perm_ref, permt_ref, lam_ref, bblk_ref, ccat_ref, d_ref, wglu_ref, bglu_ref,
                    os_ref, st_out, carry_scr, h_scr):
    i = pl.program_id(0)

    @pl.when(i == 0)
    def _():
        carry_scr[...] = jnp.zeros_like(carry_scr)

    u = p_ref[...].reshape(S5_ROWS, S5W)
    u_hi = u.astype(BF16)
    u_lo = (u - u_hi.astype(F32)).astype(BF16)
    perm = perm_ref[...]
    up_hi = jnp.dot(perm, u_hi, preferred_element_type=F32)
    up = up_hi + jnp.dot(perm, u_lo, preferred_element_type=F32)
    hr, hi = _s5_project(up_hi.astype(BF16), bblk_ref)
    lr = lam_ref[:, 0:S5N]
    li = lam_ref[:, S5N:2 * S5N]
    cr = carry_scr[:, 0:S5N]
    ci = carry_scr[:, S5N:2 * S5N]
    for t in range(S5_TB):
        rows = slice(NB_P * t, NB_P * (t + 1))
        cr, ci = _cmul_add(hr[rows], hi[rows], lr, li, cr, ci)
        h_scr[rows, 0:S5N] = cr
        h_scr[rows, S5N:2 * S5N] = ci
    carry_scr[:, 0:S5N] = cr
    carry_scr[:, S5N:2 * S5N] = ci
    out = _s5_readout(h_scr, up, ccat_ref, d_ref, wglu_ref, bglu_ref)
    os_ref[...] = jnp.dot(permt_ref[...], out.astype(BF16), preferred_element_type=F32).reshape(NB_P, S5_TB, S5W)

    @pl.when(i == pl.num_programs(0) - 1)
    def _():
        st_out[...] = carry_scr[...]


def _s5_prompt(ps3, prm, consts):
    return pl.pallas_call(
        _s5_prompt_body,
        grid=(L_P // S5_TB,),
        in_specs=[
            pl.BlockSpec((NB_P, S5_TB, S5W), lambda i: (0, i, 0)),
            _const_spec((S5_ROWS, S5_ROWS)), _const_spec((S5_ROWS, S5_ROWS)), _const_spec((NB_P, 2 * S5N)),
            _const_spec((2, 2, S5W // 2, S5N // 2)), _const_spec((2, 2, S5N // 2, S5W // 2)),
            _const_spec((1, S5W)), _const_spec((S5W, S5W)), _const_spec((1, S5W)),
        ],
        out_specs=[pl.BlockSpec((NB_P, S5_TB, S5W), lambda i: (0, i, 0)), _const_spec((NB_P, 2 * S5N))],
        out_shape=[jax.ShapeDtypeStruct((NB_P, L_P, S5W), F32), jax.ShapeDtypeStruct((NB_P, 2 * S5N), F32)],
        scratch_shapes=[pltpu.VMEM((NB_P, 2 * S5N), F32), pltpu.VMEM((S5_ROWS, 2 * S5N), F32)],
        compiler_params=_cparams(("arbitrary",)),
        name="s5_prompt",
    )(ps3, consts["perm"], consts["permt"], prm["lam8"], prm["bblk"], prm["ccat"], prm["d"], prm["wglu"],
      prm["bglu"])


def _s5_sample_body(p_ref, tab_ref, bblk_ref, ccat_ref, d_ref, wglu_ref, bglu_ref, s5s_in,
                    os_ref, s5s_out, h_scr):
    u = p_ref[...]
    hr, hi = _s5_project(u.astype(BF16), bblk_ref)
    nsub = CB // 8
    for idx, dsh in enumerate((1, 2, 4)):
        sr = pltpu.roll(hr, dsh, 0).reshape(nsub, 8, S5N)
        si = pltpu.roll(hi, dsh, 0).reshape(nsub, 8, S5N)
        lr = tab_ref[idx, :, 0:S5N][None]
        li = tab_ref[idx, :, S5N:2 * S5N][None]
        nr, ni = _cmul_add(hr.reshape(nsub, 8, S5N), hi.reshape(nsub, 8, S5N), lr, li, sr, si)
        hr = nr.reshape(CB, S5N)
        hi = ni.reshape(CB, S5N)
    tcr = tab_ref[3, :, 0:S5N]
    tci = tab_ref[3, :, S5N:2 * S5N]
    cr = s5s_in[:, :, 0:S5N]
    ci = s5s_in[:, :, S5N:2 * S5N]
    tr, tim = _cmul_add(hr.reshape(nsub, 8, S5N), hi.reshape(nsub, 8, S5N), tcr[None], tci[None], cr, ci)
    h_scr[:, 0:S5N] = tr.reshape(CB, S5N)
    h_scr[:, S5N:2 * S5N] = tim.reshape(CB, S5N)
    sb = lax.broadcasted_iota(jnp.int32, (SEQ_PER_CB, CB), 0)
    st = lax.broadcasted_iota(jnp.int32, (SEQ_PER_CB, CB), 1)
    sel = (st == L_S * sb + (L_S - 1)).astype(BF16)
    s5s_out[...] = _split_dot_l(sel, h_scr[...], 3)
    os_ref[...] = _s5_readout(h_scr, u, ccat_ref, d_ref, wglu_ref, bglu_ref)


def _s5_sample(ps_s, prm, st):
    seqs = lambda shape: pl.BlockSpec((SEQ_PER_CB,) + shape, lambda i: (i,) + (0,) * len(shape))
    return pl.pallas_call(
        _s5_sample_body,
        grid=(NSC,),
        in_specs=[
            _mixer_blk(S5W), _const_spec((4, 8, 2 * S5N)), _const_spec((2, 2, S5W // 2, S5N // 2)),
            _const_spec((2, 2, S5N // 2, S5W // 2)), _const_spec((1, S5W)), _const_spec((S5W, S5W)),
            _const_spec((1, S5W)), seqs((1, 2 * S5N)),
        ],
        out_specs=[_mixer_blk(S5W), seqs((2 * S5N,))],
        out_shape=[jax.ShapeDtypeStruct((ROWS_S, S5W), F32), jax.ShapeDtypeStruct((NB_S, 2 * S5N), F32)],
        scratch_shapes=[pltpu.VMEM((CB, 2 * S5N), F32)],
        compiler_params=_cparams(("parallel",)),
        name="s5_sample",
    )(ps_s, prm["tab"], prm["bblk"], prm["ccat"], prm["d"], prm["wglu"], prm["bglu"], st)


def _ssd_block(prompt, p_ref, cw_ref, cb_ref, dtb_ref, aneg_ref, dx_ref, sg_ref, e64x_ref, e128x_ref,
               g192_ref, tril_ref, om_ref, cbuf, st_scr, i, ssp_out=None, convp_out=None, hist_ref=None,
               sss_in=None, sss_out=None, convs_out=None):
    z = p_ref[:, 0:SSW]
    xbc = p_ref[:, SSW:SSW + CONVC]
    dtr = p_ref[:, SSW + CONVC:P_SS]
    acc = cb_ref[...] + cw_ref[3:4, :] * xbc
    if prompt:
        cbuf[8:8 + CB, :] = xbc
        for k in (1, 2, 3):
            acc = acc + cw_ref[3 - k:4 - k, :] * cbuf[8 - k:8 - k + CB, :]
        cbuf[0:8, :] = cbuf[CB:CB + 8, :]
    else:
        convs_out[...] = xbc
        tl = lax.broadcasted_iota(jnp.int32, (CB, CONVC), 0) % L_S
        hist = hist_ref[...]
        for k in (1, 2, 3):
            hk = hist if k == 3 else pltpu.roll(hist, CB - (3 - k), 0)
            sh = jnp.where(tl >= k, pltpu.roll(xbc, k, 0), hk)
            acc = acc + cw_ref[3 - k:4 - k, :] * sh
    xc = _silu(acc)
    xs = xc[:, 0:SSW]
    bm = xc[:, SSW:SSW + 2 * HD]
    cm = xc[:, SSW + 2 * HD:CONVC]
    xdt = dtr + dtb_ref[...]
    dt = jnp.maximum(xdt, 0.0) + jnp.log1p(jnp.exp(-jnp.abs(xdt)))
    la = dt * aneg_ref[...]
    tril = tril_ref[...]
    b6 = _split_dot_l(tril, la, 3)
    bx = _split_dot(b6, e64x_ref[...], 3)
    bxw = _split_dot(b6, e128x_ref[...], 3)
    dtx = _split_dot(dt, e64x_ref[...], 2)
    bm_rep = _concat_heads([bm[:, 0:HD]] * 3 + [bm[:, HD:2 * HD]] * 3)
    cm_rep = _concat_heads([cm[:, 0:HD]] * 3 + [cm[:, HD:2 * HD]] * 3)
    kh = bm_rep * dtx
    qt = cm_rep * jnp.exp(bx)
    blast = bx[CB - 1:CB, :] if prompt else _seq_last_rows(bx)
    kd = kh * jnp.exp(blast - bx)
    kdt = kd.T
    mask = tril > 0
    parts = []
    for h in range(NH):
        sl = slice(HD * h, HD * (h + 1))
        bcol = bxw[:, 128 * h:128 * (h + 1)]
        decay = jnp.exp(jnp.where(mask, bcol - bcol.T, -1e30))
        sc = _bdot_nt(cm_rep[:, sl], kh[:, sl]) * decay
        oh = _bdot(sc, xs[:, sl])
        if prompt:
            st = st_scr[h]
            oh = oh + _bdot(qt[:, sl], st)
            st_scr[h] = st * jnp.exp(blast[:, sl]) + _bdot(kdt[sl, :], xs[:, sl])
        else:
            sts = sss_in[:, h].reshape(SEQ_PER_CB * HD, HD)
            oh = oh + _bdot(_seq_expand_lanes(qt[:, sl]), sts)
            dec3 = jnp.exp(blast[:, sl]).reshape(SEQ_PER_CB, L_S, HD)[:, L_S - 1:L_S, :]
            dec_rows = jnp.broadcast_to(dec3, (SEQ_PER_CB, HD, HD)).reshape(SEQ_PER_CB * HD, HD)
            upd = _bdot(_seq_expand_rows(kdt[sl, :]), xs[:, sl])
            sss_out[:, h] = (sts * dec_rows + upd).reshape(SEQ_PER_CB, HD, HD)
        parts.append(oh)
    if prompt:
        @pl.when(i % CH_PER_SEQ == CH_PER_SEQ - 1)
        def _():
            ssp_out[0] = st_scr[...]
            convp_out[0] = cbuf[0:8, :]
    y = (_concat_heads(parts) + dx_ref[...] * xs) * _silu(z)
    ms = _split_dot(y * y, g192_ref[...], 2) * (1.0 / (SSW // 2))
    om_ref[...] = y * lax.rsqrt(ms + RMS_EPS) * sg_ref[...]


def _ssd_body(p_ref, cw_ref, cb_ref, dtb_ref, aneg_ref, dx_ref, sg_ref, e64x_ref, e128x_ref, g192_ref,
              trilp_ref, trils_ref, hist_ref, sss_in, om_ref, ssp_out, sss_out, convp_out, convs_out,
              cbuf, st_scr):
    i = pl.program_id(0)
    common = (p_ref, cw_ref, cb_ref, dtb_ref, aneg_ref, dx_ref, sg_ref, e64x_ref, e128x_ref, g192_ref)

    @pl.when((i < NPC) & (i % CH_PER_SEQ == 0))
    def _():
        st_scr[...] = jnp.zeros_like(st_scr)
        cbuf[...] = jnp.zeros_like(cbuf)

    @pl.when(i < NPC)
    def _():
        _ssd_block(True, *common, trilp_ref, om_ref, cbuf, st_scr, i, ssp_out=ssp_out, convp_out=convp_out)

    @pl.when(i >= NPC)
    def _():
        _ssd_block(False, *common, trils_ref, om_ref, cbuf, st_scr, i, hist_ref=hist_ref,
                   sss_in=sss_in, sss_out=sss_out, convs_out=convs_out)


def _ssd(pm, prm, consts, hist, st):
    return pl.pallas_call(
        _ssd_body,
        grid=(NPC + NSC,),
        in_specs=[
            _mixer_blk(P_SS), _const_spec((4, CONVC)), _const_spec((1, CONVC)),
            _const_spec((1, 128)), _const_spec((1, 128)), _const_spec((1, SSW)), _const_spec((1, SSW)),
            _const_spec((128, SSW)), _const_spec((128, 2 * SSW)), _const_spec((SSW, SSW)),
            _const_spec((CB, CB)), _const_spec((CB, CB)),
            pl.BlockSpec((CB, CONVC), lambda i: (jnp.clip(i - NPC, 0, NSC - 1), 0)),
            _sstate_spec((NH, HD, HD)),
        ],
        out_specs=[_mixer_blk(SSW), _pstate_spec((NH, HD, HD)), _sstate_spec((NH, HD, HD)),
                   _pstate_spec((8, CONVC)),
                   pl.BlockSpec((CB, CONVC), lambda i: (jnp.clip(i - NPC, 0, NSC - 1), 0))],
        out_shape=[
            jax.ShapeDtypeStruct((ROWS, SSW), F32),
            jax.ShapeDtypeStruct((NB_P, NH, HD, HD), F32),
            jax.ShapeDtypeStruct((NB_S, NH, HD, HD), F32),
            jax.ShapeDtypeStruct((NB_P, 8, CONVC), F32),
            jax.ShapeDtypeStruct((ROWS_S, CONVC), F32),
        ],
        scratch_shapes=[pltpu.VMEM((CB + 8, CONVC), F32), pltpu.VMEM((NH, HD, HD), F32)],
        compiler_params=_cparams(("arbitrary",)),
        name="ssd",
    )(pm, prm["cw"], prm["cb"], prm["dtb"], prm["aneg"], prm["dx"], prm["sg"],
      consts["e64x"], consts["e128x"], consts["g192"], consts["trilp"], consts["trils"], hist, st)


def _o_core(i, oh_ref, os_ref, om_ref, x_ref, wo_ref, gp, gs, lng_ref, lnb_ref, scp, scs, shp, shs):
    mix = (_bdot(oh_ref[...], wo_ref[0:HW, :]) + _bdot(os_ref[...], wo_ref[HW:HW + S5W, :])
           + _bdot(om_ref[...], wo_ref[HW + S5W:D, :]))
    x1 = _layer_norm(ALPHA * x_ref[...] + _rowmod(i, gp, gs) * mix, lng_ref[...], lnb_ref[...])
    h2 = x1 * (1.0 + _rowmod(i, scp, scs)) + _rowmod(i, shp, shs)
    return x1, h2


def _o_body(oh_ref, os_ref, om_ref, x_ref, wo_ref, gp, gs, lng_ref, lnb_ref, scp, scs, shp, shs,
            x1_ref, h2_ref):
    i = pl.program_id(0)
    x1, h2 = _o_core(i, oh_ref, os_ref, om_ref, x_ref, wo_ref, gp, gs, lng_ref, lnb_ref, scp, scs, shp, shs)
    x1_ref[...] = x1
    h2_ref[...] = h2.astype(BF16)


def _o_router_body(oh_ref, os_ref, om_ref, x_ref, wo_ref, gp, gs, lng_ref, lnb_ref, scp, scs, shp, shs,
                   wr_ref, br_ref, x1_ref, h2_ref, route_ref):
    i = pl.program_id(0)
    x1, h2 = _o_core(i, oh_ref, os_ref, om_ref, x_ref, wo_ref, gp, gs, lng_ref, lnb_ref, scp, scs, shp, shs)
    x1_ref[...] = x1
    h2_ref[...] = h2
    h_hi = h2.astype(BF16)
    h_lo = (h2 - h_hi.astype(F32)).astype(BF16)
    logits = (jnp.dot(h_hi, wr_ref[0], preferred_element_type=F32)
              + jnp.dot(h_lo, wr_ref[0], preferred_element_type=F32)
              + jnp.dot(h_hi, wr_ref[1], preferred_element_type=F32)) + br_ref[...]
    lane = lax.broadcasted_iota(jnp.int32, (TM, 128), 1).astype(F32)
    neg = -jnp.inf
    lg = jnp.where(lane < NEXP, logits, neg)
    m1 = jnp.max(lg, axis=-1, keepdims=True)
    i1 = jnp.min(jnp.where(lg == m1, lane, 128.0), axis=-1, keepdims=True)
    lg2 = jnp.where(lane == i1, neg, lg)
    m2 = jnp.max(lg2, axis=-1, keepdims=True)
    i2 = jnp.min(jnp.where(lg2 == m2, lane, 128.0), axis=-1, keepdims=True)
    e2 = jnp.exp(m2 - m1)
    den = 1.0 + e2
    route_ref[...] = jnp.where(lane == 0.0, i1, jnp.where(lane == 1.0, i2,
                               jnp.where(lane == 2.0, 1.0 / den, jnp.where(lane == 3.0, e2 / den, 0.0))))


def _stage_o(layer, oh, os_, om, x, wo_b, ada4, ln_g, ln_b, router=None):
    in_specs = [
        _row_spec(HW), _row_spec(S5W), _row_spec(SSW), _row_spec(D), _const_spec((D, D)),
        *_mod_specs(layer, 2), _const_spec((1, D)), _const_spec((1, D)),
        *_mod_specs(layer, 4), *_mod_specs(layer, 3),
    ]
    args = [oh, os_, om, x, wo_b, ada4, ada4, ln_g, ln_b, ada4, ada4, ada4, ada4]
    out_specs = [_row_spec(D), _row_spec(D)]
    out_shape = [jax.ShapeDtypeStruct((ROWS, D), F32), jax.ShapeDtypeStruct((ROWS, D), BF16)]
    body = _o_body
    if router is not None:
        in_specs += [_const_spec((2, D, 128)), _const_spec((1, 128))]
        args += list(router)
        out_specs.append(_row_spec(128))
        out_shape[1] = jax.ShapeDtypeStruct((ROWS, D), F32)
        out_shape.append(jax.ShapeDtypeStruct((ROWS, 128), F32))
        body = _o_router_body
    return pl.pallas_call(
        body, grid=(NT,), in_specs=in_specs, out_specs=out_specs, out_shape=out_shape,
        compiler_params=_cparams(("parallel",)),
        name="stage_o_router" if router is not None else "stage_o",
    )(*args)


def _ffn_body(te_ref, nu_ref, h_ref, wg_ref, wu_ref, wd_ref, o_ref, acc_ref):
    i = pl.program_id(0)
    j = pl.program_id(1)

    @pl.when(j == 0)
    def _():
        acc_ref[...] = jnp.zeros_like(acc_ref)

    @pl.when(i < nu_ref[0])
    def _():
        h = h_ref[...].astype(BF16)
        g = jnp.dot(h, wg_ref[...], preferred_element_type=F32)
        u = jnp.dot(h, wu_ref[...], preferred_element_type=F32)
        act = (_silu(g) * u).astype(BF16)
        acc_ref[...] += jnp.dot(act, wd_ref[...], preferred_element_type=F32)

    @pl.when(j == pl.num_programs(1) - 1)
    def _():
        o_ref[...] = acc_ref[...]


def _ffn(tile_expert, n_used, h, wg, wu, wd):
    rows = h.shape[0]
    nj = DFF // TF

    def jblk(i, j, nu):
        return jnp.where(i < nu[0], j, nj - 1)

    grid_spec = pltpu.PrefetchScalarGridSpec(
        num_scalar_prefetch=2,
        grid=(rows // TMF, nj),
        in_specs=[
            pl.BlockSpec((TMF, D), lambda i, j, te, nu: (i, 0)),
            pl.BlockSpec((None, D, TF), lambda i, j, te, nu: (te[i], 0, jblk(i, j, nu))),
            pl.BlockSpec((None, D, TF), lambda i, j, te, nu: (te[i], 0, jblk(i, j, nu))),
            pl.BlockSpec((None, TF, D), lambda i, j, te, nu: (te[i], jblk(i, j, nu), 0)),
        ],
        out_specs=pl.BlockSpec((TMF, D), lambda i, j, te, nu: (i, 0)),
        scratch_shapes=[pltpu.VMEM((TMF, D), F32)],
    )
    return pl.pallas_call(
        _ffn_body, grid_spec=grid_spec,
        out_shape=jax.ShapeDtypeStruct((rows, D), F32),
        compiler_params=_cparams(("parallel", "arbitrary")),
        name="ffn",
    )(tile_expert, n_used, h, wg, wu, wd)


def _final_body(x1_ref, ya_ref, yb_ref, route_ref, gp, gs, lng_ref, lnb_ref, yp_ref, ys_ref):
    i = pl.program_id(0)
    f = route_ref[:, 2:3] * ya_ref[...] + route_ref[:, 3:4] * yb_ref[...]
    y = _layer_norm(ALPHA * x1_ref[...] + _rowmod(i, gp, gs) * f, lng_ref[...], lnb_ref[...])

    @pl.when(i < NPT)
    def _():
        yp_ref[...] = y

    @pl.when(i >= NPT)
    def _():
        ys_ref[...] = y


def _stage_final(layer, x1, ya, yb, route, ada4, ln_g, ln_b):
    return pl.pallas_call(
        _final_body,
        grid=(NT,),
        in_specs=[
            _row_spec(D), _row_spec(D), _row_spec(D), _row_spec(128),
            *_mod_specs(layer, 5), _const_spec((1, D)), _const_spec((1, D)),
        ],
        out_specs=[
            pl.BlockSpec((TM, D), lambda i: (jnp.minimum(i, NPT - 1), 0)),
            pl.BlockSpec((TM, D), lambda i: (jnp.clip(i - NPT, 0, NST - 1), 0)),
        ],
        out_shape=[jax.ShapeDtypeStruct((ROWS_P, D), F32), jax.ShapeDtypeStruct((ROWS_S, D), F32)],
        compiler_params=_cparams(("arbitrary",)),
        name="stage_final",
    )(x1, ya, yb, route, ada4, ada4, ln_g, ln_b)


def _block_ones(n, blk):
    r = jnp.arange(n) // blk
    return (r[:, None] == r[None, :]).astype(BF16)


def _consts():
    t = jnp.arange(CB)
    causal = t[:, None] >= t[None, :]
    same_seq = (t[:, None] // L_S) == (t[None, :] // L_S)
    h = jnp.arange(128)
    r = jnp.arange(S5_ROWS)
    perm = (r[None, :] == ((r % NB_P) * S5_TB + r // NB_P)[:, None]).astype(BF16)
    return {
        "perm": perm,
        "permt": perm.T,
        "e64": _block_ones(HW, HD),
        "g192": _block_ones(SSW, SSW // 2),
        "trilp": causal.astype(BF16),
        "trils": (causal & same_seq).astype(BF16),
        "e64x": (h[:, None] == (jnp.arange(SSW)[None, :] // HD)).astype(BF16),
        "e128x": (h[:, None] == (jnp.arange(2 * SSW)[None, :] // 128)).astype(BF16),
    }


def _s5_params(a_re, a_im, log_dt, b_re, b_im, c_re, c_im, d, w_glu, b_glu):
    dt = jnp.exp(log_dt)[:, None]
    mag = jnp.exp(a_re * dt)
    lam_re, lam_im = mag * jnp.cos(a_im * dt), mag * jnp.sin(a_im * dt)
    den = a_re * a_re + a_im * a_im
    nr, ni = lam_re - 1.0, lam_im
    zr = (nr * a_re + ni * a_im) / den
    zi = (ni * a_re - nr * a_im) / den
    bbar_re = zr[..., None] * b_re - zi[..., None] * b_im
    bbar_im = zr[..., None] * b_im + zi[..., None] * b_re
    eye = jnp.eye(16, dtype=F32)
    blk = lambda bb: jnp.einsum('gph,gk->ghkp', bb, eye).reshape(S5W, S5N)
    hu, hs = S5W // 2, S5N // 2
    bblk = jnp.stack([jnp.stack([blk(bb)[hu * j:hu * (j + 1), hs * j:hs * (j + 1)] for j in range(2)])
                      for bb in (bbar_re, bbar_im)]).astype(BF16)
    cblk = lambda cc: jnp.einsum('ghp,gk->gpkh', cc, eye).reshape(S5N, S5W)
    ccat = jnp.stack([jnp.stack([cblk(cc)[hs * j:hs * (j + 1), hu * j:hu * (j + 1)] for cc in (c_re, -c_im)])
                      for j in range(2)]).astype(BF16)
    lr, li = lam_re.reshape(-1), lam_im.reshape(-1)
    pows = [(jnp.ones_like(lr), jnp.zeros_like(li))]
    for _ in range(8):
        pr, pi = pows[-1]
        pows.append((pr * lr - pi * li, pr * li + pi * lr))
    rows = jnp.arange(8)[:, None]
    tabs = []
    for dsh in (1, 2, 4):
        pr, pi = pows[dsh]
        tabs.append(jnp.where(rows >= dsh, jnp.concatenate([pr, pi])[None, :], 0.0))
    tabs.append(jnp.stack([jnp.concatenate(pows[r + 1]) for r in range(8)]))
    return {
        "tab": jnp.stack(tabs), "bblk": bblk, "ccat": ccat,
        "lam8": jnp.broadcast_to(jnp.concatenate([lr, li])[None, :], (NB_P, 2 * S5N)),
        "d": d.reshape(1, S5W), "wglu": w_glu.astype(BF16), "bglu": b_glu.reshape(1, S5W),
    }


def _pad_lanes(v, n=128):
    return jnp.pad(v, (0, n - v.shape[0])).reshape(1, n)


def kernel(x_prompt, x_sample, c_prompt, c_sample, state_hgrn, state_s5, state_ssd, state_ssd_conv, w_ada, b_ada, ln_g, ln_b, w_in, w_out, hgrn_lb_logits, hgrn_norm_g, s5_a_re, s5_a_im, s5_log_dt, s5_b_re, s5_b_im, s5_c_re, s5_c_im, s5_d, s5_w_glu, s5_b_glu, ssd_conv_w, ssd_conv_b, ssd_dt_bias, ssd_a_log, ssd_d, ssd_norm_g, ffn_w_gate, ffn_w_up, ffn_w_down, moe_w_router, moe_b_router, moe_w_gate, moe_w_up, moe_w_down):
    consts = _consts()
    c_all = jnp.concatenate([c_sample, c_prompt], axis=0)
    ada4 = _ada(c_all, w_ada, b_ada).reshape(DEPTH, NB_S + NB_P, 1, 6 * D)

    lb_all = jnp.cumsum(jax.nn.softmax(hgrn_lb_logits, axis=0), axis=0)
    lb_all = lb_all - lb_all[0]

    xp = x_prompt.reshape(ROWS_P, D)
    xs = x_sample.reshape(ROWS_S, D)
    new_h, new_s5, new_m, new_c = [], [], [], []
    x1 = f = route = None
    for l in range(DEPTH):
        w_in_b = jnp.pad(w_in[l], ((0, 0), (0, N_IN_PAD - N_IN))).astype(BF16)
        if l == 0:
            x, ph, ps, pm = _stage_a0(xp, xs, ada4, w_in_b)
        else:
            x, ph, ps, pm = _stage_a1(l, x1, f, ada4, ln_g[l - 1, 1].reshape(1, D), ln_b[l - 1, 1].reshape(1, D),
                                      w_in_b)
        oh, hg_p, hg_s = _hgrn(ph, lb_all[l].reshape(1, HW), hgrn_norm_g[l].reshape(1, HW), consts,
                               jnp.swapaxes(state_hgrn[l], -1, -2))
        s5p = _s5_params(s5_a_re[l], s5_a_im[l], s5_log_dt[l], s5_b_re[l], s5_b_im[l], s5_c_re[l], s5_c_im[l],
                         s5_d[l], s5_w_glu[l], s5_b_glu[l])
        os_p, s5_p = _s5_prompt(ps[:ROWS_P].reshape(NB_P, L_P, S5W), s5p, consts)
        os_s, s5_s = _s5_sample(ps[ROWS_P:], s5p, state_s5[l].reshape(NB_S, 1, 2 * S5N))
        os_ = jnp.concatenate([os_p.reshape(ROWS_P, S5W), os_s], axis=0)
        ssd_prm = {
            "cw": ssd_conv_w[l], "cb": ssd_conv_b[l].reshape(1, CONVC),
            "dtb": _pad_lanes(ssd_dt_bias[l]), "aneg": _pad_lanes(-jnp.exp(ssd_a_log[l])),
            "dx": jnp.repeat(ssd_d[l], HD).reshape(1, SSW), "sg": ssd_norm_g[l].reshape(1, SSW),
        }
        hist = jnp.pad(state_ssd_conv[l], ((0, 0), (0, L_S - 3), (0, 0))).reshape(ROWS_S, CONVC)
        om, ss_p, ss_s, tail_p, xbc_s = _ssd(pm, ssd_prm, consts, hist, state_ssd[l])
        conv_p = tail_p[:, 8 - 3:]
        conv_s = xbc_s.reshape(NB_S, L_S, CONVC)[:, L_S - 3:]
        new_h.append((jnp.swapaxes(hg_p, -1, -2), jnp.swapaxes(hg_s, -1, -2)))
        new_s5.append((s5_p.reshape(NB_P, 2, 16, 64), s5_s.reshape(NB_S, 2, 16, 64)))
        new_m.append((ss_p, ss_s))
        new_c.append((conv_p, conv_s))

        wo_b = w_out[l].astype(BF16)
        lg, lbb = ln_g[l, 0].reshape(1, D), ln_b[l, 0].reshape(1, D)
        j = l // 2
        if l % 2 == 0:
            x1, h2 = _stage_o(l, oh, os_, om, x, wo_b, ada4, lg, lbb)
            f = _ffn(jnp.zeros((ROWS // TMF,), jnp.int32), jnp.full((1,), ROWS // TMF, jnp.int32), h2,
                     ffn_w_gate[j:j + 1].astype(BF16),
                     ffn_w_up[j:j + 1].astype(BF16), ffn_w_down[j:j + 1].astype(BF16))
        else:
            wr = jnp.pad(moe_w_router[j], ((0, 0), (0, 128 - NEXP)))
            wr_hi = wr.astype(BF16)
            wr = jnp.stack([wr_hi, (wr - wr_hi.astype(F32)).astype(BF16)])
            br = _pad_lanes(moe_b_router[j])
            x1, h2, route = _stage_o(l, oh, os_, om, x, wo_b, ada4, lg, lbb, router=(wr, br))
            flat_e = route[:, 0:2].astype(jnp.int32).reshape(-1)
            onehot = (flat_e[:, None] == jnp.arange(NEXP)[None, :]).astype(jnp.int32)
            csum = jnp.cumsum(onehot, axis=0)
            counts = csum[-1]
            rank = jnp.take_along_axis(csum, flat_e[:, None], axis=1)[:, 0] - 1
            padded = ((counts + TMF - 1) // TMF) * TMF
            pend = jnp.cumsum(padded)
            pstart = pend - padded
            dest = pstart[flat_e] + rank
            n_pad = 2 * ROWS + NEXP * TMF
            src_tok = jnp.zeros((n_pad,), jnp.int32).at[dest].set(jnp.arange(2 * ROWS, dtype=jnp.int32) // 2)
            tile_start = jnp.arange(n_pad // TMF, dtype=jnp.int32) * TMF
            tile_e = jnp.minimum(jnp.sum((pend[None, :] <= tile_start[:, None]).astype(jnp.int32), axis=1),
                                 NEXP - 1)
            n_used = (pend[NEXP - 1:NEXP] // TMF).astype(jnp.int32)
            y_sorted = _ffn(tile_e, n_used, h2[src_tok], moe_w_gate[j].astype(BF16), moe_w_up[j].astype(BF16),
                            moe_w_down[j].astype(BF16))
            pos = dest.reshape(ROWS, 2)
            ya = y_sorted[pos[:, 0]]
            yb = y_sorted[pos[:, 1]]
    y_p, y_s = _stage_final(DEPTH - 1, x1, ya, yb, route, ada4, ln_g[DEPTH - 1, 1].reshape(1, D),
                            ln_b[DEPTH - 1, 1].reshape(1, D))
    stack = lambda lst, k: jnp.stack([t[k] for t in lst])
    return (y_p.reshape(NB_P, L_P, D), y_s.reshape(NB_S, L_S, D),
            stack(new_h, 0), stack(new_s5, 0), stack(new_m, 0), stack(new_c, 0),
            stack(new_h, 1), stack(new_s5, 1), stack(new_m, 1), stack(new_c, 1))
```

```python
import functools
import math

import jax
import jax.numpy as jnp
import numpy as np
from jax import lax
from jax.experimental import pallas as pl
from jax.experimental.pallas import tpu as pltpu

F32 = jnp.float32
BF16 = jnp.bfloat16

D = 1024
NB_P, L_P = 8, 2048
NB_S, L_S = 128, 8
ROWS_P = NB_P * L_P
ROWS_S = NB_S * L_S
ROWS = ROWS_P + ROWS_S
DEPTH = 2
HW = 384
S5W = 256
SSW = 384
NH = 6
HD = 64
S5N = 1024
CONVC = 640
N_IN = 2822
N_IN_PAD = 2944
P_HG = 1536
P_SS = 1152
DFF = 2816
NEXP = 8
ALPHA = (2 * DEPTH) ** 0.25
LN_EPS = 1e-5
RMS_EPS = 1e-6

TM = 512
NPT = ROWS_P // TM
NST = ROWS_S // TM
NT = NPT + NST
SEQ_PER_TILE = TM // L_S
TILES_PER_SEQ = L_P // TM

CB = 128
NPC = ROWS_P // CB
NSC = ROWS_S // CB
CH_PER_SEQ = L_P // CB
SEQ_PER_CB = CB // L_S

TMF = 512
TF = 1408
MOE_CHUNKS = 2
MOE_ROWS = ROWS // MOE_CHUNKS
MOE_TILES = MOE_ROWS // TM
VMEM_LIMIT = 56 * 1024 * 1024


def _cparams(sem):
    return pltpu.CompilerParams(dimension_semantics=sem, vmem_limit_bytes=VMEM_LIMIT)


def _bdot(a, b):
    return jnp.dot(a.astype(BF16), b.astype(BF16), preferred_element_type=F32)


def _bdot_nt(a, b):
    return lax.dot_general(a.astype(BF16), b.astype(BF16), (((1,), (1,)), ((), ())),
                           preferred_element_type=F32)


def _split_dot(x, e, passes):
    acc = None
    r = x
    for _ in range(passes):
        hi = r.astype(BF16)
        d = jnp.dot(hi, e, preferred_element_type=F32)
        acc = d if acc is None else acc + d
        r = r - hi.astype(F32)
    return acc


def _split_dot_l(e, x, passes):
    acc = None
    r = x
    for _ in range(passes):
        hi = r.astype(BF16)
        d = jnp.dot(e, hi, preferred_element_type=F32)
        acc = d if acc is None else acc + d
        r = r - hi.astype(F32)
    return acc


def _silu(x):
    return x * jax.nn.sigmoid(x)


def _layer_norm(x, g, b):
    mu = jnp.mean(x, -1, keepdims=True)
    xc = x - mu
    var = jnp.mean(xc * xc, -1, keepdims=True)
    return xc * lax.rsqrt(var + LN_EPS) * g + b


def _rowmod(i, p_ref, s_ref):
    s = jnp.broadcast_to(s_ref[...], (SEQ_PER_TILE, L_S, D)).reshape(TM, D)
    return jnp.where(i < NPT, p_ref[0], s)


ADA_TN = 1536


def _ada_body(c_ref, w_ref, b_ref, o_ref):
    o_ref[...] = _bdot(_silu(c_ref[...]), w_ref[...]) + b_ref[...]


def _ada(c_all, w_ada, b_ada):
    nc = c_all.shape[0]
    return pl.pallas_call(
        _ada_body,
        grid=(DEPTH, 6 * D // ADA_TN),
        in_specs=[
            pl.BlockSpec((nc, D), lambda l, j: (0, 0)),
            pl.BlockSpec((None, D, ADA_TN), lambda l, j: (l, 0, j)),
            pl.BlockSpec((None, 1, ADA_TN), lambda l, j: (l, 0, j)),
        ],
        out_specs=pl.BlockSpec((None, nc, ADA_TN), lambda l, j: (l, 0, j)),
        out_shape=jax.ShapeDtypeStruct((DEPTH, nc, 6 * D), F32),
        compiler_params=_cparams(("parallel", "parallel")),
        name="ada",
    )(c_all, w_ada, b_ada.reshape(DEPTH, 1, 6 * D))


def _mod_specs(layer, k):
    ps = pl.BlockSpec((None, 1, 1, D),
                      lambda i: (layer, NB_S + jnp.minimum(i // TILES_PER_SEQ, NB_P - 1), 0, k))
    ss = pl.BlockSpec((None, SEQ_PER_TILE, 1, D),
                      lambda i: (layer, jnp.clip(i - NPT, 0, NST - 1), 0, k))
    return [ps, ss]


def _row_spec(width):
    return pl.BlockSpec((TM, width), lambda i: (i, 0))


def _const_spec(shape):
    nd = len(shape)
    return pl.BlockSpec(shape, lambda *_: (0,) * nd)


def _proj_out(x, i, scp, scs, shp, shs, w_ref, ph_ref, ps_ref, pm_ref):
    h = x * (1.0 + _rowmod(i, scp, scs)) + _rowmod(i, shp, shs)
    proj = jnp.dot(h.astype(BF16), w_ref[...], preferred_element_type=F32)
    ph_ref[...] = proj[:, 0:P_HG]
    ps_ref[...] = proj[:, P_HG:P_HG + S5W]
    pm_ref[...] = proj[:, P_HG + S5W:N_IN_PAD]


def _a0_body(xp_ref, xs_ref, scp, scs, shp, shs, w_ref, x_ref, ph_ref, ps_ref, pm_ref):
    i = pl.program_id(0)
    x = jnp.where(i < NPT, xp_ref[...], xs_ref[...])
    x_ref[...] = x
    _proj_out(x, i, scp, scs, shp, shs, w_ref, ph_ref, ps_ref, pm_ref)


def _a1_body(x1_ref, f_ref, gp, gs, lng_ref, lnb_ref, scp, scs, shp, shs, w_ref,
             x_ref, ph_ref, ps_ref, pm_ref):
    i = pl.program_id(0)
    x = _layer_norm(ALPHA * x1_ref[...] + _rowmod(i, gp, gs) * f_ref[...], lng_ref[...], lnb_ref[...])
    x_ref[...] = x
    _proj_out(x, i, scp, scs, shp, shs, w_ref, ph_ref, ps_ref, pm_ref)


def _a_out():
    specs = [_row_spec(D), _row_spec(P_HG), _row_spec(S5W), _row_spec(P_SS)]
    shapes = [jax.ShapeDtypeStruct((ROWS, w), F32) for w in (D, P_HG, S5W, P_SS)]
    return specs, shapes


def _stage_a0(xp, xs, ada4, w_in_b):
    out_specs, out_shape = _a_out()
    return pl.pallas_call(
        _a0_body,
        grid=(NT,),
        in_specs=[
            pl.BlockSpec((TM, D), lambda i: (jnp.minimum(i, NPT - 1), 0)),
            pl.BlockSpec((TM, D), lambda i: (jnp.clip(i - NPT, 0, NST - 1), 0)),
            *_mod_specs(0, 1), *_mod_specs(0, 0),
            _const_spec((D, N_IN_PAD)),
        ],
        out_specs=out_specs, out_shape=out_shape,
        compiler_params=_cparams(("parallel",)),
        name="stage_a0",
    )(xp, xs, ada4, ada4, ada4, ada4, w_in_b)


def _stage_a1(layer, x1, f, ada4, ln_g, ln_b, w_in_b):
    out_specs, out_shape = _a_out()
    return pl.pallas_call(
        _a1_body,
        grid=(NT,),
        in_specs=[
            _row_spec(D), _row_spec(D),
            *_mod_specs(layer - 1, 5),
            _const_spec((1, D)), _const_spec((1, D)),
            *_mod_specs(layer, 1), *_mod_specs(layer, 0),
            _const_spec((D, N_IN_PAD)),
        ],
        out_specs=out_specs, out_shape=out_shape,
        compiler_params=_cparams(("parallel",)),
        name="stage_a1",
    )(x1, f, ada4, ada4, ln_g, ln_b, ada4, ada4, ada4, ada4, w_in_b)


def _mixer_blk(width, nsub=1):
    return pl.BlockSpec((nsub * CB, width), lambda i: (i, 0))


def _pstate_spec(shape, nsub=1):
    nd = len(shape)
    return pl.BlockSpec((1,) + shape,
                        lambda i: (jnp.minimum(i // (CH_PER_SEQ // nsub), NB_P - 1),) + (0,) * nd)


def _sstate_spec(shape, nsub=1):
    nd = len(shape)
    return pl.BlockSpec((nsub * SEQ_PER_CB,) + shape,
                        lambda i: (jnp.clip(i - NPC // nsub, 0, NSC // nsub - 1),) + (0,) * nd)


def _seq_last_rows(x):
    w = x.shape[-1]
    x3 = x.reshape(SEQ_PER_CB, L_S, w)
    return jnp.broadcast_to(x3[:, L_S - 1:L_S, :], (SEQ_PER_CB, L_S, w)).reshape(CB, w)


def _concat_heads(parts):
    return jnp.concatenate(parts, axis=1)


def _stack_select(shape, row_div, lane_div):
    r = lax.broadcasted_iota(jnp.int32, shape, 0) // row_div
    c = lax.broadcasted_iota(jnp.int32, shape, 1) // lane_div
    return r == c


def _seq_expand_lanes(qh):
    q2 = jnp.concatenate([qh, qh], axis=1)
    q16 = jnp.concatenate([q2] * (SEQ_PER_CB // 2), axis=1)
    return jnp.where(_stack_select((CB, SEQ_PER_CB * HD), L_S, HD), q16, 0.0)


def _seq_expand_rows(xt):
    t = jnp.broadcast_to(xt[None], (SEQ_PER_CB, HD, CB)).reshape(SEQ_PER_CB * HD, CB)
    return jnp.where(_stack_select((SEQ_PER_CB * HD, CB), HD, L_S), t, 0.0)


def _fold_seq_lanes(full):
    acc = full[:, 0:128]
    for j in range(1, SEQ_PER_CB * HD // 128):
        acc = acc + full[:, 128 * j:128 * (j + 1)]
    return acc[:, 0:HD] + acc[:, HD:2 * HD]


HGRN_BASE = 32
EXP_RANGE_MAX = 80.0


def _hgrn_block(prompt, p_ref, lb_ref, hg_ref, e64_ref, tril_ref, oh_ref, st_scr, o_scr, i,
                stp_out=None, sts_in=None, sts_out=None):
    lb = lb_ref[...]
    qr = p_ref[:, 0:HW]
    fr = p_ref[:, HW:2 * HW]
    v = p_ref[:, 2 * HW:3 * HW]
    gr = p_ref[:, 3 * HW:4 * HW]
    ls = jnp.minimum(fr, 0.0) - jnp.log1p(jnp.exp(-jnp.abs(fr)))
    a = jnp.log(lb)
    bb = jnp.log1p(-lb) + ls
    lf = jnp.maximum(a, bb) + jnp.log1p(jnp.exp(-jnp.abs(a - bb)))
    kk = (1.0 - lb) * jax.nn.sigmoid(-fr)
    q = _silu(qr)
    b = _split_dot_l(tril_ref[...], lf, 3)

    e64 = e64_ref[...]
    ti = lax.broadcasted_iota(jnp.int32, (CB, CB), 0)
    si = lax.broadcasted_iota(jnp.int32, (CB, CB), 1)

    def diag8():
        nsub = CB // 8
        b3 = b.reshape(nsub, 8, HW)
        q3 = q.reshape(nsub, 8, HW)
        k3 = kk.reshape(nsub, 8, HW)
        v3 = v.reshape(nsub, 8, HW)
        r3 = lax.broadcasted_iota(jnp.int32, (nsub, 8, HW), 1)
        o = jnp.zeros((CB, HW), F32)
        for s in range(8):
            dlt = jnp.minimum(b3 - b3[:, s:s + 1, :], 0.0)
            w = jnp.where(r3 >= s, jnp.exp(dlt), 0.0) * q3 * k3[:, s:s + 1, :]
            hsum = jnp.dot(w.reshape(CB, HW).astype(BF16), e64, preferred_element_type=F32)
            o = o + hsum * jnp.broadcast_to(v3[:, s:s + 1, :], (nsub, 8, HW)).reshape(CB, HW)
        return o

    def level_terms(m):
        terms = []
        while m < CB:
            nb = CB // (2 * m)
            b4 = b.reshape(nb, 2 * m, HW)
            bmid = b4[:, m - 1:m, :]
            pos = lax.broadcasted_iota(jnp.int32, (nb, 2 * m, HW), 1)
            qq = jnp.where(pos >= m, q.reshape(nb, 2 * m, HW) * jnp.exp(jnp.minimum(b4 - bmid, 0.0)), 0.0)
            kq = jnp.where(pos < m, kk.reshape(nb, 2 * m, HW) * jnp.exp(jnp.minimum(bmid - b4, 0.0)), 0.0)
            terms.append((qq.reshape(CB, HW), kq.reshape(CB, HW), (ti // (2 * m)) == (si // (2 * m))))
            m *= 2
        return terms

    def scores_times_v(terms):
        parts = []
        for h in range(NH):
            sl = slice(HD * h, HD * (h + 1))
            sc = None
            for qq, kq, keep in terms:
                t = jnp.where(keep, _bdot_nt(qq[:, sl], kq[:, sl]), 0.0)
                sc = t if sc is None else sc + t
            parts.append(_bdot(sc, v[:, sl]))
        return _concat_heads(parts)

    base = HGRN_BASE if prompt else L_S
    nbase = CB // base
    bb3 = b.reshape(nbase, base, HW)
    top = bb3[:, 0:1, :] - lf.reshape(nbase, base, HW)[:, 0:1, :]
    decay_range = jnp.max(top - bb3[:, base - 1:base, :])
    fast = decay_range <= EXP_RANGE_MAX

    @pl.when(jnp.logical_not(fast))
    def _():
        o_scr[...] = diag8() + (scores_times_v(level_terms(8)) if prompt else 0.0)

    qf = (q.reshape(nbase, base, HW) * jnp.exp(bb3 - top)).reshape(CB, HW)
    kf = (kk.reshape(nbase, base, HW) * jnp.exp(top - bb3)).reshape(CB, HW)
    keep = ((ti // base) == (si // base)) & (si <= ti)
    o_fast = scores_times_v([(qf, kf, keep)] + (level_terms(base) if prompt else []))
    o = jnp.where(fast, o_fast, o_scr[...])
    qt = q * jnp.exp(b)
    if prompt:
        blast = b[CB - 1:CB, :]
        kd = kk * jnp.exp(blast - b)
        vt = v.T
        parts = []
        for h in range(NH):
            sl = slice(HD * h, HD * (h + 1))
            st = st_scr[h]
            oh = o[:, sl] + _bdot_nt(qt[:, sl], st)
            st_new = st * jnp.exp(blast[:, sl]) + _bdot(vt[sl, :], kd[:, sl])
            st_scr[h] = st_new
            parts.append(oh)

        @pl.when(i % CH_PER_SEQ == CH_PER_SEQ - 1)
        def _():
            stp_out[0] = st_scr[...]
    else:
        blast = _seq_last_rows(b)
        kd = kk * jnp.exp(blast - b)
        dec = jnp.exp(blast)
        vt = v.T
        parts = []
        for h in range(NH):
            sl = slice(HD * h, HD * (h + 1))
            sts = sts_in[:, h].reshape(SEQ_PER_CB * HD, HD)
            full = _bdot_nt(qt[:, sl], sts)
            sel = jnp.where(_stack_select((CB, SEQ_PER_CB * HD), L_S, HD), full, 0.0)
            parts.append(o[:, sl] + _fold_seq_lanes(sel))
            dec3 = dec[:, sl].reshape(SEQ_PER_CB, L_S, HD)[:, L_S - 1:L_S, :]
            dec_rows = jnp.broadcast_to(dec3, (SEQ_PER_CB, HD, HD)).reshape(SEQ_PER_CB * HD, HD)
            upd = _bdot(_seq_expand_rows(vt[sl, :]), kd[:, sl])
            sts_out[:, h] = (sts * dec_rows + upd).reshape(SEQ_PER_CB, HD, HD)
    oall = _concat_heads(parts)
    ms = _split_dot(oall * oall, e64, 1) * (1.0 / HD)
    oh_ref[...] = oall * lax.rsqrt(ms + RMS_EPS) * hg_ref[...] * _silu(gr)


def _hgrn_body(p_ref, lb_ref, hg_ref, e64_ref, trilp_ref, trils_ref, sts_in,
               oh_ref, stp_out, sts_out, st_scr, o_scr):
    i = pl.program_id(0)

    @pl.when(i == 0)
    def _():
        o_scr[...] = jnp.zeros_like(o_scr)

    @pl.when((i < NPC) & (i % CH_PER_SEQ == 0))
    def _():
        st_scr[...] = jnp.zeros_like(st_scr)

    @pl.when(i < NPC)
    def _():
        _hgrn_block(True, p_ref, lb_ref, hg_ref, e64_ref, trilp_ref, oh_ref, st_scr, o_scr, i,
                    stp_out=stp_out)

    @pl.when(i >= NPC)
    def _():
        _hgrn_block(False, p_ref, lb_ref, hg_ref, e64_ref, trils_ref, oh_ref, st_scr, o_scr, i,
                    sts_in=sts_in, sts_out=sts_out)


def _hgrn(ph, lb, hg, consts, st_t):
    return pl.pallas_call(
        _hgrn_body,
        grid=(NPC + NSC,),
        in_specs=[
            _mixer_blk(P_HG), _const_spec((1, HW)), _const_spec((1, HW)),
            _const_spec((HW, HW)), _const_spec((CB, CB)), _const_spec((CB, CB)),
            _sstate_spec((NH, HD, HD)),
        ],
        out_specs=[_mixer_blk(HW), _pstate_spec((NH, HD, HD)), _sstate_spec((NH, HD, HD))],
        out_shape=[
            jax.ShapeDtypeStruct((ROWS, HW), F32),
            jax.ShapeDtypeStruct((NB_P, NH, HD, HD), F32),
            jax.ShapeDtypeStruct((NB_S, NH, HD, HD), F32),
        ],
        scratch_shapes=[pltpu.VMEM((NH, HD, HD), F32), pltpu.VMEM((CB, HW), F32)],
        compiler_params=_cparams(("arbitrary",)),
        name="hgrn",
    )(ph, lb, hg, consts["e64"], consts["trilp"], consts["trils"], st_t)


def _cmul_add(hr, hi, lr, li, sr, si):
    return hr + lr * sr - li * si, hi + lr * si + li * sr


def _s5_project(ub, bblk_ref):
    halves = [ub[:, (S5W // 2) * j:(S5W // 2) * (j + 1)] for j in range(2)]
    hr = jnp.concatenate([jnp.dot(halves[j], bblk_ref[0, j], preferred_element_type=F32) for j in range(2)], axis=1)
    hi = jnp.concatenate([jnp.dot(halves[j], bblk_ref[1, j], preferred_element_type=F32) for j in range(2)], axis=1)
    return hr, hi


def _s5_readout(h_scr, u, ccat_ref, d_ref, wglu_ref, bglu_ref):
    hs = S5N // 2
    ch = [_bdot(h_scr[:, hs * j:hs * (j + 1)], ccat_ref[j, 0])
          + _bdot(h_scr[:, S5N + hs * j:S5N + hs * (j + 1)], ccat_ref[j, 1]) for j in range(2)]
    y = jnp.concatenate(ch, axis=1) + d_ref[...] * u
    c0 = math.sqrt(2.0 / math.pi)
    y = y * (0.5 * (1.0 + jnp.tanh(c0 * (y + 0.044715 * (y * y * y)))))
    return y * jax.nn.sigmoid(_bdot(y, wglu_ref[...]) + bglu_ref[...])


S5_TB = 32
S5_ROWS = NB_P * S5_TB


def _s5_prompt_body(p_ref, perm_ref, permt_ref, lam_ref, bblk_ref, ccat_ref, d_ref, wglu_ref, bglu_ref,
                    os_ref, st_out, carry_scr, h_scr):
    i = pl.program_id(0)

    @pl.when(i == 0)
    def _():
        carry_scr[...] = jnp.zeros_like(carry_scr)

    u = p_ref[...].reshape(S5_ROWS, S5W)
    u_hi = u.astype(BF16)
    u_lo = (u - u_hi.astype(F32)).astype(BF16)
    perm = perm_ref[...]
    up_hi = jnp.dot(perm, u_hi, preferred_element_type=F32)
    up = up_hi + jnp.dot(perm, u_lo, preferred_element_type=F32)
    hr, hi = _s5_project(up_hi.astype(BF16), bblk_ref)
    lr = lam_ref[:, 0:S5N]
    li = lam_ref[:, S5N:2 * S5N]
    cr = carry_scr[:, 0:S5N]
    ci = carry_scr[:, S5N:2 * S5N]
    for t in range(S5_TB):
        rows = slice(NB_P * t, NB_P * (t + 1))
        cr, ci = _cmul_add(hr[rows], hi[rows], lr, li, cr, ci)
        h_scr[rows, 0:S5N] = cr
        h_scr[rows, S5N:2 * S5N] = ci
    carry_scr[:, 0:S5N] = cr
    carry_scr[:, S5N:2 * S5N] = ci
    out = _s5_readout(h_scr, up, ccat_ref, d_ref, wglu_ref, bglu_ref)
    os_ref[...] = jnp.dot(permt_ref[...], out.astype(BF16), preferred_element_type=F32).reshape(NB_P, S5_TB, S5W)

    @pl.when(i == pl.num_programs(0) - 1)
    def _():
        st_out[...] = carry_scr[...]


def _s5_prompt(ps3, prm, consts):
    return pl.pallas_call(
        _s5_prompt_body,
        grid=(L_P // S5_TB,),
        in_specs=[
            pl.BlockSpec((NB_P, S5_TB, S5W), lambda i: (0, i, 0)),
            _const_spec((S5_ROWS, S5_ROWS)), _const_spec((S5_ROWS, S5_ROWS)), _const_spec((NB_P, 2 * S5N)),
            _const_spec((2, 2, S5W // 2, S5N // 2)), _const_spec((2, 2, S5N // 2, S5W // 2)),
            _const_spec((1, S5W)), _const_spec((S5W, S5W)), _const_spec((1, S5W)),
        ],
        out_specs=[pl.BlockSpec((NB_P, S5_TB, S5W), lambda i: (0, i, 0)), _const_spec((NB_P, 2 * S5N))],
        out_shape=[jax.ShapeDtypeStruct((NB_P, L_P, S5W), F32), jax.ShapeDtypeStruct((NB_P, 2 * S5N), F32)],
        scratch_shapes=[pltpu.VMEM((NB_P, 2 * S5N), F32), pltpu.VMEM((S5_ROWS, 2 * S5N), F32)],
        compiler_params=_cparams(("arbitrary",)),
        name="s5_prompt",
    )(ps3, consts["perm"], consts["permt"], prm["lam8"], prm["bblk"], prm["ccat"], prm["d"], prm["wglu"],
      prm["bglu"])


def _s5_sample_body(p_ref, tab_ref, bblk_ref, ccat_ref, d_ref, wglu_ref, bglu_ref, s5s_in,
                    os_ref, s5s_out, h_scr):
    u = p_ref[...]
    hr, hi = _s5_project(u.astype(BF16), bblk_ref)
    nsub = CB // 8
    for idx, dsh in enumerate((1, 2, 4)):
        sr = pltpu.roll(hr, dsh, 0).reshape(nsub, 8, S5N)
        si = pltpu.roll(hi, dsh, 0).reshape(nsub, 8, S5N)
        lr = tab_ref[idx, :, 0:S5N][None]
        li = tab_ref[idx, :, S5N:2 * S5N][None]
        nr, ni = _cmul_add(hr.reshape(nsub, 8, S5N), hi.reshape(nsub, 8, S5N), lr, li, sr, si)
        hr = nr.reshape(CB, S5N)
        hi = ni.reshape(CB, S5N)
    tcr = tab_ref[3, :, 0:S5N]
    tci = tab_ref[3, :, S5N:2 * S5N]
    cr = s5s_in[:, :, 0:S5N]
    ci = s5s_in[:, :, S5N:2 * S5N]
    tr, tim = _cmul_add(hr.reshape(nsub, 8, S5N), hi.reshape(nsub, 8, S5N), tcr[None], tci[None], cr, ci)
    h_scr[:, 0:S5N] = tr.reshape(CB, S5N)
    h_scr[:, S5N:2 * S5N] = tim.reshape(CB, S5N)
    sb = lax.broadcasted_iota(jnp.int32, (SEQ_PER_CB, CB), 0)
    st = lax.broadcasted_iota(jnp.int32, (SEQ_PER_CB, CB), 1)
    sel = (st == L_S * sb + (L_S - 1)).astype(BF16)
    s5s_out[...] = _split_dot_l(sel, h_scr[...], 3)
    os_ref[...] = _s5_readout(h_scr, u, ccat_ref, d_ref, wglu_ref, bglu_ref)


def _s5_sample(ps_s, prm, st):
    seqs = lambda shape: pl.BlockSpec((SEQ_PER_CB,) + shape, lambda i: (i,) + (0,) * len(shape))
    return pl.pallas_call(
        _s5_sample_body,
        grid=(NSC,),
        in_specs=[
            _mixer_blk(S5W), _const_spec((4, 8, 2 * S5N)), _const_spec((2, 2, S5W // 2, S5N // 2)),
            _const_spec((2, 2, S5N // 2, S5W // 2)), _const_spec((1, S5W)), _const_spec((S5W, S5W)),
            _const_spec((1, S5W)), seqs((1, 2 * S5N)),
        ],
        out_specs=[_mixer_blk(S5W), seqs((2 * S5N,))],
        out_shape=[jax.ShapeDtypeStruct((ROWS_S, S5W), F32), jax.ShapeDtypeStruct((NB_S, 2 * S5N), F32)],
        scratch_shapes=[pltpu.VMEM((CB, 2 * S5N), F32)],
        compiler_params=_cparams(("parallel",)),
        name="s5_sample",
    )(ps_s, prm["tab"], prm["bblk"], prm["ccat"], prm["d"], prm["wglu"], prm["bglu"], st)


def _ssd_block(prompt, p_ref, cw_ref, cb_ref, dtb_ref, aneg_ref, dx_ref, sg_ref,
               g192_ref, tril_ref, om_ref, cbuf, st_scr, sub, is_last=None, ssp_out=None, convp_out=None,
               hist_ref=None, sss_in=None, sss_out=None, convs_out=None):
    rows = slice(sub * CB, (sub + 1) * CB)
    seqs = slice(sub * SEQ_PER_CB, (sub + 1) * SEQ_PER_CB)
    z = p_ref[rows, 0:SSW]
    xbc = p_ref[rows, SSW:SSW + CONVC]
    dtr = p_ref[rows, SSW + CONVC:P_SS]
    acc = cb_ref[...] + cw_ref[3:4, :] * xbc
    if prompt:
        cbuf[8:8 + CB, :] = xbc
        for k in (1, 2, 3):
            acc = acc + cw_ref[3 - k:4 - k, :] * cbuf[8 - k:8 - k + CB, :]
        cbuf[0:8, :] = cbuf[CB:CB + 8, :]
    else:
        convs_out[rows, :] = xbc
        tl = lax.broadcasted_iota(jnp.int32, (CB, CONVC), 0) % L_S
        hist = hist_ref[rows, :]
        for k in (1, 2, 3):
            hk = hist if k == 3 else pltpu.roll(hist, CB - (3 - k), 0)
            sh = jnp.where(tl >= k, pltpu.roll(xbc, k, 0), hk)
            acc = acc + cw_ref[3 - k:4 - k, :] * sh
    xc = _silu(acc)
    xs = xc[:, 0:SSW]
    bm = xc[:, SSW:SSW + 2 * HD]
    cm = xc[:, SSW + 2 * HD:CONVC]
    xdt = dtr + dtb_ref[...]
    dt = jnp.maximum(xdt, 0.0) + jnp.log1p(jnp.exp(-jnp.abs(xdt)))
    la = dt * aneg_ref[...]
    tril = tril_ref[...]
    b6 = _split_dot_l(tril, la, 3)
    lane = lax.broadcasted_iota(jnp.int32, (CB, 128), 1)
    bxw = [jnp.broadcast_to(b6[:, h:h + 1], (CB, 128)) for h in range(NH)]
    dtw = [jnp.broadcast_to(dt[:, h:h + 1], (CB, 128)) for h in range(NH)]
    pair = lambda cols: _concat_heads([jnp.where(lane < HD, cols[2 * j], cols[2 * j + 1]) for j in range(NH // 2)])
    bx = pair(bxw)
    dtx = pair(dtw)
    bm_rep = _concat_heads([bm[:, 0:HD]] * 3 + [bm[:, HD:2 * HD]] * 3)
    cm_rep = _concat_heads([cm[:, 0:HD]] * 3 + [cm[:, HD:2 * HD]] * 3)
    kh = bm_rep * dtx
    qt = cm_rep * jnp.exp(bx)
    blast = bx[CB - 1:CB, :] if prompt else _seq_last_rows(bx)
    kd = kh * jnp.exp(blast - bx)
    kdt = kd.T
    mask = tril > 0
    parts = []
    for h in range(NH):
        sl = slice(HD * h, HD * (h + 1))
        bcol = bxw[h]
        decay = jnp.exp(jnp.where(mask, bcol - bcol.T, -1e30))
        sc = _bdot_nt(cm_rep[:, sl], kh[:, sl]) * decay
        oh = _bdot(sc, xs[:, sl])
        if prompt:
            st = st_scr[h]
            oh = oh + _bdot(qt[:, sl], st)
            st_scr[h] = st * jnp.exp(blast[:, sl]) + _bdot(kdt[sl, :], xs[:, sl])
        else:
            sts = sss_in[seqs, h].reshape(SEQ_PER_CB * HD, HD)
            oh = oh + _bdot(_seq_expand_lanes(qt[:, sl]), sts)
            dec3 = jnp.exp(blast[:, sl]).reshape(SEQ_PER_CB, L_S, HD)[:, L_S - 1:L_S, :]
            dec_rows = jnp.broadcast_to(dec3, (SEQ_PER_CB, HD, HD)).reshape(SEQ_PER_CB * HD, HD)
            upd = _bdot(_seq_expand_rows(kdt[sl, :]), xs[:, sl])
            sss_out[seqs, h] = (sts * dec_rows + upd).reshape(SEQ_PER_CB, HD, HD)
        parts.append(oh)
    if is_last is not None:
        @pl.when(is_last)
        def _():
            ssp_out[0] = st_scr[...]
            convp_out[0] = cbuf[0:8, :]
    y = (_concat_heads(parts) + dx_ref[...] * xs) * _silu(z)
    ms = _split_dot(y * y, g192_ref[...], 1) * (1.0 / (SSW // 2))
    om_ref[rows, :] = y * lax.rsqrt(ms + RMS_EPS) * sg_ref[...]


def _ssd_body(p_ref, cw_ref, cb_ref, dtb_ref, aneg_ref, dx_ref, sg_ref, g192_ref,
              trilp_ref, trils_ref, hist_ref, sss_in, om_ref, ssp_out, sss_out, convp_out, convs_out,
              cbuf, st_scr):
    i = pl.program_id(0)
    common = (p_ref, cw_ref, cb_ref, dtb_ref, aneg_ref, dx_ref, sg_ref, g192_ref)
    npc, steps_per_seq = NPC // SSD_SUB, CH_PER_SEQ // SSD_SUB

    @pl.when((i < npc) & (i % steps_per_seq == 0))
    def _():
        st_scr[...] = jnp.zeros_like(st_scr)
        cbuf[...] = jnp.zeros_like(cbuf)

    @pl.when(i < npc)
    def _():
        for sub in range(SSD_SUB):
            is_last = (i % steps_per_seq == steps_per_seq - 1) if sub == SSD_SUB - 1 else None
            _ssd_block(True, *common, trilp_ref, om_ref, cbuf, st_scr, sub, is_last=is_last,
                       ssp_out=ssp_out, convp_out=convp_out)

    @pl.when(i >= npc)
    def _():
        for sub in range(SSD_SUB):
            _ssd_block(False, *common, trils_ref, om_ref, cbuf, st_scr, sub, hist_ref=hist_ref,
                       sss_in=sss_in, sss_out=sss_out, convs_out=convs_out)


SSD_SUB = 2


def _ssd(pm, prm, consts, hist, st):
    sample_rows = pl.BlockSpec((SSD_SUB * CB, CONVC),
                               lambda i: (jnp.clip(i - NPC // SSD_SUB, 0, NSC // SSD_SUB - 1), 0))
    return pl.pallas_call(
        _ssd_body,
        grid=((NPC + NSC) // SSD_SUB,),
        in_specs=[
            _mixer_blk(P_SS, SSD_SUB), _const_spec((4, CONVC)), _const_spec((1, CONVC)),
            _const_spec((1, 128)), _const_spec((1, 128)), _const_spec((1, SSW)), _const_spec((1, SSW)),
            _const_spec((SSW, SSW)),
            _const_spec((CB, CB)), _const_spec((CB, CB)),
            sample_rows,
            _sstate_spec((NH, HD, HD), SSD_SUB),
        ],
        out_specs=[_mixer_blk(SSW, SSD_SUB), _pstate_spec((NH, HD, HD), SSD_SUB),
                   _sstate_spec((NH, HD, HD), SSD_SUB), _pstate_spec((8, CONVC), SSD_SUB), sample_rows],
        out_shape=[
            jax.ShapeDtypeStruct((ROWS, SSW), F32),
            jax.ShapeDtypeStruct((NB_P, NH, HD, HD), F32),
            jax.ShapeDtypeStruct((NB_S, NH, HD, HD), F32),
            jax.ShapeDtypeStruct((NB_P, 8, CONVC), F32),
            jax.ShapeDtypeStruct((ROWS_S, CONVC), F32),
        ],
        scratch_shapes=[pltpu.VMEM((CB + 8, CONVC), F32), pltpu.VMEM((NH, HD, HD), F32)],
        compiler_params=_cparams(("arbitrary",)),
        name="ssd",
    )(pm, prm["cw"], prm["cb"], prm["dtb"], prm["aneg"], prm["dx"], prm["sg"],
      consts["g192"], consts["trilp"], consts["trils"], hist, st)


def _o_core(i, oh_ref, os_ref, om_ref, x_ref, wo_ref, gp, gs, lng_ref, lnb_ref, scp, scs, shp, shs):
    mix = (_bdot(oh_ref[...], wo_ref[0:HW, :]) + _bdot(os_ref[...], wo_ref[HW:HW + S5W, :])
           + _bdot(om_ref[...], wo_ref[HW + S5W:D, :]))
    x1 = _layer_norm(ALPHA * x_ref[...] + _rowmod(i, gp, gs) * mix, lng_ref[...], lnb_ref[...])
    h2 = x1 * (1.0 + _rowmod(i, scp, scs)) + _rowmod(i, shp, shs)
    return x1, h2


def _o_body(oh_ref, os_ref, om_ref, x_ref, wo_ref, gp, gs, lng_ref, lnb_ref, scp, scs, shp, shs,
            x1_ref, h2_ref):
    i = pl.program_id(0)
    x1, h2 = _o_core(i, oh_ref, os_ref, om_ref, x_ref, wo_ref, gp, gs, lng_ref, lnb_ref, scp, scs, shp, shs)
    x1_ref[...] = x1
    h2_ref[...] = h2.astype(BF16)


def _o_router_body(oh_ref, os_ref, om_ref, x_ref, wo_ref, gp, gs, lng_ref, lnb_ref, scp, scs, shp, shs,
                   wr_ref, br_ref, x1_ref, h2_ref, route_ref):
    i = pl.program_id(0)
    x1, h2 = _o_core(i, oh_ref, os_ref, om_ref, x_ref, wo_ref, gp, gs, lng_ref, lnb_ref, scp, scs, shp, shs)
    x1_ref[...] = x1
    h2_ref[...] = h2
    h_hi = h2.astype(BF16)
    h_lo = (h2 - h_hi.astype(F32)).astype(BF16)
    logits = (jnp.dot(h_hi, wr_ref[0], preferred_element_type=F32)
              + jnp.dot(h_lo, wr_ref[0], preferred_element_type=F32)
              + jnp.dot(h_hi, wr_ref[1], preferred_element_type=F32)) + br_ref[...]
    lane = lax.broadcasted_iota(jnp.int32, (TM, 128), 1).astype(F32)
    neg = -jnp.inf
    lg = jnp.where(lane < NEXP, logits, neg)
    m1 = jnp.max(lg, axis=-1, keepdims=True)
    i1 = jnp.min(jnp.where(lg == m1, lane, 128.0), axis=-1, keepdims=True)
    lg2 = jnp.where(lane == i1, neg, lg)
    m2 = jnp.max(lg2, axis=-1, keepdims=True)
    i2 = jnp.min(jnp.where(lg2 == m2, lane, 128.0), axis=-1, keepdims=True)
    e2 = jnp.exp(m2 - m1)
    den = 1.0 + e2
    route_ref[...] = jnp.where(lane == 0.0, i1, jnp.where(lane == 1.0, i2,
                               jnp.where(lane == 2.0, 1.0 / den, jnp.where(lane == 3.0, e2 / den, 0.0))))


def _stage_o(layer, oh, os_, om, x, wo_b, ada4, ln_g, ln_b, router=None):
    in_specs = [
        _row_spec(HW), _row_spec(S5W), _row_spec(SSW), _row_spec(D), _const_spec((D, D)),
        *_mod_specs(layer, 2), _const_spec((1, D)), _const_spec((1, D)),
        *_mod_specs(layer, 4), *_mod_specs(layer, 3),
    ]
    args = [oh, os_, om, x, wo_b, ada4, ada4, ln_g, ln_b, ada4, ada4, ada4, ada4]
    out_specs = [_row_spec(D), _row_spec(D)]
    out_shape = [jax.ShapeDtypeStruct((ROWS, D), F32), jax.ShapeDtypeStruct((ROWS, D), BF16)]
    body = _o_body
    if router is not None:
        in_specs += [_const_spec((2, D, 128)), _const_spec((1, 128))]
        args += list(router)
        out_specs.append(_row_spec(128))
        out_shape[1] = jax.ShapeDtypeStruct((ROWS, D), F32)
        out_shape.append(jax.ShapeDtypeStruct((ROWS, 128), F32))
        body = _o_router_body
    return pl.pallas_call(
        body, grid=(NT,), in_specs=in_specs, out_specs=out_specs, out_shape=out_shape,
        compiler_params=_cparams(("parallel",)),
        name="stage_o_router" if router is not None else "stage_o",
    )(*args)


def _ffn_body(te_ref, nu_ref, h_ref, wg_ref, wu_ref, wd_ref, o_ref, acc_ref):
    i = pl.program_id(0)
    j = pl.program_id(1)

    @pl.when(j == 0)
    def _():
        acc_ref[...] = jnp.zeros_like(acc_ref)

    @pl.when(i < nu_ref[0])
    def _():
        h = h_ref[...].astype(BF16)
        g = jnp.dot(h, wg_ref[...], preferred_element_type=F32)
        u = jnp.dot(h, wu_ref[...], preferred_element_type=F32)
        act = (_silu(g) * u).astype(BF16)
        acc_ref[...] += jnp.dot(act, wd_ref[...], preferred_element_type=F32)

    @pl.when(j == pl.num_programs(1) - 1)
    def _():
        o_ref[...] = acc_ref[...]


def _ffn(tile_expert, n_used, h, wg, wu, wd):
    rows = h.shape[0]
    nj = DFF // TF

    def jblk(i, j, nu):
        return jnp.where(i < nu[0], j, nj - 1)

    grid_spec = pltpu.PrefetchScalarGridSpec(
        num_scalar_prefetch=2,
        grid=(rows // TMF, nj),
        in_specs=[
            pl.BlockSpec((TMF, D), lambda i, j, te, nu: (i, 0)),
            pl.BlockSpec((None, D, TF), lambda i, j, te, nu: (te[i], 0, jblk(i, j, nu))),
            pl.BlockSpec((None, D, TF), lambda i, j, te, nu: (te[i], 0, jblk(i, j, nu))),
            pl.BlockSpec((None, TF, D), lambda i, j, te, nu: (te[i], jblk(i, j, nu), 0)),
        ],
        out_specs=pl.BlockSpec((TMF, D), lambda i, j, te, nu: (i, 0)),
        scratch_shapes=[pltpu.VMEM((TMF, D), F32)],
    )
    return pl.pallas_call(
        _ffn_body, grid_spec=grid_spec,
        out_shape=jax.ShapeDtypeStruct((rows, D), F32),
        compiler_params=_cparams(("parallel", "arbitrary")),
        name="ffn",
    )(tile_expert, n_used, h, wg, wu, wd)


def _final_body(x1_ref, *refs):
    ya_refs = refs[0:MOE_CHUNKS]
    yb_refs = refs[MOE_CHUNKS:2 * MOE_CHUNKS]
    route_ref, gp, gs, lng_ref, lnb_ref, yp_ref, ys_ref = refs[2 * MOE_CHUNKS:]
    i = pl.program_id(0)
    ya = ya_refs[0][...]
    yb = yb_refs[0][...]
    for c in range(1, MOE_CHUNKS):
        ya = jnp.where(i >= c * MOE_TILES, ya_refs[c][...], ya)
        yb = jnp.where(i >= c * MOE_TILES, yb_refs[c][...], yb)
    f = route_ref[:, 2:3] * ya + route_ref[:, 3:4] * yb
    y = _layer_norm(ALPHA * x1_ref[...] + _rowmod(i, gp, gs) * f, lng_ref[...], lnb_ref[...])

    @pl.when(i < NPT)
    def _():
        yp_ref[...] = y

    @pl.when(i >= NPT)
    def _():
        ys_ref[...] = y


def _stage_final(layer, x1, ya, yb, route, ada4, ln_g, ln_b):
    def chunk_spec(c):
        return pl.BlockSpec((TM, D), lambda i: (jnp.clip(i - c * MOE_TILES, 0, MOE_TILES - 1), 0))

    chunk_specs = [chunk_spec(c) for c in range(MOE_CHUNKS)]
    return pl.pallas_call(
        _final_body,
        grid=(NT,),
        in_specs=[
            _row_spec(D), *chunk_specs, *chunk_specs, _row_spec(128),
            *_mod_specs(layer, 5), _const_spec((1, D)), _const_spec((1, D)),
        ],
        out_specs=[
            pl.BlockSpec((TM, D), lambda i: (jnp.minimum(i, NPT - 1), 0)),
            pl.BlockSpec((TM, D), lambda i: (jnp.clip(i - NPT, 0, NST - 1), 0)),
        ],
        out_shape=[jax.ShapeDtypeStruct((ROWS_P, D), F32), jax.ShapeDtypeStruct((ROWS_S, D), F32)],
        compiler_params=_cparams(("arbitrary",)),
        name="stage_final",
    )(x1, *ya, *yb, route, ada4, ada4, ln_g, ln_b)


def _block_ones(n, blk):
    r = np.arange(n) // blk
    return r[:, None] == r[None, :]


def _consts():
    t = np.arange(CB)
    causal = t[:, None] >= t[None, :]
    same_seq = (t[:, None] // L_S) == (t[None, :] // L_S)
    r = np.arange(S5_ROWS)
    perm = r[None, :] == ((r % NB_P) * S5_TB + r // NB_P)[:, None]
    mats = {
        "perm": perm,
        "permt": perm.T,
        "e64": _block_ones(HW, HD),
        "g192": _block_ones(SSW, SSW // 2),
        "trilp": causal,
        "trils": causal & same_seq,
    }
    return {k: jnp.asarray(v.astype(np.float32), dtype=BF16) for k, v in mats.items()}


def _s5_params(a_re, a_im, log_dt, b_re, b_im, c_re, c_im, d, w_glu, b_glu):
    dt = jnp.exp(log_dt)[:, None]
    mag = jnp.exp(a_re * dt)
    lam_re, lam_im = mag * jnp.cos(a_im * dt), mag * jnp.sin(a_im * dt)
    den = a_re * a_re + a_im * a_im
    nr, ni = lam_re - 1.0, lam_im
    zr = (nr * a_re + ni * a_im) / den
    zi = (ni * a_re - nr * a_im) / den
    bbar_re = zr[..., None] * b_re - zi[..., None] * b_im
    bbar_im = zr[..., None] * b_im + zi[..., None] * b_re
    eye = jnp.eye(16, dtype=F32)
    blk = lambda bb: jnp.einsum('gph,gk->ghkp', bb, eye).reshape(S5W, S5N)
    hu, hs = S5W // 2, S5N // 2
    bblk = jnp.stack([jnp.stack([blk(bb)[hu * j:hu * (j + 1), hs * j:hs * (j + 1)] for j in range(2)])
                      for bb in (bbar_re, bbar_im)]).astype(BF16)
    cblk = lambda cc: jnp.einsum('ghp,gk->gpkh', cc, eye).reshape(S5N, S5W)
    ccat = jnp.stack([jnp.stack([cblk(cc)[hs * j:hs * (j + 1), hu * j:hu * (j + 1)] for cc in (c_re, -c_im)])
                      for j in range(2)]).astype(BF16)
    lr, li = lam_re.reshape(-1), lam_im.reshape(-1)
    pows = [(jnp.ones_like(lr), jnp.zeros_like(li))]
    for _ in range(8):
        pr, pi = pows[-1]
        pows.append((pr * lr - pi * li, pr * li + pi * lr))
    rows = jnp.arange(8)[:, None]
    tabs = []
    for dsh in (1, 2, 4):
        pr, pi = pows[dsh]
        tabs.append(jnp.where(rows >= dsh, jnp.concatenate([pr, pi])[None, :], 0.0))
    tabs.append(jnp.stack([jnp.concatenate(pows[r + 1]) for r in range(8)]))
    return {
        "tab": jnp.stack(tabs), "bblk": bblk, "ccat": ccat,
        "lam8": jnp.broadcast_to(jnp.concatenate([lr, li])[None, :], (NB_P, 2 * S5N)),
        "d": d.reshape(1, S5W), "wglu": w_glu.astype(BF16), "bglu": b_glu.reshape(1, S5W),
    }


def _pad_lanes(v, n=128):
    return jnp.pad(v, (0, n - v.shape[0])).reshape(1, n)


def kernel(x_prompt, x_sample, c_prompt, c_sample, state_hgrn, state_s5, state_ssd, state_ssd_conv, w_ada, b_ada, ln_g, ln_b, w_in, w_out, hgrn_lb_logits, hgrn_norm_g, s5_a_re, s5_a_im, s5_log_dt, s5_b_re, s5_b_im, s5_c_re, s5_c_im, s5_d, s5_w_glu, s5_b_glu, ssd_conv_w, ssd_conv_b, ssd_dt_bias, ssd_a_log, ssd_d, ssd_norm_g, ffn_w_gate, ffn_w_up, ffn_w_down, moe_w_router, moe_b_router, moe_w_gate, moe_w_up, moe_w_down):
    consts = _consts()
    c_all = jnp.concatenate([c_sample, c_prompt], axis=0)
    ada4 = _ada(c_all, w_ada, b_ada).reshape(DEPTH, NB_S + NB_P, 1, 6 * D)

    lb_all = jnp.cumsum(jax.nn.softmax(hgrn_lb_logits, axis=0), axis=0)
    lb_all = lb_all - lb_all[0]

    xp = x_prompt.reshape(ROWS_P, D)
    xs = x_sample.reshape(ROWS_S, D)
    new_h, new_s5, new_m, new_c = [], [], [], []
    x1 = f = route = None
    for l in range(DEPTH):
        w_in_b = jnp.pad(w_in[l], ((0, 0), (0, N_IN_PAD - N_IN))).astype(BF16)
        if l == 0:
            x, ph, ps, pm = _stage_a0(xp, xs, ada4, w_in_b)
        else:
            x, ph, ps, pm = _stage_a1(l, x1, f, ada4, ln_g[l - 1, 1].reshape(1, D), ln_b[l - 1, 1].reshape(1, D),
                                      w_in_b)
        oh, hg_p, hg_s = _hgrn(ph, lb_all[l].reshape(1, HW), hgrn_norm_g[l].reshape(1, HW), consts,
                               jnp.swapaxes(state_hgrn[l], -1, -2))
        s5p = _s5_params(s5_a_re[l], s5_a_im[l], s5_log_dt[l], s5_b_re[l], s5_b_im[l], s5_c_re[l], s5_c_im[l],
                         s5_d[l], s5_w_glu[l], s5_b_glu[l])
        os_p, s5_p = _s5_prompt(ps[:ROWS_P].reshape(NB_P, L_P, S5W), s5p, consts)
        os_s, s5_s = _s5_sample(ps[ROWS_P:], s5p, state_s5[l].reshape(NB_S, 1, 2 * S5N))
        os_ = jnp.concatenate([os_p.reshape(ROWS_P, S5W), os_s], axis=0)
        ssd_prm = {
            "cw": ssd_conv_w[l], "cb": ssd_conv_b[l].reshape(1, CONVC),
            "dtb": _pad_lanes(ssd_dt_bias[l]), "aneg": _pad_lanes(-jnp.exp(ssd_a_log[l])),
            "dx": jnp.repeat(ssd_d[l], HD).reshape(1, SSW), "sg": ssd_norm_g[l].reshape(1, SSW),
        }
        hist = jnp.pad(state_ssd_conv[l], ((0, 0), (0, L_S - 3), (0, 0))).reshape(ROWS_S, CONVC)
        om, ss_p, ss_s, tail_p, xbc_s = _ssd(pm, ssd_prm, consts, hist, state_ssd[l])
        conv_p = tail_p[:, 8 - 3:]
        conv_s = xbc_s.reshape(NB_S, L_S, CONVC)[:, L_S - 3:]
        new_h.append((jnp.swapaxes(hg_p, -1, -2), jnp.swapaxes(hg_s, -1, -2)))
        new_s5.append((s5_p.reshape(NB_P, 2, 16, 64), s5_s.reshape(NB_S, 2, 16, 64)))
        new_m.append((ss_p, ss_s))
        new_c.append((conv_p, conv_s))

        wo_b = w_out[l].astype(BF16)
        lg, lbb = ln_g[l, 0].reshape(1, D), ln_b[l, 0].reshape(1, D)
        j = l // 2
        if l % 2 == 0:
            x1, h2 = _stage_o(l, oh, os_, om, x, wo_b, ada4, lg, lbb)
            f = _ffn(jnp.zeros((ROWS // TMF,), jnp.int32), jnp.full((1,), ROWS // TMF, jnp.int32), h2,
                     ffn_w_gate[j:j + 1].astype(BF16),
                     ffn_w_up[j:j + 1].astype(BF16), ffn_w_down[j:j + 1].astype(BF16))
        else:
            wr = jnp.pad(moe_w_router[j], ((0, 0), (0, 128 - NEXP)))
            wr_hi = wr.astype(BF16)
            wr = jnp.stack([wr_hi, (wr - wr_hi.astype(F32)).astype(BF16)])
            br = _pad_lanes(moe_b_router[j])
            x1, h2, route = _stage_o(l, oh, os_, om, x, wo_b, ada4, lg, lbb, router=(wr, br))
            wg_b, wu_b, wd_b = (moe_w_gate[j].astype(BF16), moe_w_up[j].astype(BF16),
                                moe_w_down[j].astype(BF16))
            ya, yb = [], []
            for c in range(MOE_CHUNKS):
                flat_e = route[c * MOE_ROWS:(c + 1) * MOE_ROWS, 0:2].astype(jnp.int32).reshape(-1)
                onehot = (flat_e[:, None] == jnp.arange(NEXP)[None, :]).astype(jnp.int32)
                csum = jnp.cumsum(onehot, axis=0)
                counts = csum[-1]
                rank = jnp.take_along_axis(csum, flat_e[:, None], axis=1)[:, 0] - 1
                padded = ((counts + TMF - 1) // TMF) * TMF
                pend = jnp.cumsum(padded)
                pstart = pend - padded
                dest = pstart[flat_e] + rank
                n_pad = 2 * MOE_ROWS + NEXP * TMF
                tok = c * MOE_ROWS + jnp.arange(2 * MOE_ROWS, dtype=jnp.int32) // 2
                src_tok = jnp.full((n_pad,), c * MOE_ROWS, jnp.int32).at[dest].set(tok)
                tile_start = jnp.arange(n_pad // TMF, dtype=jnp.int32) * TMF
                tile_e = jnp.minimum(jnp.sum((pend[None, :] <= tile_start[:, None]).astype(jnp.int32), axis=1),
                                     NEXP - 1)
                n_used = (pend[NEXP - 1:NEXP] // TMF).astype(jnp.int32)
                y_sorted = _ffn(tile_e, n_used, h2[src_tok], wg_b, wu_b, wd_b)
                pos = dest.reshape(MOE_ROWS, 2)
                ya.append(y_sorted[pos[:, 0]])
                yb.append(y_sorted[pos[:, 1]])
    y_p, y_s = _stage_final(DEPTH - 1, x1, ya, yb, route, ada4, ln_g[DEPTH - 1, 1].reshape(1, D),
                            ln_b[DEPTH - 1, 1].reshape(1, D))
    stack = lambda lst, k: jnp.stack([t[k] for t in lst])
    return (y_p.reshape(NB_P, L_P, D), y_s.reshape(NB_S, L_S, D),
            stack(new_h, 0), stack(new_s5, 0), stack(new_m, 0), stack(new_c, 0),
            stack(new_h, 1), stack(new_s5, 1), stack(new_m, 1), stack(new_c, 1))
```

```python
import functools
import math

import jax
import jax.numpy as jnp
import numpy as np
from jax import lax
from jax.experimental import pallas as pl
from jax.experimental.pallas import tpu as pltpu

F32 = jnp.float32
BF16 = jnp.bfloat16

D = 1024
NB_P, L_P = 8, 2048
NB_S, L_S = 128, 8
ROWS_P = NB_P * L_P
ROWS_S = NB_S * L_S
ROWS = ROWS_P + ROWS_S
DEPTH = 2
HW = 384
S5W = 256
SSW = 384
NH = 6
HD = 64
S5N = 1024
CONVC = 640
N_IN = 2822
N_IN_PAD = 2944
P_HG = 1536
P_SS = 1152
DFF = 2816
NEXP = 8
ALPHA = (2 * DEPTH) ** 0.25
LN_EPS = 1e-5
RMS_EPS = 1e-6

TM = 512
NPT = ROWS_P // TM
NST = ROWS_S // TM
NT = NPT + NST
SEQ_PER_TILE = TM // L_S
TILES_PER_SEQ = L_P // TM

CB = 128
NPC = ROWS_P // CB
NSC = ROWS_S // CB
CH_PER_SEQ = L_P // CB
SEQ_PER_CB = CB // L_S

TMF = 512
TF = 1408
MOE_CHUNKS = 1
MOE_ROWS = ROWS // MOE_CHUNKS
MOE_TILES = MOE_ROWS // TM
VMEM_LIMIT = 56 * 1024 * 1024


def _cparams(sem):
    return pltpu.CompilerParams(dimension_semantics=sem, vmem_limit_bytes=VMEM_LIMIT)


def _bdot(a, b):
    return jnp.dot(a.astype(BF16), b.astype(BF16), preferred_element_type=F32)


def _bdot_nt(a, b):
    return lax.dot_general(a.astype(BF16), b.astype(BF16), (((1,), (1,)), ((), ())),
                           preferred_element_type=F32)


def _split_dot(x, e, passes):
    acc = None
    r = x
    for _ in range(passes):
        hi = r.astype(BF16)
        d = jnp.dot(hi, e, preferred_element_type=F32)
        acc = d if acc is None else acc + d
        r = r - hi.astype(F32)
    return acc


def _split_dot_l(e, x, passes):
    acc = None
    r = x
    for _ in range(passes):
        hi = r.astype(BF16)
        d = jnp.dot(e, hi, preferred_element_type=F32)
        acc = d if acc is None else acc + d
        r = r - hi.astype(F32)
    return acc


def _silu(x):
    return x * jax.nn.sigmoid(x)


def _layer_norm(x, g, b):
    mu = jnp.mean(x, -1, keepdims=True)
    xc = x - mu
    var = jnp.mean(xc * xc, -1, keepdims=True)
    return xc * lax.rsqrt(var + LN_EPS) * g + b


def _rowmod(i, p_ref, s_ref):
    s = jnp.broadcast_to(s_ref[...], (SEQ_PER_TILE, L_S, D)).reshape(TM, D)
    return jnp.where(i < NPT, p_ref[0], s)


ADA_TN = 1536


def _ada_body(c_ref, w_ref, b_ref, o_ref):
    o_ref[...] = _bdot(_silu(c_ref[...]), w_ref[...]) + b_ref[...]


def _ada(c_all, w_ada, b_ada):
    nc = c_all.shape[0]
    return pl.pallas_call(
        _ada_body,
        grid=(DEPTH, 6 * D // ADA_TN),
        in_specs=[
            pl.BlockSpec((nc, D), lambda l, j: (0, 0)),
            pl.BlockSpec((None, D, ADA_TN), lambda l, j: (l, 0, j)),
            pl.BlockSpec((None, 1, ADA_TN), lambda l, j: (l, 0, j)),
        ],
        out_specs=pl.BlockSpec((None, nc, ADA_TN), lambda l, j: (l, 0, j)),
        out_shape=jax.ShapeDtypeStruct((DEPTH, nc, 6 * D), F32),
        compiler_params=_cparams(("parallel", "parallel")),
        name="ada",
    )(c_all, w_ada, b_ada.reshape(DEPTH, 1, 6 * D))


def _mod_specs(layer, k):
    ps = pl.BlockSpec((None, 1, 1, D),
                      lambda i: (layer, NB_S + jnp.minimum(i // TILES_PER_SEQ, NB_P - 1), 0, k))
    ss = pl.BlockSpec((None, SEQ_PER_TILE, 1, D),
                      lambda i: (layer, jnp.clip(i - NPT, 0, NST - 1), 0, k))
    return [ps, ss]


def _row_spec(width):
    return pl.BlockSpec((TM, width), lambda i: (i, 0))


def _const_spec(shape):
    nd = len(shape)
    return pl.BlockSpec(shape, lambda *_: (0,) * nd)


def _proj_out(x, i, scp, scs, shp, shs, w_ref, ph_ref, ps_ref, pm_ref):
    h = x * (1.0 + _rowmod(i, scp, scs)) + _rowmod(i, shp, shs)
    proj = jnp.dot(h.astype(BF16), w_ref[...], preferred_element_type=F32)
    ph_ref[...] = proj[:, 0:P_HG]
    ps_ref[...] = proj[:, P_HG:P_HG + S5W]
    pm_ref[...] = proj[:, P_HG + S5W:N_IN_PAD]


def _a0_body(xp_ref, xs_ref, scp, scs, shp, shs, w_ref, x_ref, ph_ref, ps_ref, pm_ref):
    i = pl.program_id(0)
    x = jnp.where(i < NPT, xp_ref[...], xs_ref[...])
    x_ref[...] = x
    _proj_out(x, i, scp, scs, shp, shs, w_ref, ph_ref, ps_ref, pm_ref)


def _a1_body(x1_ref, f_ref, gp, gs, lng_ref, lnb_ref, scp, scs, shp, shs, w_ref,
             x_ref, ph_ref, ps_ref, pm_ref):
    i = pl.program_id(0)
    x = _layer_norm(ALPHA * x1_ref[...] + _rowmod(i, gp, gs) * f_ref[...], lng_ref[...], lnb_ref[...])
    x_ref[...] = x
    _proj_out(x, i, scp, scs, shp, shs, w_ref, ph_ref, ps_ref, pm_ref)


def _a_out():
    specs = [_row_spec(D), _row_spec(P_HG), _row_spec(S5W), _row_spec(P_SS)]
    shapes = [jax.ShapeDtypeStruct((ROWS, w), F32) for w in (D, P_HG, S5W, P_SS)]
    return specs, shapes


def _stage_a0(xp, xs, ada4, w_in_b):
    out_specs, out_shape = _a_out()
    return pl.pallas_call(
        _a0_body,
        grid=(NT,),
        in_specs=[
            pl.BlockSpec((TM, D), lambda i: (jnp.minimum(i, NPT - 1), 0)),
            pl.BlockSpec((TM, D), lambda i: (jnp.clip(i - NPT, 0, NST - 1), 0)),
            *_mod_specs(0, 1), *_mod_specs(0, 0),
            _const_spec((D, N_IN_PAD)),
        ],
        out_specs=out_specs, out_shape=out_shape,
        compiler_params=_cparams(("parallel",)),
        name="stage_a0",
    )(xp, xs, ada4, ada4, ada4, ada4, w_in_b)


def _stage_a1(layer, x1, f, ada4, ln_g, ln_b, w_in_b):
    out_specs, out_shape = _a_out()
    return pl.pallas_call(
        _a1_body,
        grid=(NT,),
        in_specs=[
            _row_spec(D), _row_spec(D),
            *_mod_specs(layer - 1, 5),
            _const_spec((1, D)), _const_spec((1, D)),
            *_mod_specs(layer, 1), *_mod_specs(layer, 0),
            _const_spec((D, N_IN_PAD)),
        ],
        out_specs=out_specs, out_shape=out_shape,
        compiler_params=_cparams(("parallel",)),
        name="stage_a1",
    )(x1, f, ada4, ada4, ln_g, ln_b, ada4, ada4, ada4, ada4, w_in_b)


def _mixer_blk(width, nsub=1):
    return pl.BlockSpec((nsub * CB, width), lambda i: (i, 0))


def _pstate_spec(shape, nsub=1):
    nd = len(shape)
    return pl.BlockSpec((1,) + shape,
                        lambda i: (jnp.minimum(i // (CH_PER_SEQ // nsub), NB_P - 1),) + (0,) * nd)


def _sstate_spec(shape, nsub=1):
    nd = len(shape)
    return pl.BlockSpec((nsub * SEQ_PER_CB,) + shape,
                        lambda i: (jnp.clip(i - NPC // nsub, 0, NSC // nsub - 1),) + (0,) * nd)


def _seq_last_rows(x):
    w = x.shape[-1]
    x3 = x.reshape(SEQ_PER_CB, L_S, w)
    return jnp.broadcast_to(x3[:, L_S - 1:L_S, :], (SEQ_PER_CB, L_S, w)).reshape(CB, w)


def _concat_heads(parts):
    return jnp.concatenate(parts, axis=1)


def _stack_select(shape, row_div, lane_div):
    r = lax.broadcasted_iota(jnp.int32, shape, 0) // row_div
    c = lax.broadcasted_iota(jnp.int32, shape, 1) // lane_div
    return r == c


def _seq_expand_lanes(qh):
    q2 = jnp.concatenate([qh, qh], axis=1)
    q16 = jnp.concatenate([q2] * (SEQ_PER_CB // 2), axis=1)
    return jnp.where(_stack_select((CB, SEQ_PER_CB * HD), L_S, HD), q16, 0.0)


def _seq_expand_rows(xt):
    t = jnp.broadcast_to(xt[None], (SEQ_PER_CB, HD, CB)).reshape(SEQ_PER_CB * HD, CB)
    return jnp.where(_stack_select((SEQ_PER_CB * HD, CB), HD, L_S), t, 0.0)


def _fold_seq_lanes(full):
    acc = full[:, 0:128]
    for j in range(1, SEQ_PER_CB * HD // 128):
        acc = acc + full[:, 128 * j:128 * (j + 1)]
    return acc[:, 0:HD] + acc[:, HD:2 * HD]


HGRN_BASE = 32
EXP_RANGE_MAX = 80.0


def _hgrn_block(prompt, p_ref, lb_ref, hg_ref, e64_ref, tril_ref, oh_ref, st_scr, o_scr, i,
                stp_out=None, sts_in=None, sts_out=None):
    lb = lb_ref[...]
    qr = p_ref[:, 0:HW]
    fr = p_ref[:, HW:2 * HW]
    v = p_ref[:, 2 * HW:3 * HW]
    gr = p_ref[:, 3 * HW:4 * HW]
    ls = jnp.minimum(fr, 0.0) - jnp.log1p(jnp.exp(-jnp.abs(fr)))
    a = jnp.log(lb)
    bb = jnp.log1p(-lb) + ls
    lf = jnp.maximum(a, bb) + jnp.log1p(jnp.exp(-jnp.abs(a - bb)))
    kk = (1.0 - lb) * jax.nn.sigmoid(-fr)
    q = _silu(qr)
    b = _split_dot_l(tril_ref[...], lf, 3)

    e64 = e64_ref[...]
    ti = lax.broadcasted_iota(jnp.int32, (CB, CB), 0)
    si = lax.broadcasted_iota(jnp.int32, (CB, CB), 1)

    def diag8():
        nsub = CB // 8
        b3 = b.reshape(nsub, 8, HW)
        q3 = q.reshape(nsub, 8, HW)
        k3 = kk.reshape(nsub, 8, HW)
        v3 = v.reshape(nsub, 8, HW)
        r3 = lax.broadcasted_iota(jnp.int32, (nsub, 8, HW), 1)
        o = jnp.zeros((CB, HW), F32)
        for s in range(8):
            dlt = jnp.minimum(b3 - b3[:, s:s + 1, :], 0.0)
            w = jnp.where(r3 >= s, jnp.exp(dlt), 0.0) * q3 * k3[:, s:s + 1, :]
            hsum = jnp.dot(w.reshape(CB, HW).astype(BF16), e64, preferred_element_type=F32)
            o = o + hsum * jnp.broadcast_to(v3[:, s:s + 1, :], (nsub, 8, HW)).reshape(CB, HW)
        return o

    def level_terms(m):
        terms = []
        while m < CB:
            nb = CB // (2 * m)
            b4 = b.reshape(nb, 2 * m, HW)
            bmid = b4[:, m - 1:m, :]
            pos = lax.broadcasted_iota(jnp.int32, (nb, 2 * m, HW), 1)
            qq = jnp.where(pos >= m, q.reshape(nb, 2 * m, HW) * jnp.exp(jnp.minimum(b4 - bmid, 0.0)), 0.0)
            kq = jnp.where(pos < m, kk.reshape(nb, 2 * m, HW) * jnp.exp(jnp.minimum(bmid - b4, 0.0)), 0.0)
            terms.append((qq.reshape(CB, HW), kq.reshape(CB, HW), (ti // (2 * m)) == (si // (2 * m))))
            m *= 2
        return terms

    def scores_times_v(terms):
        parts = []
        for h in range(NH):
            sl = slice(HD * h, HD * (h + 1))
            sc = None
            for qq, kq, keep in terms:
                t = jnp.where(keep, _bdot_nt(qq[:, sl], kq[:, sl]), 0.0)
                sc = t if sc is None else sc + t
            parts.append(_bdot(sc, v[:, sl]))
        return _concat_heads(parts)

    base = HGRN_BASE if prompt else L_S
    nbase = CB // base
    bb3 = b.reshape(nbase, base, HW)
    top = bb3[:, 0:1, :] - lf.reshape(nbase, base, HW)[:, 0:1, :]
    decay_range = jnp.max(top - bb3[:, base - 1:base, :])
    fast = decay_range <= EXP_RANGE_MAX

    @pl.when(jnp.logical_not(fast))
    def _():
        o_scr[...] = diag8() + (scores_times_v(level_terms(8)) if prompt else 0.0)

    qf = (q.reshape(nbase, base, HW) * jnp.exp(bb3 - top)).reshape(CB, HW)
    kf = (kk.reshape(nbase, base, HW) * jnp.exp(top - bb3)).reshape(CB, HW)
    keep = ((ti // base) == (si // base)) & (si <= ti)
    o_fast = scores_times_v([(qf, kf, keep)] + (level_terms(base) if prompt else []))
    o = jnp.where(fast, o_fast, o_scr[...])
    qt = q * jnp.exp(b)
    if prompt:
        blast = b[CB - 1:CB, :]
        kd = kk * jnp.exp(blast - b)
        vt = v.T
        parts = []
        for h in range(NH):
            sl = slice(HD * h, HD * (h + 1))
            st = st_scr[h]
            oh = o[:, sl] + _bdot_nt(qt[:, sl], st)
            st_new = st * jnp.exp(blast[:, sl]) + _bdot(vt[sl, :], kd[:, sl])
            st_scr[h] = st_new
            parts.append(oh)

        @pl.when(i % CH_PER_SEQ == CH_PER_SEQ - 1)
        def _():
            stp_out[0] = st_scr[...]
    else:
        blast = _seq_last_rows(b)
        kd = kk * jnp.exp(blast - b)
        dec = jnp.exp(blast)
        vt = v.T
        parts = []
        for h in range(NH):
            sl = slice(HD * h, HD * (h + 1))
            sts = sts_in[:, h].reshape(SEQ_PER_CB * HD, HD)
            full = _bdot_nt(qt[:, sl], sts)
            sel = jnp.where(_stack_select((CB, SEQ_PER_CB * HD), L_S, HD), full, 0.0)
            parts.append(o[:, sl] + _fold_seq_lanes(sel))
            dec3 = dec[:, sl].reshape(SEQ_PER_CB, L_S, HD)[:, L_S - 1:L_S, :]
            dec_rows = jnp.broadcast_to(dec3, (SEQ_PER_CB, HD, HD)).reshape(SEQ_PER_CB * HD, HD)
            upd = _bdot(_seq_expand_rows(vt[sl, :]), kd[:, sl])
            sts_out[:, h] = (sts * dec_rows + upd).reshape(SEQ_PER_CB, HD, HD)
    oall = _concat_heads(parts)
    ms = _split_dot(oall * oall, e64, 1) * (1.0 / HD)
    oh_ref[...] = oall * lax.rsqrt(ms + RMS_EPS) * hg_ref[...] * _silu(gr)


def _hgrn_body(p_ref, lb_ref, hg_ref, e64_ref, trilp_ref, trils_ref, sts_in,
               oh_ref, stp_out, sts_out, st_scr, o_scr):
    i = pl.program_id(0)

    @pl.when(i == 0)
    def _():
        o_scr[...] = jnp.zeros_like(o_scr)

    @pl.when((i < NPC) & (i % CH_PER_SEQ == 0))
    def _():
        st_scr[...] = jnp.zeros_like(st_scr)

    @pl.when(i < NPC)
    def _():
        _hgrn_block(True, p_ref, lb_ref, hg_ref, e64_ref, trilp_ref, oh_ref, st_scr, o_scr, i,
                    stp_out=stp_out)

    @pl.when(i >= NPC)
    def _():
        _hgrn_block(False, p_ref, lb_ref, hg_ref, e64_ref, trils_ref, oh_ref, st_scr, o_scr, i,
                    sts_in=sts_in, sts_out=sts_out)


def _hgrn(ph, lb, hg, consts, st_t):
    return pl.pallas_call(
        _hgrn_body,
        grid=(NPC + NSC,),
        in_specs=[
            _mixer_blk(P_HG), _const_spec((1, HW)), _const_spec((1, HW)),
            _const_spec((HW, HW)), _const_spec((CB, CB)), _const_spec((CB, CB)),
            _sstate_spec((NH, HD, HD)),
        ],
        out_specs=[_mixer_blk(HW), _pstate_spec((NH, HD, HD)), _sstate_spec((NH, HD, HD))],
        out_shape=[
            jax.ShapeDtypeStruct((ROWS, HW), F32),
            jax.ShapeDtypeStruct((NB_P, NH, HD, HD), F32),
            jax.ShapeDtypeStruct((NB_S, NH, HD, HD), F32),
        ],
        scratch_shapes=[pltpu.VMEM((NH, HD, HD), F32), pltpu.VMEM((CB, HW), F32)],
        compiler_params=_cparams(("arbitrary",)),
        name="hgrn",
    )(ph, lb, hg, consts["e64"], consts["trilp"], consts["trils"], st_t)


def _cmul_add(hr, hi, lr, li, sr, si):
    return hr + lr * sr - li * si, hi + lr * si + li * sr


def _s5_project(ub, bblk_ref):
    halves = [ub[:, (S5W // 2) * j:(S5W // 2) * (j + 1)] for j in range(2)]
    hr = jnp.concatenate([jnp.dot(halves[j], bblk_ref[0, j], preferred_element_type=F32) for j in range(2)], axis=1)
    hi = jnp.concatenate([jnp.dot(halves[j], bblk_ref[1, j], preferred_element_type=F32) for j in range(2)], axis=1)
    return hr, hi


def _s5_readout(h_scr, u, ccat_ref, d_ref, wglu_ref, bglu_ref):
    hs = S5N // 2
    ch = [_bdot(h_scr[:, hs * j:hs * (j + 1)], ccat_ref[j, 0])
          + _bdot(h_scr[:, S5N + hs * j:S5N + hs * (j + 1)], ccat_ref[j, 1]) for j in range(2)]
    y = jnp.concatenate(ch, axis=1) + d_ref[...] * u
    c0 = math.sqrt(2.0 / math.pi)
    y = y * (0.5 * (1.0 + jnp.tanh(c0 * (y + 0.044715 * (y * y * y)))))
    return y * jax.nn.sigmoid(_bdot(y, wglu_ref[...]) + bglu_ref[...])


S5_TB = 32
S5_ROWS = NB_P * S5_TB


def _s5_prompt_body(p_ref, perm_ref, permt_ref, lam_ref, bblk_ref, ccat_ref, d_ref, wglu_ref, bglu_ref,
                    os_ref, st_out, carry_scr, h_scr):
    i = pl.program_id(0)

    @pl.when(i == 0)
    def _():
        carry_scr[...] = jnp.zeros_like(carry_scr)

    u = p_ref[...].reshape(S5_ROWS, S5W)
    u_hi = u.astype(BF16)
    u_lo = (u - u_hi.astype(F32)).astype(BF16)
    perm = perm_ref[...]
    up_hi = jnp.dot(perm, u_hi, preferred_element_type=F32)
    up = up_hi + jnp.dot(perm, u_lo, preferred_element_type=F32)
    hr, hi = _s5_project(up_hi.astype(BF16), bblk_ref)
    lr = lam_ref[:, 0:S5N]
    li = lam_ref[:, S5N:2 * S5N]
    cr = carry_scr[:, 0:S5N]
    ci = carry_scr[:, S5N:2 * S5N]
    for t in range(S5_TB):
        rows = slice(NB_P * t, NB_P * (t + 1))
        cr, ci = _cmul_add(hr[rows], hi[rows], lr, li, cr, ci)
        h_scr[rows, 0:S5N] = cr
        h_scr[rows, S5N:2 * S5N] = ci
    carry_scr[:, 0:S5N] = cr
    carry_scr[:, S5N:2 * S5N] = ci
    out = _s5_readout(h_scr, up, ccat_ref, d_ref, wglu_ref, bglu_ref)
    os_ref[...] = jnp.dot(permt_ref[...], out.astype(BF16), preferred_element_type=F32).reshape(NB_P, S5_TB, S5W)

    @pl.when(i == pl.num_programs(0) - 1)
    def _():
        st_out[...] = carry_scr[...]


def _s5_prompt(ps3, prm, consts):
    return pl.pallas_call(
        _s5_prompt_body,
        grid=(L_P // S5_TB,),
        in_specs=[
            pl.BlockSpec((NB_P, S5_TB, S5W), lambda i: (0, i, 0)),
            _const_spec((S5_ROWS, S5_ROWS)), _const_spec((S5_ROWS, S5_ROWS)), _const_spec((NB_P, 2 * S5N)),
            _const_spec((2, 2, S5W // 2, S5N // 2)), _const_spec((2, 2, S5N // 2, S5W // 2)),
            _const_spec((1, S5W)), _const_spec((S5W, S5W)), _const_spec((1, S5W)),
        ],
        out_specs=[pl.BlockSpec((NB_P, S5_TB, S5W), lambda i: (0, i, 0)), _const_spec((NB_P, 2 * S5N))],
        out_shape=[jax.ShapeDtypeStruct((NB_P, L_P, S5W), F32), jax.ShapeDtypeStruct((NB_P, 2 * S5N), F32)],
        scratch_shapes=[pltpu.VMEM((NB_P, 2 * S5N), F32), pltpu.VMEM((S5_ROWS, 2 * S5N), F32)],
        compiler_params=_cparams(("arbitrary",)),
        name="s5_prompt",
    )(ps3, consts["perm"], consts["permt"], prm["lam8"], prm["bblk"], prm["ccat"], prm["d"], prm["wglu"],
      prm["bglu"])


def _s5_sample_body(p_ref, tab_ref, bblk_ref, ccat_ref, d_ref, wglu_ref, bglu_ref, s5s_in,
                    os_ref, s5s_out, h_scr):
    u = p_ref[...]
    hr, hi = _s5_project(u.astype(BF16), bblk_ref)
    nsub = CB // 8
    for idx, dsh in enumerate((1, 2, 4)):
        sr = pltpu.roll(hr, dsh, 0).reshape(nsub, 8, S5N)
        si = pltpu.roll(hi, dsh, 0).reshape(nsub, 8, S5N)
        lr = tab_ref[idx, :, 0:S5N][None]
        li = tab_ref[idx, :, S5N:2 * S5N][None]
        nr, ni = _cmul_add(hr.reshape(nsub, 8, S5N), hi.reshape(nsub, 8, S5N), lr, li, sr, si)
        hr = nr.reshape(CB, S5N)
        hi = ni.reshape(CB, S5N)
    tcr = tab_ref[3, :, 0:S5N]
    tci = tab_ref[3, :, S5N:2 * S5N]
    cr = s5s_in[:, :, 0:S5N]
    ci = s5s_in[:, :, S5N:2 * S5N]
    tr, tim = _cmul_add(hr.reshape(nsub, 8, S5N), hi.reshape(nsub, 8, S5N), tcr[None], tci[None], cr, ci)
    h_scr[:, 0:S5N] = tr.reshape(CB, S5N)
    h_scr[:, S5N:2 * S5N] = tim.reshape(CB, S5N)
    sb = lax.broadcasted_iota(jnp.int32, (SEQ_PER_CB, CB), 0)
    st = lax.broadcasted_iota(jnp.int32, (SEQ_PER_CB, CB), 1)
    sel = (st == L_S * sb + (L_S - 1)).astype(BF16)
    s5s_out[...] = _split_dot_l(sel, h_scr[...], 3)
    os_ref[...] = _s5_readout(h_scr, u, ccat_ref, d_ref, wglu_ref, bglu_ref)


def _s5_sample(ps_s, prm, st):
    seqs = lambda shape: pl.BlockSpec((SEQ_PER_CB,) + shape, lambda i: (i,) + (0,) * len(shape))
    return pl.pallas_call(
        _s5_sample_body,
        grid=(NSC,),
        in_specs=[
            _mixer_blk(S5W), _const_spec((4, 8, 2 * S5N)), _const_spec((2, 2, S5W // 2, S5N // 2)),
            _const_spec((2, 2, S5N // 2, S5W // 2)), _const_spec((1, S5W)), _const_spec((S5W, S5W)),
            _const_spec((1, S5W)), seqs((1, 2 * S5N)),
        ],
        out_specs=[_mixer_blk(S5W), seqs((2 * S5N,))],
        out_shape=[jax.ShapeDtypeStruct((ROWS_S, S5W), F32), jax.ShapeDtypeStruct((NB_S, 2 * S5N), F32)],
        scratch_shapes=[pltpu.VMEM((CB, 2 * S5N), F32)],
        compiler_params=_cparams(("parallel",)),
        name="s5_sample",
    )(ps_s, prm["tab"], prm["bblk"], prm["ccat"], prm["d"], prm["wglu"], prm["bglu"], st)


def _ssd_block(prompt, p_ref, cw_ref, cb_ref, dtb_ref, aneg_ref, dx_ref, sg_ref,
               g192_ref, tril_ref, om_ref, cbuf, st_scr, sub, is_last=None, ssp_out=None, convp_out=None,
               hist_ref=None, sss_in=None, sss_out=None, convs_out=None):
    rows = slice(sub * CB, (sub + 1) * CB)
    seqs = slice(sub * SEQ_PER_CB, (sub + 1) * SEQ_PER_CB)
    z = p_ref[rows, 0:SSW]
    xbc = p_ref[rows, SSW:SSW + CONVC]
    dtr = p_ref[rows, SSW + CONVC:P_SS]
    acc = cb_ref[...] + cw_ref[3:4, :] * xbc
    if prompt:
        cbuf[8:8 + CB, :] = xbc
        for k in (1, 2, 3):
            acc = acc + cw_ref[3 - k:4 - k, :] * cbuf[8 - k:8 - k + CB, :]
        cbuf[0:8, :] = cbuf[CB:CB + 8, :]
    else:
        convs_out[rows, :] = xbc
        tl = lax.broadcasted_iota(jnp.int32, (CB, CONVC), 0) % L_S
        hist = hist_ref[rows, :]
        for k in (1, 2, 3):
            hk = hist if k == 3 else pltpu.roll(hist, CB - (3 - k), 0)
            sh = jnp.where(tl >= k, pltpu.roll(xbc, k, 0), hk)
            acc = acc + cw_ref[3 - k:4 - k, :] * sh
    xc = _silu(acc)
    xs = xc[:, 0:SSW]
    bm = xc[:, SSW:SSW + 2 * HD]
    cm = xc[:, SSW + 2 * HD:CONVC]
    xdt = dtr + dtb_ref[...]
    dt = jnp.maximum(xdt, 0.0) + jnp.log1p(jnp.exp(-jnp.abs(xdt)))
    la = dt * aneg_ref[...]
    tril = tril_ref[...]
    b6 = _split_dot_l(tril, la, 3)
    lane = lax.broadcasted_iota(jnp.int32, (CB, 128), 1)
    bxw = [jnp.broadcast_to(b6[:, h:h + 1], (CB, 128)) for h in range(NH)]
    dtw = [jnp.broadcast_to(dt[:, h:h + 1], (CB, 128)) for h in range(NH)]
    pair = lambda cols: _concat_heads([jnp.where(lane < HD, cols[2 * j], cols[2 * j + 1]) for j in range(NH // 2)])
    bx = pair(bxw)
    dtx = pair(dtw)
    bm_rep = _concat_heads([bm[:, 0:HD]] * 3 + [bm[:, HD:2 * HD]] * 3)
    cm_rep = _concat_heads([cm[:, 0:HD]] * 3 + [cm[:, HD:2 * HD]] * 3)
    kh = bm_rep * dtx
    qt = cm_rep * jnp.exp(bx)
    blast = bx[CB - 1:CB, :] if prompt else _seq_last_rows(bx)
    kd = kh * jnp.exp(blast - bx)
    kdt = kd.T
    mask = tril > 0
    parts = []
    for h in range(NH):
        sl = slice(HD * h, HD * (h + 1))
        bcol = bxw[h]
        decay = jnp.exp(jnp.where(mask, bcol - bcol.T, -1e30))
        sc = _bdot_nt(cm_rep[:, sl], kh[:, sl]) * decay
        oh = _bdot(sc, xs[:, sl])
        if prompt:
            st = st_scr[h]
            oh = oh + _bdot(qt[:, sl], st)
            st_scr[h] = st * jnp.exp(blast[:, sl]) + _bdot(kdt[sl, :], xs[:, sl])
        else:
            sts = sss_in[seqs, h].reshape(SEQ_PER_CB * HD, HD)
            oh = oh + _bdot(_seq_expand_lanes(qt[:, sl]), sts)
            dec3 = jnp.exp(blast[:, sl]).reshape(SEQ_PER_CB, L_S, HD)[:, L_S - 1:L_S, :]
            dec_rows = jnp.broadcast_to(dec3, (SEQ_PER_CB, HD, HD)).reshape(SEQ_PER_CB * HD, HD)
            upd = _bdot(_seq_expand_rows(kdt[sl, :]), xs[:, sl])
            sss_out[seqs, h] = (sts * dec_rows + upd).reshape(SEQ_PER_CB, HD, HD)
        parts.append(oh)
    if is_last is not None:
        @pl.when(is_last)
        def _():
            ssp_out[0] = st_scr[...]
            convp_out[0] = cbuf[0:8, :]
    y = (_concat_heads(parts) + dx_ref[...] * xs) * _silu(z)
    ms = _split_dot(y * y, g192_ref[...], 1) * (1.0 / (SSW // 2))
    om_ref[rows, :] = y * lax.rsqrt(ms + RMS_EPS) * sg_ref[...]


def _ssd_body(p_ref, cw_ref, cb_ref, dtb_ref, aneg_ref, dx_ref, sg_ref, g192_ref,
              trilp_ref, trils_ref, hist_ref, sss_in, om_ref, ssp_out, sss_out, convp_out, convs_out,
              cbuf, st_scr):
    i = pl.program_id(0)
    common = (p_ref, cw_ref, cb_ref, dtb_ref, aneg_ref, dx_ref, sg_ref, g192_ref)
    npc, steps_per_seq = NPC // SSD_SUB, CH_PER_SEQ // SSD_SUB

    @pl.when((i < npc) & (i % steps_per_seq == 0))
    def _():
        st_scr[...] = jnp.zeros_like(st_scr)
        cbuf[...] = jnp.zeros_like(cbuf)

    @pl.when(i < npc)
    def _():
        for sub in range(SSD_SUB):
            is_last = (i % steps_per_seq == steps_per_seq - 1) if sub == SSD_SUB - 1 else None
            _ssd_block(True, *common, trilp_ref, om_ref, cbuf, st_scr, sub, is_last=is_last,
                       ssp_out=ssp_out, convp_out=convp_out)

    @pl.when(i >= npc)
    def _():
        for sub in range(SSD_SUB):
            _ssd_block(False, *common, trils_ref, om_ref, cbuf, st_scr, sub, hist_ref=hist_ref,
                       sss_in=sss_in, sss_out=sss_out, convs_out=convs_out)


SSD_SUB = 2


def _ssd(pm, prm, consts, hist, st):
    sample_rows = pl.BlockSpec((SSD_SUB * CB, CONVC),
                               lambda i: (jnp.clip(i - NPC // SSD_SUB, 0, NSC // SSD_SUB - 1), 0))
    return pl.pallas_call(
        _ssd_body,
        grid=((NPC + NSC) // SSD_SUB,),
        in_specs=[
            _mixer_blk(P_SS, SSD_SUB), _const_spec((4, CONVC)), _const_spec((1, CONVC)),
            _const_spec((1, 128)), _const_spec((1, 128)), _const_spec((1, SSW)), _const_spec((1, SSW)),
            _const_spec((SSW, SSW)),
            _const_spec((CB, CB)), _const_spec((CB, CB)),
            sample_rows,
            _sstate_spec((NH, HD, HD), SSD_SUB),
        ],
        out_specs=[_mixer_blk(SSW, SSD_SUB), _pstate_spec((NH, HD, HD), SSD_SUB),
                   _sstate_spec((NH, HD, HD), SSD_SUB), _pstate_spec((8, CONVC), SSD_SUB), sample_rows],
        out_shape=[
            jax.ShapeDtypeStruct((ROWS, SSW), F32),
            jax.ShapeDtypeStruct((NB_P, NH, HD, HD), F32),
            jax.ShapeDtypeStruct((NB_S, NH, HD, HD), F32),
            jax.ShapeDtypeStruct((NB_P, 8, CONVC), F32),
            jax.ShapeDtypeStruct((ROWS_S, CONVC), F32),
        ],
        scratch_shapes=[pltpu.VMEM((CB + 8, CONVC), F32), pltpu.VMEM((NH, HD, HD), F32)],
        compiler_params=_cparams(("arbitrary",)),
        name="ssd",
    )(pm, prm["cw"], prm["cb"], prm["dtb"], prm["aneg"], prm["dx"], prm["sg"],
      consts["g192"], consts["trilp"], consts["trils"], hist, st)


def _o_core(i, oh_ref, os_ref, om_ref, x_ref, wo_ref, gp, gs, lng_ref, lnb_ref, scp, scs, shp, shs):
    mix = (_bdot(oh_ref[...], wo_ref[0:HW, :]) + _bdot(os_ref[...], wo_ref[HW:HW + S5W, :])
           + _bdot(om_ref[...], wo_ref[HW + S5W:D, :]))
    x1 = _layer_norm(ALPHA * x_ref[...] + _rowmod(i, gp, gs) * mix, lng_ref[...], lnb_ref[...])
    h2 = x1 * (1.0 + _rowmod(i, scp, scs)) + _rowmod(i, shp, shs)
    return x1, h2


def _o_body(oh_ref, os_ref, om_ref, x_ref, wo_ref, gp, gs, lng_ref, lnb_ref, scp, scs, shp, shs,
            x1_ref, h2_ref):
    i = pl.program_id(0)
    x1, h2 = _o_core(i, oh_ref, os_ref, om_ref, x_ref, wo_ref, gp, gs, lng_ref, lnb_ref, scp, scs, shp, shs)
    x1_ref[...] = x1
    h2_ref[...] = h2.astype(BF16)


def _o_router_body(oh_ref, os_ref, om_ref, x_ref, wo_ref, gp, gs, lng_ref, lnb_ref, scp, scs, shp, shs,
                   wr_ref, br_ref, x1_ref, h2_ref, route_ref):
    i = pl.program_id(0)
    x1, h2 = _o_core(i, oh_ref, os_ref, om_ref, x_ref, wo_ref, gp, gs, lng_ref, lnb_ref, scp, scs, shp, shs)
    x1_ref[...] = x1
    for c in range(D // 128):
        h2_ref[:, c, :] = h2[:, 128 * c:128 * (c + 1)]
    h_hi = h2.astype(BF16)
    h_lo = (h2 - h_hi.astype(F32)).astype(BF16)
    logits = (jnp.dot(h_hi, wr_ref[0], preferred_element_type=F32)
              + jnp.dot(h_lo, wr_ref[0], preferred_element_type=F32)
              + jnp.dot(h_hi, wr_ref[1], preferred_element_type=F32)) + br_ref[...]
    lane = lax.broadcasted_iota(jnp.int32, (TM, 128), 1).astype(F32)
    neg = -jnp.inf
    lg = jnp.where(lane < NEXP, logits, neg)
    m1 = jnp.max(lg, axis=-1, keepdims=True)
    i1 = jnp.min(jnp.where(lg == m1, lane, 128.0), axis=-1, keepdims=True)
    lg2 = jnp.where(lane == i1, neg, lg)
    m2 = jnp.max(lg2, axis=-1, keepdims=True)
    i2 = jnp.min(jnp.where(lg2 == m2, lane, 128.0), axis=-1, keepdims=True)
    e2 = jnp.exp(m2 - m1)
    den = 1.0 + e2
    route_ref[...] = jnp.where(lane == 0.0, i1, jnp.where(lane == 1.0, i2,
                               jnp.where(lane == 2.0, 1.0 / den, jnp.where(lane == 3.0, e2 / den, 0.0))))


def _stage_o(layer, oh, os_, om, x, wo_b, ada4, ln_g, ln_b, router=None):
    in_specs = [
        _row_spec(HW), _row_spec(S5W), _row_spec(SSW), _row_spec(D), _const_spec((D, D)),
        *_mod_specs(layer, 2), _const_spec((1, D)), _const_spec((1, D)),
        *_mod_specs(layer, 4), *_mod_specs(layer, 3),
    ]
    args = [oh, os_, om, x, wo_b, ada4, ada4, ln_g, ln_b, ada4, ada4, ada4, ada4]
    out_specs = [_row_spec(D), _row_spec(D)]
    out_shape = [jax.ShapeDtypeStruct((ROWS, D), F32), jax.ShapeDtypeStruct((ROWS, D), BF16)]
    body = _o_body
    if router is not None:
        in_specs += [_const_spec((2, D, 128)), _const_spec((1, 128))]
        args += list(router)
        out_specs.append(_row_spec(128))
        out_specs[1] = pl.BlockSpec((TM, D // 128, 128), lambda i: (i, 0, 0))
        out_shape[1] = jax.ShapeDtypeStruct((ROWS, D // 128, 128), F32)
        out_shape.append(jax.ShapeDtypeStruct((ROWS, 128), F32))
        body = _o_router_body
    return pl.pallas_call(
        body, grid=(NT,), in_specs=in_specs, out_specs=out_specs, out_shape=out_shape,
        compiler_params=_cparams(("parallel",)),
        name="stage_o_router" if router is not None else "stage_o",
    )(*args)


def _ffn_body(te_ref, nu_ref, h_ref, wg_ref, wu_ref, wd_ref, o_ref, acc_ref, *h_scr):
    i = pl.program_id(0)
    j = pl.program_id(1)

    @pl.when(j == 0)
    def _():
        acc_ref[...] = jnp.zeros_like(acc_ref)
        if h_scr:
            for c in range(D // 128):
                h_scr[0][:, 128 * c:128 * (c + 1)] = h_ref[:, c, :].astype(BF16)

    @pl.when(i < nu_ref[0])
    def _():
        h = h_scr[0][...] if h_scr else h_ref[...]
        g = jnp.dot(h, wg_ref[...], preferred_element_type=F32)
        u = jnp.dot(h, wu_ref[...], preferred_element_type=F32)
        act = (_silu(g) * u).astype(BF16)
        acc_ref[...] += jnp.dot(act, wd_ref[...], preferred_element_type=F32)

    @pl.when(j == pl.num_programs(1) - 1)
    def _():
        o_ref[...] = acc_ref[...]


def _ffn(tile_expert, n_used, h, wg, wu, wd):
    rows = h.shape[0]
    nj = DFF // TF
    tiled_rows = h.ndim == 3

    def jblk(i, j, nu):
        return jnp.where(i < nu[0], j, nj - 1)

    if tiled_rows:
        h_spec = pl.BlockSpec((TMF, D // 128, 128), lambda i, j, te, nu: (i, 0, 0))
    else:
        h_spec = pl.BlockSpec((TMF, D), lambda i, j, te, nu: (i, 0))
    grid_spec = pltpu.PrefetchScalarGridSpec(
        num_scalar_prefetch=2,
        grid=(rows // TMF, nj),
        in_specs=[
            h_spec,
            pl.BlockSpec((None, D, TF), lambda i, j, te, nu: (te[i], 0, jblk(i, j, nu))),
            pl.BlockSpec((None, D, TF), lambda i, j, te, nu: (te[i], 0, jblk(i, j, nu))),
            pl.BlockSpec((None, TF, D), lambda i, j, te, nu: (te[i], jblk(i, j, nu), 0)),
        ],
        out_specs=pl.BlockSpec((TMF, D), lambda i, j, te, nu: (i, 0)),
        scratch_shapes=[pltpu.VMEM((TMF, D), F32)] + ([pltpu.VMEM((TMF, D), BF16)] if tiled_rows else []),
    )
    return pl.pallas_call(
        _ffn_body, grid_spec=grid_spec,
        out_shape=jax.ShapeDtypeStruct((rows, D), F32),
        compiler_params=_cparams(("parallel", "arbitrary")),
        name="ffn",
    )(tile_expert, n_used, h, wg, wu, wd)


def _final_body(x1_ref, *refs):
    ya_refs = refs[0:MOE_CHUNKS]
    yb_refs = refs[MOE_CHUNKS:2 * MOE_CHUNKS]
    route_ref, gp, gs, lng_ref, lnb_ref, yp_ref, ys_ref = refs[2 * MOE_CHUNKS:]
    i = pl.program_id(0)
    ya = ya_refs[0][...]
    yb = yb_refs[0][...]
    for c in range(1, MOE_CHUNKS):
        ya = jnp.where(i >= c * MOE_TILES, ya_refs[c][...], ya)
        yb = jnp.where(i >= c * MOE_TILES, yb_refs[c][...], yb)
    f = route_ref[:, 2:3] * ya + route_ref[:, 3:4] * yb
    y = _layer_norm(ALPHA * x1_ref[...] + _rowmod(i, gp, gs) * f, lng_ref[...], lnb_ref[...])

    @pl.when(i < NPT)
    def _():
        yp_ref[...] = y

    @pl.when(i >= NPT)
    def _():
        ys_ref[...] = y


def _stage_final(layer, x1, ya, yb, route, ada4, ln_g, ln_b):
    def chunk_spec(c):
        return pl.BlockSpec((TM, D), lambda i: (jnp.clip(i - c * MOE_TILES, 0, MOE_TILES - 1), 0))

    chunk_specs = [chunk_spec(c) for c in range(MOE_CHUNKS)]
    return pl.pallas_call(
        _final_body,
        grid=(NT,),
        in_specs=[
            _row_spec(D), *chunk_specs, *chunk_specs, _row_spec(128),
            *_mod_specs(layer, 5), _const_spec((1, D)), _const_spec((1, D)),
        ],
        out_specs=[
            pl.BlockSpec((TM, D), lambda i: (jnp.minimum(i, NPT - 1), 0)),
            pl.BlockSpec((TM, D), lambda i: (jnp.clip(i - NPT, 0, NST - 1), 0)),
        ],
        out_shape=[jax.ShapeDtypeStruct((ROWS_P, D), F32), jax.ShapeDtypeStruct((ROWS_S, D), F32)],
        compiler_params=_cparams(("arbitrary",)),
        name="stage_final",
    )(x1, *ya, *yb, route, ada4, ada4, ln_g, ln_b)


def _block_ones(n, blk):
    r = np.arange(n) // blk
    return r[:, None] == r[None, :]


def _consts():
    t = np.arange(CB)
    causal = t[:, None] >= t[None, :]
    same_seq = (t[:, None] // L_S) == (t[None, :] // L_S)
    r = np.arange(S5_ROWS)
    perm = r[None, :] == ((r % NB_P) * S5_TB + r // NB_P)[:, None]
    mats = {
        "perm": perm,
        "permt": perm.T,
        "e64": _block_ones(HW, HD),
        "g192": _block_ones(SSW, SSW // 2),
        "trilp": causal,
        "trils": causal & same_seq,
    }
    return {k: jnp.asarray(v.astype(np.float32), dtype=BF16) for k, v in mats.items()}


def _s5_params(a_re, a_im, log_dt, b_re, b_im, c_re, c_im, d, w_glu, b_glu):
    dt = jnp.exp(log_dt)[:, None]
    mag = jnp.exp(a_re * dt)
    lam_re, lam_im = mag * jnp.cos(a_im * dt), mag * jnp.sin(a_im * dt)
    den = a_re * a_re + a_im * a_im
    nr, ni = lam_re - 1.0, lam_im
    zr = (nr * a_re + ni * a_im) / den
    zi = (ni * a_re - nr * a_im) / den
    bbar_re = zr[..., None] * b_re - zi[..., None] * b_im
    bbar_im = zr[..., None] * b_im + zi[..., None] * b_re
    eye = jnp.eye(16, dtype=F32)
    blk = lambda bb: jnp.einsum('gph,gk->ghkp', bb, eye).reshape(S5W, S5N)
    hu, hs = S5W // 2, S5N // 2
    bblk = jnp.stack([jnp.stack([blk(bb)[hu * j:hu * (j + 1), hs * j:hs * (j + 1)] for j in range(2)])
                      for bb in (bbar_re, bbar_im)]).astype(BF16)
    cblk = lambda cc: jnp.einsum('ghp,gk->gpkh', cc, eye).reshape(S5N, S5W)
    ccat = jnp.stack([jnp.stack([cblk(cc)[hs * j:hs * (j + 1), hu * j:hu * (j + 1)] for cc in (c_re, -c_im)])
                      for j in range(2)]).astype(BF16)
    lr, li = lam_re.reshape(-1), lam_im.reshape(-1)
    pows = [(jnp.ones_like(lr), jnp.zeros_like(li))]
    for _ in range(8):
        pr, pi = pows[-1]
        pows.append((pr * lr - pi * li, pr * li + pi * lr))
    rows = jnp.arange(8)[:, None]
    tabs = []
    for dsh in (1, 2, 4):
        pr, pi = pows[dsh]
        tabs.append(jnp.where(rows >= dsh, jnp.concatenate([pr, pi])[None, :], 0.0))
    tabs.append(jnp.stack([jnp.concatenate(pows[r + 1]) for r in range(8)]))
    return {
        "tab": jnp.stack(tabs), "bblk": bblk, "ccat": ccat,
        "lam8": jnp.broadcast_to(jnp.concatenate([lr, li])[None, :], (NB_P, 2 * S5N)),
        "d": d.reshape(1, S5W), "wglu": w_glu.astype(BF16), "bglu": b_glu.reshape(1, S5W),
    }


def _pad_lanes(v, n=128):
    return jnp.pad(v, (0, n - v.shape[0])).reshape(1, n)


def kernel(x_prompt, x_sample, c_prompt, c_sample, state_hgrn, state_s5, state_ssd, state_ssd_conv, w_ada, b_ada, ln_g, ln_b, w_in, w_out, hgrn_lb_logits, hgrn_norm_g, s5_a_re, s5_a_im, s5_log_dt, s5_b_re, s5_b_im, s5_c_re, s5_c_im, s5_d, s5_w_glu, s5_b_glu, ssd_conv_w, ssd_conv_b, ssd_dt_bias, ssd_a_log, ssd_d, ssd_norm_g, ffn_w_gate, ffn_w_up, ffn_w_down, moe_w_router, moe_b_router, moe_w_gate, moe_w_up, moe_w_down):
    consts = _consts()
    c_all = jnp.concatenate([c_sample, c_prompt], axis=0)
    ada4 = _ada(c_all, w_ada, b_ada).reshape(DEPTH, NB_S + NB_P, 1, 6 * D)

    lb_all = jnp.cumsum(jax.nn.softmax(hgrn_lb_logits, axis=0), axis=0)
    lb_all = lb_all - lb_all[0]

    xp = x_prompt.reshape(ROWS_P, D)
    xs = x_sample.reshape(ROWS_S, D)
    new_h, new_s5, new_m, new_c = [], [], [], []
    x1 = f = route = None
    for l in range(DEPTH):
        w_in_b = jnp.pad(w_in[l], ((0, 0), (0, N_IN_PAD - N_IN))).astype(BF16)
        if l == 0:
            x, ph, ps, pm = _stage_a0(xp, xs, ada4, w_in_b)
        else:
            x, ph, ps, pm = _stage_a1(l, x1, f, ada4, ln_g[l - 1, 1].reshape(1, D), ln_b[l - 1, 1].reshape(1, D),
                                      w_in_b)
        oh, hg_p, hg_s = _hgrn(ph, lb_all[l].reshape(1, HW), hgrn_norm_g[l].reshape(1, HW), consts,
                               jnp.swapaxes(state_hgrn[l], -1, -2))
        s5p = _s5_params(s5_a_re[l], s5_a_im[l], s5_log_dt[l], s5_b_re[l], s5_b_im[l], s5_c_re[l], s5_c_im[l],
                         s5_d[l], s5_w_glu[l], s5_b_glu[l])
        os_p, s5_p = _s5_prompt(ps[:ROWS_P].reshape(NB_P, L_P, S5W), s5p, consts)
        os_s, s5_s = _s5_sample(ps[ROWS_P:], s5p, state_s5[l].reshape(NB_S, 1, 2 * S5N))
        os_ = jnp.concatenate([os_p.reshape(ROWS_P, S5W), os_s], axis=0)
        ssd_prm = {
            "cw": ssd_conv_w[l], "cb": ssd_conv_b[l].reshape(1, CONVC),
            "dtb": _pad_lanes(ssd_dt_bias[l]), "aneg": _pad_lanes(-jnp.exp(ssd_a_log[l])),
            "dx": jnp.repeat(ssd_d[l], HD).reshape(1, SSW), "sg": ssd_norm_g[l].reshape(1, SSW),
        }
        hist = jnp.pad(state_ssd_conv[l], ((0, 0), (0, L_S - 3), (0, 0))).reshape(ROWS_S, CONVC)
        om, ss_p, ss_s, tail_p, xbc_s = _ssd(pm, ssd_prm, consts, hist, state_ssd[l])
        conv_p = tail_p[:, 8 - 3:]
        conv_s = xbc_s.reshape(NB_S, L_S, CONVC)[:, L_S - 3:]
        new_h.append((jnp.swapaxes(hg_p, -1, -2), jnp.swapaxes(hg_s, -1, -2)))
        new_s5.append((s5_p.reshape(NB_P, 2, 16, 64), s5_s.reshape(NB_S, 2, 16, 64)))
        new_m.append((ss_p, ss_s))
        new_c.append((conv_p, conv_s))

        wo_b = w_out[l].astype(BF16)
        lg, lbb = ln_g[l, 0].reshape(1, D), ln_b[l, 0].reshape(1, D)
        j = l // 2
        if l % 2 == 0:
            x1, h2 = _stage_o(l, oh, os_, om, x, wo_b, ada4, lg, lbb)
            f = _ffn(jnp.zeros((ROWS // TMF,), jnp.int32), jnp.full((1,), ROWS // TMF, jnp.int32), h2,
                     ffn_w_gate[j:j + 1].astype(BF16),
                     ffn_w_up[j:j + 1].astype(BF16), ffn_w_down[j:j + 1].astype(BF16))
        else:
            wr = jnp.pad(moe_w_router[j], ((0, 0), (0, 128 - NEXP)))
            wr_hi = wr.astype(BF16)
            wr = jnp.stack([wr_hi, (wr - wr_hi.astype(F32)).astype(BF16)])
            br = _pad_lanes(moe_b_router[j])
            x1, h2, route = _stage_o(l, oh, os_, om, x, wo_b, ada4, lg, lbb, router=(wr, br))
            wg_b, wu_b, wd_b = (moe_w_gate[j].astype(BF16), moe_w_up[j].astype(BF16),
                                moe_w_down[j].astype(BF16))
            ya, yb = [], []
            for c in range(MOE_CHUNKS):
                flat_e = route[c * MOE_ROWS:(c + 1) * MOE_ROWS, 0:2].astype(jnp.int32).reshape(-1)
                onehot = (flat_e[:, None] == jnp.arange(NEXP)[None, :]).astype(jnp.int32)
                csum = jnp.cumsum(onehot, axis=0)
                counts = csum[-1]
                rank = jnp.take_along_axis(csum, flat_e[:, None], axis=1)[:, 0] - 1
                padded = ((counts + TMF - 1) // TMF) * TMF
                pend = jnp.cumsum(padded)
                pstart = pend - padded
                dest = pstart[flat_e] + rank
                n_pad = 2 * MOE_ROWS + NEXP * TMF
                tok = c * MOE_ROWS + jnp.arange(2 * MOE_ROWS, dtype=jnp.int32) // 2
                src_tok = jnp.full((n_pad,), c * MOE_ROWS, jnp.int32).at[dest].set(tok)
                tile_start = jnp.arange(n_pad // TMF, dtype=jnp.int32) * TMF
                tile_e = jnp.minimum(jnp.sum((pend[None, :] <= tile_start[:, None]).astype(jnp.int32), axis=1),
                                     NEXP - 1)
                n_used = (pend[NEXP - 1:NEXP] // TMF).astype(jnp.int32)
                y_sorted = _ffn(tile_e, n_used, h2[src_tok], wg_b, wu_b, wd_b)
                pos = dest.reshape(MOE_ROWS, 2)
                ya.append(y_sorted[pos[:, 0]])
                yb.append(y_sorted[pos[:, 1]])
    y_p, y_s = _stage_final(DEPTH - 1, x1, ya, yb, route, ada4, ln_g[DEPTH - 1, 1].reshape(1, D),
                            ln_b[DEPTH - 1, 1].reshape(1, D))
    stack = lambda lst, k: jnp.stack([t[k] for t in lst])
    return (y_p.reshape(NB_P, L_P, D), y_s.reshape(NB_S, L_S, D),
            stack(new_h, 0), stack(new_s5, 0), stack(new_m, 0), stack(new_c, 0),
            stack(new_h, 1), stack(new_s5, 1), stack(new_m, 1), stack(new_c, 1))
```

```python
import functools
import math

import jax
import jax.numpy as jnp
import numpy as np
from jax import lax
from jax.experimental import pallas as pl
from jax.experimental.pallas import tpu as pltpu

F32 = jnp.float32
BF16 = jnp.bfloat16

D = 1024
NB_P, L_P = 8, 2048
NB_S, L_S = 128, 8
ROWS_P = NB_P * L_P
ROWS_S = NB_S * L_S
ROWS = ROWS_P + ROWS_S
DEPTH = 2
HW = 384
S5W = 256
SSW = 384
NH = 6
HD = 64
S5N = 1024
CONVC = 640
N_IN = 2822
N_IN_PAD = 2944
P_HG = 1536
P_SS = 1152
DFF = 2816
NEXP = 8
ALPHA = (2 * DEPTH) ** 0.25
LN_EPS = 1e-5
RMS_EPS = 1e-6

TM = 512
NPT = ROWS_P // TM
NST = ROWS_S // TM
NT = NPT + NST
SEQ_PER_TILE = TM // L_S
TILES_PER_SEQ = L_P // TM

CB = 128
NPC = ROWS_P // CB
NSC = ROWS_S // CB
CH_PER_SEQ = L_P // CB
SEQ_PER_CB = CB // L_S

TMF = 512
TF = 1408
MOE_CHUNKS = 1
MOE_ROWS = ROWS // MOE_CHUNKS
MOE_TILES = MOE_ROWS // TM
VMEM_LIMIT = 56 * 1024 * 1024


def _cparams(sem):
    return pltpu.CompilerParams(dimension_semantics=sem, vmem_limit_bytes=VMEM_LIMIT)


def _bdot(a, b):
    return jnp.dot(a.astype(BF16), b.astype(BF16), preferred_element_type=F32)


def _bdot_nt(a, b):
    return lax.dot_general(a.astype(BF16), b.astype(BF16), (((1,), (1,)), ((), ())),
                           preferred_element_type=F32)


def _split_dot(x, e, passes):
    acc = None
    r = x
    for _ in range(passes):
        hi = r.astype(BF16)
        d = jnp.dot(hi, e, preferred_element_type=F32)
        acc = d if acc is None else acc + d
        r = r - hi.astype(F32)
    return acc


def _split_dot_l(e, x, passes):
    acc = None
    r = x
    for _ in range(passes):
        hi = r.astype(BF16)
        d = jnp.dot(e, hi, preferred_element_type=F32)
        acc = d if acc is None else acc + d
        r = r - hi.astype(F32)
    return acc


def _silu(x):
    return x * jax.nn.sigmoid(x)


def _layer_norm(x, g, b):
    mu = jnp.mean(x, -1, keepdims=True)
    xc = x - mu
    var = jnp.mean(xc * xc, -1, keepdims=True)
    return xc * lax.rsqrt(var + LN_EPS) * g + b


def _rowmod(i, p_ref, s_ref):
    s = jnp.broadcast_to(s_ref[...], (SEQ_PER_TILE, L_S, D)).reshape(TM, D)
    return jnp.where(i < NPT, p_ref[0], s)


ADA_TN = 1536


def _ada_body(c_ref, w_ref, b_ref, o_ref):
    o_ref[...] = _bdot(_silu(c_ref[...]), w_ref[...]) + b_ref[...]


def _ada(c_all, w_ada, b_ada):
    nc = c_all.shape[0]
    return pl.pallas_call(
        _ada_body,
        grid=(DEPTH, 6 * D // ADA_TN),
        in_specs=[
            pl.BlockSpec((nc, D), lambda l, j: (0, 0)),
            pl.BlockSpec((None, D, ADA_TN), lambda l, j: (l, 0, j)),
            pl.BlockSpec((None, 1, ADA_TN), lambda l, j: (l, 0, j)),
        ],
        out_specs=pl.BlockSpec((None, nc, ADA_TN), lambda l, j: (l, 0, j)),
        out_shape=jax.ShapeDtypeStruct((DEPTH, nc, 6 * D), F32),
        compiler_params=_cparams(("parallel", "parallel")),
        name="ada",
    )(c_all, w_ada, b_ada.reshape(DEPTH, 1, 6 * D))


def _mod_specs(layer, k):
    ps = pl.BlockSpec((None, 1, 1, D),
                      lambda i: (layer, NB_S + jnp.minimum(i // TILES_PER_SEQ, NB_P - 1), 0, k))
    ss = pl.BlockSpec((None, SEQ_PER_TILE, 1, D),
                      lambda i: (layer, jnp.clip(i - NPT, 0, NST - 1), 0, k))
    return [ps, ss]


def _row_spec(width):
    return pl.BlockSpec((TM, width), lambda i: (i, 0))


def _const_spec(shape):
    nd = len(shape)
    return pl.BlockSpec(shape, lambda *_: (0,) * nd)


def _proj_out(x, i, scp, scs, shp, shs, w_ref, ph_ref, ps_ref, pm_ref):
    h = x * (1.0 + _rowmod(i, scp, scs)) + _rowmod(i, shp, shs)
    proj = jnp.dot(h.astype(BF16), w_ref[...], preferred_element_type=F32)
    ph_ref[...] = proj[:, 0:P_HG]
    ps_ref[...] = proj[:, P_HG:P_HG + S5W]
    pm_ref[...] = proj[:, P_HG + S5W:N_IN_PAD]


def _a0_body(xp_ref, xs_ref, scp, scs, shp, shs, w_ref, x_ref, ph_ref, ps_ref, pm_ref):
    i = pl.program_id(0)
    x = jnp.where(i < NPT, xp_ref[...], xs_ref[...])
    x_ref[...] = x
    _proj_out(x, i, scp, scs, shp, shs, w_ref, ph_ref, ps_ref, pm_ref)


def _a1_body(x1_ref, f_ref, gp, gs, lng_ref, lnb_ref, scp, scs, shp, shs, w_ref,
             x_ref, ph_ref, ps_ref, pm_ref):
    i = pl.program_id(0)
    x = _layer_norm(ALPHA * x1_ref[...] + _rowmod(i, gp, gs) * f_ref[...], lng_ref[...], lnb_ref[...])
    x_ref[...] = x
    _proj_out(x, i, scp, scs, shp, shs, w_ref, ph_ref, ps_ref, pm_ref)


def _a_out():
    specs = [_row_spec(D), _row_spec(P_HG), _row_spec(S5W), _row_spec(P_SS)]
    shapes = [jax.ShapeDtypeStruct((ROWS, w), F32) for w in (D, P_HG, S5W, P_SS)]
    return specs, shapes


def _stage_a0(xp, xs, ada4, w_in_b):
    out_specs, out_shape = _a_out()
    return pl.pallas_call(
        _a0_body,
        grid=(NT,),
        in_specs=[
            pl.BlockSpec((TM, D), lambda i: (jnp.minimum(i, NPT - 1), 0)),
            pl.BlockSpec((TM, D), lambda i: (jnp.clip(i - NPT, 0, NST - 1), 0)),
            *_mod_specs(0, 1), *_mod_specs(0, 0),
            _const_spec((D, N_IN_PAD)),
        ],
        out_specs=out_specs, out_shape=out_shape,
        compiler_params=_cparams(("parallel",)),
        name="stage_a0",
    )(xp, xs, ada4, ada4, ada4, ada4, w_in_b)


def _stage_a1(layer, x1, f, ada4, ln_g, ln_b, w_in_b):
    out_specs, out_shape = _a_out()
    return pl.pallas_call(
        _a1_body,
        grid=(NT,),
        in_specs=[
            _row_spec(D), _row_spec(D),
            *_mod_specs(layer - 1, 5),
            _const_spec((1, D)), _const_spec((1, D)),
            *_mod_specs(layer, 1), *_mod_specs(layer, 0),
            _const_spec((D, N_IN_PAD)),
        ],
        out_specs=out_specs, out_shape=out_shape,
        compiler_params=_cparams(("parallel",)),
        name="stage_a1",
    )(x1, f, ada4, ada4, ln_g, ln_b, ada4, ada4, ada4, ada4, w_in_b)


def _mixer_blk(width, nsub=1):
    return pl.BlockSpec((nsub * CB, width), lambda i: (i, 0))


def _pstate_spec(shape, nsub=1):
    nd = len(shape)
    return pl.BlockSpec((1,) + shape,
                        lambda i: (jnp.minimum(i // (CH_PER_SEQ // nsub), NB_P - 1),) + (0,) * nd)


def _sstate_spec(shape, nsub=1):
    nd = len(shape)
    return pl.BlockSpec((nsub * SEQ_PER_CB,) + shape,
                        lambda i: (jnp.clip(i - NPC // nsub, 0, NSC // nsub - 1),) + (0,) * nd)


def _cumsum_rows(x, span):
    r = lax.broadcasted_iota(jnp.int32, x.shape, 0) & (span - 1)
    d = 1
    while d < span:
        x = x + jnp.where(r >= d, pltpu.roll(x, d, 0), 0.0)
        d *= 2
    return x


def _seq_last_rows(x):
    w = x.shape[-1]
    x3 = x.reshape(SEQ_PER_CB, L_S, w)
    return jnp.broadcast_to(x3[:, L_S - 1:L_S, :], (SEQ_PER_CB, L_S, w)).reshape(CB, w)


def _concat_heads(parts):
    return jnp.concatenate(parts, axis=1)


def _stack_select(shape, row_div, lane_div):
    r = lax.broadcasted_iota(jnp.int32, shape, 0) // row_div
    c = lax.broadcasted_iota(jnp.int32, shape, 1) // lane_div
    return r == c


def _seq_expand_lanes(qh):
    q2 = jnp.concatenate([qh, qh], axis=1)
    q16 = jnp.concatenate([q2] * (SEQ_PER_CB // 2), axis=1)
    return jnp.where(_stack_select((CB, SEQ_PER_CB * HD), L_S, HD), q16, 0.0)


def _seq_expand_rows(xt):
    t = jnp.broadcast_to(xt[None], (SEQ_PER_CB, HD, CB)).reshape(SEQ_PER_CB * HD, CB)
    return jnp.where(_stack_select((SEQ_PER_CB * HD, CB), HD, L_S), t, 0.0)


def _fold_seq_lanes(full):
    acc = full[:, 0:128]
    for j in range(1, SEQ_PER_CB * HD // 128):
        acc = acc + full[:, 128 * j:128 * (j + 1)]
    return acc[:, 0:HD] + acc[:, HD:2 * HD]


HGRN_BASE = 32
EXP_RANGE_MAX = 80.0


def _hgrn_block(prompt, p_ref, lb_ref, hg_ref, e64_ref, oh_ref, st_scr, o_scr, i,
                stp_out=None, sts_in=None, sts_out=None):
    lb = lb_ref[...]
    qr = p_ref[:, 0:HW]
    fr = p_ref[:, HW:2 * HW]
    v = p_ref[:, 2 * HW:3 * HW]
    gr = p_ref[:, 3 * HW:4 * HW]
    ls = jnp.minimum(fr, 0.0) - jnp.log1p(jnp.exp(-jnp.abs(fr)))
    a = jnp.log(lb)
    bb = jnp.log1p(-lb) + ls
    lf = jnp.maximum(a, bb) + jnp.log1p(jnp.exp(-jnp.abs(a - bb)))
    kk = (1.0 - lb) * jax.nn.sigmoid(-fr)
    q = _silu(qr)
    b = _cumsum_rows(lf, CB if prompt else L_S)

    e64 = e64_ref[...]
    ti = lax.broadcasted_iota(jnp.int32, (CB, CB), 0)
    si = lax.broadcasted_iota(jnp.int32, (CB, CB), 1)

    def diag8():
        nsub = CB // 8
        b3 = b.reshape(nsub, 8, HW)
        q3 = q.reshape(nsub, 8, HW)
        k3 = kk.reshape(nsub, 8, HW)
        v3 = v.reshape(nsub, 8, HW)
        r3 = lax.broadcasted_iota(jnp.int32, (nsub, 8, HW), 1)
        o = jnp.zeros((CB, HW), F32)
        for s in range(8):
            dlt = jnp.minimum(b3 - b3[:, s:s + 1, :], 0.0)
            w = jnp.where(r3 >= s, jnp.exp(dlt), 0.0) * q3 * k3[:, s:s + 1, :]
            hsum = jnp.dot(w.reshape(CB, HW).astype(BF16), e64, preferred_element_type=F32)
            o = o + hsum * jnp.broadcast_to(v3[:, s:s + 1, :], (nsub, 8, HW)).reshape(CB, HW)
        return o

    def level_terms(m):
        terms = []
        while m < CB:
            nb = CB // (2 * m)
            b4 = b.reshape(nb, 2 * m, HW)
            bmid = b4[:, m - 1:m, :]
            pos = lax.broadcasted_iota(jnp.int32, (nb, 2 * m, HW), 1)
            qq = jnp.where(pos >= m, q.reshape(nb, 2 * m, HW) * jnp.exp(jnp.minimum(b4 - bmid, 0.0)), 0.0)
            kq = jnp.where(pos < m, kk.reshape(nb, 2 * m, HW) * jnp.exp(jnp.minimum(bmid - b4, 0.0)), 0.0)
            terms.append((qq.reshape(CB, HW), kq.reshape(CB, HW), (ti // (2 * m)) == (si // (2 * m))))
            m *= 2
        return terms

    def scores_times_v(terms):
        lane = lax.broadcasted_iota(jnp.int32, (CB, 2 * HD), 1)
        parts = []
        for j in range(NH // 2):
            scs = []
            for h in (2 * j, 2 * j + 1):
                sl = slice(HD * h, HD * (h + 1))
                sc = None
                for qq, kq, keep in terms:
                    t = jnp.where(keep, _bdot_nt(qq[:, sl], kq[:, sl]), 0.0)
                    sc = t if sc is None else sc + t
                scs.append(sc)
            vp = v[:, 2 * HD * j:2 * HD * (j + 1)]
            vdiag = jnp.concatenate([jnp.where(lane < HD, vp, 0.0), jnp.where(lane >= HD, vp, 0.0)], axis=0)
            parts.append(_bdot(jnp.concatenate(scs, axis=1), vdiag))
        return _concat_heads(parts)

    base = HGRN_BASE if prompt else L_S
    nbase = CB // base
    bb3 = b.reshape(nbase, base, HW)
    top = bb3[:, 0:1, :] - lf.reshape(nbase, base, HW)[:, 0:1, :]
    decay_range = jnp.max(top - bb3[:, base - 1:base, :])
    fast = decay_range <= EXP_RANGE_MAX

    @pl.when(jnp.logical_not(fast))
    def _():
        o_scr[...] = diag8() + (scores_times_v(level_terms(8)) if prompt else 0.0)

    qf = (q.reshape(nbase, base, HW) * jnp.exp(bb3 - top)).reshape(CB, HW)
    kf = (kk.reshape(nbase, base, HW) * jnp.exp(top - bb3)).reshape(CB, HW)
    keep = ((ti // base) == (si // base)) & (si <= ti)
    o_fast = scores_times_v([(qf, kf, keep)] + (level_terms(base) if prompt else []))
    o = jnp.where(fast, o_fast, o_scr[...])
    qt = q * jnp.exp(b)
    if prompt:
        blast = b[CB - 1:CB, :]
        kd = kk * jnp.exp(blast - b)
        vt = v.T
        same_head = (ti // HD) == (si // HD)
        parts = []
        for j in range(NH // 2):
            pr = slice(2 * HD * j, 2 * HD * (j + 1))
            st = st_scr[j]
            parts.append(o[:, pr] + _bdot_nt(qt[:, pr], st))
            st_scr[j] = st * jnp.exp(blast[:, pr]) + jnp.where(same_head, _bdot(vt[pr, :], kd[:, pr]), 0.0)

        @pl.when(i % CH_PER_SEQ == CH_PER_SEQ - 1)
        def _():
            stp_out[0] = st_scr[...]
    else:
        blast = _seq_last_rows(b)
        kd = kk * jnp.exp(blast - b)
        dec = jnp.exp(blast)
        vt = v.T
        parts = []
        for h in range(NH):
            sl = slice(HD * h, HD * (h + 1))
            sts = sts_in[:, h].reshape(SEQ_PER_CB * HD, HD)
            full = _bdot_nt(qt[:, sl], sts)
            sel = jnp.where(_stack_select((CB, SEQ_PER_CB * HD), L_S, HD), full, 0.0)
            parts.append(o[:, sl] + _fold_seq_lanes(sel))
            dec3 = dec[:, sl].reshape(SEQ_PER_CB, L_S, HD)[:, L_S - 1:L_S, :]
            dec_rows = jnp.broadcast_to(dec3, (SEQ_PER_CB, HD, HD)).reshape(SEQ_PER_CB * HD, HD)
            upd = _bdot(_seq_expand_rows(vt[sl, :]), kd[:, sl])
            sts_out[:, h] = (sts * dec_rows + upd).reshape(SEQ_PER_CB, HD, HD)
    oall = _concat_heads(parts)
    ms = _split_dot(oall * oall, e64, 1) * (1.0 / HD)
    oh_ref[...] = oall * lax.rsqrt(ms + RMS_EPS) * hg_ref[...] * _silu(gr)


def _hgrn_body(p_ref, lb_ref, hg_ref, e64_ref, sts_in,
               oh_ref, stp_out, sts_out, st_scr, o_scr):
    i = pl.program_id(0)

    @pl.when(i == 0)
    def _():
        o_scr[...] = jnp.zeros_like(o_scr)

    @pl.when((i < NPC) & (i % CH_PER_SEQ == 0))
    def _():
        st_scr[...] = jnp.zeros_like(st_scr)

    @pl.when(i < NPC)
    def _():
        _hgrn_block(True, p_ref, lb_ref, hg_ref, e64_ref, oh_ref, st_scr, o_scr, i, stp_out=stp_out)

    @pl.when(i >= NPC)
    def _():
        _hgrn_block(False, p_ref, lb_ref, hg_ref, e64_ref, oh_ref, st_scr, o_scr, i,
                    sts_in=sts_in, sts_out=sts_out)


def _hgrn(ph, lb, hg, consts, st_t):
    return pl.pallas_call(
        _hgrn_body,
        grid=(NPC + NSC,),
        in_specs=[
            _mixer_blk(P_HG), _const_spec((1, HW)), _const_spec((1, HW)),
            _const_spec((HW, HW)),
            _sstate_spec((NH, HD, HD)),
        ],
        out_specs=[_mixer_blk(HW), _pstate_spec((NH // 2, 2 * HD, 2 * HD)), _sstate_spec((NH, HD, HD))],
        out_shape=[
            jax.ShapeDtypeStruct((ROWS, HW), F32),
            jax.ShapeDtypeStruct((NB_P, NH // 2, 2 * HD, 2 * HD), F32),
            jax.ShapeDtypeStruct((NB_S, NH, HD, HD), F32),
        ],
        scratch_shapes=[pltpu.VMEM((NH // 2, 2 * HD, 2 * HD), F32), pltpu.VMEM((CB, HW), F32)],
        compiler_params=_cparams(("arbitrary",)),
        name="hgrn",
    )(ph, lb, hg, consts["e64"], st_t)


def _cmul_add(hr, hi, lr, li, sr, si):
    return hr + lr * sr - li * si, hi + lr * si + li * sr


def _s5_project(ub, bblk_ref):
    halves = [ub[:, (S5W // 2) * j:(S5W // 2) * (j + 1)] for j in range(2)]
    hr = jnp.concatenate([jnp.dot(halves[j], bblk_ref[0, j], preferred_element_type=F32) for j in range(2)], axis=1)
    hi = jnp.concatenate([jnp.dot(halves[j], bblk_ref[1, j], preferred_element_type=F32) for j in range(2)], axis=1)
    return hr, hi


def _s5_readout(h_scr, u, ccat_ref, d_ref, wglu_ref, bglu_ref):
    hs = S5N // 2
    ch = [_bdot(h_scr[:, hs * j:hs * (j + 1)], ccat_ref[j, 0])
          + _bdot(h_scr[:, S5N + hs * j:S5N + hs * (j + 1)], ccat_ref[j, 1]) for j in range(2)]
    y = jnp.concatenate(ch, axis=1) + d_ref[...] * u
    c0 = math.sqrt(2.0 / math.pi)
    y = y * (0.5 * (1.0 + jnp.tanh(c0 * (y + 0.044715 * (y * y * y)))))
    return y * jax.nn.sigmoid(_bdot(y, wglu_ref[...]) + bglu_ref[...])


S5_TB = 32
S5_ROWS = NB_P * S5_TB


def _s5_prompt_body(p_ref, perm_ref, permt_ref, lam_ref, bblk_ref, ccat_ref, d_ref, wglu_ref, bglu_ref,
                    os_ref, st_out, carry_scr, h_scr):
    i = pl.program_id(0)

    @pl.when(i == 0)
    def _():
        carry_scr[...] = jnp.zeros_like(carry_scr)

    u = p_ref[...].reshape(S5_ROWS, S5W)
    u_hi = u.astype(BF16)
    u_lo = (u - u_hi.astype(F32)).astype(BF16)
    perm = perm_ref[...]
    up_hi = jnp.dot(perm, u_hi, preferred_element_type=F32)
    up = up_hi + jnp.dot(perm, u_lo, preferred_element_type=F32)
    hr, hi = _s5_project(up_hi.astype(BF16), bblk_ref)
    lr = lam_ref[:, 0:S5N]
    li = lam_ref[:, S5N:2 * S5N]
    cr = carry_scr[:, 0:S5N]
    ci = carry_scr[:, S5N:2 * S5N]
    for t in range(S5_TB):
        rows = slice(NB_P * t, NB_P * (t + 1))
        cr, ci = _cmul_add(hr[rows], hi[rows], lr, li, cr, ci)
        h_scr[rows, 0:S5N] = cr
        h_scr[rows, S5N:2 * S5N] = ci
    carry_scr[:, 0:S5N] = cr
    carry_scr[:, S5N:2 * S5N] = ci
    out = _s5_readout(h_scr, up, ccat_ref, d_ref, wglu_ref, bglu_ref)
    os_ref[...] = jnp.dot(permt_ref[...], out.astype(BF16), preferred_element_type=F32).reshape(NB_P, S5_TB, S5W)

    @pl.when(i == pl.num_programs(0) - 1)
    def _():
        st_out[...] = carry_scr[...]


def _s5_prompt(ps3, prm, consts):
    return pl.pallas_call(
        _s5_prompt_body,
        grid=(L_P // S5_TB,),
        in_specs=[
            pl.BlockSpec((NB_P, S5_TB, S5W), lambda i: (0, i, 0)),
            _const_spec((S5_ROWS, S5_ROWS)), _const_spec((S5_ROWS, S5_ROWS)), _const_spec((NB_P, 2 * S5N)),
            _const_spec((2, 2, S5W // 2, S5N // 2)), _const_spec((2, 2, S5N // 2, S5W // 2)),
            _const_spec((1, S5W)), _const_spec((S5W, S5W)), _const_spec((1, S5W)),
        ],
        out_specs=[pl.BlockSpec((NB_P, S5_TB, S5W), lambda i: (0, i, 0)), _const_spec((NB_P, 2 * S5N))],
        out_shape=[jax.ShapeDtypeStruct((NB_P, L_P, S5W), F32), jax.ShapeDtypeStruct((NB_P, 2 * S5N), F32)],
        scratch_shapes=[pltpu.VMEM((NB_P, 2 * S5N), F32), pltpu.VMEM((S5_ROWS, 2 * S5N), F32)],
        compiler_params=_cparams(("arbitrary",)),
        name="s5_prompt",
    )(ps3, consts["perm"], consts["permt"], prm["lam8"], prm["bblk"], prm["ccat"], prm["d"], prm["wglu"],
      prm["bglu"])


def _s5_sample_body(p_ref, tab_ref, bblk_ref, ccat_ref, d_ref, wglu_ref, bglu_ref, s5s_in,
                    os_ref, s5s_out, h_scr):
    u = p_ref[...]
    hr, hi = _s5_project(u.astype(BF16), bblk_ref)
    nsub = CB // 8
    for idx, dsh in enumerate((1, 2, 4)):
        sr = pltpu.roll(hr, dsh, 0).reshape(nsub, 8, S5N)
        si = pltpu.roll(hi, dsh, 0).reshape(nsub, 8, S5N)
        lr = tab_ref[idx, :, 0:S5N][None]
        li = tab_ref[idx, :, S5N:2 * S5N][None]
        nr, ni = _cmul_add(hr.reshape(nsub, 8, S5N), hi.reshape(nsub, 8, S5N), lr, li, sr, si)
        hr = nr.reshape(CB, S5N)
        hi = ni.reshape(CB, S5N)
    tcr = tab_ref[3, :, 0:S5N]
    tci = tab_ref[3, :, S5N:2 * S5N]
    cr = s5s_in[:, :, 0:S5N]
    ci = s5s_in[:, :, S5N:2 * S5N]
    tr, tim = _cmul_add(hr.reshape(nsub, 8, S5N), hi.reshape(nsub, 8, S5N), tcr[None], tci[None], cr, ci)
    h_scr[:, 0:S5N] = tr.reshape(CB, S5N)
    h_scr[:, S5N:2 * S5N] = tim.reshape(CB, S5N)
    sb = lax.broadcasted_iota(jnp.int32, (SEQ_PER_CB, CB), 0)
    st = lax.broadcasted_iota(jnp.int32, (SEQ_PER_CB, CB), 1)
    sel = (st == L_S * sb + (L_S - 1)).astype(BF16)
    s5s_out[...] = _split_dot_l(sel, h_scr[...], 3)
    os_ref[...] = _s5_readout(h_scr, u, ccat_ref, d_ref, wglu_ref, bglu_ref)


def _s5_sample(ps_s, prm, st):
    seqs = lambda shape: pl.BlockSpec((SEQ_PER_CB,) + shape, lambda i: (i,) + (0,) * len(shape))
    return pl.pallas_call(
        _s5_sample_body,
        grid=(NSC,),
        in_specs=[
            _mixer_blk(S5W), _const_spec((4, 8, 2 * S5N)), _const_spec((2, 2, S5W // 2, S5N // 2)),
            _const_spec((2, 2, S5N // 2, S5W // 2)), _const_spec((1, S5W)), _const_spec((S5W, S5W)),
            _const_spec((1, S5W)), seqs((1, 2 * S5N)),
        ],
        out_specs=[_mixer_blk(S5W), seqs((2 * S5N,))],
        out_shape=[jax.ShapeDtypeStruct((ROWS_S, S5W), F32), jax.ShapeDtypeStruct((NB_S, 2 * S5N), F32)],
        scratch_shapes=[pltpu.VMEM((CB, 2 * S5N), F32)],
        compiler_params=_cparams(("parallel",)),
        name="s5_sample",
    )(ps_s, prm["tab"], prm["bblk"], prm["ccat"], prm["d"], prm["wglu"], prm["bglu"], st)


def _ssd_block(prompt, p_ref, cw_ref, cb_ref, dtb_ref, aneg_ref, dx_ref, sg_ref,
               g192_ref, tril_ref, om_ref, cbuf, st_scr, sub, is_last=None, ssp_out=None, convp_out=None,
               hist_ref=None, sss_in=None, sss_out=None, convs_out=None):
    rows = slice(sub * CB, (sub + 1) * CB)
    seqs = slice(sub * SEQ_PER_CB, (sub + 1) * SEQ_PER_CB)
    z = p_ref[rows, 0:SSW]
    xbc = p_ref[rows, SSW:SSW + CONVC]
    dtr = p_ref[rows, SSW + CONVC:P_SS]
    acc = cb_ref[...] + cw_ref[3:4, :] * xbc
    if prompt:
        cbuf[8:8 + CB, :] = xbc
        for k in (1, 2, 3):
            acc = acc + cw_ref[3 - k:4 - k, :] * cbuf[8 - k:8 - k + CB, :]
        cbuf[0:8, :] = cbuf[CB:CB + 8, :]
    else:
        convs_out[rows, :] = xbc
        tl = lax.broadcasted_iota(jnp.int32, (CB, CONVC), 0) % L_S
        hist = hist_ref[rows, :]
        for k in (1, 2, 3):
            hk = hist if k == 3 else pltpu.roll(hist, CB - (3 - k), 0)
            sh = jnp.where(tl >= k, pltpu.roll(xbc, k, 0), hk)
            acc = acc + cw_ref[3 - k:4 - k, :] * sh
    xc = _silu(acc)
    xs = xc[:, 0:SSW]
    bm = xc[:, SSW:SSW + 2 * HD]
    cm = xc[:, SSW + 2 * HD:CONVC]
    xdt = dtr + dtb_ref[...]
    dt = jnp.maximum(xdt, 0.0) + jnp.log1p(jnp.exp(-jnp.abs(xdt)))
    la = dt * aneg_ref[...]
    tril = tril_ref[...]
    b6 = _cumsum_rows(la, CB if prompt else L_S)
    lane = lax.broadcasted_iota(jnp.int32, (CB, 128), 1)
    bxw = [jnp.broadcast_to(b6[:, h:h + 1], (CB, 128)) for h in range(NH)]
    dtw = [jnp.broadcast_to(dt[:, h:h + 1], (CB, 128)) for h in range(NH)]
    pair = lambda cols: _concat_heads([jnp.where(lane < HD, cols[2 * j], cols[2 * j + 1]) for j in range(NH // 2)])
    bx = pair(bxw)
    dtx = pair(dtw)
    bm_rep = _concat_heads([bm[:, 0:HD]] * 3 + [bm[:, HD:2 * HD]] * 3)
    cm_rep = _concat_heads([cm[:, 0:HD]] * 3 + [cm[:, HD:2 * HD]] * 3)
    kh = bm_rep * dtx
    qt = cm_rep * jnp.exp(bx)
    blast = bx[CB - 1:CB, :] if prompt else _seq_last_rows(bx)
    kd = kh * jnp.exp(blast - bx)
    kdt = kd.T
    mask = tril > 0
    parts = []
    for h in range(NH):
        sl = slice(HD * h, HD * (h + 1))
        bcol = bxw[h]
        decay = jnp.exp(jnp.where(mask, bcol - bcol.T, -1e30))
        sc = _bdot_nt(cm_rep[:, sl], kh[:, sl]) * decay
        oh = _bdot(sc, xs[:, sl])
        if prompt:
            st = st_scr[h]
            oh = oh + _bdot(qt[:, sl], st)
            st_scr[h] = st * jnp.exp(blast[:, sl]) + _bdot(kdt[sl, :], xs[:, sl])
        else:
            sts = sss_in[seqs, h].reshape(SEQ_PER_CB * HD, HD)
            oh = oh + _bdot(_seq_expand_lanes(qt[:, sl]), sts)
            dec3 = jnp.exp(blast[:, sl]).reshape(SEQ_PER_CB, L_S, HD)[:, L_S - 1:L_S, :]
            dec_rows = jnp.broadcast_to(dec3, (SEQ_PER_CB, HD, HD)).reshape(SEQ_PER_CB * HD, HD)
            upd = _bdot(_seq_expand_rows(kdt[sl, :]), xs[:, sl])
            sss_out[seqs, h] = (sts * dec_rows + upd).reshape(SEQ_PER_CB, HD, HD)
        parts.append(oh)
    if is_last is not None:
        @pl.when(is_last)
        def _():
            ssp_out[0] = st_scr[...]
            convp_out[0] = cbuf[0:8, :]
    y = (_concat_heads(parts) + dx_ref[...] * xs) * _silu(z)
    ms = _split_dot(y * y, g192_ref[...], 1) * (1.0 / (SSW // 2))
    om_ref[rows, :] = y * lax.rsqrt(ms + RMS_EPS) * sg_ref[...]


def _ssd_body(p_ref, cw_ref, cb_ref, dtb_ref, aneg_ref, dx_ref, sg_ref, g192_ref,
              trilp_ref, trils_ref, hist_ref, sss_in, om_ref, ssp_out, sss_out, convp_out, convs_out,
              cbuf, st_scr):
    i = pl.program_id(0)
    common = (p_ref, cw_ref, cb_ref, dtb_ref, aneg_ref, dx_ref, sg_ref, g192_ref)
    npc, steps_per_seq = NPC // SSD_SUB, CH_PER_SEQ // SSD_SUB

    @pl.when((i < npc) & (i % steps_per_seq == 0))
    def _():
        st_scr[...] = jnp.zeros_like(st_scr)
        cbuf[...] = jnp.zeros_like(cbuf)

    @pl.when(i < npc)
    def _():
        for sub in range(SSD_SUB):
            is_last = (i % steps_per_seq == steps_per_seq - 1) if sub == SSD_SUB - 1 else None
            _ssd_block(True, *common, trilp_ref, om_ref, cbuf, st_scr, sub, is_last=is_last,
                       ssp_out=ssp_out, convp_out=convp_out)

    @pl.when(i >= npc)
    def _():
        for sub in range(SSD_SUB):
            _ssd_block(False, *common, trils_ref, om_ref, cbuf, st_scr, sub, hist_ref=hist_ref,
                       sss_in=sss_in, sss_out=sss_out, convs_out=convs_out)


SSD_SUB = 2


def _ssd(pm, prm, consts, hist, st):
    sample_rows = pl.BlockSpec((SSD_SUB * CB, CONVC),
                               lambda i: (jnp.clip(i - NPC // SSD_SUB, 0, NSC // SSD_SUB - 1), 0))
    return pl.pallas_call(
        _ssd_body,
        grid=((NPC + NSC) // SSD_SUB,),
        in_specs=[
            _mixer_blk(P_SS, SSD_SUB), _const_spec((4, CONVC)), _const_spec((1, CONVC)),
            _const_spec((1, 128)), _const_spec((1, 128)), _const_spec((1, SSW)), _const_spec((1, SSW)),
            _const_spec((SSW, SSW)),
            _const_spec((CB, CB)), _const_spec((CB, CB)),
            sample_rows,
            _sstate_spec((NH, HD, HD), SSD_SUB),
        ],
        out_specs=[_mixer_blk(SSW, SSD_SUB), _pstate_spec((NH, HD, HD), SSD_SUB),
                   _sstate_spec((NH, HD, HD), SSD_SUB), _pstate_spec((8, CONVC), SSD_SUB), sample_rows],
        out_shape=[
            jax.ShapeDtypeStruct((ROWS, SSW), F32),
            jax.ShapeDtypeStruct((NB_P, NH, HD, HD), F32),
            jax.ShapeDtypeStruct((NB_S, NH, HD, HD), F32),
            jax.ShapeDtypeStruct((NB_P, 8, CONVC), F32),
            jax.ShapeDtypeStruct((ROWS_S, CONVC), F32),
        ],
        scratch_shapes=[pltpu.VMEM((CB + 8, CONVC), F32), pltpu.VMEM((NH, HD, HD), F32)],
        compiler_params=_cparams(("arbitrary",)),
        name="ssd",
    )(pm, prm["cw"], prm["cb"], prm["dtb"], prm["aneg"], prm["dx"], prm["sg"],
      consts["g192"], consts["trilp"], consts["trils"], hist, st)


def _o_core(i, oh_ref, os_ref, om_ref, x_ref, wo_ref, gp, gs, lng_ref, lnb_ref, scp, scs, shp, shs):
    mix = (_bdot(oh_ref[...], wo_ref[0:HW, :]) + _bdot(os_ref[...], wo_ref[HW:HW + S5W, :])
           + _bdot(om_ref[...], wo_ref[HW + S5W:D, :]))
    x1 = _layer_norm(ALPHA * x_ref[...] + _rowmod(i, gp, gs) * mix, lng_ref[...], lnb_ref[...])
    h2 = x1 * (1.0 + _rowmod(i, scp, scs)) + _rowmod(i, shp, shs)
    return x1, h2


def _o_body(oh_ref, os_ref, om_ref, x_ref, wo_ref, gp, gs, lng_ref, lnb_ref, scp, scs, shp, shs,
            x1_ref, h2_ref):
    i = pl.program_id(0)
    x1, h2 = _o_core(i, oh_ref, os_ref, om_ref, x_ref, wo_ref, gp, gs, lng_ref, lnb_ref, scp, scs, shp, shs)
    x1_ref[...] = x1
    h2_ref[...] = h2.astype(BF16)


def _o_router_body(oh_ref, os_ref, om_ref, x_ref, wo_ref, gp, gs, lng_ref, lnb_ref, scp, scs, shp, shs,
                   wr_ref, br_ref, x1_ref, h2_ref, route_ref):
    i = pl.program_id(0)
    x1, h2 = _o_core(i, oh_ref, os_ref, om_ref, x_ref, wo_ref, gp, gs, lng_ref, lnb_ref, scp, scs, shp, shs)
    x1_ref[...] = x1
    h2_ref[...] = h2
    h_hi = h2.astype(BF16)
    h_lo = (h2 - h_hi.astype(F32)).astype(BF16)
    logits = (jnp.dot(h_hi, wr_ref[0], preferred_element_type=F32)
              + jnp.dot(h_lo, wr_ref[0], preferred_element_type=F32)
              + jnp.dot(h_hi, wr_ref[1], preferred_element_type=F32)) + br_ref[...]
    lane = lax.broadcasted_iota(jnp.int32, (TM, 128), 1).astype(F32)
    neg = -jnp.inf
    lg = jnp.where(lane < NEXP, logits, neg)
    m1 = jnp.max(lg, axis=-1, keepdims=True)
    i1 = jnp.min(jnp.where(lg == m1, lane, 128.0), axis=-1, keepdims=True)
    lg2 = jnp.where(lane == i1, neg, lg)
    m2 = jnp.max(lg2, axis=-1, keepdims=True)
    i2 = jnp.min(jnp.where(lg2 == m2, lane, 128.0), axis=-1, keepdims=True)
    e2 = jnp.exp(m2 - m1)
    den = 1.0 + e2
    route_ref[...] = jnp.where(lane == 0.0, i1, jnp.where(lane == 1.0, i2,
                               jnp.where(lane == 2.0, 1.0 / den, jnp.where(lane == 3.0, e2 / den, 0.0))))


def _stage_o(layer, oh, os_, om, x, wo_b, ada4, ln_g, ln_b, router=None):
    in_specs = [
        _row_spec(HW), _row_spec(S5W), _row_spec(SSW), _row_spec(D), _const_spec((D, D)),
        *_mod_specs(layer, 2), _const_spec((1, D)), _const_spec((1, D)),
        *_mod_specs(layer, 4), *_mod_specs(layer, 3),
    ]
    args = [oh, os_, om, x, wo_b, ada4, ada4, ln_g, ln_b, ada4, ada4, ada4, ada4]
    out_specs = [_row_spec(D), _row_spec(D)]
    out_shape = [jax.ShapeDtypeStruct((ROWS, D), F32), jax.ShapeDtypeStruct((ROWS, D), BF16)]
    body = _o_body
    if router is not None:
        in_specs += [_const_spec((2, D, 128)), _const_spec((1, 128))]
        args += list(router)
        out_specs.append(_row_spec(128))
        out_shape[1] = jax.ShapeDtypeStruct((ROWS, D), F32)
        out_shape.append(jax.ShapeDtypeStruct((ROWS, 128), F32))
        body = _o_router_body
    return pl.pallas_call(
        body, grid=(NT,), in_specs=in_specs, out_specs=out_specs, out_shape=out_shape,
        compiler_params=_cparams(("parallel",)),
        name="stage_o_router" if router is not None else "stage_o",
    )(*args)


def _ffn_body(te_ref, nu_ref, h_ref, wg_ref, wu_ref, wd_ref, o_ref, acc_ref):
    i = pl.program_id(0)
    j = pl.program_id(1)

    @pl.when(j == 0)
    def _():
        acc_ref[...] = jnp.zeros_like(acc_ref)

    @pl.when(i < nu_ref[0])
    def _():
        h = h_ref[...].astype(BF16)
        g = jnp.dot(h, wg_ref[...], preferred_element_type=F32)
        u = jnp.dot(h, wu_ref[...], preferred_element_type=F32)
        act = (_silu(g) * u).astype(BF16)
        acc_ref[...] += jnp.dot(act, wd_ref[...], preferred_element_type=F32)

    @pl.when(j == pl.num_programs(1) - 1)
    def _():
        o_ref[...] = acc_ref[...]


def _ffn(tile_expert, n_used, h, wg, wu, wd):
    rows = h.shape[0]
    nj = DFF // TF

    def jblk(i, j, nu):
        return jnp.where(i < nu[0], j, nj - 1)

    grid_spec = pltpu.PrefetchScalarGridSpec(
        num_scalar_prefetch=2,
        grid=(rows // TMF, nj),
        in_specs=[
            pl.BlockSpec((TMF, D), lambda i, j, te, nu: (i, 0)),
            pl.BlockSpec((None, D, TF), lambda i, j, te, nu: (te[i], 0, jblk(i, j, nu))),
            pl.BlockSpec((None, D, TF), lambda i, j, te, nu: (te[i], 0, jblk(i, j, nu))),
            pl.BlockSpec((None, TF, D), lambda i, j, te, nu: (te[i], jblk(i, j, nu), 0)),
        ],
        out_specs=pl.BlockSpec((TMF, D), lambda i, j, te, nu: (i, 0)),
        scratch_shapes=[pltpu.VMEM((TMF, D), F32)],
    )
    return pl.pallas_call(
        _ffn_body, grid_spec=grid_spec,
        out_shape=jax.ShapeDtypeStruct((rows, D), F32),
        compiler_params=_cparams(("parallel", "arbitrary")),
        name="ffn",
    )(tile_expert, n_used, h, wg, wu, wd)


def _final_body(x1_ref, *refs):
    ya_refs = refs[0:MOE_CHUNKS]
    yb_refs = refs[MOE_CHUNKS:2 * MOE_CHUNKS]
    route_ref, gp, gs, lng_ref, lnb_ref, yp_ref, ys_ref = refs[2 * MOE_CHUNKS:]
    i = pl.program_id(0)
    ya = ya_refs[0][...]
    yb = yb_refs[0][...]
    for c in range(1, MOE_CHUNKS):
        ya = jnp.where(i >= c * MOE_TILES, ya_refs[c][...], ya)
        yb = jnp.where(i >= c * MOE_TILES, yb_refs[c][...], yb)
    f = route_ref[:, 2:3] * ya + route_ref[:, 3:4] * yb
    y = _layer_norm(ALPHA * x1_ref[...] + _rowmod(i, gp, gs) * f, lng_ref[...], lnb_ref[...])

    @pl.when(i < NPT)
    def _():
        yp_ref[...] = y

    @pl.when(i >= NPT)
    def _():
        ys_ref[...] = y


def _stage_final(layer, x1, ya, yb, route, ada4, ln_g, ln_b):
    def chunk_spec(c):
        return pl.BlockSpec((TM, D), lambda i: (jnp.clip(i - c * MOE_TILES, 0, MOE_TILES - 1), 0))

    chunk_specs = [chunk_spec(c) for c in range(MOE_CHUNKS)]
    return pl.pallas_call(
        _final_body,
        grid=(NT,),
        in_specs=[
            _row_spec(D), *chunk_specs, *chunk_specs, _row_spec(128),
            *_mod_specs(layer, 5), _const_spec((1, D)), _const_spec((1, D)),
        ],
        out_specs=[
            pl.BlockSpec((TM, D), lambda i: (jnp.minimum(i, NPT - 1), 0)),
            pl.BlockSpec((TM, D), lambda i: (jnp.clip(i - NPT, 0, NST - 1), 0)),
        ],
        out_shape=[jax.ShapeDtypeStruct((ROWS_P, D), F32), jax.ShapeDtypeStruct((ROWS_S, D), F32)],
        compiler_params=_cparams(("arbitrary",)),
        name="stage_final",
    )(x1, *ya, *yb, route, ada4, ada4, ln_g, ln_b)


def _block_ones(n, blk):
    r = np.arange(n) // blk
    return r[:, None] == r[None, :]


def _consts():
    t = np.arange(CB)
    causal = t[:, None] >= t[None, :]
    same_seq = (t[:, None] // L_S) == (t[None, :] // L_S)
    r = np.arange(S5_ROWS)
    perm = r[None, :] == ((r % NB_P) * S5_TB + r // NB_P)[:, None]
    mats = {
        "perm": perm,
        "permt": perm.T,
        "e64": _block_ones(HW, HD),
        "g192": _block_ones(SSW, SSW // 2),
        "trilp": causal,
        "trils": causal & same_seq,
    }
    return {k: jnp.asarray(v.astype(np.float32), dtype=BF16) for k, v in mats.items()}


def _s5_params(a_re, a_im, log_dt, b_re, b_im, c_re, c_im, d, w_glu, b_glu):
    dt = jnp.exp(log_dt)[:, None]
    mag = jnp.exp(a_re * dt)
    lam_re, lam_im = mag * jnp.cos(a_im * dt), mag * jnp.sin(a_im * dt)
    den = a_re * a_re + a_im * a_im
    nr, ni = lam_re - 1.0, lam_im
    zr = (nr * a_re + ni * a_im) / den
    zi = (ni * a_re - nr * a_im) / den
    bbar_re = zr[..., None] * b_re - zi[..., None] * b_im
    bbar_im = zr[..., None] * b_im + zi[..., None] * b_re
    eye = jnp.eye(16, dtype=F32)
    blk = lambda bb: jnp.einsum('gph,gk->ghkp', bb, eye).reshape(S5W, S5N)
    hu, hs = S5W // 2, S5N // 2
    bblk = jnp.stack([jnp.stack([blk(bb)[hu * j:hu * (j + 1), hs * j:hs * (j + 1)] for j in range(2)])
                      for bb in (bbar_re, bbar_im)]).astype(BF16)
    cblk = lambda cc: jnp.einsum('ghp,gk->gpkh', cc, eye).reshape(S5N, S5W)
    ccat = jnp.stack([jnp.stack([cblk(cc)[hs * j:hs * (j + 1), hu * j:hu * (j + 1)] for cc in (c_re, -c_im)])
                      for j in range(2)]).astype(BF16)
    lr, li = lam_re.reshape(-1), lam_im.reshape(-1)
    pows = [(jnp.ones_like(lr), jnp.zeros_like(li))]
    for _ in range(8):
        pr, pi = pows[-1]
        pows.append((pr * lr - pi * li, pr * li + pi * lr))
    rows = jnp.arange(8)[:, None]
    tabs = []
    for dsh in (1, 2, 4):
        pr, pi = pows[dsh]
        tabs.append(jnp.where(rows >= dsh, jnp.concatenate([pr, pi])[None, :], 0.0))
    tabs.append(jnp.stack([jnp.concatenate(pows[r + 1]) for r in range(8)]))
    return {
        "tab": jnp.stack(tabs), "bblk": bblk, "ccat": ccat,
        "lam8": jnp.broadcast_to(jnp.concatenate([lr, li])[None, :], (NB_P, 2 * S5N)),
        "d": d.reshape(1, S5W), "wglu": w_glu.astype(BF16), "bglu": b_glu.reshape(1, S5W),
    }


def _pad_lanes(v, n=128):
    return jnp.pad(v, (0, n - v.shape[0])).reshape(1, n)


def kernel(x_prompt, x_sample, c_prompt, c_sample, state_hgrn, state_s5, state_ssd, state_ssd_conv, w_ada, b_ada, ln_g, ln_b, w_in, w_out, hgrn_lb_logits, hgrn_norm_g, s5_a_re, s5_a_im, s5_log_dt, s5_b_re, s5_b_im, s5_c_re, s5_c_im, s5_d, s5_w_glu, s5_b_glu, ssd_conv_w, ssd_conv_b, ssd_dt_bias, ssd_a_log, ssd_d, ssd_norm_g, ffn_w_gate, ffn_w_up, ffn_w_down, moe_w_router, moe_b_router, moe_w_gate, moe_w_up, moe_w_down):
    consts = _consts()
    c_all = jnp.concatenate([c_sample, c_prompt], axis=0)
    ada4 = _ada(c_all, w_ada, b_ada).reshape(DEPTH, NB_S + NB_P, 1, 6 * D)

    lb_all = jnp.cumsum(jax.nn.softmax(hgrn_lb_logits, axis=0), axis=0)
    lb_all = lb_all - lb_all[0]

    xp = x_prompt.reshape(ROWS_P, D)
    xs = x_sample.reshape(ROWS_S, D)
    new_h, new_s5, new_m, new_c = [], [], [], []
    x1 = f = route = None
    for l in range(DEPTH):
        w_in_b = jnp.pad(w_in[l], ((0, 0), (0, N_IN_PAD - N_IN))).astype(BF16)
        if l == 0:
            x, ph, ps, pm = _stage_a0(xp, xs, ada4, w_in_b)
        else:
            x, ph, ps, pm = _stage_a1(l, x1, f, ada4, ln_g[l - 1, 1].reshape(1, D), ln_b[l - 1, 1].reshape(1, D),
                                      w_in_b)
        oh, hg_p, hg_s = _hgrn(ph, lb_all[l].reshape(1, HW), hgrn_norm_g[l].reshape(1, HW), consts,
                               jnp.swapaxes(state_hgrn[l], -1, -2))
        s5p = _s5_params(s5_a_re[l], s5_a_im[l], s5_log_dt[l], s5_b_re[l], s5_b_im[l], s5_c_re[l], s5_c_im[l],
                         s5_d[l], s5_w_glu[l], s5_b_glu[l])
        os_p, s5_p = _s5_prompt(ps[:ROWS_P].reshape(NB_P, L_P, S5W), s5p, consts)
        os_s, s5_s = _s5_sample(ps[ROWS_P:], s5p, state_s5[l].reshape(NB_S, 1, 2 * S5N))
        os_ = jnp.concatenate([os_p.reshape(ROWS_P, S5W), os_s], axis=0)
        ssd_prm = {
            "cw": ssd_conv_w[l], "cb": ssd_conv_b[l].reshape(1, CONVC),
            "dtb": _pad_lanes(ssd_dt_bias[l]), "aneg": _pad_lanes(-jnp.exp(ssd_a_log[l])),
            "dx": jnp.repeat(ssd_d[l], HD).reshape(1, SSW), "sg": ssd_norm_g[l].reshape(1, SSW),
        }
        hist = jnp.pad(state_ssd_conv[l], ((0, 0), (0, L_S - 3), (0, 0))).reshape(ROWS_S, CONVC)
        om, ss_p, ss_s, tail_p, xbc_s = _ssd(pm, ssd_prm, consts, hist, state_ssd[l])
        conv_p = tail_p[:, 8 - 3:]
        conv_s = xbc_s.reshape(NB_S, L_S, CONVC)[:, L_S - 3:]
        hp6 = hg_p.reshape(NB_P, NH // 2, 2, HD, 2, HD)
        hg_p = jnp.stack([hp6[:, :, 0, :, 0, :], hp6[:, :, 1, :, 1, :]], axis=2).reshape(NB_P, NH, HD, HD)
        new_h.append((jnp.swapaxes(hg_p, -1, -2), jnp.swapaxes(hg_s, -1, -2)))
        new_s5.append((s5_p.reshape(NB_P, 2, 16, 64), s5_s.reshape(NB_S, 2, 16, 64)))
        new_m.append((ss_p, ss_s))
        new_c.append((conv_p, conv_s))

        wo_b = w_out[l].astype(BF16)
        lg, lbb = ln_g[l, 0].reshape(1, D), ln_b[l, 0].reshape(1, D)
        j = l // 2
        if l % 2 == 0:
            x1, h2 = _stage_o(l, oh, os_, om, x, wo_b, ada4, lg, lbb)
            f = _ffn(jnp.zeros((ROWS // TMF,), jnp.int32), jnp.full((1,), ROWS // TMF, jnp.int32), h2,
                     ffn_w_gate[j:j + 1].astype(BF16),
                     ffn_w_up[j:j + 1].astype(BF16), ffn_w_down[j:j + 1].astype(BF16))
        else:
            wr = jnp.pad(moe_w_router[j], ((0, 0), (0, 128 - NEXP)))
            wr_hi = wr.astype(BF16)
            wr = jnp.stack([wr_hi, (wr - wr_hi.astype(F32)).astype(BF16)])
            br = _pad_lanes(moe_b_router[j])
            x1, h2, route = _stage_o(l, oh, os_, om, x, wo_b, ada4, lg, lbb, router=(wr, br))
            wg_b, wu_b, wd_b = (moe_w_gate[j].astype(BF16), moe_w_up[j].astype(BF16),
                                moe_w_down[j].astype(BF16))
            ya, yb = [], []
            for c in range(MOE_CHUNKS):
                flat_e = route[c * MOE_ROWS:(c + 1) * MOE_ROWS, 0:2].astype(jnp.int32).reshape(-1)
                onehot = (flat_e[:, None] == jnp.arange(NEXP)[None, :]).astype(jnp.int32)
                csum = jnp.cumsum(onehot, axis=0)
                counts = csum[-1]
                rank = jnp.take_along_axis(csum, flat_e[:, None], axis=1)[:, 0] - 1
                padded = ((counts + TMF - 1) // TMF) * TMF
                pend = jnp.cumsum(padded)
                pstart = pend - padded
                dest = pstart[flat_e] + rank
                n_pad = 2 * MOE_ROWS + NEXP * TMF
                tok = c * MOE_ROWS + jnp.arange(2 * MOE_ROWS, dtype=jnp.int32) // 2
                src_tok = jnp.full((n_pad,), c * MOE_ROWS, jnp.int32).at[dest].set(tok)
                tile_start = jnp.arange(n_pad // TMF, dtype=jnp.int32) * TMF
                tile_e = jnp.minimum(jnp.sum((pend[None, :] <= tile_start[:, None]).astype(jnp.int32), axis=1),
                                     NEXP - 1)
                n_used = (pend[NEXP - 1:NEXP] // TMF).astype(jnp.int32)
                y_sorted = _ffn(tile_e, n_used, h2[src_tok], wg_b, wu_b, wd_b)
                pos = dest.reshape(MOE_ROWS, 2)
                ya.append(y_sorted[pos[:, 0]])
                yb.append(y_sorted[pos[:, 1]])
    y_p, y_s = _stage_final(DEPTH - 1, x1, ya, yb, route, ada4, ln_g[DEPTH - 1, 1].reshape(1, D),
                            ln_b[DEPTH - 1, 1].reshape(1, D))
    stack = lambda lst, k: jnp.stack([t[k] for t in lst])
    return (y_p.reshape(NB_P, L_P, D), y_s.reshape(NB_S, L_S, D),
            stack(new_h, 0), stack(new_s5, 0), stack(new_m, 0), stack(new_c, 0),
            stack(new_h, 1), stack(new_s5, 1), stack(new_m, 1), stack(new_c, 1))
```

```python
import functools
import math

import jax
import jax.numpy as jnp
import numpy as np
from jax import lax
from jax.experimental import pallas as pl
from jax.experimental.pallas import tpu as pltpu

F32 = jnp.float32
BF16 = jnp.bfloat16

D = 1024
NB_P, L_P = 8, 2048
NB_S, L_S = 128, 8
ROWS_P = NB_P * L_P
ROWS_S = NB_S * L_S
ROWS = ROWS_P + ROWS_S
DEPTH = 2
HW = 384
S5W = 256
SSW = 384
NH = 6
HD = 64
S5N = 1024
CONVC = 640
N_IN = 2822
N_IN_PAD = 2944
P_HG = 1536
P_SS = 1152
DFF = 2816
NEXP = 8
ALPHA = (2 * DEPTH) ** 0.25
LN_EPS = 1e-5
RMS_EPS = 1e-6

TM = 512
NPT = ROWS_P // TM
NST = ROWS_S // TM
NT = NPT + NST
SEQ_PER_TILE = TM // L_S
TILES_PER_SEQ = L_P // TM

CB = 128
NPC = ROWS_P // CB
NSC = ROWS_S // CB
CH_PER_SEQ = L_P // CB
SEQ_PER_CB = CB // L_S

TMF = 512
TF = 1408
MOE_CHUNKS = 1
MOE_ROWS = ROWS // MOE_CHUNKS
MOE_TILES = MOE_ROWS // TM
VMEM_LIMIT = 56 * 1024 * 1024


def _cparams(sem):
    return pltpu.CompilerParams(dimension_semantics=sem, vmem_limit_bytes=VMEM_LIMIT)


def _bdot(a, b):
    return jnp.dot(a.astype(BF16), b.astype(BF16), preferred_element_type=F32)


def _bdot_nt(a, b):
    return lax.dot_general(a.astype(BF16), b.astype(BF16), (((1,), (1,)), ((), ())),
                           preferred_element_type=F32)


def _split_dot(x, e, passes):
    acc = None
    r = x
    for _ in range(passes):
        hi = r.astype(BF16)
        d = jnp.dot(hi, e, preferred_element_type=F32)
        acc = d if acc is None else acc + d
        r = r - hi.astype(F32)
    return acc


def _split_dot_l(e, x, passes):
    acc = None
    r = x
    for _ in range(passes):
        hi = r.astype(BF16)
        d = jnp.dot(e, hi, preferred_element_type=F32)
        acc = d if acc is None else acc + d
        r = r - hi.astype(F32)
    return acc


def _silu(x):
    return x * jax.nn.sigmoid(x)


def _layer_norm(x, g, b):
    mu = jnp.mean(x, -1, keepdims=True)
    xc = x - mu
    var = jnp.mean(xc * xc, -1, keepdims=True)
    return xc * lax.rsqrt(var + LN_EPS) * g + b


def _rowmod(i, p_ref, s_ref):
    s = jnp.broadcast_to(s_ref[...], (SEQ_PER_TILE, L_S, D)).reshape(TM, D)
    return jnp.where(i < NPT, p_ref[0], s)


ADA_TN = 1536


def _ada_body(c_ref, w_ref, b_ref, o_ref):
    o_ref[...] = _bdot(_silu(c_ref[...]), w_ref[...]) + b_ref[...]


def _ada(c_all, w_ada, b_ada):
    nc = c_all.shape[0]
    return pl.pallas_call(
        _ada_body,
        grid=(DEPTH, 6 * D // ADA_TN),
        in_specs=[
            pl.BlockSpec((nc, D), lambda l, j: (0, 0)),
            pl.BlockSpec((None, D, ADA_TN), lambda l, j: (l, 0, j)),
            pl.BlockSpec((None, 1, ADA_TN), lambda l, j: (l, 0, j)),
        ],
        out_specs=pl.BlockSpec((None, nc, ADA_TN), lambda l, j: (l, 0, j)),
        out_shape=jax.ShapeDtypeStruct((DEPTH, nc, 6 * D), F32),
        compiler_params=_cparams(("parallel", "parallel")),
        name="ada",
    )(c_all, w_ada, b_ada.reshape(DEPTH, 1, 6 * D))


def _mod_specs(layer, k):
    ps = pl.BlockSpec((None, 1, 1, D),
                      lambda i: (layer, NB_S + jnp.minimum(i // TILES_PER_SEQ, NB_P - 1), 0, k))
    ss = pl.BlockSpec((None, SEQ_PER_TILE, 1, D),
                      lambda i: (layer, jnp.clip(i - NPT, 0, NST - 1), 0, k))
    return [ps, ss]


def _row_spec(width):
    return pl.BlockSpec((TM, width), lambda i: (i, 0))


def _const_spec(shape):
    nd = len(shape)
    return pl.BlockSpec(shape, lambda *_: (0,) * nd)


def _proj_out(x, i, scp, scs, shp, shs, w_ref, ph_ref, ps_ref, pm_ref):
    h = x * (1.0 + _rowmod(i, scp, scs)) + _rowmod(i, shp, shs)
    proj = jnp.dot(h.astype(BF16), w_ref[...], preferred_element_type=F32)
    ph_ref[...] = proj[:, 0:P_HG]
    ps_ref[...] = proj[:, P_HG:P_HG + S5W]
    pm_ref[...] = proj[:, P_HG + S5W:N_IN_PAD]


def _a0_body(xp_ref, xs_ref, scp, scs, shp, shs, w_ref, x_ref, ph_ref, ps_ref, pm_ref):
    i = pl.program_id(0)
    x = jnp.where(i < NPT, xp_ref[...], xs_ref[...])
    x_ref[...] = x
    _proj_out(x, i, scp, scs, shp, shs, w_ref, ph_ref, ps_ref, pm_ref)


def _a1_body(x1_ref, f_ref, gp, gs, lng_ref, lnb_ref, scp, scs, shp, shs, w_ref,
             x_ref, ph_ref, ps_ref, pm_ref):
    i = pl.program_id(0)
    x = _layer_norm(ALPHA * x1_ref[...] + _rowmod(i, gp, gs) * f_ref[...], lng_ref[...], lnb_ref[...])
    x_ref[...] = x
    _proj_out(x, i, scp, scs, shp, shs, w_ref, ph_ref, ps_ref, pm_ref)


def _a_out():
    specs = [_row_spec(D), _row_spec(P_HG), _row_spec(S5W), _row_spec(P_SS)]
    shapes = [jax.ShapeDtypeStruct((ROWS, w), F32) for w in (D, P_HG, S5W, P_SS)]
    return specs, shapes


def _stage_a0(xp, xs, ada4, w_in_b):
    out_specs, out_shape = _a_out()
    return pl.pallas_call(
        _a0_body,
        grid=(NT,),
        in_specs=[
            pl.BlockSpec((TM, D), lambda i: (jnp.minimum(i, NPT - 1), 0)),
            pl.BlockSpec((TM, D), lambda i: (jnp.clip(i - NPT, 0, NST - 1), 0)),
            *_mod_specs(0, 1), *_mod_specs(0, 0),
            _const_spec((D, N_IN_PAD)),
        ],
        out_specs=out_specs, out_shape=out_shape,
        compiler_params=_cparams(("parallel",)),
        name="stage_a0",
    )(xp, xs, ada4, ada4, ada4, ada4, w_in_b)


def _stage_a1(layer, x1, f, ada4, ln_g, ln_b, w_in_b):
    out_specs, out_shape = _a_out()
    return pl.pallas_call(
        _a1_body,
        grid=(NT,),
        in_specs=[
            _row_spec(D), _row_spec(D),
            *_mod_specs(layer - 1, 5),
            _const_spec((1, D)), _const_spec((1, D)),
            *_mod_specs(layer, 1), *_mod_specs(layer, 0),
            _const_spec((D, N_IN_PAD)),
        ],
        out_specs=out_specs, out_shape=out_shape,
        compiler_params=_cparams(("parallel",)),
        name="stage_a1",
    )(x1, f, ada4, ada4, ln_g, ln_b, ada4, ada4, ada4, ada4, w_in_b)


def _mixer_blk(width, nsub=1):
    return pl.BlockSpec((nsub * CB, width), lambda i: (i, 0))


def _pstate_spec(shape, nsub=1):
    nd = len(shape)
    return pl.BlockSpec((1,) + shape,
                        lambda i: (jnp.minimum(i // (CH_PER_SEQ // nsub), NB_P - 1),) + (0,) * nd)


def _sstate_spec(shape, nsub=1):
    nd = len(shape)
    return pl.BlockSpec((nsub * SEQ_PER_CB,) + shape,
                        lambda i: (jnp.clip(i - NPC // nsub, 0, NSC // nsub - 1),) + (0,) * nd)


def _cumsum_rows(x, span):
    r = lax.broadcasted_iota(jnp.int32, x.shape, 0) & (span - 1)
    d = 1
    while d < span:
        x = x + jnp.where(r >= d, pltpu.roll(x, d, 0), 0.0)
        d *= 2
    return x


def _seq_last_rows(x):
    w = x.shape[-1]
    x3 = x.reshape(SEQ_PER_CB, L_S, w)
    return jnp.broadcast_to(x3[:, L_S - 1:L_S, :], (SEQ_PER_CB, L_S, w)).reshape(CB, w)


def _concat_heads(parts):
    return jnp.concatenate(parts, axis=1)


def _stack_select(shape, row_div, lane_div):
    r = lax.broadcasted_iota(jnp.int32, shape, 0) // row_div
    c = lax.broadcasted_iota(jnp.int32, shape, 1) // lane_div
    return r == c


def _seq_expand_lanes(qh):
    q2 = jnp.concatenate([qh, qh], axis=1)
    q16 = jnp.concatenate([q2] * (SEQ_PER_CB // 2), axis=1)
    return jnp.where(_stack_select((CB, SEQ_PER_CB * HD), L_S, HD), q16, 0.0)


def _seq_expand_rows(xt):
    t = jnp.broadcast_to(xt[None], (SEQ_PER_CB, HD, CB)).reshape(SEQ_PER_CB * HD, CB)
    return jnp.where(_stack_select((SEQ_PER_CB * HD, CB), HD, L_S), t, 0.0)


def _fold_seq_lanes(full):
    acc = full[:, 0:128]
    for j in range(1, SEQ_PER_CB * HD // 128):
        acc = acc + full[:, 128 * j:128 * (j + 1)]
    return acc[:, 0:HD] + acc[:, HD:2 * HD]


HGRN_BASE = 32
EXP_RANGE_MAX = 80.0


def _hgrn_block(prompt, p_ref, lb_ref, hg_ref, e64_ref, oh_ref, st_scr, o_scr, sub, is_last=None,
                stp_out=None, sts_in=None, sts_out=None):
    rows = slice(sub * CB, (sub + 1) * CB)
    seqs = slice(sub * SEQ_PER_CB, (sub + 1) * SEQ_PER_CB)
    lb = lb_ref[...]
    qr = p_ref[rows, 0:HW]
    fr = p_ref[rows, HW:2 * HW]
    v = p_ref[rows, 2 * HW:3 * HW]
    gr = p_ref[rows, 3 * HW:4 * HW]
    ls = jnp.minimum(fr, 0.0) - jnp.log1p(jnp.exp(-jnp.abs(fr)))
    a = jnp.log(lb)
    bb = jnp.log1p(-lb) + ls
    lf = jnp.maximum(a, bb) + jnp.log1p(jnp.exp(-jnp.abs(a - bb)))
    kk = (1.0 - lb) * jax.nn.sigmoid(-fr)
    q = _silu(qr)
    b = _cumsum_rows(lf, CB if prompt else L_S)

    e64 = e64_ref[...]
    ti = lax.broadcasted_iota(jnp.int32, (CB, CB), 0)
    si = lax.broadcasted_iota(jnp.int32, (CB, CB), 1)

    def diag8():
        nsub = CB // 8
        b3 = b.reshape(nsub, 8, HW)
        q3 = q.reshape(nsub, 8, HW)
        k3 = kk.reshape(nsub, 8, HW)
        v3 = v.reshape(nsub, 8, HW)
        r3 = lax.broadcasted_iota(jnp.int32, (nsub, 8, HW), 1)
        o = jnp.zeros((CB, HW), F32)
        for s in range(8):
            dlt = jnp.minimum(b3 - b3[:, s:s + 1, :], 0.0)
            w = jnp.where(r3 >= s, jnp.exp(dlt), 0.0) * q3 * k3[:, s:s + 1, :]
            hsum = jnp.dot(w.reshape(CB, HW).astype(BF16), e64, preferred_element_type=F32)
            o = o + hsum * jnp.broadcast_to(v3[:, s:s + 1, :], (nsub, 8, HW)).reshape(CB, HW)
        return o

    def level_terms(m):
        terms = []
        while m < CB:
            nb = CB // (2 * m)
            b4 = b.reshape(nb, 2 * m, HW)
            bmid = b4[:, m - 1:m, :]
            pos = lax.broadcasted_iota(jnp.int32, (nb, 2 * m, HW), 1)
            qq = jnp.where(pos >= m, q.reshape(nb, 2 * m, HW) * jnp.exp(jnp.minimum(b4 - bmid, 0.0)), 0.0)
            kq = jnp.where(pos < m, kk.reshape(nb, 2 * m, HW) * jnp.exp(jnp.minimum(bmid - b4, 0.0)), 0.0)
            terms.append((qq.reshape(CB, HW), kq.reshape(CB, HW), (ti // (2 * m)) == (si // (2 * m))))
            m *= 2
        return terms

    def scores_times_v(terms):
        lane = lax.broadcasted_iota(jnp.int32, (CB, 2 * HD), 1)
        parts = []
        for j in range(NH // 2):
            scs = []
            for h in (2 * j, 2 * j + 1):
                sl = slice(HD * h, HD * (h + 1))
                sc = None
                for qq, kq, keep in terms:
                    t = jnp.where(keep, _bdot_nt(qq[:, sl], kq[:, sl]), 0.0)
                    sc = t if sc is None else sc + t
                scs.append(sc)
            vp = v[:, 2 * HD * j:2 * HD * (j + 1)]
            vdiag = jnp.concatenate([jnp.where(lane < HD, vp, 0.0), jnp.where(lane >= HD, vp, 0.0)], axis=0)
            parts.append(_bdot(jnp.concatenate(scs, axis=1), vdiag))
        return _concat_heads(parts)

    base = HGRN_BASE if prompt else L_S
    nbase = CB // base
    bb3 = b.reshape(nbase, base, HW)
    top = bb3[:, 0:1, :] - lf.reshape(nbase, base, HW)[:, 0:1, :]
    decay_range = jnp.max(top - bb3[:, base - 1:base, :])
    in_range = decay_range <= EXP_RANGE_MAX

    def exact_path():
        o_scr[rows, :] = diag8() + (scores_times_v(level_terms(8)) if prompt else 0.0)

    def finish(fast):
        _hgrn_finish(prompt, fast, rows, seqs, q, kk, v, gr, b, bb3, top, ti, si, scores_times_v, level_terms,
                     hg_ref, e64, oh_ref, st_scr, o_scr, is_last, stp_out, sts_in, sts_out)

    return in_range, exact_path, finish


def _hgrn_finish(prompt, fast, rows, seqs, q, kk, v, gr, b, bb3, top, ti, si, scores_times_v, level_terms,
                 hg_ref, e64, oh_ref, st_scr, o_scr, is_last, stp_out, sts_in, sts_out):
    base = HGRN_BASE if prompt else L_S
    nbase = CB // base
    qf = (q.reshape(nbase, base, HW) * jnp.exp(bb3 - top)).reshape(CB, HW)
    kf = (kk.reshape(nbase, base, HW) * jnp.exp(top - bb3)).reshape(CB, HW)
    keep = ((ti // base) == (si // base)) & (si <= ti)
    o_fast = scores_times_v([(qf, kf, keep)] + (level_terms(base) if prompt else []))
    o = jnp.where(fast, o_fast, o_scr[rows, :])
    qt = q * jnp.exp(b)
    if prompt:
        blast = b[CB - 1:CB, :]
        kd = kk * jnp.exp(blast - b)
        vt = v.T
        same_head = (ti // HD) == (si // HD)
        parts = []
        for j in range(NH // 2):
            pr = slice(2 * HD * j, 2 * HD * (j + 1))
            st = st_scr[j]
            parts.append(o[:, pr] + _bdot_nt(qt[:, pr], st))
            st_scr[j] = st * jnp.exp(blast[:, pr]) + jnp.where(same_head, _bdot(vt[pr, :], kd[:, pr]), 0.0)

        if is_last is not None:
            @pl.when(is_last)
            def _():
                stp_out[0] = st_scr[...]
    else:
        blast = _seq_last_rows(b)
        kd = kk * jnp.exp(blast - b)
        dec = jnp.exp(blast)
        vt = v.T
        parts = []
        for h in range(NH):
            sl = slice(HD * h, HD * (h + 1))
            sts = sts_in[seqs, h].reshape(SEQ_PER_CB * HD, HD)
            full = _bdot_nt(qt[:, sl], sts)
            sel = jnp.where(_stack_select((CB, SEQ_PER_CB * HD), L_S, HD), full, 0.0)
            parts.append(o[:, sl] + _fold_seq_lanes(sel))
            dec3 = dec[:, sl].reshape(SEQ_PER_CB, L_S, HD)[:, L_S - 1:L_S, :]
            dec_rows = jnp.broadcast_to(dec3, (SEQ_PER_CB, HD, HD)).reshape(SEQ_PER_CB * HD, HD)
            upd = _bdot(_seq_expand_rows(vt[sl, :]), kd[:, sl])
            sts_out[seqs, h] = (sts * dec_rows + upd).reshape(SEQ_PER_CB, HD, HD)
    oall = _concat_heads(parts)
    ms = _split_dot(oall * oall, e64, 1) * (1.0 / HD)
    oh_ref[rows, :] = oall * lax.rsqrt(ms + RMS_EPS) * hg_ref[...] * _silu(gr)


HGRN_SUB = 2


def _hgrn_body(p_ref, lb_ref, hg_ref, e64_ref, sts_in,
               oh_ref, stp_out, sts_out, st_scr, o_scr):
    i = pl.program_id(0)
    npc, steps_per_seq = NPC // HGRN_SUB, CH_PER_SEQ // HGRN_SUB

    @pl.when(i == 0)
    def _():
        o_scr[...] = jnp.zeros_like(o_scr)

    @pl.when((i < npc) & (i % steps_per_seq == 0))
    def _():
        st_scr[...] = jnp.zeros_like(st_scr)

    def run(blocks):
        fast = blocks[0][0]
        for in_range, _, _ in blocks[1:]:
            fast = jnp.logical_and(fast, in_range)

        @pl.when(jnp.logical_not(fast))
        def _():
            for _, exact_path, _ in blocks:
                exact_path()

        for _, _, finish in blocks:
            finish(fast)

    @pl.when(i < npc)
    def _():
        run([_hgrn_block(True, p_ref, lb_ref, hg_ref, e64_ref, oh_ref, st_scr, o_scr, sub,
                         is_last=(i % steps_per_seq == steps_per_seq - 1) if sub == HGRN_SUB - 1 else None,
                         stp_out=stp_out) for sub in range(HGRN_SUB)])

    @pl.when(i >= npc)
    def _():
        run([_hgrn_block(False, p_ref, lb_ref, hg_ref, e64_ref, oh_ref, st_scr, o_scr, sub,
                         sts_in=sts_in, sts_out=sts_out) for sub in range(HGRN_SUB)])


def _hgrn(ph, lb, hg, consts, st_t):
    return pl.pallas_call(
        _hgrn_body,
        grid=((NPC + NSC) // HGRN_SUB,),
        in_specs=[
            _mixer_blk(P_HG, HGRN_SUB), _const_spec((1, HW)), _const_spec((1, HW)),
            _const_spec((HW, HW)),
            _sstate_spec((NH, HD, HD), HGRN_SUB),
        ],
        out_specs=[_mixer_blk(HW, HGRN_SUB), _pstate_spec((NH // 2, 2 * HD, 2 * HD), HGRN_SUB),
                   _sstate_spec((NH, HD, HD), HGRN_SUB)],
        out_shape=[
            jax.ShapeDtypeStruct((ROWS, HW), F32),
            jax.ShapeDtypeStruct((NB_P, NH // 2, 2 * HD, 2 * HD), F32),
            jax.ShapeDtypeStruct((NB_S, NH, HD, HD), F32),
        ],
        scratch_shapes=[pltpu.VMEM((NH // 2, 2 * HD, 2 * HD), F32), pltpu.VMEM((HGRN_SUB * CB, HW), F32)],
        compiler_params=_cparams(("arbitrary",)),
        name="hgrn",
    )(ph, lb, hg, consts["e64"], st_t)


def _cmul_add(hr, hi, lr, li, sr, si):
    return hr + lr * sr - li * si, hi + lr * si + li * sr


def _s5_project(ub, bblk_ref):
    halves = [ub[:, (S5W // 2) * j:(S5W // 2) * (j + 1)] for j in range(2)]
    hr = jnp.concatenate([jnp.dot(halves[j], bblk_ref[0, j], preferred_element_type=F32) for j in range(2)], axis=1)
    hi = jnp.concatenate([jnp.dot(halves[j], bblk_ref[1, j], preferred_element_type=F32) for j in range(2)], axis=1)
    return hr, hi


def _s5_readout(h_scr, u, ccat_ref, d_ref, wglu_ref, bglu_ref):
    hs = S5N // 2
    ch = [_bdot(h_scr[:, hs * j:hs * (j + 1)], ccat_ref[j, 0])
          + _bdot(h_scr[:, S5N + hs * j:S5N + hs * (j + 1)], ccat_ref[j, 1]) for j in range(2)]
    y = jnp.concatenate(ch, axis=1) + d_ref[...] * u
    c0 = math.sqrt(2.0 / math.pi)
    y = y * (0.5 * (1.0 + jnp.tanh(c0 * (y + 0.044715 * (y * y * y)))))
    return y * jax.nn.sigmoid(_bdot(y, wglu_ref[...]) + bglu_ref[...])


S5_TB = 32
S5_ROWS = NB_P * S5_TB


def _s5_prompt_body(p_ref, perm_ref, permt_ref, lam_ref, bblk_ref, ccat_ref, d_ref, wglu_ref, bglu_ref,
                    os_ref, st_out, carry_scr, h_scr):
    i = pl.program_id(0)

    @pl.when(i == 0)
    def _():
        carry_scr[...] = jnp.zeros_like(carry_scr)

    u = p_ref[...].reshape(S5_ROWS, S5W)
    u_hi = u.astype(BF16)
    u_lo = (u - u_hi.astype(F32)).astype(BF16)
    perm = perm_ref[...]
    up_hi = jnp.dot(perm, u_hi, preferred_element_type=F32)
    up = up_hi + jnp.dot(perm, u_lo, preferred_element_type=F32)
    hr, hi = _s5_project(up_hi.astype(BF16), bblk_ref)
    lr = lam_ref[:, 0:S5N]
    li = lam_ref[:, S5N:2 * S5N]
    cr = carry_scr[:, 0:S5N]
    ci = carry_scr[:, S5N:2 * S5N]
    for t in range(S5_TB):
        rows = slice(NB_P * t, NB_P * (t + 1))
        cr, ci = _cmul_add(hr[rows], hi[rows], lr, li, cr, ci)
        h_scr[rows, 0:S5N] = cr
        h_scr[rows, S5N:2 * S5N] = ci
    carry_scr[:, 0:S5N] = cr
    carry_scr[:, S5N:2 * S5N] = ci
    out = _s5_readout(h_scr, up, ccat_ref, d_ref, wglu_ref, bglu_ref)
    os_ref[...] = jnp.dot(permt_ref[...], out.astype(BF16), preferred_element_type=F32).reshape(NB_P, S5_TB, S5W)

    @pl.when(i == pl.num_programs(0) - 1)
    def _():
        st_out[...] = carry_scr[...]


def _s5_prompt(ps3, prm, consts):
    return pl.pallas_call(
        _s5_prompt_body,
        grid=(L_P // S5_TB,),
        in_specs=[
            pl.BlockSpec((NB_P, S5_TB, S5W), lambda i: (0, i, 0)),
            _const_spec((S5_ROWS, S5_ROWS)), _const_spec((S5_ROWS, S5_ROWS)), _const_spec((NB_P, 2 * S5N)),
            _const_spec((2, 2, S5W // 2, S5N // 2)), _const_spec((2, 2, S5N // 2, S5W // 2)),
            _const_spec((1, S5W)), _const_spec((S5W, S5W)), _const_spec((1, S5W)),
        ],
        out_specs=[pl.BlockSpec((NB_P, S5_TB, S5W), lambda i: (0, i, 0)), _const_spec((NB_P, 2 * S5N))],
        out_shape=[jax.ShapeDtypeStruct((NB_P, L_P, S5W), F32), jax.ShapeDtypeStruct((NB_P, 2 * S5N), F32)],
        scratch_shapes=[pltpu.VMEM((NB_P, 2 * S5N), F32), pltpu.VMEM((S5_ROWS, 2 * S5N), F32)],
        compiler_params=_cparams(("arbitrary",)),
        name="s5_prompt",
    )(ps3, consts["perm"], consts["permt"], prm["lam8"], prm["bblk"], prm["ccat"], prm["d"], prm["wglu"],
      prm["bglu"])


def _s5_sample_body(p_ref, tab_ref, bblk_ref, ccat_ref, d_ref, wglu_ref, bglu_ref, s5s_in,
                    os_ref, s5s_out, h_scr):
    u = p_ref[...]
    hr, hi = _s5_project(u.astype(BF16), bblk_ref)
    nsub = CB // 8
    for idx, dsh in enumerate((1, 2, 4)):
        sr = pltpu.roll(hr, dsh, 0).reshape(nsub, 8, S5N)
        si = pltpu.roll(hi, dsh, 0).reshape(nsub, 8, S5N)
        lr = tab_ref[idx, :, 0:S5N][None]
        li = tab_ref[idx, :, S5N:2 * S5N][None]
        nr, ni = _cmul_add(hr.reshape(nsub, 8, S5N), hi.reshape(nsub, 8, S5N), lr, li, sr, si)
        hr = nr.reshape(CB, S5N)
        hi = ni.reshape(CB, S5N)
    tcr = tab_ref[3, :, 0:S5N]
    tci = tab_ref[3, :, S5N:2 * S5N]
    cr = s5s_in[:, :, 0:S5N]
    ci = s5s_in[:, :, S5N:2 * S5N]
    tr, tim = _cmul_add(hr.reshape(nsub, 8, S5N), hi.reshape(nsub, 8, S5N), tcr[None], tci[None], cr, ci)
    h_scr[:, 0:S5N] = tr.reshape(CB, S5N)
    h_scr[:, S5N:2 * S5N] = tim.reshape(CB, S5N)
    sb = lax.broadcasted_iota(jnp.int32, (SEQ_PER_CB, CB), 0)
    st = lax.broadcasted_iota(jnp.int32, (SEQ_PER_CB, CB), 1)
    sel = (st == L_S * sb + (L_S - 1)).astype(BF16)
    s5s_out[...] = _split_dot_l(sel, h_scr[...], 3)
    os_ref[...] = _s5_readout(h_scr, u, ccat_ref, d_ref, wglu_ref, bglu_ref)


def _s5_sample(ps_s, prm, st):
    seqs = lambda shape: pl.BlockSpec((SEQ_PER_CB,) + shape, lambda i: (i,) + (0,) * len(shape))
    return pl.pallas_call(
        _s5_sample_body,
        grid=(NSC,),
        in_specs=[
            _mixer_blk(S5W), _const_spec((4, 8, 2 * S5N)), _const_spec((2, 2, S5W // 2, S5N // 2)),
            _const_spec((2, 2, S5N // 2, S5W // 2)), _const_spec((1, S5W)), _const_spec((S5W, S5W)),
            _const_spec((1, S5W)), seqs((1, 2 * S5N)),
        ],
        out_specs=[_mixer_blk(S5W), seqs((2 * S5N,))],
        out_shape=[jax.ShapeDtypeStruct((ROWS_S, S5W), F32), jax.ShapeDtypeStruct((NB_S, 2 * S5N), F32)],
        scratch_shapes=[pltpu.VMEM((CB, 2 * S5N), F32)],
        compiler_params=_cparams(("parallel",)),
        name="s5_sample",
    )(ps_s, prm["tab"], prm["bblk"], prm["ccat"], prm["d"], prm["wglu"], prm["bglu"], st)


def _ssd_block(prompt, p_ref, cw_ref, cb_ref, dtb_ref, aneg_ref, dx_ref, sg_ref,
               g192_ref, tril_ref, om_ref, cbuf, st_scr, sub, is_last=None, ssp_out=None, convp_out=None,
               hist_ref=None, sss_in=None, sss_out=None, convs_out=None):
    rows = slice(sub * CB, (sub + 1) * CB)
    seqs = slice(sub * SEQ_PER_CB, (sub + 1) * SEQ_PER_CB)
    z = p_ref[rows, 0:SSW]
    xbc = p_ref[rows, SSW:SSW + CONVC]
    dtr = p_ref[rows, SSW + CONVC:P_SS]
    acc = cb_ref[...] + cw_ref[3:4, :] * xbc
    if prompt:
        cbuf[8:8 + CB, :] = xbc
        for k in (1, 2, 3):
            acc = acc + cw_ref[3 - k:4 - k, :] * cbuf[8 - k:8 - k + CB, :]
        cbuf[0:8, :] = cbuf[CB:CB + 8, :]
    else:
        convs_out[rows, :] = xbc
        tl = lax.broadcasted_iota(jnp.int32, (CB, CONVC), 0) % L_S
        hist = hist_ref[rows, :]
        for k in (1, 2, 3):
            hk = hist if k == 3 else pltpu.roll(hist, CB - (3 - k), 0)
            sh = jnp.where(tl >= k, pltpu.roll(xbc, k, 0), hk)
            acc = acc + cw_ref[3 - k:4 - k, :] * sh
    xc = _silu(acc)
    xs = xc[:, 0:SSW]
    bm = xc[:, SSW:SSW + 2 * HD]
    cm = xc[:, SSW + 2 * HD:CONVC]
    xdt = dtr + dtb_ref[...]
    dt = jnp.maximum(xdt, 0.0) + jnp.log1p(jnp.exp(-jnp.abs(xdt)))
    la = dt * aneg_ref[...]
    tril = tril_ref[...]
    b6 = _cumsum_rows(la, CB if prompt else L_S)
    lane = lax.broadcasted_iota(jnp.int32, (CB, 128), 1)
    bxw = [jnp.broadcast_to(b6[:, h:h + 1], (CB, 128)) for h in range(NH)]
    dtw = [jnp.broadcast_to(dt[:, h:h + 1], (CB, 128)) for h in range(NH)]
    pair = lambda cols: _concat_heads([jnp.where(lane < HD, cols[2 * j], cols[2 * j + 1]) for j in range(NH // 2)])
    bx = pair(bxw)
    dtx = pair(dtw)
    bm_rep = _concat_heads([bm[:, 0:HD]] * 3 + [bm[:, HD:2 * HD]] * 3)
    cm_rep = _concat_heads([cm[:, 0:HD]] * 3 + [cm[:, HD:2 * HD]] * 3)
    kh = bm_rep * dtx
    qt = cm_rep * jnp.exp(bx)
    blast = bx[CB - 1:CB, :] if prompt else _seq_last_rows(bx)
    kd = kh * jnp.exp(blast - bx)
    kdt = kd.T
    mask = tril > 0

    def scores(h):
        sl = slice(HD * h, HD * (h + 1))
        bcol = bxw[h]
        decay = jnp.exp(jnp.where(mask, bcol - bcol.T, -1e30))
        return _bdot_nt(cm_rep[:, sl], kh[:, sl]) * decay

    parts = []
    if prompt:
        ri = lax.broadcasted_iota(jnp.int32, (2 * HD, 2 * HD), 0)
        ci = lax.broadcasted_iota(jnp.int32, (2 * HD, 2 * HD), 1)
        same_head = (ri // HD) == (ci // HD)
        for j in range(NH // 2):
            pr = slice(2 * HD * j, 2 * HD * (j + 1))
            xp = xs[:, pr]
            xdiag = jnp.concatenate([jnp.where(lane < HD, xp, 0.0), jnp.where(lane >= HD, xp, 0.0)], axis=0)
            st = st_scr[j]
            parts.append(_bdot(jnp.concatenate([scores(2 * j), scores(2 * j + 1)], axis=1), xdiag)
                         + _bdot(qt[:, pr], st))
            st_scr[j] = st * jnp.exp(blast[:, pr]) + jnp.where(same_head, _bdot(kdt[pr, :], xp), 0.0)
    else:
        for h in range(NH):
            sl = slice(HD * h, HD * (h + 1))
            sts = sss_in[seqs, h].reshape(SEQ_PER_CB * HD, HD)
            parts.append(_bdot(scores(h), xs[:, sl]) + _bdot(_seq_expand_lanes(qt[:, sl]), sts))
            dec3 = jnp.exp(blast[:, sl]).reshape(SEQ_PER_CB, L_S, HD)[:, L_S - 1:L_S, :]
            dec_rows = jnp.broadcast_to(dec3, (SEQ_PER_CB, HD, HD)).reshape(SEQ_PER_CB * HD, HD)
            upd = _bdot(_seq_expand_rows(kdt[sl, :]), xs[:, sl])
            sss_out[seqs, h] = (sts * dec_rows + upd).reshape(SEQ_PER_CB, HD, HD)
    if is_last is not None:
        @pl.when(is_last)
        def _():
            ssp_out[0] = st_scr[...]
            convp_out[0] = cbuf[0:8, :]
    y = (_concat_heads(parts) + dx_ref[...] * xs) * _silu(z)
    ms = _split_dot(y * y, g192_ref[...], 1) * (1.0 / (SSW // 2))
    om_ref[rows, :] = y * lax.rsqrt(ms + RMS_EPS) * sg_ref[...]


def _ssd_body(p_ref, cw_ref, cb_ref, dtb_ref, aneg_ref, dx_ref, sg_ref, g192_ref,
              trilp_ref, trils_ref, hist_ref, sss_in, om_ref, ssp_out, sss_out, convp_out, convs_out,
              cbuf, st_scr):
    i = pl.program_id(0)
    common = (p_ref, cw_ref, cb_ref, dtb_ref, aneg_ref, dx_ref, sg_ref, g192_ref)
    npc, steps_per_seq = NPC // SSD_SUB, CH_PER_SEQ // SSD_SUB

    @pl.when((i < npc) & (i % steps_per_seq == 0))
    def _():
        st_scr[...] = jnp.zeros_like(st_scr)
        cbuf[...] = jnp.zeros_like(cbuf)

    @pl.when(i < npc)
    def _():
        for sub in range(SSD_SUB):
            is_last = (i % steps_per_seq == steps_per_seq - 1) if sub == SSD_SUB - 1 else None
            _ssd_block(True, *common, trilp_ref, om_ref, cbuf, st_scr, sub, is_last=is_last,
                       ssp_out=ssp_out, convp_out=convp_out)

    @pl.when(i >= npc)
    def _():
        for sub in range(SSD_SUB):
            _ssd_block(False, *common, trils_ref, om_ref, cbuf, st_scr, sub, hist_ref=hist_ref,
                       sss_in=sss_in, sss_out=sss_out, convs_out=convs_out)


SSD_SUB = 2


def _ssd(pm, prm, consts, hist, st):
    sample_rows = pl.BlockSpec((SSD_SUB * CB, CONVC),
                               lambda i: (jnp.clip(i - NPC // SSD_SUB, 0, NSC // SSD_SUB - 1), 0))
    return pl.pallas_call(
        _ssd_body,
        grid=((NPC + NSC) // SSD_SUB,),
        in_specs=[
            _mixer_blk(P_SS, SSD_SUB), _const_spec((4, CONVC)), _const_spec((1, CONVC)),
            _const_spec((1, 128)), _const_spec((1, 128)), _const_spec((1, SSW)), _const_spec((1, SSW)),
            _const_spec((SSW, SSW)),
            _const_spec((CB, CB)), _const_spec((CB, CB)),
            sample_rows,
            _sstate_spec((NH, HD, HD), SSD_SUB),
        ],
        out_specs=[_mixer_blk(SSW, SSD_SUB), _pstate_spec((NH // 2, 2 * HD, 2 * HD), SSD_SUB),
                   _sstate_spec((NH, HD, HD), SSD_SUB), _pstate_spec((8, CONVC), SSD_SUB), sample_rows],
        out_shape=[
            jax.ShapeDtypeStruct((ROWS, SSW), F32),
            jax.ShapeDtypeStruct((NB_P, NH // 2, 2 * HD, 2 * HD), F32),
            jax.ShapeDtypeStruct((NB_S, NH, HD, HD), F32),
            jax.ShapeDtypeStruct((NB_P, 8, CONVC), F32),
            jax.ShapeDtypeStruct((ROWS_S, CONVC), F32),
        ],
        scratch_shapes=[pltpu.VMEM((CB + 8, CONVC), F32), pltpu.VMEM((NH // 2, 2 * HD, 2 * HD), F32)],
        compiler_params=_cparams(("arbitrary",)),
        name="ssd",
    )(pm, prm["cw"], prm["cb"], prm["dtb"], prm["aneg"], prm["dx"], prm["sg"],
      consts["g192"], consts["trilp"], consts["trils"], hist, st)


def _o_core(i, oh_ref, os_ref, om_ref, x_ref, wo_ref, gp, gs, lng_ref, lnb_ref, scp, scs, shp, shs):
    mix = (_bdot(oh_ref[...], wo_ref[0:HW, :]) + _bdot(os_ref[...], wo_ref[HW:HW + S5W, :])
           + _bdot(om_ref[...], wo_ref[HW + S5W:D, :]))
    x1 = _layer_norm(ALPHA * x_ref[...] + _rowmod(i, gp, gs) * mix, lng_ref[...], lnb_ref[...])
    h2 = x1 * (1.0 + _rowmod(i, scp, scs)) + _rowmod(i, shp, shs)
    return x1, h2


def _o_body(oh_ref, os_ref, om_ref, x_ref, wo_ref, gp, gs, lng_ref, lnb_ref, scp, scs, shp, shs,
            x1_ref, h2_ref):
    i = pl.program_id(0)
    x1, h2 = _o_core(i, oh_ref, os_ref, om_ref, x_ref, wo_ref, gp, gs, lng_ref, lnb_ref, scp, scs, shp, shs)
    x1_ref[...] = x1
    h2_ref[...] = h2.astype(BF16)


def _o_router_body(oh_ref, os_ref, om_ref, x_ref, wo_ref, gp, gs, lng_ref, lnb_ref, scp, scs, shp, shs,
                   wr_ref, br_ref, x1_ref, h2_ref, route_ref):
    i = pl.program_id(0)
    x1, h2 = _o_core(i, oh_ref, os_ref, om_ref, x_ref, wo_ref, gp, gs, lng_ref, lnb_ref, scp, scs, shp, shs)
    x1_ref[...] = x1
    h2_ref[...] = h2
    h_hi = h2.astype(BF16)
    h_lo = (h2 - h_hi.astype(F32)).astype(BF16)
    logits = (jnp.dot(h_hi, wr_ref[0], preferred_element_type=F32)
              + jnp.dot(h_lo, wr_ref[0], preferred_element_type=F32)
              + jnp.dot(h_hi, wr_ref[1], preferred_element_type=F32)) + br_ref[...]
    lane = lax.broadcasted_iota(jnp.int32, (TM, 128), 1).astype(F32)
    neg = -jnp.inf
    lg = jnp.where(lane < NEXP, logits, neg)
    m1 = jnp.max(lg, axis=-1, keepdims=True)
    i1 = jnp.min(jnp.where(lg == m1, lane, 128.0), axis=-1, keepdims=True)
    lg2 = jnp.where(lane == i1, neg, lg)
    m2 = jnp.max(lg2, axis=-1, keepdims=True)
    i2 = jnp.min(jnp.where(lg2 == m2, lane, 128.0), axis=-1, keepdims=True)
    e2 = jnp.exp(m2 - m1)
    den = 1.0 + e2
    route_ref[...] = jnp.where(lane == 0.0, i1, jnp.where(lane == 1.0, i2,
                               jnp.where(lane == 2.0, 1.0 / den, jnp.where(lane == 3.0, e2 / den, 0.0))))


def _stage_o(layer, oh, os_, om, x, wo_b, ada4, ln_g, ln_b, router=None):
    in_specs = [
        _row_spec(HW), _row_spec(S5W), _row_spec(SSW), _row_spec(D), _const_spec((D, D)),
        *_mod_specs(layer, 2), _const_spec((1, D)), _const_spec((1, D)),
        *_mod_specs(layer, 4), *_mod_specs(layer, 3),
    ]
    args = [oh, os_, om, x, wo_b, ada4, ada4, ln_g, ln_b, ada4, ada4, ada4, ada4]
    out_specs = [_row_spec(D), _row_spec(D)]
    out_shape = [jax.ShapeDtypeStruct((ROWS, D), F32), jax.ShapeDtypeStruct((ROWS, D), BF16)]
    body = _o_body
    if router is not None:
        in_specs += [_const_spec((2, D, 128)), _const_spec((1, 128))]
        args += list(router)
        out_specs.append(_row_spec(128))
        out_shape[1] = jax.ShapeDtypeStruct((ROWS, D), F32)
        out_shape.append(jax.ShapeDtypeStruct((ROWS, 128), F32))
        body = _o_router_body
    return pl.pallas_call(
        body, grid=(NT,), in_specs=in_specs, out_specs=out_specs, out_shape=out_shape,
        compiler_params=_cparams(("parallel",)),
        name="stage_o_router" if router is not None else "stage_o",
    )(*args)


def _ffn_body(te_ref, nu_ref, h_ref, wg_ref, wu_ref, wd_ref, o_ref, acc_ref):
    i = pl.program_id(0)
    j = pl.program_id(1)

    @pl.when(j == 0)
    def _():
        acc_ref[...] = jnp.zeros_like(acc_ref)

    @pl.when(i < nu_ref[0])
    def _():
        h = h_ref[...].astype(BF16)
        g = jnp.dot(h, wg_ref[...], preferred_element_type=F32)
        u = jnp.dot(h, wu_ref[...], preferred_element_type=F32)
        act = (_silu(g) * u).astype(BF16)
        acc_ref[...] += jnp.dot(act, wd_ref[...], preferred_element_type=F32)

    @pl.when(j == pl.num_programs(1) - 1)
    def _():
        o_ref[...] = acc_ref[...]


def _ffn(tile_expert, n_used, h, wg, wu, wd):
    rows = h.shape[0]
    nj = DFF // TF

    def jblk(i, j, nu):
        return jnp.where(i < nu[0], j, nj - 1)

    grid_spec = pltpu.PrefetchScalarGridSpec(
        num_scalar_prefetch=2,
        grid=(rows // TMF, nj),
        in_specs=[
            pl.BlockSpec((TMF, D), lambda i, j, te, nu: (i, 0)),
            pl.BlockSpec((None, D, TF), lambda i, j, te, nu: (te[i], 0, jblk(i, j, nu))),
            pl.BlockSpec((None, D, TF), lambda i, j, te, nu: (te[i], 0, jblk(i, j, nu))),
            pl.BlockSpec((None, TF, D), lambda i, j, te, nu: (te[i], jblk(i, j, nu), 0)),
        ],
        out_specs=pl.BlockSpec((TMF, D), lambda i, j, te, nu: (i, 0)),
        scratch_shapes=[pltpu.VMEM((TMF, D), F32)],
    )
    return pl.pallas_call(
        _ffn_body, grid_spec=grid_spec,
        out_shape=jax.ShapeDtypeStruct((rows, D), F32),
        compiler_params=_cparams(("parallel", "arbitrary")),
        name="ffn",
    )(tile_expert, n_used, h, wg, wu, wd)


def _final_body(x1_ref, *refs):
    ya_refs = refs[0:MOE_CHUNKS]
    yb_refs = refs[MOE_CHUNKS:2 * MOE_CHUNKS]
    route_ref, gp, gs, lng_ref, lnb_ref, yp_ref, ys_ref = refs[2 * MOE_CHUNKS:]
    i = pl.program_id(0)
    ya = ya_refs[0][...]
    yb = yb_refs[0][...]
    for c in range(1, MOE_CHUNKS):
        ya = jnp.where(i >= c * MOE_TILES, ya_refs[c][...], ya)
        yb = jnp.where(i >= c * MOE_TILES, yb_refs[c][...], yb)
    f = route_ref[:, 2:3] * ya + route_ref[:, 3:4] * yb
    y = _layer_norm(ALPHA * x1_ref[...] + _rowmod(i, gp, gs) * f, lng_ref[...], lnb_ref[...])

    @pl.when(i < NPT)
    def _():
        yp_ref[...] = y

    @pl.when(i >= NPT)
    def _():
        ys_ref[...] = y


def _stage_final(layer, x1, ya, yb, route, ada4, ln_g, ln_b):
    def chunk_spec(c):
        return pl.BlockSpec((TM, D), lambda i: (jnp.clip(i - c * MOE_TILES, 0, MOE_TILES - 1), 0))

    chunk_specs = [chunk_spec(c) for c in range(MOE_CHUNKS)]
    return pl.pallas_call(
        _final_body,
        grid=(NT,),
        in_specs=[
            _row_spec(D), *chunk_specs, *chunk_specs, _row_spec(128),
            *_mod_specs(layer, 5), _const_spec((1, D)), _const_spec((1, D)),
        ],
        out_specs=[
            pl.BlockSpec((TM, D), lambda i: (jnp.minimum(i, NPT - 1), 0)),
            pl.BlockSpec((TM, D), lambda i: (jnp.clip(i - NPT, 0, NST - 1), 0)),
        ],
        out_shape=[jax.ShapeDtypeStruct((ROWS_P, D), F32), jax.ShapeDtypeStruct((ROWS_S, D), F32)],
        compiler_params=_cparams(("arbitrary",)),
        name="stage_final",
    )(x1, *ya, *yb, route, ada4, ada4, ln_g, ln_b)


def _block_ones(n, blk):
    r = np.arange(n) // blk
    return r[:, None] == r[None, :]


def _consts():
    t = np.arange(CB)
    causal = t[:, None] >= t[None, :]
    same_seq = (t[:, None] // L_S) == (t[None, :] // L_S)
    r = np.arange(S5_ROWS)
    perm = r[None, :] == ((r % NB_P) * S5_TB + r // NB_P)[:, None]
    mats = {
        "perm": perm,
        "permt": perm.T,
        "e64": _block_ones(HW, HD),
        "g192": _block_ones(SSW, SSW // 2),
        "trilp": causal,
        "trils": causal & same_seq,
    }
    return {k: jnp.asarray(v.astype(np.float32), dtype=BF16) for k, v in mats.items()}


def _s5_params(a_re, a_im, log_dt, b_re, b_im, c_re, c_im, d, w_glu, b_glu):
    dt = jnp.exp(log_dt)[:, None]
    mag = jnp.exp(a_re * dt)
    lam_re, lam_im = mag * jnp.cos(a_im * dt), mag * jnp.sin(a_im * dt)
    den = a_re * a_re + a_im * a_im
    nr, ni = lam_re - 1.0, lam_im
    zr = (nr * a_re + ni * a_im) / den
    zi = (ni * a_re - nr * a_im) / den
    bbar_re = zr[..., None] * b_re - zi[..., None] * b_im
    bbar_im = zr[..., None] * b_im + zi[..., None] * b_re
    eye = jnp.eye(16, dtype=F32)
    blk = lambda bb: jnp.einsum('gph,gk->ghkp', bb, eye).reshape(S5W, S5N)
    hu, hs = S5W // 2, S5N // 2
    bblk = jnp.stack([jnp.stack([blk(bb)[hu * j:hu * (j + 1), hs * j:hs * (j + 1)] for j in range(2)])
                      for bb in (bbar_re, bbar_im)]).astype(BF16)
    cblk = lambda cc: jnp.einsum('ghp,gk->gpkh', cc, eye).reshape(S5N, S5W)
    ccat = jnp.stack([jnp.stack([cblk(cc)[hs * j:hs * (j + 1), hu * j:hu * (j + 1)] for cc in (c_re, -c_im)])
                      for j in range(2)]).astype(BF16)
    lr, li = lam_re.reshape(-1), lam_im.reshape(-1)
    pows = [(jnp.ones_like(lr), jnp.zeros_like(li))]
    for _ in range(8):
        pr, pi = pows[-1]
        pows.append((pr * lr - pi * li, pr * li + pi * lr))
    rows = jnp.arange(8)[:, None]
    tabs = []
    for dsh in (1, 2, 4):
        pr, pi = pows[dsh]
        tabs.append(jnp.where(rows >= dsh, jnp.concatenate([pr, pi])[None, :], 0.0))
    tabs.append(jnp.stack([jnp.concatenate(pows[r + 1]) for r in range(8)]))
    return {
        "tab": jnp.stack(tabs), "bblk": bblk, "ccat": ccat,
        "lam8": jnp.broadcast_to(jnp.concatenate([lr, li])[None, :], (NB_P, 2 * S5N)),
        "d": d.reshape(1, S5W), "wglu": w_glu.astype(BF16), "bglu": b_glu.reshape(1, S5W),
    }


def _pair_diag(st):
    s6 = st.reshape(NB_P, NH // 2, 2, HD, 2, HD)
    return jnp.stack([s6[:, :, 0, :, 0, :], s6[:, :, 1, :, 1, :]], axis=2).reshape(NB_P, NH, HD, HD)


def _pad_lanes(v, n=128):
    return jnp.pad(v, (0, n - v.shape[0])).reshape(1, n)


def kernel(x_prompt, x_sample, c_prompt, c_sample, state_hgrn, state_s5, state_ssd, state_ssd_conv, w_ada, b_ada, ln_g, ln_b, w_in, w_out, hgrn_lb_logits, hgrn_norm_g, s5_a_re, s5_a_im, s5_log_dt, s5_b_re, s5_b_im, s5_c_re, s5_c_im, s5_d, s5_w_glu, s5_b_glu, ssd_conv_w, ssd_conv_b, ssd_dt_bias, ssd_a_log, ssd_d, ssd_norm_g, ffn_w_gate, ffn_w_up, ffn_w_down, moe_w_router, moe_b_router, moe_w_gate, moe_w_up, moe_w_down):
    consts = _consts()
    c_all = jnp.concatenate([c_sample, c_prompt], axis=0)
    ada4 = _ada(c_all, w_ada, b_ada).reshape(DEPTH, NB_S + NB_P, 1, 6 * D)

    lb_all = jnp.cumsum(jax.nn.softmax(hgrn_lb_logits, axis=0), axis=0)
    lb_all = lb_all - lb_all[0]

    xp = x_prompt.reshape(ROWS_P, D)
    xs = x_sample.reshape(ROWS_S, D)
    new_h, new_s5, new_m, new_c = [], [], [], []
    x1 = f = route = None
    for l in range(DEPTH):
        w_in_b = jnp.pad(w_in[l], ((0, 0), (0, N_IN_PAD - N_IN))).astype(BF16)
        if l == 0:
            x, ph, ps, pm = _stage_a0(xp, xs, ada4, w_in_b)
        else:
            x, ph, ps, pm = _stage_a1(l, x1, f, ada4, ln_g[l - 1, 1].reshape(1, D), ln_b[l - 1, 1].reshape(1, D),
                                      w_in_b)
        oh, hg_p, hg_s = _hgrn(ph, lb_all[l].reshape(1, HW), hgrn_norm_g[l].reshape(1, HW), consts,
                               jnp.swapaxes(state_hgrn[l], -1, -2))
        s5p = _s5_params(s5_a_re[l], s5_a_im[l], s5_log_dt[l], s5_b_re[l], s5_b_im[l], s5_c_re[l], s5_c_im[l],
                         s5_d[l], s5_w_glu[l], s5_b_glu[l])
        os_p, s5_p = _s5_prompt(ps[:ROWS_P].reshape(NB_P, L_P, S5W), s5p, consts)
        os_s, s5_s = _s5_sample(ps[ROWS_P:], s5p, state_s5[l].reshape(NB_S, 1, 2 * S5N))
        os_ = jnp.concatenate([os_p.reshape(ROWS_P, S5W), os_s], axis=0)
        ssd_prm = {
            "cw": ssd_conv_w[l], "cb": ssd_conv_b[l].reshape(1, CONVC),
            "dtb": _pad_lanes(ssd_dt_bias[l]), "aneg": _pad_lanes(-jnp.exp(ssd_a_log[l])),
            "dx": jnp.repeat(ssd_d[l], HD).reshape(1, SSW), "sg": ssd_norm_g[l].reshape(1, SSW),
        }
        hist = jnp.pad(state_ssd_conv[l], ((0, 0), (0, L_S - 3), (0, 0))).reshape(ROWS_S, CONVC)
        om, ss_p, ss_s, tail_p, xbc_s = _ssd(pm, ssd_prm, consts, hist, state_ssd[l])
        conv_p = tail_p[:, 8 - 3:]
        conv_s = xbc_s.reshape(NB_S, L_S, CONVC)[:, L_S - 3:]
        hg_p, ss_p = _pair_diag(hg_p), _pair_diag(ss_p)
        new_h.append((jnp.swapaxes(hg_p, -1, -2), jnp.swapaxes(hg_s, -1, -2)))
        new_s5.append((s5_p.reshape(NB_P, 2, 16, 64), s5_s.reshape(NB_S, 2, 16, 64)))
        new_m.append((ss_p, ss_s))
        new_c.append((conv_p, conv_s))

        wo_b = w_out[l].astype(BF16)
        lg, lbb = ln_g[l, 0].reshape(1, D), ln_b[l, 0].reshape(1, D)
        j = l // 2
        if l % 2 == 0:
            x1, h2 = _stage_o(l, oh, os_, om, x, wo_b, ada4, lg, lbb)
            f = _ffn(jnp.zeros((ROWS // TMF,), jnp.int32), jnp.full((1,), ROWS // TMF, jnp.int32), h2,
                     ffn_w_gate[j:j + 1].astype(BF16),
                     ffn_w_up[j:j + 1].astype(BF16), ffn_w_down[j:j + 1].astype(BF16))
        else:
            wr = jnp.pad(moe_w_router[j], ((0, 0), (0, 128 - NEXP)))
            wr_hi = wr.astype(BF16)
            wr = jnp.stack([wr_hi, (wr - wr_hi.astype(F32)).astype(BF16)])
            br = _pad_lanes(moe_b_router[j])
            x1, h2, route = _stage_o(l, oh, os_, om, x, wo_b, ada4, lg, lbb, router=(wr, br))
            wg_b, wu_b, wd_b = (moe_w_gate[j].astype(BF16), moe_w_up[j].astype(BF16),
                                moe_w_down[j].astype(BF16))
            ya, yb = [], []
            for c in range(MOE_CHUNKS):
                flat_e = route[c * MOE_ROWS:(c + 1) * MOE_ROWS, 0:2].astype(jnp.int32).reshape(-1)
                onehot = (flat_e[:, None] == jnp.arange(NEXP)[None, :]).astype(jnp.int32)
                csum = jnp.cumsum(onehot, axis=0)
                counts = csum[-1]
                rank = jnp.take_along_axis(csum, flat_e[:, None], axis=1)[:, 0] - 1
                padded = ((counts + TMF - 1) // TMF) * TMF
                pend = jnp.cumsum(padded)
                pstart = pend - padded
                dest = pstart[flat_e] + rank
                n_pad = 2 * MOE_ROWS + NEXP * TMF
                tok = c * MOE_ROWS + jnp.arange(2 * MOE_ROWS, dtype=jnp.int32) // 2
                src_tok = jnp.full((n_pad,), c * MOE_ROWS, jnp.int32).at[dest].set(tok)
                tile_start = jnp.arange(n_pad // TMF, dtype=jnp.int32) * TMF
                tile_e = jnp.minimum(jnp.sum((pend[None, :] <= tile_start[:, None]).astype(jnp.int32), axis=1),
                                     NEXP - 1)
                n_used = (pend[NEXP - 1:NEXP] // TMF).astype(jnp.int32)
                y_sorted = _ffn(tile_e, n_used, h2[src_tok], wg_b, wu_b, wd_b)
                pos = dest.reshape(MOE_ROWS, 2)
                ya.append(y_sorted[pos[:, 0]])
                yb.append(y_sorted[pos[:, 1]])
    y_p, y_s = _stage_final(DEPTH - 1, x1, ya, yb, route, ada4, ln_g[DEPTH - 1, 1].reshape(1, D),
                            ln_b[DEPTH - 1, 1].reshape(1, D))
    stack = lambda lst, k: jnp.stack([t[k] for t in lst])
    return (y_p.reshape(NB_P, L_P, D), y_s.reshape(NB_S, L_S, D),
            stack(new_h, 0), stack(new_s5, 0), stack(new_m, 0), stack(new_c, 0),
            stack(new_h, 1), stack(new_s5, 1), stack(new_m, 1), stack(new_c, 1))
```

```python
import functools
import math

import jax
import jax.numpy as jnp
import numpy as np
from jax import lax
from jax.experimental import pallas as pl
from jax.experimental.pallas import tpu as pltpu

F32 = jnp.float32
BF16 = jnp.bfloat16

D = 1024
NB_P, L_P = 8, 2048
NB_S, L_S = 128, 8
ROWS_P = NB_P * L_P
ROWS_S = NB_S * L_S
ROWS = ROWS_P + ROWS_S
DEPTH = 2
HW = 384
S5W = 256
SSW = 384
NH = 6
HD = 64
S5N = 1024
CONVC = 640
N_IN = 2822
N_IN_PAD = 2944
P_HG = 1536
P_SS = 1152
DFF = 2816
NEXP = 8
ALPHA = (2 * DEPTH) ** 0.25
LN_EPS = 1e-5
RMS_EPS = 1e-6

TM = 512
NPT = ROWS_P // TM
NST = ROWS_S // TM
NT = NPT + NST
SEQ_PER_TILE = TM // L_S
TILES_PER_SEQ = L_P // TM

CB = 128
NPC = ROWS_P // CB
NSC = ROWS_S // CB
CH_PER_SEQ = L_P // CB
SEQ_PER_CB = CB // L_S

TMF = 512
TF = 1408
MOE_PARTS = 4
VMEM_LIMIT = 56 * 1024 * 1024


def _cparams(sem):
    return pltpu.CompilerParams(dimension_semantics=sem, vmem_limit_bytes=VMEM_LIMIT)


def _bdot(a, b):
    return jnp.dot(a.astype(BF16), b.astype(BF16), preferred_element_type=F32)


def _bdot_nt(a, b):
    return lax.dot_general(a.astype(BF16), b.astype(BF16), (((1,), (1,)), ((), ())),
                           preferred_element_type=F32)


def _split_dot(x, e, passes):
    acc = None
    r = x
    for _ in range(passes):
        hi = r.astype(BF16)
        d = jnp.dot(hi, e, preferred_element_type=F32)
        acc = d if acc is None else acc + d
        r = r - hi.astype(F32)
    return acc


def _split_dot_l(e, x, passes):
    acc = None
    r = x
    for _ in range(passes):
        hi = r.astype(BF16)
        d = jnp.dot(e, hi, preferred_element_type=F32)
        acc = d if acc is None else acc + d
        r = r - hi.astype(F32)
    return acc


def _silu(x):
    return x * jax.nn.sigmoid(x)


def _layer_norm(x, g, b):
    mu = jnp.mean(x, -1, keepdims=True)
    xc = x - mu
    var = jnp.mean(xc * xc, -1, keepdims=True)
    return xc * lax.rsqrt(var + LN_EPS) * g + b


def _rowmod(i, p_ref, s_ref):
    s = jnp.broadcast_to(s_ref[...], (SEQ_PER_TILE, L_S, D)).reshape(TM, D)
    return jnp.where(i < NPT, p_ref[0], s)


ADA_TN = 1536


def _ada_body(c_ref, w_ref, b_ref, o_ref):
    o_ref[...] = _bdot(_silu(c_ref[...]), w_ref[...]) + b_ref[...]


def _ada(c_all, w_ada, b_ada):
    nc = c_all.shape[0]
    return pl.pallas_call(
        _ada_body,
        grid=(DEPTH, 6 * D // ADA_TN),
        in_specs=[
            pl.BlockSpec((nc, D), lambda l, j: (0, 0)),
            pl.BlockSpec((None, D, ADA_TN), lambda l, j: (l, 0, j)),
            pl.BlockSpec((None, 1, ADA_TN), lambda l, j: (l, 0, j)),
        ],
        out_specs=pl.BlockSpec((None, nc, ADA_TN), lambda l, j: (l, 0, j)),
        out_shape=jax.ShapeDtypeStruct((DEPTH, nc, 6 * D), F32),
        compiler_params=_cparams(("parallel", "parallel")),
        name="ada",
    )(c_all, w_ada, b_ada.reshape(DEPTH, 1, 6 * D))


def _mod_specs(layer, k):
    ps = pl.BlockSpec((None, 1, 1, D),
                      lambda i: (layer, NB_S + jnp.minimum(i // TILES_PER_SEQ, NB_P - 1), 0, k))
    ss = pl.BlockSpec((None, SEQ_PER_TILE, 1, D),
                      lambda i: (layer, jnp.clip(i - NPT, 0, NST - 1), 0, k))
    return [ps, ss]


def _row_spec(width):
    return pl.BlockSpec((TM, width), lambda i: (i, 0))


def _const_spec(shape):
    nd = len(shape)
    return pl.BlockSpec(shape, lambda *_: (0,) * nd)


def _proj_out(x, i, scp, scs, shp, shs, w_ref, ph_ref, ps_ref, pm_ref):
    h = x * (1.0 + _rowmod(i, scp, scs)) + _rowmod(i, shp, shs)
    proj = jnp.dot(h.astype(BF16), w_ref[...], preferred_element_type=F32)
    ph_ref[...] = proj[:, 0:P_HG]
    ps_ref[...] = proj[:, P_HG:P_HG + S5W]
    pm_ref[...] = proj[:, P_HG + S5W:N_IN_PAD]


def _a0_body(xp_ref, xs_ref, scp, scs, shp, shs, w_ref, x_ref, ph_ref, ps_ref, pm_ref):
    i = pl.program_id(0)
    x = jnp.where(i < NPT, xp_ref[...], xs_ref[...])
    x_ref[...] = x
    _proj_out(x, i, scp, scs, shp, shs, w_ref, ph_ref, ps_ref, pm_ref)


def _a1_body(x1_ref, f_ref, gp, gs, lng_ref, lnb_ref, scp, scs, shp, shs, w_ref,
             x_ref, ph_ref, ps_ref, pm_ref):
    i = pl.program_id(0)
    x = _layer_norm(ALPHA * x1_ref[...] + _rowmod(i, gp, gs) * f_ref[...], lng_ref[...], lnb_ref[...])
    x_ref[...] = x
    _proj_out(x, i, scp, scs, shp, shs, w_ref, ph_ref, ps_ref, pm_ref)


def _a_out():
    specs = [_row_spec(D), _row_spec(P_HG), _row_spec(S5W), _row_spec(P_SS)]
    shapes = [jax.ShapeDtypeStruct((ROWS, w), F32) for w in (D, P_HG, S5W, P_SS)]
    return specs, shapes


def _stage_a0(xp, xs, ada4, w_in_b):
    out_specs, out_shape = _a_out()
    return pl.pallas_call(
        _a0_body,
        grid=(NT,),
        in_specs=[
            pl.BlockSpec((TM, D), lambda i: (jnp.minimum(i, NPT - 1), 0)),
            pl.BlockSpec((TM, D), lambda i: (jnp.clip(i - NPT, 0, NST - 1), 0)),
            *_mod_specs(0, 1), *_mod_specs(0, 0),
            _const_spec((D, N_IN_PAD)),
        ],
        out_specs=out_specs, out_shape=out_shape,
        compiler_params=_cparams(("parallel",)),
        name="stage_a0",
    )(xp, xs, ada4, ada4, ada4, ada4, w_in_b)


def _stage_a1(layer, x1, f, ada4, ln_g, ln_b, w_in_b):
    out_specs, out_shape = _a_out()
    return pl.pallas_call(
        _a1_body,
        grid=(NT,),
        in_specs=[
            _row_spec(D), _row_spec(D),
            *_mod_specs(layer - 1, 5),
            _const_spec((1, D)), _const_spec((1, D)),
            *_mod_specs(layer, 1), *_mod_specs(layer, 0),
            _const_spec((D, N_IN_PAD)),
        ],
        out_specs=out_specs, out_shape=out_shape,
        compiler_params=_cparams(("parallel",)),
        name="stage_a1",
    )(x1, f, ada4, ada4, ln_g, ln_b, ada4, ada4, ada4, ada4, w_in_b)


def _mixer_blk(width, nsub=1):
    return pl.BlockSpec((nsub * CB, width), lambda i: (i, 0))


def _pstate_spec(shape, nsub=1):
    nd = len(shape)
    return pl.BlockSpec((1,) + shape,
                        lambda i: (jnp.minimum(i // (CH_PER_SEQ // nsub), NB_P - 1),) + (0,) * nd)


def _sstate_spec(shape, nsub=1):
    nd = len(shape)
    return pl.BlockSpec((nsub * SEQ_PER_CB,) + shape,
                        lambda i: (jnp.clip(i - NPC // nsub, 0, NSC // nsub - 1),) + (0,) * nd)


def _cumsum_rows(x, span):
    r = lax.broadcasted_iota(jnp.int32, x.shape, 0) & (span - 1)
    d = 1
    while d < span:
        x = x + jnp.where(r >= d, pltpu.roll(x, d, 0), 0.0)
        d *= 2
    return x


def _seq_last_rows(x):
    w = x.shape[-1]
    x3 = x.reshape(SEQ_PER_CB, L_S, w)
    return jnp.broadcast_to(x3[:, L_S - 1:L_S, :], (SEQ_PER_CB, L_S, w)).reshape(CB, w)


def _concat_heads(parts):
    return jnp.concatenate(parts, axis=1)


def _stack_select(shape, row_div, lane_div):
    r = lax.broadcasted_iota(jnp.int32, shape, 0) // row_div
    c = lax.broadcasted_iota(jnp.int32, shape, 1) // lane_div
    return r == c


def _seq_expand_lanes(qh):
    q2 = jnp.concatenate([qh, qh], axis=1)
    q16 = jnp.concatenate([q2] * (SEQ_PER_CB // 2), axis=1)
    return jnp.where(_stack_select((CB, SEQ_PER_CB * HD), L_S, HD), q16, 0.0)


def _seq_expand_rows(xt):
    t = jnp.broadcast_to(xt[None], (SEQ_PER_CB, HD, CB)).reshape(SEQ_PER_CB * HD, CB)
    return jnp.where(_stack_select((SEQ_PER_CB * HD, CB), HD, L_S), t, 0.0)


def _fold_seq_lanes(full):
    acc = full[:, 0:128]
    for j in range(1, SEQ_PER_CB * HD // 128):
        acc = acc + full[:, 128 * j:128 * (j + 1)]
    return acc[:, 0:HD] + acc[:, HD:2 * HD]


HGRN_BASE = 32
EXP_RANGE_MAX = 80.0


def _hgrn_block(prompt, p_ref, lb_ref, hg_ref, e64_ref, oh_ref, st_scr, o_scr, sub, is_last=None,
                stp_out=None, sts_in=None, sts_out=None):
    rows = slice(sub * CB, (sub + 1) * CB)
    seqs = slice(sub * SEQ_PER_CB, (sub + 1) * SEQ_PER_CB)
    lb = lb_ref[...]
    qr = p_ref[rows, 0:HW]
    fr = p_ref[rows, HW:2 * HW]
    v = p_ref[rows, 2 * HW:3 * HW]
    gr = p_ref[rows, 3 * HW:4 * HW]
    ls = jnp.minimum(fr, 0.0) - jnp.log1p(jnp.exp(-jnp.abs(fr)))
    a = jnp.log(lb)
    bb = jnp.log1p(-lb) + ls
    lf = jnp.maximum(a, bb) + jnp.log1p(jnp.exp(-jnp.abs(a - bb)))
    kk = (1.0 - lb) * jax.nn.sigmoid(-fr)
    q = _silu(qr)
    b = _cumsum_rows(lf, CB if prompt else L_S)

    e64 = e64_ref[...]
    ti = lax.broadcasted_iota(jnp.int32, (CB, CB), 0)
    si = lax.broadcasted_iota(jnp.int32, (CB, CB), 1)

    def diag8():
        nsub = CB // 8
        b3 = b.reshape(nsub, 8, HW)
        q3 = q.reshape(nsub, 8, HW)
        k3 = kk.reshape(nsub, 8, HW)
        v3 = v.reshape(nsub, 8, HW)
        r3 = lax.broadcasted_iota(jnp.int32, (nsub, 8, HW), 1)
        o = jnp.zeros((CB, HW), F32)
        for s in range(8):
            dlt = jnp.minimum(b3 - b3[:, s:s + 1, :], 0.0)
            w = jnp.where(r3 >= s, jnp.exp(dlt), 0.0) * q3 * k3[:, s:s + 1, :]
            hsum = jnp.dot(w.reshape(CB, HW).astype(BF16), e64, preferred_element_type=F32)
            o = o + hsum * jnp.broadcast_to(v3[:, s:s + 1, :], (nsub, 8, HW)).reshape(CB, HW)
        return o

    def level_terms(m):
        terms = []
        while m < CB:
            nb = CB // (2 * m)
            b4 = b.reshape(nb, 2 * m, HW)
            bmid = b4[:, m - 1:m, :]
            pos = lax.broadcasted_iota(jnp.int32, (nb, 2 * m, HW), 1)
            qq = jnp.where(pos >= m, q.reshape(nb, 2 * m, HW) * jnp.exp(jnp.minimum(b4 - bmid, 0.0)), 0.0)
            kq = jnp.where(pos < m, kk.reshape(nb, 2 * m, HW) * jnp.exp(jnp.minimum(bmid - b4, 0.0)), 0.0)
            terms.append((qq.reshape(CB, HW), kq.reshape(CB, HW), (ti // (2 * m)) == (si // (2 * m))))
            m *= 2
        return terms

    def scores_times_v(terms):
        lane = lax.broadcasted_iota(jnp.int32, (CB, 2 * HD), 1)
        parts = []
        for j in range(NH // 2):
            scs = []
            for h in (2 * j, 2 * j + 1):
                sl = slice(HD * h, HD * (h + 1))
                sc = None
                for qq, kq, keep in terms:
                    t = jnp.where(keep, _bdot_nt(qq[:, sl], kq[:, sl]), 0.0)
                    sc = t if sc is None else sc + t
                scs.append(sc)
            vp = v[:, 2 * HD * j:2 * HD * (j + 1)]
            vdiag = jnp.concatenate([jnp.where(lane < HD, vp, 0.0), jnp.where(lane >= HD, vp, 0.0)], axis=0)
            parts.append(_bdot(jnp.concatenate(scs, axis=1), vdiag))
        return _concat_heads(parts)

    base = HGRN_BASE if prompt else L_S
    nbase = CB // base
    bb3 = b.reshape(nbase, base, HW)
    top = bb3[:, 0:1, :] - lf.reshape(nbase, base, HW)[:, 0:1, :]
    decay_range = jnp.max(top - bb3[:, base - 1:base, :])
    in_range = decay_range <= EXP_RANGE_MAX

    def exact_path():
        o_scr[rows, :] = diag8() + (scores_times_v(level_terms(8)) if prompt else 0.0)

    def finish(fast):
        _hgrn_finish(prompt, fast, rows, seqs, q, kk, v, gr, b, bb3, top, ti, si, scores_times_v, level_terms,
                     hg_ref, e64, oh_ref, st_scr, o_scr, is_last, stp_out, sts_in, sts_out)

    return in_range, exact_path, finish


def _hgrn_finish(prompt, fast, rows, seqs, q, kk, v, gr, b, bb3, top, ti, si, scores_times_v, level_terms,
                 hg_ref, e64, oh_ref, st_scr, o_scr, is_last, stp_out, sts_in, sts_out):
    base = HGRN_BASE if prompt else L_S
    nbase = CB // base
    qf = (q.reshape(nbase, base, HW) * jnp.exp(bb3 - top)).reshape(CB, HW)
    kf = (kk.reshape(nbase, base, HW) * jnp.exp(top - bb3)).reshape(CB, HW)
    keep = ((ti // base) == (si // base)) & (si <= ti)
    o_fast = scores_times_v([(qf, kf, keep)] + (level_terms(base) if prompt else []))
    o = jnp.where(fast, o_fast, o_scr[rows, :])
    qt = q * jnp.exp(b)
    if prompt:
        blast = b[CB - 1:CB, :]
        kd = kk * jnp.exp(blast - b)
        vt = v.T
        same_head = (ti // HD) == (si // HD)
        parts = []
        for j in range(NH // 2):
            pr = slice(2 * HD * j, 2 * HD * (j + 1))
            st = st_scr[j]
            parts.append(o[:, pr] + _bdot_nt(qt[:, pr], st))
            st_scr[j] = st * jnp.exp(blast[:, pr]) + jnp.where(same_head, _bdot(vt[pr, :], kd[:, pr]), 0.0)

        if is_last is not None:
            @pl.when(is_last)
            def _():
                stp_out[0] = st_scr[...]
    else:
        blast = _seq_last_rows(b)
        kd = kk * jnp.exp(blast - b)
        dec = jnp.exp(blast)
        vt = v.T
        parts = []
        for h in range(NH):
            sl = slice(HD * h, HD * (h + 1))
            sts = sts_in[seqs, h].reshape(SEQ_PER_CB * HD, HD)
            full = _bdot_nt(qt[:, sl], sts)
            sel = jnp.where(_stack_select((CB, SEQ_PER_CB * HD), L_S, HD), full, 0.0)
            parts.append(o[:, sl] + _fold_seq_lanes(sel))
            dec3 = dec[:, sl].reshape(SEQ_PER_CB, L_S, HD)[:, L_S - 1:L_S, :]
            dec_rows = jnp.broadcast_to(dec3, (SEQ_PER_CB, HD, HD)).reshape(SEQ_PER_CB * HD, HD)
            upd = _bdot(_seq_expand_rows(vt[sl, :]), kd[:, sl])
            sts_out[seqs, h] = (sts * dec_rows + upd).reshape(SEQ_PER_CB, HD, HD)
    oall = _concat_heads(parts)
    ms = _split_dot(oall * oall, e64, 1) * (1.0 / HD)
    oh_ref[rows, :] = oall * lax.rsqrt(ms + RMS_EPS) * hg_ref[...] * _silu(gr)


HGRN_SUB = 2


def _hgrn_body(p_ref, lb_ref, hg_ref, e64_ref, sts_in,
               oh_ref, stp_out, sts_out, st_scr, o_scr):
    i = pl.program_id(0)
    npc, steps_per_seq = NPC // HGRN_SUB, CH_PER_SEQ // HGRN_SUB

    @pl.when(i == 0)
    def _():
        o_scr[...] = jnp.zeros_like(o_scr)

    @pl.when((i < npc) & (i % steps_per_seq == 0))
    def _():
        st_scr[...] = jnp.zeros_like(st_scr)

    def run(blocks):
        fast = blocks[0][0]
        for in_range, _, _ in blocks[1:]:
            fast = jnp.logical_and(fast, in_range)

        @pl.when(jnp.logical_not(fast))
        def _():
            for _, exact_path, _ in blocks:
                exact_path()

        for _, _, finish in blocks:
            finish(fast)

    @pl.when(i < npc)
    def _():
        run([_hgrn_block(True, p_ref, lb_ref, hg_ref, e64_ref, oh_ref, st_scr, o_scr, sub,
                         is_last=(i % steps_per_seq == steps_per_seq - 1) if sub == HGRN_SUB - 1 else None,
                         stp_out=stp_out) for sub in range(HGRN_SUB)])

    @pl.when(i >= npc)
    def _():
        run([_hgrn_block(False, p_ref, lb_ref, hg_ref, e64_ref, oh_ref, st_scr, o_scr, sub,
                         sts_in=sts_in, sts_out=sts_out) for sub in range(HGRN_SUB)])


def _hgrn(ph, lb, hg, consts, st_t):
    return pl.pallas_call(
        _hgrn_body,
        grid=((NPC + NSC) // HGRN_SUB,),
        in_specs=[
            _mixer_blk(P_HG, HGRN_SUB), _const_spec((1, HW)), _const_spec((1, HW)),
            _const_spec((HW, HW)),
            _sstate_spec((NH, HD, HD), HGRN_SUB),
        ],
        out_specs=[_mixer_blk(HW, HGRN_SUB), _pstate_spec((NH // 2, 2 * HD, 2 * HD), HGRN_SUB),
                   _sstate_spec((NH, HD, HD), HGRN_SUB)],
        out_shape=[
            jax.ShapeDtypeStruct((ROWS, HW), F32),
            jax.ShapeDtypeStruct((NB_P, NH // 2, 2 * HD, 2 * HD), F32),
            jax.ShapeDtypeStruct((NB_S, NH, HD, HD), F32),
        ],
        scratch_shapes=[pltpu.VMEM((NH // 2, 2 * HD, 2 * HD), F32), pltpu.VMEM((HGRN_SUB * CB, HW), F32)],
        compiler_params=_cparams(("arbitrary",)),
        name="hgrn",
    )(ph, lb, hg, consts["e64"], st_t)


def _cmul_add(hr, hi, lr, li, sr, si):
    return hr + lr * sr - li * si, hi + lr * si + li * sr


def _s5_project(ub, bblk_ref):
    halves = [ub[:, (S5W // 2) * j:(S5W // 2) * (j + 1)] for j in range(2)]
    hr = jnp.concatenate([jnp.dot(halves[j], bblk_ref[0, j], preferred_element_type=F32) for j in range(2)], axis=1)
    hi = jnp.concatenate([jnp.dot(halves[j], bblk_ref[1, j], preferred_element_type=F32) for j in range(2)], axis=1)
    return hr, hi


def _s5_readout(h_scr, u, ccat_ref, d_ref, wglu_ref, bglu_ref):
    hs = S5N // 2
    ch = [_bdot(h_scr[:, hs * j:hs * (j + 1)], ccat_ref[j, 0])
          + _bdot(h_scr[:, S5N + hs * j:S5N + hs * (j + 1)], ccat_ref[j, 1]) for j in range(2)]
    y = jnp.concatenate(ch, axis=1) + d_ref[...] * u
    c0 = math.sqrt(2.0 / math.pi)
    y = y * (0.5 * (1.0 + jnp.tanh(c0 * (y + 0.044715 * (y * y * y)))))
    return y * jax.nn.sigmoid(_bdot(y, wglu_ref[...]) + bglu_ref[...])


S5_TB = 32
S5_ROWS = NB_P * S5_TB


def _s5_prompt_body(p_ref, perm_ref, permt_ref, lam_ref, bblk_ref, ccat_ref, d_ref, wglu_ref, bglu_ref,
                    os_ref, st_out, carry_scr, h_scr):
    i = pl.program_id(0)

    @pl.when(i == 0)
    def _():
        carry_scr[...] = jnp.zeros_like(carry_scr)

    u = p_ref[...].reshape(S5_ROWS, S5W)
    u_hi = u.astype(BF16)
    u_lo = (u - u_hi.astype(F32)).astype(BF16)
    perm = perm_ref[...]
    up_hi = jnp.dot(perm, u_hi, preferred_element_type=F32)
    up = up_hi + jnp.dot(perm, u_lo, preferred_element_type=F32)
    hr, hi = _s5_project(up_hi.astype(BF16), bblk_ref)
    lr = lam_ref[:, 0:S5N]
    li = lam_ref[:, S5N:2 * S5N]
    cr = carry_scr[:, 0:S5N]
    ci = carry_scr[:, S5N:2 * S5N]
    for t in range(S5_TB):
        rows = slice(NB_P * t, NB_P * (t + 1))
        cr, ci = _cmul_add(hr[rows], hi[rows], lr, li, cr, ci)
        h_scr[rows, 0:S5N] = cr
        h_scr[rows, S5N:2 * S5N] = ci
    carry_scr[:, 0:S5N] = cr
    carry_scr[:, S5N:2 * S5N] = ci
    out = _s5_readout(h_scr, up, ccat_ref, d_ref, wglu_ref, bglu_ref)
    os_ref[...] = jnp.dot(permt_ref[...], out.astype(BF16), preferred_element_type=F32).reshape(NB_P, S5_TB, S5W)

    @pl.when(i == pl.num_programs(0) - 1)
    def _():
        st_out[...] = carry_scr[...]


def _s5_prompt(ps3, prm, consts):
    return pl.pallas_call(
        _s5_prompt_body,
        grid=(L_P // S5_TB,),
        in_specs=[
            pl.BlockSpec((NB_P, S5_TB, S5W), lambda i: (0, i, 0)),
            _const_spec((S5_ROWS, S5_ROWS)), _const_spec((S5_ROWS, S5_ROWS)), _const_spec((NB_P, 2 * S5N)),
            _const_spec((2, 2, S5W // 2, S5N // 2)), _const_spec((2, 2, S5N // 2, S5W // 2)),
            _const_spec((1, S5W)), _const_spec((S5W, S5W)), _const_spec((1, S5W)),
        ],
        out_specs=[pl.BlockSpec((NB_P, S5_TB, S5W), lambda i: (0, i, 0)), _const_spec((NB_P, 2 * S5N))],
        out_shape=[jax.ShapeDtypeStruct((NB_P, L_P, S5W), F32), jax.ShapeDtypeStruct((NB_P, 2 * S5N), F32)],
        scratch_shapes=[pltpu.VMEM((NB_P, 2 * S5N), F32), pltpu.VMEM((S5_ROWS, 2 * S5N), F32)],
        compiler_params=_cparams(("arbitrary",)),
        name="s5_prompt",
    )(ps3, consts["perm"], consts["permt"], prm["lam8"], prm["bblk"], prm["ccat"], prm["d"], prm["wglu"],
      prm["bglu"])


def _s5_sample_body(p_ref, tab_ref, bblk_ref, ccat_ref, d_ref, wglu_ref, bglu_ref, s5s_in,
                    os_ref, s5s_out, h_scr):
    u = p_ref[...]
    hr, hi = _s5_project(u.astype(BF16), bblk_ref)
    nsub = CB // 8
    for idx, dsh in enumerate((1, 2, 4)):
        sr = pltpu.roll(hr, dsh, 0).reshape(nsub, 8, S5N)
        si = pltpu.roll(hi, dsh, 0).reshape(nsub, 8, S5N)
        lr = tab_ref[idx, :, 0:S5N][None]
        li = tab_ref[idx, :, S5N:2 * S5N][None]
        nr, ni = _cmul_add(hr.reshape(nsub, 8, S5N), hi.reshape(nsub, 8, S5N), lr, li, sr, si)
        hr = nr.reshape(CB, S5N)
        hi = ni.reshape(CB, S5N)
    tcr = tab_ref[3, :, 0:S5N]
    tci = tab_ref[3, :, S5N:2 * S5N]
    cr = s5s_in[:, :, 0:S5N]
    ci = s5s_in[:, :, S5N:2 * S5N]
    tr, tim = _cmul_add(hr.reshape(nsub, 8, S5N), hi.reshape(nsub, 8, S5N), tcr[None], tci[None], cr, ci)
    h_scr[:, 0:S5N] = tr.reshape(CB, S5N)
    h_scr[:, S5N:2 * S5N] = tim.reshape(CB, S5N)
    sb = lax.broadcasted_iota(jnp.int32, (SEQ_PER_CB, CB), 0)
    st = lax.broadcasted_iota(jnp.int32, (SEQ_PER_CB, CB), 1)
    sel = (st == L_S * sb + (L_S - 1)).astype(BF16)
    s5s_out[...] = _split_dot_l(sel, h_scr[...], 3)
    os_ref[...] = _s5_readout(h_scr, u, ccat_ref, d_ref, wglu_ref, bglu_ref)


def _s5_sample(ps_s, prm, st):
    seqs = lambda shape: pl.BlockSpec((SEQ_PER_CB,) + shape, lambda i: (i,) + (0,) * len(shape))
    return pl.pallas_call(
        _s5_sample_body,
        grid=(NSC,),
        in_specs=[
            _mixer_blk(S5W), _const_spec((4, 8, 2 * S5N)), _const_spec((2, 2, S5W // 2, S5N // 2)),
            _const_spec((2, 2, S5N // 2, S5W // 2)), _const_spec((1, S5W)), _const_spec((S5W, S5W)),
            _const_spec((1, S5W)), seqs((1, 2 * S5N)),
        ],
        out_specs=[_mixer_blk(S5W), seqs((2 * S5N,))],
        out_shape=[jax.ShapeDtypeStruct((ROWS_S, S5W), F32), jax.ShapeDtypeStruct((NB_S, 2 * S5N), F32)],
        scratch_shapes=[pltpu.VMEM((CB, 2 * S5N), F32)],
        compiler_params=_cparams(("parallel",)),
        name="s5_sample",
    )(ps_s, prm["tab"], prm["bblk"], prm["ccat"], prm["d"], prm["wglu"], prm["bglu"], st)


def _ssd_block(prompt, p_ref, cw_ref, cb_ref, dtb_ref, aneg_ref, dx_ref, sg_ref,
               g192_ref, tril_ref, om_ref, cbuf, st_scr, sub, is_last=None, ssp_out=None, convp_out=None,
               hist_ref=None, sss_in=None, sss_out=None, convs_out=None):
    rows = slice(sub * CB, (sub + 1) * CB)
    seqs = slice(sub * SEQ_PER_CB, (sub + 1) * SEQ_PER_CB)
    z = p_ref[rows, 0:SSW]
    xbc = p_ref[rows, SSW:SSW + CONVC]
    dtr = p_ref[rows, SSW + CONVC:P_SS]
    acc = cb_ref[...] + cw_ref[3:4, :] * xbc
    if prompt:
        cbuf[8:8 + CB, :] = xbc
        for k in (1, 2, 3):
            acc = acc + cw_ref[3 - k:4 - k, :] * cbuf[8 - k:8 - k + CB, :]
        cbuf[0:8, :] = cbuf[CB:CB + 8, :]
    else:
        convs_out[rows, :] = xbc
        tl = lax.broadcasted_iota(jnp.int32, (CB, CONVC), 0) % L_S
        hist = hist_ref[rows, :]
        for k in (1, 2, 3):
            hk = hist if k == 3 else pltpu.roll(hist, CB - (3 - k), 0)
            sh = jnp.where(tl >= k, pltpu.roll(xbc, k, 0), hk)
            acc = acc + cw_ref[3 - k:4 - k, :] * sh
    xc = _silu(acc)
    xs = xc[:, 0:SSW]
    bm = xc[:, SSW:SSW + 2 * HD]
    cm = xc[:, SSW + 2 * HD:CONVC]
    xdt = dtr + dtb_ref[...]
    dt = jnp.maximum(xdt, 0.0) + jnp.log1p(jnp.exp(-jnp.abs(xdt)))
    la = dt * aneg_ref[...]
    tril = tril_ref[...]
    b6 = _cumsum_rows(la, CB if prompt else L_S)
    lane = lax.broadcasted_iota(jnp.int32, (CB, 128), 1)
    bxw = [jnp.broadcast_to(b6[:, h:h + 1], (CB, 128)) for h in range(NH)]
    dtw = [jnp.broadcast_to(dt[:, h:h + 1], (CB, 128)) for h in range(NH)]
    pair = lambda cols: _concat_heads([jnp.where(lane < HD, cols[2 * j], cols[2 * j + 1]) for j in range(NH // 2)])
    bx = pair(bxw)
    dtx = pair(dtw)
    bm_rep = _concat_heads([bm[:, 0:HD]] * 3 + [bm[:, HD:2 * HD]] * 3)
    cm_rep = _concat_heads([cm[:, 0:HD]] * 3 + [cm[:, HD:2 * HD]] * 3)
    kh = bm_rep * dtx
    qt = cm_rep * jnp.exp(bx)
    blast = bx[CB - 1:CB, :] if prompt else _seq_last_rows(bx)
    kd = kh * jnp.exp(blast - bx)
    kdt = kd.T
    mask = tril > 0

    def scores(h):
        sl = slice(HD * h, HD * (h + 1))
        bcol = bxw[h]
        decay = jnp.exp(jnp.where(mask, bcol - bcol.T, -1e30))
        return _bdot_nt(cm_rep[:, sl], kh[:, sl]) * decay

    parts = []
    if prompt:
        ri = lax.broadcasted_iota(jnp.int32, (2 * HD, 2 * HD), 0)
        ci = lax.broadcasted_iota(jnp.int32, (2 * HD, 2 * HD), 1)
        same_head = (ri // HD) == (ci // HD)
        for j in range(NH // 2):
            pr = slice(2 * HD * j, 2 * HD * (j + 1))
            xp = xs[:, pr]
            xdiag = jnp.concatenate([jnp.where(lane < HD, xp, 0.0), jnp.where(lane >= HD, xp, 0.0)], axis=0)
            st = st_scr[j]
            parts.append(_bdot(jnp.concatenate([scores(2 * j), scores(2 * j + 1)], axis=1), xdiag)
                         + _bdot(qt[:, pr], st))
            st_scr[j] = st * jnp.exp(blast[:, pr]) + jnp.where(same_head, _bdot(kdt[pr, :], xp), 0.0)
    else:
        for h in range(NH):
            sl = slice(HD * h, HD * (h + 1))
            sts = sss_in[seqs, h].reshape(SEQ_PER_CB * HD, HD)
            parts.append(_bdot(scores(h), xs[:, sl]) + _bdot(_seq_expand_lanes(qt[:, sl]), sts))
            dec3 = jnp.exp(blast[:, sl]).reshape(SEQ_PER_CB, L_S, HD)[:, L_S - 1:L_S, :]
            dec_rows = jnp.broadcast_to(dec3, (SEQ_PER_CB, HD, HD)).reshape(SEQ_PER_CB * HD, HD)
            upd = _bdot(_seq_expand_rows(kdt[sl, :]), xs[:, sl])
            sss_out[seqs, h] = (sts * dec_rows + upd).reshape(SEQ_PER_CB, HD, HD)
    if is_last is not None:
        @pl.when(is_last)
        def _():
            ssp_out[0] = st_scr[...]
            convp_out[0] = cbuf[0:8, :]
    y = (_concat_heads(parts) + dx_ref[...] * xs) * _silu(z)
    ms = _split_dot(y * y, g192_ref[...], 1) * (1.0 / (SSW // 2))
    om_ref[rows, :] = y * lax.rsqrt(ms + RMS_EPS) * sg_ref[...]


def _ssd_body(p_ref, cw_ref, cb_ref, dtb_ref, aneg_ref, dx_ref, sg_ref, g192_ref,
              trilp_ref, trils_ref, hist_ref, sss_in, om_ref, ssp_out, sss_out, convp_out, convs_out,
              cbuf, st_scr):
    i = pl.program_id(0)
    common = (p_ref, cw_ref, cb_ref, dtb_ref, aneg_ref, dx_ref, sg_ref, g192_ref)
    npc, steps_per_seq = NPC // SSD_SUB, CH_PER_SEQ // SSD_SUB

    @pl.when((i < npc) & (i % steps_per_seq == 0))
    def _():
        st_scr[...] = jnp.zeros_like(st_scr)
        cbuf[...] = jnp.zeros_like(cbuf)

    @pl.when(i < npc)
    def _():
        for sub in range(SSD_SUB):
            is_last = (i % steps_per_seq == steps_per_seq - 1) if sub == SSD_SUB - 1 else None
            _ssd_block(True, *common, trilp_ref, om_ref, cbuf, st_scr, sub, is_last=is_last,
                       ssp_out=ssp_out, convp_out=convp_out)

    @pl.when(i >= npc)
    def _():
        for sub in range(SSD_SUB):
            _ssd_block(False, *common, trils_ref, om_ref, cbuf, st_scr, sub, hist_ref=hist_ref,
                       sss_in=sss_in, sss_out=sss_out, convs_out=convs_out)


SSD_SUB = 2


def _ssd(pm, prm, consts, hist, st):
    sample_rows = pl.BlockSpec((SSD_SUB * CB, CONVC),
                               lambda i: (jnp.clip(i - NPC // SSD_SUB, 0, NSC // SSD_SUB - 1), 0))
    return pl.pallas_call(
        _ssd_body,
        grid=((NPC + NSC) // SSD_SUB,),
        in_specs=[
            _mixer_blk(P_SS, SSD_SUB), _const_spec((4, CONVC)), _const_spec((1, CONVC)),
            _const_spec((1, 128)), _const_spec((1, 128)), _const_spec((1, SSW)), _const_spec((1, SSW)),
            _const_spec((SSW, SSW)),
            _const_spec((CB, CB)), _const_spec((CB, CB)),
            sample_rows,
            _sstate_spec((NH, HD, HD), SSD_SUB),
        ],
        out_specs=[_mixer_blk(SSW, SSD_SUB), _pstate_spec((NH // 2, 2 * HD, 2 * HD), SSD_SUB),
                   _sstate_spec((NH, HD, HD), SSD_SUB), _pstate_spec((8, CONVC), SSD_SUB), sample_rows],
        out_shape=[
            jax.ShapeDtypeStruct((ROWS, SSW), F32),
            jax.ShapeDtypeStruct((NB_P, NH // 2, 2 * HD, 2 * HD), F32),
            jax.ShapeDtypeStruct((NB_S, NH, HD, HD), F32),
            jax.ShapeDtypeStruct((NB_P, 8, CONVC), F32),
            jax.ShapeDtypeStruct((ROWS_S, CONVC), F32),
        ],
        scratch_shapes=[pltpu.VMEM((CB + 8, CONVC), F32), pltpu.VMEM((NH // 2, 2 * HD, 2 * HD), F32)],
        compiler_params=_cparams(("arbitrary",)),
        name="ssd",
    )(pm, prm["cw"], prm["cb"], prm["dtb"], prm["aneg"], prm["dx"], prm["sg"],
      consts["g192"], consts["trilp"], consts["trils"], hist, st)


def _o_core(i, oh_ref, os_ref, om_ref, x_ref, wo_ref, gp, gs, lng_ref, lnb_ref, scp, scs, shp, shs):
    mix = (_bdot(oh_ref[...], wo_ref[0:HW, :]) + _bdot(os_ref[...], wo_ref[HW:HW + S5W, :])
           + _bdot(om_ref[...], wo_ref[HW + S5W:D, :]))
    x1 = _layer_norm(ALPHA * x_ref[...] + _rowmod(i, gp, gs) * mix, lng_ref[...], lnb_ref[...])
    h2 = x1 * (1.0 + _rowmod(i, scp, scs)) + _rowmod(i, shp, shs)
    return x1, h2


def _o_body(oh_ref, os_ref, om_ref, x_ref, wo_ref, gp, gs, lng_ref, lnb_ref, scp, scs, shp, shs,
            x1_ref, h2_ref):
    i = pl.program_id(0)
    x1, h2 = _o_core(i, oh_ref, os_ref, om_ref, x_ref, wo_ref, gp, gs, lng_ref, lnb_ref, scp, scs, shp, shs)
    x1_ref[...] = x1
    h2_ref[...] = h2.astype(BF16)


def _o_router_body(oh_ref, os_ref, om_ref, x_ref, wo_ref, gp, gs, lng_ref, lnb_ref, scp, scs, shp, shs,
                   wr_ref, br_ref, x1_ref, h2_ref, route_ref):
    i = pl.program_id(0)
    x1, h2 = _o_core(i, oh_ref, os_ref, om_ref, x_ref, wo_ref, gp, gs, lng_ref, lnb_ref, scp, scs, shp, shs)
    x1_ref[...] = x1
    h2_ref[...] = h2
    h_hi = h2.astype(BF16)
    h_lo = (h2 - h_hi.astype(F32)).astype(BF16)
    logits = (jnp.dot(h_hi, wr_ref[0], preferred_element_type=F32)
              + jnp.dot(h_lo, wr_ref[0], preferred_element_type=F32)
              + jnp.dot(h_hi, wr_ref[1], preferred_element_type=F32)) + br_ref[...]
    lane = lax.broadcasted_iota(jnp.int32, (TM, 128), 1).astype(F32)
    neg = -jnp.inf
    lg = jnp.where(lane < NEXP, logits, neg)
    m1 = jnp.max(lg, axis=-1, keepdims=True)
    i1 = jnp.min(jnp.where(lg == m1, lane, 128.0), axis=-1, keepdims=True)
    lg2 = jnp.where(lane == i1, neg, lg)
    m2 = jnp.max(lg2, axis=-1, keepdims=True)
    i2 = jnp.min(jnp.where(lg2 == m2, lane, 128.0), axis=-1, keepdims=True)
    e2 = jnp.exp(m2 - m1)
    den = 1.0 + e2
    route_ref[...] = jnp.where(lane == 0.0, i1, jnp.where(lane == 1.0, i2,
                               jnp.where(lane == 2.0, 1.0 / den, jnp.where(lane == 3.0, e2 / den, 0.0))))


def _stage_o(layer, oh, os_, om, x, wo_b, ada4, ln_g, ln_b, router=None):
    in_specs = [
        _row_spec(HW), _row_spec(S5W), _row_spec(SSW), _row_spec(D), _const_spec((D, D)),
        *_mod_specs(layer, 2), _const_spec((1, D)), _const_spec((1, D)),
        *_mod_specs(layer, 4), *_mod_specs(layer, 3),
    ]
    args = [oh, os_, om, x, wo_b, ada4, ada4, ln_g, ln_b, ada4, ada4, ada4, ada4]
    out_specs = [_row_spec(D), _row_spec(D)]
    out_shape = [jax.ShapeDtypeStruct((ROWS, D), F32), jax.ShapeDtypeStruct((ROWS, D), BF16)]
    body = _o_body
    if router is not None:
        in_specs += [_const_spec((2, D, 128)), _const_spec((1, 128))]
        args += list(router)
        out_specs.append(_row_spec(128))
        out_shape[1] = jax.ShapeDtypeStruct((ROWS, D), F32)
        out_shape.append(jax.ShapeDtypeStruct((ROWS, 128), F32))
        body = _o_router_body
    return pl.pallas_call(
        body, grid=(NT,), in_specs=in_specs, out_specs=out_specs, out_shape=out_shape,
        compiler_params=_cparams(("parallel",)),
        name="stage_o_router" if router is not None else "stage_o",
    )(*args)


def _ffn_body(te_ref, nu_ref, h_ref, wg_ref, wu_ref, wd_ref, *refs):
    o_ref, acc_ref = refs[-2:]
    i = pl.program_id(0)
    j = pl.program_id(1)

    @pl.when(j == 0)
    def _():
        acc_ref[...] = jnp.zeros_like(acc_ref)

    @pl.when(i < nu_ref[0])
    def _():
        h = h_ref[...].astype(BF16)
        g = jnp.dot(h, wg_ref[...], preferred_element_type=F32)
        u = jnp.dot(h, wu_ref[...], preferred_element_type=F32)
        act = (_silu(g) * u).astype(BF16)
        acc_ref[...] += jnp.dot(act, wd_ref[...], preferred_element_type=F32)

    @pl.when(j == pl.num_programs(1) - 1)
    def _():
        o_ref[...] = acc_ref[...]


def _ffn(tile_expert, n_used, h, wg, wu, wd, out_rows=None, tile0=0, earlier=None):
    rows = h.shape[0]
    nj = DFF // TF

    def jblk(i, j, nu):
        return jnp.where(i < nu[0], j, nj - 1)

    in_specs = [
        pl.BlockSpec((TMF, D), lambda i, j, te, nu: (i, 0)),
        pl.BlockSpec((None, D, TF), lambda i, j, te, nu: (te[i], 0, jblk(i, j, nu))),
        pl.BlockSpec((None, D, TF), lambda i, j, te, nu: (te[i], 0, jblk(i, j, nu))),
        pl.BlockSpec((None, TF, D), lambda i, j, te, nu: (te[i], jblk(i, j, nu), 0)),
    ]
    args = [tile_expert, n_used, h, wg, wu, wd]
    aliases = {}
    if earlier is not None:
        in_specs.append(pl.BlockSpec(memory_space=pl.ANY))
        aliases = {len(args): 0}
        args.append(earlier)
    grid_spec = pltpu.PrefetchScalarGridSpec(
        num_scalar_prefetch=2,
        grid=(rows // TMF, nj),
        in_specs=in_specs,
        out_specs=pl.BlockSpec((TMF, D), lambda i, j, te, nu: (i + tile0, 0)),
        scratch_shapes=[pltpu.VMEM((TMF, D), F32)],
    )
    return pl.pallas_call(
        _ffn_body, grid_spec=grid_spec,
        out_shape=jax.ShapeDtypeStruct((rows if out_rows is None else out_rows, D), F32),
        input_output_aliases=aliases,
        compiler_params=_cparams(("parallel", "arbitrary")),
        name="ffn",
    )(*args)


def _final_body(x1_ref, ya_ref, yb_ref, route_ref, gp, gs, lng_ref, lnb_ref, yp_ref, ys_ref):
    i = pl.program_id(0)
    f = route_ref[:, 2:3] * ya_ref[...] + route_ref[:, 3:4] * yb_ref[...]
    y = _layer_norm(ALPHA * x1_ref[...] + _rowmod(i, gp, gs) * f, lng_ref[...], lnb_ref[...])

    @pl.when(i < NPT)
    def _():
        yp_ref[...] = y

    @pl.when(i >= NPT)
    def _():
        ys_ref[...] = y


def _stage_final(layer, x1, ya, yb, route, ada4, ln_g, ln_b):
    return pl.pallas_call(
        _final_body,
        grid=(NT,),
        in_specs=[
            _row_spec(D), _row_spec(D), _row_spec(D), _row_spec(128),
            *_mod_specs(layer, 5), _const_spec((1, D)), _const_spec((1, D)),
        ],
        out_specs=[
            pl.BlockSpec((TM, D), lambda i: (jnp.minimum(i, NPT - 1), 0)),
            pl.BlockSpec((TM, D), lambda i: (jnp.clip(i - NPT, 0, NST - 1), 0)),
        ],
        out_shape=[jax.ShapeDtypeStruct((ROWS_P, D), F32), jax.ShapeDtypeStruct((ROWS_S, D), F32)],
        compiler_params=_cparams(("arbitrary",)),
        name="stage_final",
    )(x1, ya, yb, route, ada4, ada4, ln_g, ln_b)


def _block_ones(n, blk):
    r = np.arange(n) // blk
    return r[:, None] == r[None, :]


def _consts():
    t = np.arange(CB)
    causal = t[:, None] >= t[None, :]
    same_seq = (t[:, None] // L_S) == (t[None, :] // L_S)
    r = np.arange(S5_ROWS)
    perm = r[None, :] == ((r % NB_P) * S5_TB + r // NB_P)[:, None]
    mats = {
        "perm": perm,
        "permt": perm.T,
        "e64": _block_ones(HW, HD),
        "g192": _block_ones(SSW, SSW // 2),
        "trilp": causal,
        "trils": causal & same_seq,
    }
    return {k: jnp.asarray(v.astype(np.float32), dtype=BF16) for k, v in mats.items()}


def _s5_params(a_re, a_im, log_dt, b_re, b_im, c_re, c_im, d, w_glu, b_glu):
    dt = jnp.exp(log_dt)[:, None]
    mag = jnp.exp(a_re * dt)
    lam_re, lam_im = mag * jnp.cos(a_im * dt), mag * jnp.sin(a_im * dt)
    den = a_re * a_re + a_im * a_im
    nr, ni = lam_re - 1.0, lam_im
    zr = (nr * a_re + ni * a_im) / den
    zi = (ni * a_re - nr * a_im) / den
    bbar_re = zr[..., None] * b_re - zi[..., None] * b_im
    bbar_im = zr[..., None] * b_im + zi[..., None] * b_re
    eye = jnp.eye(16, dtype=F32)
    blk = lambda bb: jnp.einsum('gph,gk->ghkp', bb, eye).reshape(S5W, S5N)
    hu, hs = S5W // 2, S5N // 2
    bblk = jnp.stack([jnp.stack([blk(bb)[hu * j:hu * (j + 1), hs * j:hs * (j + 1)] for j in range(2)])
                      for bb in (bbar_re, bbar_im)]).astype(BF16)
    cblk = lambda cc: jnp.einsum('ghp,gk->gpkh', cc, eye).reshape(S5N, S5W)
    ccat = jnp.stack([jnp.stack([cblk(cc)[hs * j:hs * (j + 1), hu * j:hu * (j + 1)] for cc in (c_re, -c_im)])
                      for j in range(2)]).astype(BF16)
    lr, li = lam_re.reshape(-1), lam_im.reshape(-1)
    pows = [(jnp.ones_like(lr), jnp.zeros_like(li))]
    for _ in range(8):
        pr, pi = pows[-1]
        pows.append((pr * lr - pi * li, pr * li + pi * lr))
    rows = jnp.arange(8)[:, None]
    tabs = []
    for dsh in (1, 2, 4):
        pr, pi = pows[dsh]
        tabs.append(jnp.where(rows >= dsh, jnp.concatenate([pr, pi])[None, :], 0.0))
    tabs.append(jnp.stack([jnp.concatenate(pows[r + 1]) for r in range(8)]))
    return {
        "tab": jnp.stack(tabs), "bblk": bblk, "ccat": ccat,
        "lam8": jnp.broadcast_to(jnp.concatenate([lr, li])[None, :], (NB_P, 2 * S5N)),
        "d": d.reshape(1, S5W), "wglu": w_glu.astype(BF16), "bglu": b_glu.reshape(1, S5W),
    }


def _pair_diag(st):
    s6 = st.reshape(NB_P, NH // 2, 2, HD, 2, HD)
    return jnp.stack([s6[:, :, 0, :, 0, :], s6[:, :, 1, :, 1, :]], axis=2).reshape(NB_P, NH, HD, HD)


def _pad_lanes(v, n=128):
    return jnp.pad(v, (0, n - v.shape[0])).reshape(1, n)


def kernel(x_prompt, x_sample, c_prompt, c_sample, state_hgrn, state_s5, state_ssd, state_ssd_conv, w_ada, b_ada, ln_g, ln_b, w_in, w_out, hgrn_lb_logits, hgrn_norm_g, s5_a_re, s5_a_im, s5_log_dt, s5_b_re, s5_b_im, s5_c_re, s5_c_im, s5_d, s5_w_glu, s5_b_glu, ssd_conv_w, ssd_conv_b, ssd_dt_bias, ssd_a_log, ssd_d, ssd_norm_g, ffn_w_gate, ffn_w_up, ffn_w_down, moe_w_router, moe_b_router, moe_w_gate, moe_w_up, moe_w_down):
    consts = _consts()
    c_all = jnp.concatenate([c_sample, c_prompt], axis=0)
    ada4 = _ada(c_all, w_ada, b_ada).reshape(DEPTH, NB_S + NB_P, 1, 6 * D)

    lb_all = jnp.cumsum(jax.nn.softmax(hgrn_lb_logits, axis=0), axis=0)
    lb_all = lb_all - lb_all[0]

    xp = x_prompt.reshape(ROWS_P, D)
    xs = x_sample.reshape(ROWS_S, D)
    new_h, new_s5, new_m, new_c = [], [], [], []
    x1 = f = route = None
    for l in range(DEPTH):
        w_in_b = jnp.pad(w_in[l], ((0, 0), (0, N_IN_PAD - N_IN))).astype(BF16)
        if l == 0:
            x, ph, ps, pm = _stage_a0(xp, xs, ada4, w_in_b)
        else:
            x, ph, ps, pm = _stage_a1(l, x1, f, ada4, ln_g[l - 1, 1].reshape(1, D), ln_b[l - 1, 1].reshape(1, D),
                                      w_in_b)
        oh, hg_p, hg_s = _hgrn(ph, lb_all[l].reshape(1, HW), hgrn_norm_g[l].reshape(1, HW), consts,
                               jnp.swapaxes(state_hgrn[l], -1, -2))
        s5p = _s5_params(s5_a_re[l], s5_a_im[l], s5_log_dt[l], s5_b_re[l], s5_b_im[l], s5_c_re[l], s5_c_im[l],
                         s5_d[l], s5_w_glu[l], s5_b_glu[l])
        os_p, s5_p = _s5_prompt(ps[:ROWS_P].reshape(NB_P, L_P, S5W), s5p, consts)
        os_s, s5_s = _s5_sample(ps[ROWS_P:], s5p, state_s5[l].reshape(NB_S, 1, 2 * S5N))
        os_ = jnp.concatenate([os_p.reshape(ROWS_P, S5W), os_s], axis=0)
        ssd_prm = {
            "cw": ssd_conv_w[l], "cb": ssd_conv_b[l].reshape(1, CONVC),
            "dtb": _pad_lanes(ssd_dt_bias[l]), "aneg": _pad_lanes(-jnp.exp(ssd_a_log[l])),
            "dx": jnp.repeat(ssd_d[l], HD).reshape(1, SSW), "sg": ssd_norm_g[l].reshape(1, SSW),
        }
        hist = jnp.pad(state_ssd_conv[l], ((0, 0), (0, L_S - 3), (0, 0))).reshape(ROWS_S, CONVC)
        om, ss_p, ss_s, tail_p, xbc_s = _ssd(pm, ssd_prm, consts, hist, state_ssd[l])
        conv_p = tail_p[:, 8 - 3:]
        conv_s = xbc_s.reshape(NB_S, L_S, CONVC)[:, L_S - 3:]
        hg_p, ss_p = _pair_diag(hg_p), _pair_diag(ss_p)
        new_h.append((jnp.swapaxes(hg_p, -1, -2), jnp.swapaxes(hg_s, -1, -2)))
        new_s5.append((s5_p.reshape(NB_P, 2, 16, 64), s5_s.reshape(NB_S, 2, 16, 64)))
        new_m.append((ss_p, ss_s))
        new_c.append((conv_p, conv_s))

        wo_b = w_out[l].astype(BF16)
        lg, lbb = ln_g[l, 0].reshape(1, D), ln_b[l, 0].reshape(1, D)
        j = l // 2
        if l % 2 == 0:
            x1, h2 = _stage_o(l, oh, os_, om, x, wo_b, ada4, lg, lbb)
            f = _ffn(jnp.zeros((ROWS // TMF,), jnp.int32), jnp.full((1,), ROWS // TMF, jnp.int32), h2,
                     ffn_w_gate[j:j + 1].astype(BF16),
                     ffn_w_up[j:j + 1].astype(BF16), ffn_w_down[j:j + 1].astype(BF16))
        else:
            wr = jnp.pad(moe_w_router[j], ((0, 0), (0, 128 - NEXP)))
            wr_hi = wr.astype(BF16)
            wr = jnp.stack([wr_hi, (wr - wr_hi.astype(F32)).astype(BF16)])
            br = _pad_lanes(moe_b_router[j])
            x1, h2, route = _stage_o(l, oh, os_, om, x, wo_b, ada4, lg, lbb, router=(wr, br))
            wg_b, wu_b, wd_b = (moe_w_gate[j].astype(BF16), moe_w_up[j].astype(BF16),
                                moe_w_down[j].astype(BF16))
            flat_e = route[:, 0:2].astype(jnp.int32).reshape(-1)
            onehot = (flat_e[:, None] == jnp.arange(NEXP)[None, :]).astype(jnp.int32)
            csum = jnp.cumsum(onehot, axis=0)
            counts = csum[-1]
            rank = jnp.take_along_axis(csum, flat_e[:, None], axis=1)[:, 0] - 1
            padded = ((counts + TMF - 1) // TMF) * TMF
            pend = jnp.cumsum(padded)
            pstart = pend - padded
            dest = pstart[flat_e] + rank
            n_pad = 2 * ROWS + NEXP * TMF
            src_tok = jnp.zeros((n_pad,), jnp.int32).at[dest].set(jnp.arange(2 * ROWS, dtype=jnp.int32) // 2)
            tile_start = jnp.arange(n_pad // TMF, dtype=jnp.int32) * TMF
            tile_e = jnp.minimum(jnp.sum((pend[None, :] <= tile_start[:, None]).astype(jnp.int32), axis=1),
                                 NEXP - 1)
            n_used = (pend[NEXP - 1:NEXP] // TMF).astype(jnp.int32)
            part_tiles = n_pad // TMF // MOE_PARTS
            part_rows = part_tiles * TMF
            y_sorted = None
            for c in range(MOE_PARTS):
                y_sorted = _ffn(tile_e[c * part_tiles:(c + 1) * part_tiles],
                                jnp.clip(n_used - c * part_tiles, 0, part_tiles),
                                h2[src_tok[c * part_rows:(c + 1) * part_rows]], wg_b, wu_b, wd_b,
                                out_rows=n_pad, tile0=c * part_tiles, earlier=y_sorted)
            pos = dest.reshape(ROWS, 2)
            ya = y_sorted[pos[:, 0]]
            yb = y_sorted[pos[:, 1]]
    y_p, y_s = _stage_final(DEPTH - 1, x1, ya, yb, route, ada4, ln_g[DEPTH - 1, 1].reshape(1, D),
                            ln_b[DEPTH - 1, 1].reshape(1, D))
    stack = lambda lst, k: jnp.stack([t[k] for t in lst])
    return (y_p.reshape(NB_P, L_P, D), y_s.reshape(NB_S, L_S, D),
            stack(new_h, 0), stack(new_s5, 0), stack(new_m, 0), stack(new_c, 0),
            stack(new_h, 1), stack(new_s5, 1), stack(new_m, 1), stack(new_c, 1))
```

```python
import functools
import math

import jax
import jax.numpy as jnp
import numpy as np
from jax import lax
from jax.experimental import pallas as pl
from jax.experimental.pallas import tpu as pltpu

F32 = jnp.float32
BF16 = jnp.bfloat16

D = 1024
NB_P, L_P = 8, 2048
NB_S, L_S = 128, 8
ROWS_P = NB_P * L_P
ROWS_S = NB_S * L_S
ROWS = ROWS_P + ROWS_S
DEPTH = 2
HW = 384
S5W = 256
SSW = 384
NH = 6
HD = 64
S5N = 1024
CONVC = 640
N_IN = 2822
N_IN_PAD = 2944
P_HG = 1536
P_SS = 1152
DFF = 2816
NEXP = 8
ALPHA = (2 * DEPTH) ** 0.25
LN_EPS = 1e-5
RMS_EPS = 1e-6

TM = 512
NPT = ROWS_P // TM
NST = ROWS_S // TM
NT = NPT + NST
SEQ_PER_TILE = TM // L_S
TILES_PER_SEQ = L_P // TM

CB = 128
NPC = ROWS_P // CB
NSC = ROWS_S // CB
CH_PER_SEQ = L_P // CB
SEQ_PER_CB = CB // L_S

TMF = 512
TF = 1408
MOE_PARTS = 4
VMEM_LIMIT = 56 * 1024 * 1024


def _cparams(sem):
    return pltpu.CompilerParams(dimension_semantics=sem, vmem_limit_bytes=VMEM_LIMIT)


def _bdot(a, b):
    return jnp.dot(a.astype(BF16), b.astype(BF16), preferred_element_type=F32)


def _bdot_nt(a, b):
    return lax.dot_general(a.astype(BF16), b.astype(BF16), (((1,), (1,)), ((), ())),
                           preferred_element_type=F32)


def _split_dot(x, e, passes):
    acc = None
    r = x
    for _ in range(passes):
        hi = r.astype(BF16)
        d = jnp.dot(hi, e, preferred_element_type=F32)
        acc = d if acc is None else acc + d
        r = r - hi.astype(F32)
    return acc


def _split_dot_l(e, x, passes):
    acc = None
    r = x
    for _ in range(passes):
        hi = r.astype(BF16)
        d = jnp.dot(e, hi, preferred_element_type=F32)
        acc = d if acc is None else acc + d
        r = r - hi.astype(F32)
    return acc


def _silu(x):
    return x * jax.nn.sigmoid(x)


def _layer_norm(x, g, b):
    mu = jnp.mean(x, -1, keepdims=True)
    xc = x - mu
    var = jnp.mean(xc * xc, -1, keepdims=True)
    return xc * lax.rsqrt(var + LN_EPS) * g + b


def _rowmod(i, p_ref, s_ref):
    s = jnp.broadcast_to(s_ref[...], (SEQ_PER_TILE, L_S, D)).reshape(TM, D)
    return jnp.where(i < NPT, p_ref[0], s)


ADA_TN = 1536


def _ada_body(c_ref, w_ref, b_ref, o_ref):
    o_ref[...] = _bdot(_silu(c_ref[...]), w_ref[...]) + b_ref[...]


def _ada(c_all, w_ada, b_ada):
    nc = c_all.shape[0]
    return pl.pallas_call(
        _ada_body,
        grid=(DEPTH, 6 * D // ADA_TN),
        in_specs=[
            pl.BlockSpec((nc, D), lambda l, j: (0, 0)),
            pl.BlockSpec((None, D, ADA_TN), lambda l, j: (l, 0, j)),
            pl.BlockSpec((None, 1, ADA_TN), lambda l, j: (l, 0, j)),
        ],
        out_specs=pl.BlockSpec((None, nc, ADA_TN), lambda l, j: (l, 0, j)),
        out_shape=jax.ShapeDtypeStruct((DEPTH, nc, 6 * D), F32),
        compiler_params=_cparams(("parallel", "parallel")),
        name="ada",
    )(c_all, w_ada, b_ada.reshape(DEPTH, 1, 6 * D))


def _mod_specs(layer, k):
    ps = pl.BlockSpec((None, 1, 1, D),
                      lambda i: (layer, NB_S + jnp.minimum(i // TILES_PER_SEQ, NB_P - 1), 0, k))
    ss = pl.BlockSpec((None, SEQ_PER_TILE, 1, D),
                      lambda i: (layer, jnp.clip(i - NPT, 0, NST - 1), 0, k))
    return [ps, ss]


def _row_spec(width):
    return pl.BlockSpec((TM, width), lambda i: (i, 0))


def _const_spec(shape):
    nd = len(shape)
    return pl.BlockSpec(shape, lambda *_: (0,) * nd)


def _prompt_rows_spec(width):
    return pl.BlockSpec((TM, width), lambda i: (jnp.minimum(i, NPT - 1), 0))


def _sample_rows_spec(width):
    return pl.BlockSpec((TM, width), lambda i: (jnp.clip(i - NPT, 0, NST - 1), 0))


def _proj_out(x, i, scp, scs, shp, shs, w_ref, ph_ref, psp_ref, pss_ref, pm_ref):
    h = x * (1.0 + _rowmod(i, scp, scs)) + _rowmod(i, shp, shs)
    proj = jnp.dot(h.astype(BF16), w_ref[...], preferred_element_type=F32)
    ph_ref[...] = proj[:, 0:P_HG]
    pm_ref[...] = proj[:, P_HG + S5W:N_IN_PAD]

    @pl.when(i < NPT)
    def _():
        psp_ref[...] = proj[:, P_HG:P_HG + S5W]

    @pl.when(i >= NPT)
    def _():
        pss_ref[...] = proj[:, P_HG:P_HG + S5W]


def _a0_body(xp_ref, xs_ref, scp, scs, shp, shs, w_ref, ph_ref, psp_ref, pss_ref, pm_ref):
    i = pl.program_id(0)
    x = jnp.where(i < NPT, xp_ref[...], xs_ref[...])
    _proj_out(x, i, scp, scs, shp, shs, w_ref, ph_ref, psp_ref, pss_ref, pm_ref)


def _a1_body(x1_ref, f_ref, gp, gs, lng_ref, lnb_ref, scp, scs, shp, shs, w_ref,
             x_ref, ph_ref, psp_ref, pss_ref, pm_ref):
    i = pl.program_id(0)
    x = _layer_norm(ALPHA * x1_ref[...] + _rowmod(i, gp, gs) * f_ref[...], lng_ref[...], lnb_ref[...])
    x_ref[...] = x
    _proj_out(x, i, scp, scs, shp, shs, w_ref, ph_ref, psp_ref, pss_ref, pm_ref)


def _a_out():
    specs = [_row_spec(P_HG), _prompt_rows_spec(S5W), _sample_rows_spec(S5W), _row_spec(P_SS)]
    shapes = [jax.ShapeDtypeStruct((ROWS, P_HG), F32), jax.ShapeDtypeStruct((ROWS_P, S5W), F32),
              jax.ShapeDtypeStruct((ROWS_S, S5W), F32), jax.ShapeDtypeStruct((ROWS, P_SS), F32)]
    return specs, shapes


def _stage_a0(xp, xs, ada4, w_in_b):
    out_specs, out_shape = _a_out()
    return pl.pallas_call(
        _a0_body,
        grid=(NT,),
        in_specs=[
            _prompt_rows_spec(D), _sample_rows_spec(D),
            *_mod_specs(0, 1), *_mod_specs(0, 0),
            _const_spec((D, N_IN_PAD)),
        ],
        out_specs=out_specs, out_shape=out_shape,
        compiler_params=_cparams(("arbitrary",)),
        name="stage_a0",
    )(xp, xs, ada4, ada4, ada4, ada4, w_in_b)


def _stage_a1(layer, x1, f, ada4, ln_g, ln_b, w_in_b):
    out_specs, out_shape = _a_out()
    return pl.pallas_call(
        _a1_body,
        grid=(NT,),
        in_specs=[
            _row_spec(D), _row_spec(D),
            *_mod_specs(layer - 1, 5),
            _const_spec((1, D)), _const_spec((1, D)),
            *_mod_specs(layer, 1), *_mod_specs(layer, 0),
            _const_spec((D, N_IN_PAD)),
        ],
        out_specs=[_row_spec(D)] + out_specs, out_shape=[jax.ShapeDtypeStruct((ROWS, D), F32)] + out_shape,
        compiler_params=_cparams(("arbitrary",)),
        name="stage_a1",
    )(x1, f, ada4, ada4, ln_g, ln_b, ada4, ada4, ada4, ada4, w_in_b)


def _mixer_blk(width, nsub=1):
    return pl.BlockSpec((nsub * CB, width), lambda i: (i, 0))


def _pstate_spec(shape, nsub=1):
    nd = len(shape)
    return pl.BlockSpec((1,) + shape,
                        lambda i: (jnp.minimum(i // (CH_PER_SEQ // nsub), NB_P - 1),) + (0,) * nd)


def _sstate_spec(shape, nsub=1):
    nd = len(shape)
    return pl.BlockSpec((nsub * SEQ_PER_CB,) + shape,
                        lambda i: (jnp.clip(i - NPC // nsub, 0, NSC // nsub - 1),) + (0,) * nd)


def _cumsum_rows(x, span):
    r = lax.broadcasted_iota(jnp.int32, x.shape, 0) & (span - 1)
    d = 1
    while d < span:
        x = x + jnp.where(r >= d, pltpu.roll(x, d, 0), 0.0)
        d *= 2
    return x


def _seq_last_rows(x):
    w = x.shape[-1]
    x3 = x.reshape(SEQ_PER_CB, L_S, w)
    return jnp.broadcast_to(x3[:, L_S - 1:L_S, :], (SEQ_PER_CB, L_S, w)).reshape(CB, w)


def _concat_heads(parts):
    return jnp.concatenate(parts, axis=1)


def _stack_select(shape, row_div, lane_div):
    r = lax.broadcasted_iota(jnp.int32, shape, 0) // row_div
    c = lax.broadcasted_iota(jnp.int32, shape, 1) // lane_div
    return r == c


def _seq_expand_lanes(qh):
    q2 = jnp.concatenate([qh, qh], axis=1)
    q16 = jnp.concatenate([q2] * (SEQ_PER_CB // 2), axis=1)
    return jnp.where(_stack_select((CB, SEQ_PER_CB * HD), L_S, HD), q16, 0.0)


def _seq_expand_rows(xt):
    t = jnp.broadcast_to(xt[None], (SEQ_PER_CB, HD, CB)).reshape(SEQ_PER_CB * HD, CB)
    return jnp.where(_stack_select((SEQ_PER_CB * HD, CB), HD, L_S), t, 0.0)


def _fold_seq_lanes(full):
    acc = full[:, 0:128]
    for j in range(1, SEQ_PER_CB * HD // 128):
        acc = acc + full[:, 128 * j:128 * (j + 1)]
    return acc[:, 0:HD] + acc[:, HD:2 * HD]


HGRN_BASE = 32
EXP_RANGE_MAX = 80.0


def _hgrn_block(prompt, p_ref, lb_ref, hg_ref, e64_ref, oh_ref, st_scr, o_scr, sub, is_last=None,
                stp_out=None, sts_in=None, sts_out=None):
    rows = slice(sub * CB, (sub + 1) * CB)
    seqs = slice(sub * SEQ_PER_CB, (sub + 1) * SEQ_PER_CB)
    lb = lb_ref[...]
    qr = p_ref[rows, 0:HW]
    fr = p_ref[rows, HW:2 * HW]
    v = p_ref[rows, 2 * HW:3 * HW]
    gr = p_ref[rows, 3 * HW:4 * HW]
    e = jnp.exp(-jnp.abs(fr))
    ope = 1.0 + e
    ls = jnp.minimum(fr, 0.0) - jnp.log(ope)
    a = jnp.log(lb)
    bb = jnp.log1p(-lb) + ls
    lf = jnp.maximum(a, bb) + jnp.log(1.0 + jnp.exp(-jnp.abs(a - bb)))
    rcp = 1.0 / ope
    kk = (1.0 - lb) * jnp.where(fr >= 0.0, e * rcp, rcp)
    q = _silu(qr)
    b = _cumsum_rows(lf, CB if prompt else L_S)

    e64 = e64_ref[...]
    ti = lax.broadcasted_iota(jnp.int32, (CB, CB), 0)
    si = lax.broadcasted_iota(jnp.int32, (CB, CB), 1)

    def diag8():
        nsub = CB // 8
        b3 = b.reshape(nsub, 8, HW)
        q3 = q.reshape(nsub, 8, HW)
        k3 = kk.reshape(nsub, 8, HW)
        v3 = v.reshape(nsub, 8, HW)
        r3 = lax.broadcasted_iota(jnp.int32, (nsub, 8, HW), 1)
        o = jnp.zeros((CB, HW), F32)
        for s in range(8):
            dlt = jnp.minimum(b3 - b3[:, s:s + 1, :], 0.0)
            w = jnp.where(r3 >= s, jnp.exp(dlt), 0.0) * q3 * k3[:, s:s + 1, :]
            hsum = jnp.dot(w.reshape(CB, HW).astype(BF16), e64, preferred_element_type=F32)
            o = o + hsum * jnp.broadcast_to(v3[:, s:s + 1, :], (nsub, 8, HW)).reshape(CB, HW)
        return o

    def level_terms(m):
        terms = []
        while m < CB:
            nb = CB // (2 * m)
            b4 = b.reshape(nb, 2 * m, HW)
            bmid = b4[:, m - 1:m, :]
            pos = lax.broadcasted_iota(jnp.int32, (nb, 2 * m, HW), 1)
            qq = jnp.where(pos >= m, q.reshape(nb, 2 * m, HW) * jnp.exp(jnp.minimum(b4 - bmid, 0.0)), 0.0)
            kq = jnp.where(pos < m, kk.reshape(nb, 2 * m, HW) * jnp.exp(jnp.minimum(bmid - b4, 0.0)), 0.0)
            terms.append((qq.reshape(CB, HW), kq.reshape(CB, HW), (ti // (2 * m)) == (si // (2 * m))))
            m *= 2
        return terms

    def scores_times_v(terms):
        lane = lax.broadcasted_iota(jnp.int32, (CB, 2 * HD), 1)
        parts = []
        for j in range(NH // 2):
            scs = []
            for h in (2 * j, 2 * j + 1):
                sl = slice(HD * h, HD * (h + 1))
                sc = None
                for qq, kq, keep in terms:
                    t = jnp.where(keep, _bdot_nt(qq[:, sl], kq[:, sl]), 0.0)
                    sc = t if sc is None else sc + t
                scs.append(sc)
            vp = v[:, 2 * HD * j:2 * HD * (j + 1)]
            vdiag = jnp.concatenate([jnp.where(lane < HD, vp, 0.0), jnp.where(lane >= HD, vp, 0.0)], axis=0)
            parts.append(_bdot(jnp.concatenate(scs, axis=1), vdiag))
        return _concat_heads(parts)

    base = HGRN_BASE if prompt else L_S
    nbase = CB // base
    bb3 = b.reshape(nbase, base, HW)
    top = bb3[:, 0:1, :] - lf.reshape(nbase, base, HW)[:, 0:1, :]
    decay_range = jnp.max(top - bb3[:, base - 1:base, :])
    in_range = decay_range <= EXP_RANGE_MAX

    def exact_path():
        o_scr[rows, :] = diag8() + (scores_times_v(level_terms(8)) if prompt else 0.0)

    def finish(fast):
        _hgrn_finish(prompt, fast, rows, seqs, q, kk, v, gr, b, bb3, top, ti, si, scores_times_v, level_terms,
                     hg_ref, e64, oh_ref, st_scr, o_scr, is_last, stp_out, sts_in, sts_out)

    return in_range, exact_path, finish


def _hgrn_finish(prompt, fast, rows, seqs, q, kk, v, gr, b, bb3, top, ti, si, scores_times_v, level_terms,
                 hg_ref, e64, oh_ref, st_scr, o_scr, is_last, stp_out, sts_in, sts_out):
    base = HGRN_BASE if prompt else L_S
    nbase = CB // base
    qf = (q.reshape(nbase, base, HW) * jnp.exp(bb3 - top)).reshape(CB, HW)
    kf = (kk.reshape(nbase, base, HW) * jnp.exp(top - bb3)).reshape(CB, HW)
    keep = ((ti // base) == (si // base)) & (si <= ti)
    o_fast = scores_times_v([(qf, kf, keep)] + (level_terms(base) if prompt else []))
    o = jnp.where(fast, o_fast, o_scr[rows, :])
    qt = q * jnp.exp(b)
    if prompt:
        blast = b[CB - 1:CB, :]
        kd = kk * jnp.exp(blast - b)
        vt = v.T
        same_head = (ti // HD) == (si // HD)
        parts = []
        for j in range(NH // 2):
            pr = slice(2 * HD * j, 2 * HD * (j + 1))
            st = st_scr[j]
            parts.append(o[:, pr] + _bdot_nt(qt[:, pr], st))
            st_scr[j] = st * jnp.exp(blast[:, pr]) + jnp.where(same_head, _bdot(vt[pr, :], kd[:, pr]), 0.0)

        if is_last is not None:
            @pl.when(is_last)
            def _():
                stp_out[0] = st_scr[...]
    else:
        blast = _seq_last_rows(b)
        kd = kk * jnp.exp(blast - b)
        dec = jnp.exp(blast)
        vt = v.T
        parts = []
        for h in range(NH):
            sl = slice(HD * h, HD * (h + 1))
            sts = sts_in[seqs, h].reshape(SEQ_PER_CB * HD, HD)
            full = _bdot_nt(qt[:, sl], sts)
            sel = jnp.where(_stack_select((CB, SEQ_PER_CB * HD), L_S, HD), full, 0.0)
            parts.append(o[:, sl] + _fold_seq_lanes(sel))
            dec3 = dec[:, sl].reshape(SEQ_PER_CB, L_S, HD)[:, L_S - 1:L_S, :]
            dec_rows = jnp.broadcast_to(dec3, (SEQ_PER_CB, HD, HD)).reshape(SEQ_PER_CB * HD, HD)
            upd = _bdot(_seq_expand_rows(vt[sl, :]), kd[:, sl])
            sts_out[seqs, h] = (sts * dec_rows + upd).reshape(SEQ_PER_CB, HD, HD)
    oall = _concat_heads(parts)
    ms = _split_dot(oall * oall, e64, 1) * (1.0 / HD)
    oh_ref[rows, :] = oall * lax.rsqrt(ms + RMS_EPS) * hg_ref[...] * _silu(gr)


HGRN_SUB = 2


def _hgrn_body(p_ref, lb_ref, hg_ref, e64_ref, sts_in,
               oh_ref, stp_out, sts_out, st_scr, o_scr):
    i = pl.program_id(0)
    npc, steps_per_seq = NPC // HGRN_SUB, CH_PER_SEQ // HGRN_SUB

    @pl.when(i == 0)
    def _():
        o_scr[...] = jnp.zeros_like(o_scr)

    @pl.when((i < npc) & (i % steps_per_seq == 0))
    def _():
        st_scr[...] = jnp.zeros_like(st_scr)

    def run(blocks):
        fast = blocks[0][0]
        for in_range, _, _ in blocks[1:]:
            fast = jnp.logical_and(fast, in_range)

        @pl.when(jnp.logical_not(fast))
        def _():
            for _, exact_path, _ in blocks:
                exact_path()

        for _, _, finish in blocks:
            finish(fast)

    @pl.when(i < npc)
    def _():
        run([_hgrn_block(True, p_ref, lb_ref, hg_ref, e64_ref, oh_ref, st_scr, o_scr, sub,
                         is_last=(i % steps_per_seq == steps_per_seq - 1) if sub == HGRN_SUB - 1 else None,
                         stp_out=stp_out) for sub in range(HGRN_SUB)])

    @pl.when(i >= npc)
    def _():
        run([_hgrn_block(False, p_ref, lb_ref, hg_ref, e64_ref, oh_ref, st_scr, o_scr, sub,
                         sts_in=sts_in, sts_out=sts_out) for sub in range(HGRN_SUB)])


def _hgrn(ph, lb, hg, consts, st_t):
    return pl.pallas_call(
        _hgrn_body,
        grid=((NPC + NSC) // HGRN_SUB,),
        in_specs=[
            _mixer_blk(P_HG, HGRN_SUB), _const_spec((1, HW)), _const_spec((1, HW)),
            _const_spec((HW, HW)),
            _sstate_spec((NH, HD, HD), HGRN_SUB),
        ],
        out_specs=[_mixer_blk(HW, HGRN_SUB), _pstate_spec((NH // 2, 2 * HD, 2 * HD), HGRN_SUB),
                   _sstate_spec((NH, HD, HD), HGRN_SUB)],
        out_shape=[
            jax.ShapeDtypeStruct((ROWS, HW), F32),
            jax.ShapeDtypeStruct((NB_P, NH // 2, 2 * HD, 2 * HD), F32),
            jax.ShapeDtypeStruct((NB_S, NH, HD, HD), F32),
        ],
        scratch_shapes=[pltpu.VMEM((NH // 2, 2 * HD, 2 * HD), F32), pltpu.VMEM((HGRN_SUB * CB, HW), F32)],
        compiler_params=_cparams(("arbitrary",)),
        name="hgrn",
    )(ph, lb, hg, consts["e64"], st_t)


def _cmul_add(hr, hi, lr, li, sr, si):
    return hr + lr * sr - li * si, hi + lr * si + li * sr


def _s5_project(ub, bblk_ref):
    halves = [ub[:, (S5W // 2) * j:(S5W // 2) * (j + 1)] for j in range(2)]
    hr = jnp.concatenate([jnp.dot(halves[j], bblk_ref[0, j], preferred_element_type=F32) for j in range(2)], axis=1)
    hi = jnp.concatenate([jnp.dot(halves[j], bblk_ref[1, j], preferred_element_type=F32) for j in range(2)], axis=1)
    return hr, hi


def _s5_readout(h_scr, u, ccat_ref, d_ref, wglu_ref, bglu_ref):
    hs = S5N // 2
    ch = [_bdot(h_scr[:, hs * j:hs * (j + 1)], ccat_ref[j, 0])
          + _bdot(h_scr[:, S5N + hs * j:S5N + hs * (j + 1)], ccat_ref[j, 1]) for j in range(2)]
    y = jnp.concatenate(ch, axis=1) + d_ref[...] * u
    c0 = math.sqrt(2.0 / math.pi)
    y = y * (0.5 * (1.0 + jnp.tanh(c0 * (y + 0.044715 * (y * y * y)))))
    return y * jax.nn.sigmoid(_bdot(y, wglu_ref[...]) + bglu_ref[...])


S5_TB = 32
S5_ROWS = NB_P * S5_TB


def _s5_prompt_body(p_ref, perm_ref, permt_ref, lam_ref, bblk_ref, ccat_ref, d_ref, wglu_ref, bglu_ref,
                    os_ref, st_out, carry_scr, h_scr):
    i = pl.program_id(0)

    @pl.when(i == 0)
    def _():
        carry_scr[...] = jnp.zeros_like(carry_scr)

    u = p_ref[...].reshape(S5_ROWS, S5W)
    u_hi = u.astype(BF16)
    u_lo = (u - u_hi.astype(F32)).astype(BF16)
    perm = perm_ref[...]
    up_hi = jnp.dot(perm, u_hi, preferred_element_type=F32)
    up = up_hi + jnp.dot(perm, u_lo, preferred_element_type=F32)
    hr, hi = _s5_project(up_hi.astype(BF16), bblk_ref)
    lr = lam_ref[:, 0:S5N]
    li = lam_ref[:, S5N:2 * S5N]
    cr = carry_scr[:, 0:S5N]
    ci = carry_scr[:, S5N:2 * S5N]
    for t in range(S5_TB):
        rows = slice(NB_P * t, NB_P * (t + 1))
        cr, ci = _cmul_add(hr[rows], hi[rows], lr, li, cr, ci)
        h_scr[rows, 0:S5N] = cr
        h_scr[rows, S5N:2 * S5N] = ci
    carry_scr[:, 0:S5N] = cr
    carry_scr[:, S5N:2 * S5N] = ci
    out = _s5_readout(h_scr, up, ccat_ref, d_ref, wglu_ref, bglu_ref)
    os_ref[...] = jnp.dot(permt_ref[...], out.astype(BF16), preferred_element_type=F32).reshape(NB_P, S5_TB, S5W)

    @pl.when(i == pl.num_programs(0) - 1)
    def _():
        st_out[...] = carry_scr[...]


def _s5_prompt(ps3, prm, consts):
    return pl.pallas_call(
        _s5_prompt_body,
        grid=(L_P // S5_TB,),
        in_specs=[
            pl.BlockSpec((NB_P, S5_TB, S5W), lambda i: (0, i, 0)),
            _const_spec((S5_ROWS, S5_ROWS)), _const_spec((S5_ROWS, S5_ROWS)), _const_spec((NB_P, 2 * S5N)),
            _const_spec((2, 2, S5W // 2, S5N // 2)), _const_spec((2, 2, S5N // 2, S5W // 2)),
            _const_spec((1, S5W)), _const_spec((S5W, S5W)), _const_spec((1, S5W)),
        ],
        out_specs=[pl.BlockSpec((NB_P, S5_TB, S5W), lambda i: (0, i, 0)), _const_spec((NB_P, 2 * S5N))],
        out_shape=[jax.ShapeDtypeStruct((NB_P, L_P, S5W), F32), jax.ShapeDtypeStruct((NB_P, 2 * S5N), F32)],
        scratch_shapes=[pltpu.VMEM((NB_P, 2 * S5N), F32), pltpu.VMEM((S5_ROWS, 2 * S5N), F32)],
        compiler_params=_cparams(("arbitrary",)),
        name="s5_prompt",
    )(ps3, consts["perm"], consts["permt"], prm["lam8"], prm["bblk"], prm["ccat"], prm["d"], prm["wglu"],
      prm["bglu"])


def _s5_sample_body(p_ref, tab_ref, bblk_ref, ccat_ref, d_ref, wglu_ref, bglu_ref, s5s_in,
                    os_ref, s5s_out, h_scr):
    u = p_ref[...]
    hr, hi = _s5_project(u.astype(BF16), bblk_ref)
    nsub = CB // 8
    for idx, dsh in enumerate((1, 2, 4)):
        sr = pltpu.roll(hr, dsh, 0).reshape(nsub, 8, S5N)
        si = pltpu.roll(hi, dsh, 0).reshape(nsub, 8, S5N)
        lr = tab_ref[idx, :, 0:S5N][None]
        li = tab_ref[idx, :, S5N:2 * S5N][None]
        nr, ni = _cmul_add(hr.reshape(nsub, 8, S5N), hi.reshape(nsub, 8, S5N), lr, li, sr, si)
        hr = nr.reshape(CB, S5N)
        hi = ni.reshape(CB, S5N)
    tcr = tab_ref[3, :, 0:S5N]
    tci = tab_ref[3, :, S5N:2 * S5N]
    cr = s5s_in[:, :, 0:S5N]
    ci = s5s_in[:, :, S5N:2 * S5N]
    tr, tim = _cmul_add(hr.reshape(nsub, 8, S5N), hi.reshape(nsub, 8, S5N), tcr[None], tci[None], cr, ci)
    h_scr[:, 0:S5N] = tr.reshape(CB, S5N)
    h_scr[:, S5N:2 * S5N] = tim.reshape(CB, S5N)
    sb = lax.broadcasted_iota(jnp.int32, (SEQ_PER_CB, CB), 0)
    st = lax.broadcasted_iota(jnp.int32, (SEQ_PER_CB, CB), 1)
    sel = (st == L_S * sb + (L_S - 1)).astype(BF16)
    s5s_out[...] = _split_dot_l(sel, h_scr[...], 3)
    os_ref[...] = _s5_readout(h_scr, u, ccat_ref, d_ref, wglu_ref, bglu_ref)


def _s5_sample(ps_s, prm, st):
    seqs = lambda shape: pl.BlockSpec((SEQ_PER_CB,) + shape, lambda i: (i,) + (0,) * len(shape))
    return pl.pallas_call(
        _s5_sample_body,
        grid=(NSC,),
        in_specs=[
            _mixer_blk(S5W), _const_spec((4, 8, 2 * S5N)), _const_spec((2, 2, S5W // 2, S5N // 2)),
            _const_spec((2, 2, S5N // 2, S5W // 2)), _const_spec((1, S5W)), _const_spec((S5W, S5W)),
            _const_spec((1, S5W)), seqs((1, 2 * S5N)),
        ],
        out_specs=[_mixer_blk(S5W), seqs((2 * S5N,))],
        out_shape=[jax.ShapeDtypeStruct((ROWS_S, S5W), F32), jax.ShapeDtypeStruct((NB_S, 2 * S5N), F32)],
        scratch_shapes=[pltpu.VMEM((CB, 2 * S5N), F32)],
        compiler_params=_cparams(("parallel",)),
        name="s5_sample",
    )(ps_s, prm["tab"], prm["bblk"], prm["ccat"], prm["d"], prm["wglu"], prm["bglu"], st)


def _ssd_block(prompt, p_ref, cw_ref, cb_ref, dtb_ref, aneg_ref, dx_ref, sg_ref,
               g192_ref, tril_ref, om_ref, cbuf, st_scr, sub, is_last=None, ssp_out=None, convp_out=None,
               hist_ref=None, sss_in=None, sss_out=None, convs_out=None):
    rows = slice(sub * CB, (sub + 1) * CB)
    seqs = slice(sub * SEQ_PER_CB, (sub + 1) * SEQ_PER_CB)
    z = p_ref[rows, 0:SSW]
    xbc = p_ref[rows, SSW:SSW + CONVC]
    dtr = p_ref[rows, SSW + CONVC:P_SS]
    acc = cb_ref[...] + cw_ref[3:4, :] * xbc
    if prompt:
        cbuf[8:8 + CB, :] = xbc
        for k in (1, 2, 3):
            acc = acc + cw_ref[3 - k:4 - k, :] * cbuf[8 - k:8 - k + CB, :]
        cbuf[0:8, :] = cbuf[CB:CB + 8, :]
    else:
        convs_out[rows, :] = xbc
        tl = lax.broadcasted_iota(jnp.int32, (CB, CONVC), 0) % L_S
        hist = hist_ref[rows, :]
        for k in (1, 2, 3):
            hk = hist if k == 3 else pltpu.roll(hist, CB - (3 - k), 0)
            sh = jnp.where(tl >= k, pltpu.roll(xbc, k, 0), hk)
            acc = acc + cw_ref[3 - k:4 - k, :] * sh
    xc = _silu(acc)
    xs = xc[:, 0:SSW]
    bm = xc[:, SSW:SSW + 2 * HD]
    cm = xc[:, SSW + 2 * HD:CONVC]
    xdt = dtr + dtb_ref[...]
    dt = jnp.maximum(xdt, 0.0) + jnp.log1p(jnp.exp(-jnp.abs(xdt)))
    la = dt * aneg_ref[...]
    tril = tril_ref[...]
    b6 = _cumsum_rows(la, CB if prompt else L_S)
    lane = lax.broadcasted_iota(jnp.int32, (CB, 128), 1)
    bxw = [jnp.broadcast_to(b6[:, h:h + 1], (CB, 128)) for h in range(NH)]
    dtw = [jnp.broadcast_to(dt[:, h:h + 1], (CB, 128)) for h in range(NH)]
    pair = lambda cols: _concat_heads([jnp.where(lane < HD, cols[2 * j], cols[2 * j + 1]) for j in range(NH // 2)])
    bx = pair(bxw)
    dtx = pair(dtw)
    bm_rep = _concat_heads([bm[:, 0:HD]] * 3 + [bm[:, HD:2 * HD]] * 3)
    cm_rep = _concat_heads([cm[:, 0:HD]] * 3 + [cm[:, HD:2 * HD]] * 3)
    kh = bm_rep * dtx
    qt = cm_rep * jnp.exp(bx)
    blast = bx[CB - 1:CB, :] if prompt else _seq_last_rows(bx)
    kd = kh * jnp.exp(blast - bx)
    kdt = kd.T
    mask = tril > 0

    def scores(h):
        sl = slice(HD * h, HD * (h + 1))
        bcol = bxw[h]
        decay = jnp.exp(jnp.where(mask, bcol - bcol.T, -1e30))
        return _bdot_nt(cm_rep[:, sl], kh[:, sl]) * decay

    parts = []
    if prompt:
        ri = lax.broadcasted_iota(jnp.int32, (2 * HD, 2 * HD), 0)
        ci = lax.broadcasted_iota(jnp.int32, (2 * HD, 2 * HD), 1)
        same_head = (ri // HD) == (ci // HD)
        for j in range(NH // 2):
            pr = slice(2 * HD * j, 2 * HD * (j + 1))
            xp = xs[:, pr]
            xdiag = jnp.concatenate([jnp.where(lane < HD, xp, 0.0), jnp.where(lane >= HD, xp, 0.0)], axis=0)
            st = st_scr[j]
            parts.append(_bdot(jnp.concatenate([scores(2 * j), scores(2 * j + 1)], axis=1), xdiag)
                         + _bdot(qt[:, pr], st))
            st_scr[j] = st * jnp.exp(blast[:, pr]) + jnp.where(same_head, _bdot(kdt[pr, :], xp), 0.0)
    else:
        for h in range(NH):
            sl = slice(HD * h, HD * (h + 1))
            sts = sss_in[seqs, h].reshape(SEQ_PER_CB * HD, HD)
            parts.append(_bdot(scores(h), xs[:, sl]) + _bdot(_seq_expand_lanes(qt[:, sl]), sts))
            dec3 = jnp.exp(blast[:, sl]).reshape(SEQ_PER_CB, L_S, HD)[:, L_S - 1:L_S, :]
            dec_rows = jnp.broadcast_to(dec3, (SEQ_PER_CB, HD, HD)).reshape(SEQ_PER_CB * HD, HD)
            upd = _bdot(_seq_expand_rows(kdt[sl, :]), xs[:, sl])
            sss_out[seqs, h] = (sts * dec_rows + upd).reshape(SEQ_PER_CB, HD, HD)
    if is_last is not None:
        @pl.when(is_last)
        def _():
            ssp_out[0] = st_scr[...]
            convp_out[0] = cbuf[0:8, :]
    y = (_concat_heads(parts) + dx_ref[...] * xs) * _silu(z)
    ms = _split_dot(y * y, g192_ref[...], 1) * (1.0 / (SSW // 2))
    om_ref[rows, :] = y * lax.rsqrt(ms + RMS_EPS) * sg_ref[...]


def _ssd_body(p_ref, cw_ref, cb_ref, dtb_ref, aneg_ref, dx_ref, sg_ref, g192_ref,
              trilp_ref, trils_ref, hist_ref, sss_in, *refs):
    om_ref, ssp_out, sss_out, convp_out, convs_out, cbuf, st_scr = refs[-7:]
    i = pl.program_id(0)
    common = (p_ref, cw_ref, cb_ref, dtb_ref, aneg_ref, dx_ref, sg_ref, g192_ref)
    npc, steps_per_seq = NPC // SSD_SUB, CH_PER_SEQ // SSD_SUB

    @pl.when((i < npc) & (i % steps_per_seq == 0))
    def _():
        st_scr[...] = jnp.zeros_like(st_scr)
        cbuf[...] = jnp.zeros_like(cbuf)

    @pl.when(i < npc)
    def _():
        for sub in range(SSD_SUB):
            is_last = (i % steps_per_seq == steps_per_seq - 1) if sub == SSD_SUB - 1 else None
            _ssd_block(True, *common, trilp_ref, om_ref, cbuf, st_scr, sub, is_last=is_last,
                       ssp_out=ssp_out, convp_out=convp_out)

    @pl.when(i >= npc)
    def _():
        for sub in range(SSD_SUB):
            _ssd_block(False, *common, trils_ref, om_ref, cbuf, st_scr, sub, hist_ref=hist_ref,
                       sss_in=sss_in, sss_out=sss_out, convs_out=convs_out)


SSD_SUB = 2


def _ssd(pm, prm, consts, hist, st, layer, earlier):
    sample_rows = pl.BlockSpec((SSD_SUB * CB, CONVC),
                               lambda i: (jnp.clip(i - NPC // SSD_SUB, 0, NSC // SSD_SUB - 1), 0))
    in_specs = [
        _mixer_blk(P_SS, SSD_SUB), _const_spec((4, CONVC)), _const_spec((1, CONVC)),
        _const_spec((1, 128)), _const_spec((1, 128)), _const_spec((1, SSW)), _const_spec((1, SSW)),
        _const_spec((SSW, SSW)),
        _const_spec((CB, CB)), _const_spec((CB, CB)),
        sample_rows,
        _sstate_spec((NH, HD, HD), SSD_SUB),
    ]
    args = [pm, prm["cw"], prm["cb"], prm["dtb"], prm["aneg"], prm["dx"], prm["sg"],
            consts["g192"], consts["trilp"], consts["trils"], hist, st]
    aliases = {}
    if earlier is not None:
        in_specs.append(pl.BlockSpec(memory_space=pl.ANY))
        aliases = {len(args): 2}
        args.append(earlier)
    layer_slab = pl.BlockSpec((None, SSD_SUB * SEQ_PER_CB, NH, HD, HD),
                              lambda i: (layer, jnp.clip(i - NPC // SSD_SUB, 0, NSC // SSD_SUB - 1), 0, 0, 0))
    return pl.pallas_call(
        _ssd_body,
        grid=((NPC + NSC) // SSD_SUB,),
        in_specs=in_specs,
        out_specs=[_mixer_blk(SSW, SSD_SUB), _pstate_spec((NH // 2, 2 * HD, 2 * HD), SSD_SUB),
                   layer_slab, _pstate_spec((8, CONVC), SSD_SUB), sample_rows],
        out_shape=[
            jax.ShapeDtypeStruct((ROWS, SSW), F32),
            jax.ShapeDtypeStruct((NB_P, NH // 2, 2 * HD, 2 * HD), F32),
            jax.ShapeDtypeStruct((DEPTH, NB_S, NH, HD, HD), F32),
            jax.ShapeDtypeStruct((NB_P, 8, CONVC), F32),
            jax.ShapeDtypeStruct((ROWS_S, CONVC), F32),
        ],
        scratch_shapes=[pltpu.VMEM((CB + 8, CONVC), F32), pltpu.VMEM((NH // 2, 2 * HD, 2 * HD), F32)],
        input_output_aliases=aliases,
        compiler_params=_cparams(("arbitrary",)),
        name="ssd",
    )(*args)


def _o_core(split_x, i, oh_ref, osp_ref, oss_ref, om_ref, *refs):
    if split_x:
        x = jnp.where(i < NPT, refs[0][...], refs[1][...])
        refs = refs[2:]
    else:
        x = refs[0][...]
        refs = refs[1:]
    wo_ref, gp, gs, lng_ref, lnb_ref, scp, scs, shp, shs = refs[:9]
    os_ = jnp.where(i < NPT, osp_ref[...], oss_ref[...])
    mix = (_bdot(oh_ref[...], wo_ref[0:HW, :]) + _bdot(os_, wo_ref[HW:HW + S5W, :])
           + _bdot(om_ref[...], wo_ref[HW + S5W:D, :]))
    x1 = _layer_norm(ALPHA * x + _rowmod(i, gp, gs) * mix, lng_ref[...], lnb_ref[...])
    h2 = x1 * (1.0 + _rowmod(i, scp, scs)) + _rowmod(i, shp, shs)
    return x1, h2, refs[9:]


def _o_body(split_x, *refs):
    i = pl.program_id(0)
    x1, h2, (x1_ref, h2_ref) = _o_core(split_x, i, *refs)
    x1_ref[...] = x1
    h2_ref[...] = h2.astype(BF16)


def _o_router_body(split_x, *refs):
    i = pl.program_id(0)
    x1, h2, (wr_ref, br_ref, x1_ref, h2_ref, route_ref) = _o_core(split_x, i, *refs)
    x1_ref[...] = x1
    h2_ref[...] = h2
    h_hi = h2.astype(BF16)
    h_lo = (h2 - h_hi.astype(F32)).astype(BF16)
    logits = (jnp.dot(h_hi, wr_ref[0], preferred_element_type=F32)
              + jnp.dot(h_lo, wr_ref[0], preferred_element_type=F32)
              + jnp.dot(h_hi, wr_ref[1], preferred_element_type=F32)) + br_ref[...]
    lane = lax.broadcasted_iota(jnp.int32, (TM, 128), 1).astype(F32)
    neg = -jnp.inf
    lg = jnp.where(lane < NEXP, logits, neg)
    m1 = jnp.max(lg, axis=-1, keepdims=True)
    i1 = jnp.min(jnp.where(lg == m1, lane, 128.0), axis=-1, keepdims=True)
    lg2 = jnp.where(lane == i1, neg, lg)
    m2 = jnp.max(lg2, axis=-1, keepdims=True)
    i2 = jnp.min(jnp.where(lg2 == m2, lane, 128.0), axis=-1, keepdims=True)
    e2 = jnp.exp(m2 - m1)
    den = 1.0 + e2
    route_ref[...] = jnp.where(lane == 0.0, i1, jnp.where(lane == 1.0, i2,
                               jnp.where(lane == 2.0, 1.0 / den, jnp.where(lane == 3.0, e2 / den, 0.0))))


def _stage_o(layer, oh, os_p, os_s, om, x, wo_b, ada4, ln_g, ln_b, router=None):
    split_x = isinstance(x, tuple)
    x_specs = [_prompt_rows_spec(D), _sample_rows_spec(D)] if split_x else [_row_spec(D)]
    in_specs = [
        _row_spec(HW), _prompt_rows_spec(S5W), _sample_rows_spec(S5W), _row_spec(SSW), *x_specs,
        _const_spec((D, D)),
        *_mod_specs(layer, 2), _const_spec((1, D)), _const_spec((1, D)),
        *_mod_specs(layer, 4), *_mod_specs(layer, 3),
    ]
    args = [oh, os_p, os_s, om, *(x if split_x else (x,)), wo_b, ada4, ada4, ln_g, ln_b, ada4, ada4, ada4, ada4]
    out_specs = [_row_spec(D), _row_spec(D)]
    out_shape = [jax.ShapeDtypeStruct((ROWS, D), F32), jax.ShapeDtypeStruct((ROWS, D), BF16)]
    body = _o_body
    if router is not None:
        in_specs += [_const_spec((2, D, 128)), _const_spec((1, 128))]
        args += list(router)
        out_specs.append(_row_spec(128))
        out_shape[1] = jax.ShapeDtypeStruct((ROWS, D), F32)
        out_shape.append(jax.ShapeDtypeStruct((ROWS, 128), F32))
        body = _o_router_body
    return pl.pallas_call(
        functools.partial(body, split_x), grid=(NT,), in_specs=in_specs, out_specs=out_specs,
        out_shape=out_shape,
        compiler_params=_cparams(("parallel",)),
        name="stage_o_router" if router is not None else "stage_o",
    )(*args)


def _ffn_body(te_ref, nu_ref, h_ref, wg_ref, wu_ref, wd_ref, *refs):
    o_ref, acc_ref = refs[-2:]
    i = pl.program_id(0)
    j = pl.program_id(1)

    @pl.when(j == 0)
    def _():
        acc_ref[...] = jnp.zeros_like(acc_ref)

    @pl.when(i < nu_ref[0])
    def _():
        h = h_ref[...].astype(BF16)
        g = jnp.dot(h, wg_ref[...], preferred_element_type=F32)
        u = jnp.dot(h, wu_ref[...], preferred_element_type=F32)
        act = (_silu(g) * u).astype(BF16)
        acc_ref[...] += jnp.dot(act, wd_ref[...], preferred_element_type=F32)

    @pl.when(j == pl.num_programs(1) - 1)
    def _():
        o_ref[...] = acc_ref[...]


def _ffn(tile_expert, n_used, h, wg, wu, wd, out_rows=None, tile0=0, earlier=None):
    rows = h.shape[0]
    nj = DFF // TF

    def jblk(i, j, nu):
        return jnp.where(i < nu[0], j, nj - 1)

    in_specs = [
        pl.BlockSpec((TMF, D), lambda i, j, te, nu: (i, 0)),
        pl.BlockSpec((None, D, TF), lambda i, j, te, nu: (te[i], 0, jblk(i, j, nu))),
        pl.BlockSpec((None, D, TF), lambda i, j, te, nu: (te[i], 0, jblk(i, j, nu))),
        pl.BlockSpec((None, TF, D), lambda i, j, te, nu: (te[i], jblk(i, j, nu), 0)),
    ]
    args = [tile_expert, n_used, h, wg, wu, wd]
    aliases = {}
    if earlier is not None:
        in_specs.append(pl.BlockSpec(memory_space=pl.ANY))
        aliases = {len(args): 0}
        args.append(earlier)
    grid_spec = pltpu.PrefetchScalarGridSpec(
        num_scalar_prefetch=2,
        grid=(rows // TMF, nj),
        in_specs=in_specs,
        out_specs=pl.BlockSpec((TMF, D), lambda i, j, te, nu: (i + tile0, 0)),
        scratch_shapes=[pltpu.VMEM((TMF, D), F32)],
    )
    return pl.pallas_call(
        _ffn_body, grid_spec=grid_spec,
        out_shape=jax.ShapeDtypeStruct((rows if out_rows is None else out_rows, D), F32),
        input_output_aliases=aliases,
        compiler_params=_cparams(("parallel", "arbitrary")),
        name="ffn",
    )(*args)


def _final_body(x1_ref, ya_ref, yb_ref, route_ref, gp, gs, lng_ref, lnb_ref, yp_ref, ys_ref):
    i = pl.program_id(0)
    f = route_ref[:, 2:3] * ya_ref[...] + route_ref[:, 3:4] * yb_ref[...]
    y = _layer_norm(ALPHA * x1_ref[...] + _rowmod(i, gp, gs) * f, lng_ref[...], lnb_ref[...])

    @pl.when(i < NPT)
    def _():
        yp_ref[...] = y

    @pl.when(i >= NPT)
    def _():
        ys_ref[...] = y


def _stage_final(layer, x1, ya, yb, route, ada4, ln_g, ln_b):
    return pl.pallas_call(
        _final_body,
        grid=(NT,),
        in_specs=[
            _row_spec(D), _row_spec(D), _row_spec(D), _row_spec(128),
            *_mod_specs(layer, 5), _const_spec((1, D)), _const_spec((1, D)),
        ],
        out_specs=[
            pl.BlockSpec((TM, D), lambda i: (jnp.minimum(i, NPT - 1), 0)),
            pl.BlockSpec((TM, D), lambda i: (jnp.clip(i - NPT, 0, NST - 1), 0)),
        ],
        out_shape=[jax.ShapeDtypeStruct((ROWS_P, D), F32), jax.ShapeDtypeStruct((ROWS_S, D), F32)],
        compiler_params=_cparams(("arbitrary",)),
        name="stage_final",
    )(x1, ya, yb, route, ada4, ada4, ln_g, ln_b)


def _block_ones(n, blk):
    r = np.arange(n) // blk
    return r[:, None] == r[None, :]


def _consts():
    t = np.arange(CB)
    causal = t[:, None] >= t[None, :]
    same_seq = (t[:, None] // L_S) == (t[None, :] // L_S)
    r = np.arange(S5_ROWS)
    perm = r[None, :] == ((r % NB_P) * S5_TB + r // NB_P)[:, None]
    mats = {
        "perm": perm,
        "permt": perm.T,
        "e64": _block_ones(HW, HD),
        "g192": _block_ones(SSW, SSW // 2),
        "trilp": causal,
        "trils": causal & same_seq,
    }
    return {k: jnp.asarray(v.astype(np.float32), dtype=BF16) for k, v in mats.items()}


def _s5_params(a_re, a_im, log_dt, b_re, b_im, c_re, c_im, d, w_glu, b_glu):
    dt = jnp.exp(log_dt)[:, None]
    mag = jnp.exp(a_re * dt)
    lam_re, lam_im = mag * jnp.cos(a_im * dt), mag * jnp.sin(a_im * dt)
    den = a_re * a_re + a_im * a_im
    nr, ni = lam_re - 1.0, lam_im
    zr = (nr * a_re + ni * a_im) / den
    zi = (ni * a_re - nr * a_im) / den
    bbar_re = zr[..., None] * b_re - zi[..., None] * b_im
    bbar_im = zr[..., None] * b_im + zi[..., None] * b_re
    eye = jnp.eye(16, dtype=F32)
    blk = lambda bb: jnp.einsum('gph,gk->ghkp', bb, eye).reshape(S5W, S5N)
    hu, hs = S5W // 2, S5N // 2
    bblk = jnp.stack([jnp.stack([blk(bb)[hu * j:hu * (j + 1), hs * j:hs * (j + 1)] for j in range(2)])
                      for bb in (bbar_re, bbar_im)]).astype(BF16)
    cblk = lambda cc: jnp.einsum('ghp,gk->gpkh', cc, eye).reshape(S5N, S5W)
    ccat = jnp.stack([jnp.stack([cblk(cc)[hs * j:hs * (j + 1), hu * j:hu * (j + 1)] for cc in (c_re, -c_im)])
                      for j in range(2)]).astype(BF16)
    lr, li = lam_re.reshape(-1), lam_im.reshape(-1)
    pows = [(jnp.ones_like(lr), jnp.zeros_like(li))]
    for _ in range(8):
        pr, pi = pows[-1]
        pows.append((pr * lr - pi * li, pr * li + pi * lr))
    rows = jnp.arange(8)[:, None]
    tabs = []
    for dsh in (1, 2, 4):
        pr, pi = pows[dsh]
        tabs.append(jnp.where(rows >= dsh, jnp.concatenate([pr, pi])[None, :], 0.0))
    tabs.append(jnp.stack([jnp.concatenate(pows[r + 1]) for r in range(8)]))
    return {
        "tab": jnp.stack(tabs), "bblk": bblk, "ccat": ccat,
        "lam8": jnp.broadcast_to(jnp.concatenate([lr, li])[None, :], (NB_P, 2 * S5N)),
        "d": d.reshape(1, S5W), "wglu": w_glu.astype(BF16), "bglu": b_glu.reshape(1, S5W),
    }


def _pair_diag(st):
    s6 = st.reshape(NB_P, NH // 2, 2, HD, 2, HD)
    return jnp.stack([s6[:, :, 0, :, 0, :], s6[:, :, 1, :, 1, :]], axis=2).reshape(NB_P, NH, HD, HD)


def _pad_lanes(v, n=128):
    return jnp.pad(v, (0, n - v.shape[0])).reshape(1, n)


def kernel(x_prompt, x_sample, c_prompt, c_sample, state_hgrn, state_s5, state_ssd, state_ssd_conv, w_ada, b_ada, ln_g, ln_b, w_in, w_out, hgrn_lb_logits, hgrn_norm_g, s5_a_re, s5_a_im, s5_log_dt, s5_b_re, s5_b_im, s5_c_re, s5_c_im, s5_d, s5_w_glu, s5_b_glu, ssd_conv_w, ssd_conv_b, ssd_dt_bias, ssd_a_log, ssd_d, ssd_norm_g, ffn_w_gate, ffn_w_up, ffn_w_down, moe_w_router, moe_b_router, moe_w_gate, moe_w_up, moe_w_down):
    consts = _consts()
    c_all = jnp.concatenate([c_sample, c_prompt], axis=0)
    ada4 = _ada(c_all, w_ada, b_ada).reshape(DEPTH, NB_S + NB_P, 1, 6 * D)

    lb_all = jnp.cumsum(jax.nn.softmax(hgrn_lb_logits, axis=0), axis=0)
    lb_all = lb_all - lb_all[0]

    xp = x_prompt.reshape(ROWS_P, D)
    xs = x_sample.reshape(ROWS_S, D)
    new_h, new_s5, new_m, new_c = [], [], [], []
    x1 = f = route = ss_s = None
    for l in range(DEPTH):
        w_in_b = jnp.pad(w_in[l], ((0, 0), (0, N_IN_PAD - N_IN))).astype(BF16)
        if l == 0:
            x = (xp, xs)
            ph, ps_p, ps_s, pm = _stage_a0(xp, xs, ada4, w_in_b)
        else:
            x, ph, ps_p, ps_s, pm = _stage_a1(l, x1, f, ada4, ln_g[l - 1, 1].reshape(1, D),
                                              ln_b[l - 1, 1].reshape(1, D), w_in_b)
        oh, hg_p, hg_s = _hgrn(ph, lb_all[l].reshape(1, HW), hgrn_norm_g[l].reshape(1, HW), consts,
                               jnp.swapaxes(state_hgrn[l], -1, -2))
        s5p = _s5_params(s5_a_re[l], s5_a_im[l], s5_log_dt[l], s5_b_re[l], s5_b_im[l], s5_c_re[l], s5_c_im[l],
                         s5_d[l], s5_w_glu[l], s5_b_glu[l])
        os_p, s5_p = _s5_prompt(ps_p.reshape(NB_P, L_P, S5W), s5p, consts)
        os_s, s5_s = _s5_sample(ps_s, s5p, state_s5[l].reshape(NB_S, 1, 2 * S5N))
        os_p = os_p.reshape(ROWS_P, S5W)
        ssd_prm = {
            "cw": ssd_conv_w[l], "cb": ssd_conv_b[l].reshape(1, CONVC),
            "dtb": _pad_lanes(ssd_dt_bias[l]), "aneg": _pad_lanes(-jnp.exp(ssd_a_log[l])),
            "dx": jnp.repeat(ssd_d[l], HD).reshape(1, SSW), "sg": ssd_norm_g[l].reshape(1, SSW),
        }
        hist = jnp.pad(state_ssd_conv[l], ((0, 0), (0, L_S - 3), (0, 0))).reshape(ROWS_S, CONVC)
        om, ss_p, ss_s, tail_p, xbc_s = _ssd(pm, ssd_prm, consts, hist, state_ssd[l], l, ss_s)
        conv_p = tail_p[:, 8 - 3:]
        conv_s = xbc_s.reshape(NB_S, L_S, CONVC)[:, L_S - 3:]
        hg_p, ss_p = _pair_diag(hg_p), _pair_diag(ss_p)
        new_h.append((jnp.swapaxes(hg_p, -1, -2), jnp.swapaxes(hg_s, -1, -2)))
        new_s5.append((s5_p.reshape(NB_P, 2, 16, 64), s5_s.reshape(NB_S, 2, 16, 64)))
        new_m.append((ss_p,))
        new_c.append((conv_p, conv_s))

        wo_b = w_out[l].astype(BF16)
        lg, lbb = ln_g[l, 0].reshape(1, D), ln_b[l, 0].reshape(1, D)
        j = l // 2
        if l % 2 == 0:
            x1, h2 = _stage_o(l, oh, os_p, os_s, om, x, wo_b, ada4, lg, lbb)
            f = _ffn(jnp.zeros((ROWS // TMF,), jnp.int32), jnp.full((1,), ROWS // TMF, jnp.int32), h2,
                     ffn_w_gate[j:j + 1].astype(BF16),
                     ffn_w_up[j:j + 1].astype(BF16), ffn_w_down[j:j + 1].astype(BF16))
        else:
            wr = jnp.pad(moe_w_router[j], ((0, 0), (0, 128 - NEXP)))
            wr_hi = wr.astype(BF16)
            wr = jnp.stack([wr_hi, (wr - wr_hi.astype(F32)).astype(BF16)])
            br = _pad_lanes(moe_b_router[j])
            x1, h2, route = _stage_o(l, oh, os_p, os_s, om, x, wo_b, ada4, lg, lbb, router=(wr, br))
            wg_b, wu_b, wd_b = (moe_w_gate[j].astype(BF16), moe_w_up[j].astype(BF16),
                                moe_w_down[j].astype(BF16))
            flat_e = route[:, 0:2].astype(jnp.int32).reshape(-1)
            onehot = (flat_e[:, None] == jnp.arange(NEXP)[None, :]).astype(jnp.int32)
            csum = jnp.cumsum(onehot, axis=0)
            counts = csum[-1]
            rank = jnp.take_along_axis(csum, flat_e[:, None], axis=1)[:, 0] - 1
            padded = ((counts + TMF - 1) // TMF) * TMF
            pend = jnp.cumsum(padded)
            pstart = pend - padded
            dest = pstart[flat_e] + rank
            n_pad = 2 * ROWS + NEXP * TMF
            src_tok = jnp.zeros((n_pad,), jnp.int32).at[dest].set(jnp.arange(2 * ROWS, dtype=jnp.int32) // 2)
            tile_start = jnp.arange(n_pad // TMF, dtype=jnp.int32) * TMF
            tile_e = jnp.minimum(jnp.sum((pend[None, :] <= tile_start[:, None]).astype(jnp.int32), axis=1),
                                 NEXP - 1)
            n_used = (pend[NEXP - 1:NEXP] // TMF).astype(jnp.int32)
            part_tiles = n_pad // TMF // MOE_PARTS
            part_rows = part_tiles * TMF
            y_sorted = None
            for c in range(MOE_PARTS):
                y_sorted = _ffn(tile_e[c * part_tiles:(c + 1) * part_tiles],
                                jnp.clip(n_used - c * part_tiles, 0, part_tiles),
                                h2[src_tok[c * part_rows:(c + 1) * part_rows]], wg_b, wu_b, wd_b,
                                out_rows=n_pad, tile0=c * part_tiles, earlier=y_sorted)
            pos = dest.reshape(ROWS, 2)
            ya = y_sorted[pos[:, 0]]
            yb = y_sorted[pos[:, 1]]
    y_p, y_s = _stage_final(DEPTH - 1, x1, ya, yb, route, ada4, ln_g[DEPTH - 1, 1].reshape(1, D),
                            ln_b[DEPTH - 1, 1].reshape(1, D))
    stack = lambda lst, k: jnp.stack([t[k] for t in lst])
    return (y_p.reshape(NB_P, L_P, D), y_s.reshape(NB_S, L_S, D),
            stack(new_h, 0), stack(new_s5, 0), stack(new_m, 0), stack(new_c, 0),
            stack(new_h, 1), stack(new_s5, 1), ss_s, stack(new_c, 1))
```

```python
import functools
import math

import jax
import jax.numpy as jnp
import numpy as np
from jax import lax
from jax.experimental import pallas as pl
from jax.experimental.pallas import tpu as pltpu

F32 = jnp.float32
BF16 = jnp.bfloat16

D = 1024
NB_P, L_P = 8, 2048
NB_S, L_S = 128, 8
ROWS_P = NB_P * L_P
ROWS_S = NB_S * L_S
ROWS = ROWS_P + ROWS_S
DEPTH = 2
HW = 384
S5W = 256
SSW = 384
NH = 6
HD = 64
S5N = 1024
CONVC = 640
N_IN = 2822
N_IN_PAD = 2944
P_HG = 1536
P_SS = 1152
DFF = 2816
NEXP = 8
ALPHA = (2 * DEPTH) ** 0.25
LN_EPS = 1e-5
RMS_EPS = 1e-6

TM = 512
NPT = ROWS_P // TM
NST = ROWS_S // TM
NT = NPT + NST
SEQ_PER_TILE = TM // L_S
TILES_PER_SEQ = L_P // TM

CB = 128
NPC = ROWS_P // CB
NSC = ROWS_S // CB
CH_PER_SEQ = L_P // CB
SEQ_PER_CB = CB // L_S

TMF = 512
TF = 1408
VMEM_LIMIT = 56 * 1024 * 1024


def _cparams(sem):
    return pltpu.CompilerParams(dimension_semantics=sem, vmem_limit_bytes=VMEM_LIMIT)


def _bdot(a, b):
    return jnp.dot(a.astype(BF16), b.astype(BF16), preferred_element_type=F32)


def _bdot_nt(a, b):
    return lax.dot_general(a.astype(BF16), b.astype(BF16), (((1,), (1,)), ((), ())),
                           preferred_element_type=F32)


def _split_dot(x, e, passes):
    acc = None
    r = x
    for _ in range(passes):
        hi = r.astype(BF16)
        d = jnp.dot(hi, e, preferred_element_type=F32)
        acc = d if acc is None else acc + d
        r = r - hi.astype(F32)
    return acc


def _split_dot_l(e, x, passes):
    acc = None
    r = x
    for _ in range(passes):
        hi = r.astype(BF16)
        d = jnp.dot(e, hi, preferred_element_type=F32)
        acc = d if acc is None else acc + d
        r = r - hi.astype(F32)
    return acc


def _silu(x):
    return x * jax.nn.sigmoid(x)


def _layer_norm(x, g, b):
    mu = jnp.mean(x, -1, keepdims=True)
    xc = x - mu
    var = jnp.mean(xc * xc, -1, keepdims=True)
    return xc * lax.rsqrt(var + LN_EPS) * g + b


def _rowmod(i, p_ref, s_ref):
    s = jnp.broadcast_to(s_ref[...], (SEQ_PER_TILE, L_S, D)).reshape(TM, D)
    return jnp.where(i < NPT, p_ref[0], s)


ADA_TN = 1536


def _ada_body(c_ref, w_ref, b_ref, o_ref):
    o_ref[...] = _bdot(_silu(c_ref[...]), w_ref[...]) + b_ref[...]


def _ada(c_all, w_ada, b_ada):
    nc = c_all.shape[0]
    return pl.pallas_call(
        _ada_body,
        grid=(DEPTH, 6 * D // ADA_TN),
        in_specs=[
            pl.BlockSpec((nc, D), lambda l, j: (0, 0)),
            pl.BlockSpec((None, D, ADA_TN), lambda l, j: (l, 0, j)),
            pl.BlockSpec((None, 1, ADA_TN), lambda l, j: (l, 0, j)),
        ],
        out_specs=pl.BlockSpec((None, nc, ADA_TN), lambda l, j: (l, 0, j)),
        out_shape=jax.ShapeDtypeStruct((DEPTH, nc, 6 * D), F32),
        compiler_params=_cparams(("parallel", "parallel")),
        name="ada",
    )(c_all, w_ada, b_ada.reshape(DEPTH, 1, 6 * D))


def _mod_specs(layer, k):
    ps = pl.BlockSpec((None, 1, 1, D),
                      lambda i: (layer, NB_S + jnp.minimum(i // TILES_PER_SEQ, NB_P - 1), 0, k))
    ss = pl.BlockSpec((None, SEQ_PER_TILE, 1, D),
                      lambda i: (layer, jnp.clip(i - NPT, 0, NST - 1), 0, k))
    return [ps, ss]


def _row_spec(width):
    return pl.BlockSpec((TM, width), lambda i: (i, 0))


def _const_spec(shape):
    nd = len(shape)
    return pl.BlockSpec(shape, lambda *_: (0,) * nd)


def _prompt_rows_spec(width):
    return pl.BlockSpec((TM, width), lambda i: (jnp.minimum(i, NPT - 1), 0))


def _sample_rows_spec(width):
    return pl.BlockSpec((TM, width), lambda i: (jnp.clip(i - NPT, 0, NST - 1), 0))


def _proj_out(x, i, scp, scs, shp, shs, w_ref, ph_ref, psp_ref, pss_ref, pm_ref):
    h = x * (1.0 + _rowmod(i, scp, scs)) + _rowmod(i, shp, shs)
    proj = jnp.dot(h.astype(BF16), w_ref[...], preferred_element_type=F32)
    ph_ref[...] = proj[:, 0:P_HG]
    pm_ref[...] = proj[:, P_HG + S5W:N_IN_PAD]

    @pl.when(i < NPT)
    def _():
        psp_ref[...] = proj[:, P_HG:P_HG + S5W]

    @pl.when(i >= NPT)
    def _():
        pss_ref[...] = proj[:, P_HG:P_HG + S5W]


def _a0_body(xp_ref, xs_ref, scp, scs, shp, shs, w_ref, ph_ref, psp_ref, pss_ref, pm_ref):
    i = pl.program_id(0)
    x = jnp.where(i < NPT, xp_ref[...], xs_ref[...])
    _proj_out(x, i, scp, scs, shp, shs, w_ref, ph_ref, psp_ref, pss_ref, pm_ref)


def _a1_body(x1_ref, f_ref, gp, gs, lng_ref, lnb_ref, scp, scs, shp, shs, w_ref,
             x_ref, ph_ref, psp_ref, pss_ref, pm_ref):
    i = pl.program_id(0)
    x = _layer_norm(ALPHA * x1_ref[...] + _rowmod(i, gp, gs) * f_ref[...], lng_ref[...], lnb_ref[...])
    x_ref[...] = x
    _proj_out(x, i, scp, scs, shp, shs, w_ref, ph_ref, psp_ref, pss_ref, pm_ref)


def _a_out():
    specs = [_row_spec(P_HG), _prompt_rows_spec(S5W), _sample_rows_spec(S5W), _row_spec(P_SS)]
    shapes = [jax.ShapeDtypeStruct((ROWS, P_HG), F32), jax.ShapeDtypeStruct((ROWS_P, S5W), F32),
              jax.ShapeDtypeStruct((ROWS_S, S5W), F32), jax.ShapeDtypeStruct((ROWS, P_SS), F32)]
    return specs, shapes


def _stage_a0(xp, xs, ada4, w_in_b):
    out_specs, out_shape = _a_out()
    return pl.pallas_call(
        _a0_body,
        grid=(NT,),
        in_specs=[
            _prompt_rows_spec(D), _sample_rows_spec(D),
            *_mod_specs(0, 1), *_mod_specs(0, 0),
            _const_spec((D, N_IN_PAD)),
        ],
        out_specs=out_specs, out_shape=out_shape,
        compiler_params=_cparams(("arbitrary",)),
        name="stage_a0",
    )(xp, xs, ada4, ada4, ada4, ada4, w_in_b)


def _stage_a1(layer, x1, f, ada4, ln_g, ln_b, w_in_b):
    out_specs, out_shape = _a_out()
    return pl.pallas_call(
        _a1_body,
        grid=(NT,),
        in_specs=[
            _row_spec(D), _row_spec(D),
            *_mod_specs(layer - 1, 5),
            _const_spec((1, D)), _const_spec((1, D)),
            *_mod_specs(layer, 1), *_mod_specs(layer, 0),
            _const_spec((D, N_IN_PAD)),
        ],
        out_specs=[_row_spec(D)] + out_specs, out_shape=[jax.ShapeDtypeStruct((ROWS, D), F32)] + out_shape,
        compiler_params=_cparams(("arbitrary",)),
        name="stage_a1",
    )(x1, f, ada4, ada4, ln_g, ln_b, ada4, ada4, ada4, ada4, w_in_b)


def _mixer_blk(width, nsub=1):
    return pl.BlockSpec((nsub * CB, width), lambda i: (i, 0))


def _pstate_spec(shape, nsub=1):
    nd = len(shape)
    return pl.BlockSpec((1,) + shape,
                        lambda i: (jnp.minimum(i // (CH_PER_SEQ // nsub), NB_P - 1),) + (0,) * nd)


def _sstate_spec(shape, nsub=1):
    nd = len(shape)
    return pl.BlockSpec((nsub * SEQ_PER_CB,) + shape,
                        lambda i: (jnp.clip(i - NPC // nsub, 0, NSC // nsub - 1),) + (0,) * nd)


def _cumsum_rows(x, span):
    r = lax.broadcasted_iota(jnp.int32, x.shape, 0) & (span - 1)
    d = 1
    while d < span:
        x = x + jnp.where(r >= d, pltpu.roll(x, d, 0), 0.0)
        d *= 2
    return x


def _seq_last_rows(x):
    w = x.shape[-1]
    x3 = x.reshape(SEQ_PER_CB, L_S, w)
    return jnp.broadcast_to(x3[:, L_S - 1:L_S, :], (SEQ_PER_CB, L_S, w)).reshape(CB, w)


def _concat_heads(parts):
    return jnp.concatenate(parts, axis=1)


def _stack_select(shape, row_div, lane_div):
    r = lax.broadcasted_iota(jnp.int32, shape, 0) // row_div
    c = lax.broadcasted_iota(jnp.int32, shape, 1) // lane_div
    return r == c


def _seq_expand_lanes(qh):
    q2 = jnp.concatenate([qh, qh], axis=1)
    q16 = jnp.concatenate([q2] * (SEQ_PER_CB // 2), axis=1)
    return jnp.where(_stack_select((CB, SEQ_PER_CB * HD), L_S, HD), q16, 0.0)


def _seq_expand_rows(xt):
    t = jnp.broadcast_to(xt[None], (SEQ_PER_CB, HD, CB)).reshape(SEQ_PER_CB * HD, CB)
    return jnp.where(_stack_select((SEQ_PER_CB * HD, CB), HD, L_S), t, 0.0)


def _fold_seq_lanes(full):
    acc = full[:, 0:128]
    for j in range(1, SEQ_PER_CB * HD // 128):
        acc = acc + full[:, 128 * j:128 * (j + 1)]
    return acc[:, 0:HD] + acc[:, HD:2 * HD]


HGRN_BASE = 32
EXP_RANGE_MAX = 80.0


def _hgrn_block(prompt, p_ref, lb_ref, hg_ref, e64_ref, oh_ref, st_scr, o_scr, sub, is_last=None,
                stp_out=None, sts_in=None, sts_out=None):
    rows = slice(sub * CB, (sub + 1) * CB)
    seqs = slice(sub * SEQ_PER_CB, (sub + 1) * SEQ_PER_CB)
    lb = lb_ref[...]
    qr = p_ref[rows, 0:HW]
    fr = p_ref[rows, HW:2 * HW]
    v = p_ref[rows, 2 * HW:3 * HW]
    gr = p_ref[rows, 3 * HW:4 * HW]
    e = jnp.exp(-jnp.abs(fr))
    ope = 1.0 + e
    ls = jnp.minimum(fr, 0.0) - jnp.log(ope)
    a = jnp.log(lb)
    bb = jnp.log1p(-lb) + ls
    lf = jnp.maximum(a, bb) + jnp.log(1.0 + jnp.exp(-jnp.abs(a - bb)))
    rcp = 1.0 / ope
    kk = (1.0 - lb) * jnp.where(fr >= 0.0, e * rcp, rcp)
    q = _silu(qr)
    b = _cumsum_rows(lf, CB if prompt else L_S)

    e64 = e64_ref[...]
    ti = lax.broadcasted_iota(jnp.int32, (CB, CB), 0)
    si = lax.broadcasted_iota(jnp.int32, (CB, CB), 1)

    def diag8():
        nsub = CB // 8
        b3 = b.reshape(nsub, 8, HW)
        q3 = q.reshape(nsub, 8, HW)
        k3 = kk.reshape(nsub, 8, HW)
        v3 = v.reshape(nsub, 8, HW)
        r3 = lax.broadcasted_iota(jnp.int32, (nsub, 8, HW), 1)
        o = jnp.zeros((CB, HW), F32)
        for s in range(8):
            dlt = jnp.minimum(b3 - b3[:, s:s + 1, :], 0.0)
            w = jnp.where(r3 >= s, jnp.exp(dlt), 0.0) * q3 * k3[:, s:s + 1, :]
            hsum = jnp.dot(w.reshape(CB, HW).astype(BF16), e64, preferred_element_type=F32)
            o = o + hsum * jnp.broadcast_to(v3[:, s:s + 1, :], (nsub, 8, HW)).reshape(CB, HW)
        return o

    def level_terms(m):
        terms = []
        while m < CB:
            nb = CB // (2 * m)
            b4 = b.reshape(nb, 2 * m, HW)
            bmid = b4[:, m - 1:m, :]
            pos = lax.broadcasted_iota(jnp.int32, (nb, 2 * m, HW), 1)
            qq = jnp.where(pos >= m, q.reshape(nb, 2 * m, HW) * jnp.exp(jnp.minimum(b4 - bmid, 0.0)), 0.0)
            kq = jnp.where(pos < m, kk.reshape(nb, 2 * m, HW) * jnp.exp(jnp.minimum(bmid - b4, 0.0)), 0.0)
            terms.append((qq.reshape(CB, HW), kq.reshape(CB, HW), (ti // (2 * m)) == (si // (2 * m))))
            m *= 2
        return terms

    def scores_times_v(terms):
        lane = lax.broadcasted_iota(jnp.int32, (CB, 2 * HD), 1)
        parts = []
        for j in range(NH // 2):
            scs = []
            for h in (2 * j, 2 * j + 1):
                sl = slice(HD * h, HD * (h + 1))
                sc = None
                for qq, kq, keep in terms:
                    t = jnp.where(keep, _bdot_nt(qq[:, sl], kq[:, sl]), 0.0)
                    sc = t if sc is None else sc + t
                scs.append(sc)
            vp = v[:, 2 * HD * j:2 * HD * (j + 1)]
            vdiag = jnp.concatenate([jnp.where(lane < HD, vp, 0.0), jnp.where(lane >= HD, vp, 0.0)], axis=0)
            parts.append(_bdot(jnp.concatenate(scs, axis=1), vdiag))
        return _concat_heads(parts)

    base = HGRN_BASE if prompt else L_S
    nbase = CB // base
    bb3 = b.reshape(nbase, base, HW)
    top = bb3[:, 0:1, :] - lf.reshape(nbase, base, HW)[:, 0:1, :]
    decay_range = jnp.max(top - bb3[:, base - 1:base, :])
    in_range = decay_range <= EXP_RANGE_MAX

    def exact_path():
        o_scr[rows, :] = diag8() + (scores_times_v(level_terms(8)) if prompt else 0.0)

    def finish(fast):
        _hgrn_finish(prompt, fast, rows, seqs, q, kk, v, gr, b, bb3, top, ti, si, scores_times_v, level_terms,
                     hg_ref, e64, oh_ref, st_scr, o_scr, is_last, stp_out, sts_in, sts_out)

    return in_range, exact_path, finish


def _hgrn_finish(prompt, fast, rows, seqs, q, kk, v, gr, b, bb3, top, ti, si, scores_times_v, level_terms,
                 hg_ref, e64, oh_ref, st_scr, o_scr, is_last, stp_out, sts_in, sts_out):
    base = HGRN_BASE if prompt else L_S
    nbase = CB // base
    qf = (q.reshape(nbase, base, HW) * jnp.exp(bb3 - top)).reshape(CB, HW)
    kf = (kk.reshape(nbase, base, HW) * jnp.exp(top - bb3)).reshape(CB, HW)
    keep = ((ti // base) == (si // base)) & (si <= ti)
    o_fast = scores_times_v([(qf, kf, keep)] + (level_terms(base) if prompt else []))
    o = jnp.where(fast, o_fast, o_scr[rows, :])
    qt = q * jnp.exp(b)
    if prompt:
        blast = b[CB - 1:CB, :]
        kd = kk * jnp.exp(blast - b)
        vt = v.T
        same_head = (ti // HD) == (si // HD)
        parts = []
        for j in range(NH // 2):
            pr = slice(2 * HD * j, 2 * HD * (j + 1))
            st = st_scr[j]
            parts.append(o[:, pr] + _bdot_nt(qt[:, pr], st))
            st_scr[j] = st * jnp.exp(blast[:, pr]) + jnp.where(same_head, _bdot(vt[pr, :], kd[:, pr]), 0.0)

        if is_last is not None:
            @pl.when(is_last)
            def _():
                stp_out[0] = st_scr[...]
    else:
        blast = _seq_last_rows(b)
        kd = kk * jnp.exp(blast - b)
        dec = jnp.exp(blast)
        vt = v.T
        parts = []
        for h in range(NH):
            sl = slice(HD * h, HD * (h + 1))
            sts = sts_in[seqs, h].reshape(SEQ_PER_CB * HD, HD)
            full = _bdot_nt(qt[:, sl], sts)
            sel = jnp.where(_stack_select((CB, SEQ_PER_CB * HD), L_S, HD), full, 0.0)
            parts.append(o[:, sl] + _fold_seq_lanes(sel))
            dec3 = dec[:, sl].reshape(SEQ_PER_CB, L_S, HD)[:, L_S - 1:L_S, :]
            dec_rows = jnp.broadcast_to(dec3, (SEQ_PER_CB, HD, HD)).reshape(SEQ_PER_CB * HD, HD)
            upd = _bdot(_seq_expand_rows(vt[sl, :]), kd[:, sl])
            sts_out[seqs, h] = (sts * dec_rows + upd).reshape(SEQ_PER_CB, HD, HD)
    oall = _concat_heads(parts)
    ms = _split_dot(oall * oall, e64, 1) * (1.0 / HD)
    oh_ref[rows, :] = oall * lax.rsqrt(ms + RMS_EPS) * hg_ref[...] * _silu(gr)


HGRN_SUB = 2


def _hgrn_body(p_ref, lb_ref, hg_ref, e64_ref, sts_in,
               oh_ref, stp_out, sts_out, st_scr, o_scr):
    i = pl.program_id(0)
    npc, steps_per_seq = NPC // HGRN_SUB, CH_PER_SEQ // HGRN_SUB

    @pl.when(i == 0)
    def _():
        o_scr[...] = jnp.zeros_like(o_scr)

    @pl.when((i < npc) & (i % steps_per_seq == 0))
    def _():
        st_scr[...] = jnp.zeros_like(st_scr)

    def run(blocks):
        fast = blocks[0][0]
        for in_range, _, _ in blocks[1:]:
            fast = jnp.logical_and(fast, in_range)

        @pl.when(jnp.logical_not(fast))
        def _():
            for _, exact_path, _ in blocks:
                exact_path()

        for _, _, finish in blocks:
            finish(fast)

    @pl.when(i < npc)
    def _():
        run([_hgrn_block(True, p_ref, lb_ref, hg_ref, e64_ref, oh_ref, st_scr, o_scr, sub,
                         is_last=(i % steps_per_seq == steps_per_seq - 1) if sub == HGRN_SUB - 1 else None,
                         stp_out=stp_out) for sub in range(HGRN_SUB)])

    @pl.when(i >= npc)
    def _():
        run([_hgrn_block(False, p_ref, lb_ref, hg_ref, e64_ref, oh_ref, st_scr, o_scr, sub,
                         sts_in=sts_in, sts_out=sts_out) for sub in range(HGRN_SUB)])


def _hgrn(ph, lb, hg, consts, st_t):
    return pl.pallas_call(
        _hgrn_body,
        grid=((NPC + NSC) // HGRN_SUB,),
        in_specs=[
            _mixer_blk(P_HG, HGRN_SUB), _const_spec((1, HW)), _const_spec((1, HW)),
            _const_spec((HW, HW)),
            _sstate_spec((NH, HD, HD), HGRN_SUB),
        ],
        out_specs=[_mixer_blk(HW, HGRN_SUB), _pstate_spec((NH // 2, 2 * HD, 2 * HD), HGRN_SUB),
                   _sstate_spec((NH, HD, HD), HGRN_SUB)],
        out_shape=[
            jax.ShapeDtypeStruct((ROWS, HW), F32),
            jax.ShapeDtypeStruct((NB_P, NH // 2, 2 * HD, 2 * HD), F32),
            jax.ShapeDtypeStruct((NB_S, NH, HD, HD), F32),
        ],
        scratch_shapes=[pltpu.VMEM((NH // 2, 2 * HD, 2 * HD), F32), pltpu.VMEM((HGRN_SUB * CB, HW), F32)],
        compiler_params=_cparams(("arbitrary",)),
        name="hgrn",
    )(ph, lb, hg, consts["e64"], st_t)


def _cmul_add(hr, hi, lr, li, sr, si):
    return hr + lr * sr - li * si, hi + lr * si + li * sr


def _s5_project(ub, bblk_ref):
    halves = [ub[:, (S5W // 2) * j:(S5W // 2) * (j + 1)] for j in range(2)]
    hr = jnp.concatenate([jnp.dot(halves[j], bblk_ref[0, j], preferred_element_type=F32) for j in range(2)], axis=1)
    hi = jnp.concatenate([jnp.dot(halves[j], bblk_ref[1, j], preferred_element_type=F32) for j in range(2)], axis=1)
    return hr, hi


def _s5_readout(h_scr, u, ccat_ref, d_ref, wglu_ref, bglu_ref):
    hs = S5N // 2
    ch = [_bdot(h_scr[:, hs * j:hs * (j + 1)], ccat_ref[j, 0])
          + _bdot(h_scr[:, S5N + hs * j:S5N + hs * (j + 1)], ccat_ref[j, 1]) for j in range(2)]
    y = jnp.concatenate(ch, axis=1) + d_ref[...] * u
    c0 = math.sqrt(2.0 / math.pi)
    y = y * (0.5 * (1.0 + jnp.tanh(c0 * (y + 0.044715 * (y * y * y)))))
    return y * jax.nn.sigmoid(_bdot(y, wglu_ref[...]) + bglu_ref[...])


S5_TB = 32
S5_ROWS = NB_P * S5_TB


def _s5_prompt_body(p_ref, perm_ref, permt_ref, lam_ref, bblk_ref, ccat_ref, d_ref, wglu_ref, bglu_ref,
                    os_ref, st_out, carry_scr, h_scr):
    i = pl.program_id(0)

    @pl.when(i == 0)
    def _():
        carry_scr[...] = jnp.zeros_like(carry_scr)

    u = p_ref[...].reshape(S5_ROWS, S5W)
    u_hi = u.astype(BF16)
    u_lo = (u - u_hi.astype(F32)).astype(BF16)
    perm = perm_ref[...]
    up_hi = jnp.dot(perm, u_hi, preferred_element_type=F32)
    up = up_hi + jnp.dot(perm, u_lo, preferred_element_type=F32)
    hr, hi = _s5_project(up_hi.astype(BF16), bblk_ref)
    lr = lam_ref[:, 0:S5N]
    li = lam_ref[:, S5N:2 * S5N]
    cr = carry_scr[:, 0:S5N]
    ci = carry_scr[:, S5N:2 * S5N]
    for t in range(S5_TB):
        rows = slice(NB_P * t, NB_P * (t + 1))
        cr, ci = _cmul_add(hr[rows], hi[rows], lr, li, cr, ci)
        h_scr[rows, 0:S5N] = cr
        h_scr[rows, S5N:2 * S5N] = ci
    carry_scr[:, 0:S5N] = cr
    carry_scr[:, S5N:2 * S5N] = ci
    out = _s5_readout(h_scr, up, ccat_ref, d_ref, wglu_ref, bglu_ref)
    os_ref[...] = jnp.dot(permt_ref[...], out.astype(BF16), preferred_element_type=F32).reshape(NB_P, S5_TB, S5W)

    @pl.when(i == pl.num_programs(0) - 1)
    def _():
        st_out[...] = carry_scr[...]


def _s5_prompt(ps3, prm, consts):
    return pl.pallas_call(
        _s5_prompt_body,
        grid=(L_P // S5_TB,),
        in_specs=[
            pl.BlockSpec((NB_P, S5_TB, S5W), lambda i: (0, i, 0)),
            _const_spec((S5_ROWS, S5_ROWS)), _const_spec((S5_ROWS, S5_ROWS)), _const_spec((NB_P, 2 * S5N)),
            _const_spec((2, 2, S5W // 2, S5N // 2)), _const_spec((2, 2, S5N // 2, S5W // 2)),
            _const_spec((1, S5W)), _const_spec((S5W, S5W)), _const_spec((1, S5W)),
        ],
        out_specs=[pl.BlockSpec((NB_P, S5_TB, S5W), lambda i: (0, i, 0)), _const_spec((NB_P, 2 * S5N))],
        out_shape=[jax.ShapeDtypeStruct((NB_P, L_P, S5W), F32), jax.ShapeDtypeStruct((NB_P, 2 * S5N), F32)],
        scratch_shapes=[pltpu.VMEM((NB_P, 2 * S5N), F32), pltpu.VMEM((S5_ROWS, 2 * S5N), F32)],
        compiler_params=_cparams(("arbitrary",)),
        name="s5_prompt",
    )(ps3, consts["perm"], consts["permt"], prm["lam8"], prm["bblk"], prm["ccat"], prm["d"], prm["wglu"],
      prm["bglu"])


def _s5_sample_body(p_ref, tab_ref, bblk_ref, ccat_ref, d_ref, wglu_ref, bglu_ref, s5s_in,
                    os_ref, s5s_out, h_scr):
    u = p_ref[...]
    hr, hi = _s5_project(u.astype(BF16), bblk_ref)
    nsub = CB // 8
    for idx, dsh in enumerate((1, 2, 4)):
        sr = pltpu.roll(hr, dsh, 0).reshape(nsub, 8, S5N)
        si = pltpu.roll(hi, dsh, 0).reshape(nsub, 8, S5N)
        lr = tab_ref[idx, :, 0:S5N][None]
        li = tab_ref[idx, :, S5N:2 * S5N][None]
        nr, ni = _cmul_add(hr.reshape(nsub, 8, S5N), hi.reshape(nsub, 8, S5N), lr, li, sr, si)
        hr = nr.reshape(CB, S5N)
        hi = ni.reshape(CB, S5N)
    tcr = tab_ref[3, :, 0:S5N]
    tci = tab_ref[3, :, S5N:2 * S5N]
    cr = s5s_in[:, :, 0:S5N]
    ci = s5s_in[:, :, S5N:2 * S5N]
    tr, tim = _cmul_add(hr.reshape(nsub, 8, S5N), hi.reshape(nsub, 8, S5N), tcr[None], tci[None], cr, ci)
    h_scr[:, 0:S5N] = tr.reshape(CB, S5N)
    h_scr[:, S5N:2 * S5N] = tim.reshape(CB, S5N)
    sb = lax.broadcasted_iota(jnp.int32, (SEQ_PER_CB, CB), 0)
    st = lax.broadcasted_iota(jnp.int32, (SEQ_PER_CB, CB), 1)
    sel = (st == L_S * sb + (L_S - 1)).astype(BF16)
    s5s_out[...] = _split_dot_l(sel, h_scr[...], 3)
    os_ref[...] = _s5_readout(h_scr, u, ccat_ref, d_ref, wglu_ref, bglu_ref)


def _s5_sample(ps_s, prm, st):
    seqs = lambda shape: pl.BlockSpec((SEQ_PER_CB,) + shape, lambda i: (i,) + (0,) * len(shape))
    return pl.pallas_call(
        _s5_sample_body,
        grid=(NSC,),
        in_specs=[
            _mixer_blk(S5W), _const_spec((4, 8, 2 * S5N)), _const_spec((2, 2, S5W // 2, S5N // 2)),
            _const_spec((2, 2, S5N // 2, S5W // 2)), _const_spec((1, S5W)), _const_spec((S5W, S5W)),
            _const_spec((1, S5W)), seqs((1, 2 * S5N)),
        ],
        out_specs=[_mixer_blk(S5W), seqs((2 * S5N,))],
        out_shape=[jax.ShapeDtypeStruct((ROWS_S, S5W), F32), jax.ShapeDtypeStruct((NB_S, 2 * S5N), F32)],
        scratch_shapes=[pltpu.VMEM((CB, 2 * S5N), F32)],
        compiler_params=_cparams(("parallel",)),
        name="s5_sample",
    )(ps_s, prm["tab"], prm["bblk"], prm["ccat"], prm["d"], prm["wglu"], prm["bglu"], st)


def _ssd_block(prompt, p_ref, cw_ref, cb_ref, dtb_ref, aneg_ref, dx_ref, sg_ref,
               g192_ref, tril_ref, om_ref, cbuf, st_scr, sub, is_last=None, ssp_out=None, convp_out=None,
               hist_ref=None, sss_in=None, sss_out=None, convs_out=None):
    rows = slice(sub * CB, (sub + 1) * CB)
    seqs = slice(sub * SEQ_PER_CB, (sub + 1) * SEQ_PER_CB)
    z = p_ref[rows, 0:SSW]
    xbc = p_ref[rows, SSW:SSW + CONVC]
    dtr = p_ref[rows, SSW + CONVC:P_SS]
    acc = cb_ref[...] + cw_ref[3:4, :] * xbc
    if prompt:
        cbuf[8:8 + CB, :] = xbc
        for k in (1, 2, 3):
            acc = acc + cw_ref[3 - k:4 - k, :] * cbuf[8 - k:8 - k + CB, :]
        cbuf[0:8, :] = cbuf[CB:CB + 8, :]
    else:
        convs_out[rows, :] = xbc
        tl = lax.broadcasted_iota(jnp.int32, (CB, CONVC), 0) % L_S
        hist = hist_ref[rows, :]
        for k in (1, 2, 3):
            hk = hist if k == 3 else pltpu.roll(hist, CB - (3 - k), 0)
            sh = jnp.where(tl >= k, pltpu.roll(xbc, k, 0), hk)
            acc = acc + cw_ref[3 - k:4 - k, :] * sh
    xc = _silu(acc)
    xs = xc[:, 0:SSW]
    bm = xc[:, SSW:SSW + 2 * HD]
    cm = xc[:, SSW + 2 * HD:CONVC]
    xdt = dtr + dtb_ref[...]
    dt = jnp.maximum(xdt, 0.0) + jnp.log1p(jnp.exp(-jnp.abs(xdt)))
    la = dt * aneg_ref[...]
    tril = tril_ref[...]
    b6 = _cumsum_rows(la, CB if prompt else L_S)
    lane = lax.broadcasted_iota(jnp.int32, (CB, 128), 1)
    bxw = [jnp.broadcast_to(b6[:, h:h + 1], (CB, 128)) for h in range(NH)]
    dtw = [jnp.broadcast_to(dt[:, h:h + 1], (CB, 128)) for h in range(NH)]
    pair = lambda cols: _concat_heads([jnp.where(lane < HD, cols[2 * j], cols[2 * j + 1]) for j in range(NH // 2)])
    bx = pair(bxw)
    dtx = pair(dtw)
    bm_rep = _concat_heads([bm[:, 0:HD]] * 3 + [bm[:, HD:2 * HD]] * 3)
    cm_rep = _concat_heads([cm[:, 0:HD]] * 3 + [cm[:, HD:2 * HD]] * 3)
    kh = bm_rep * dtx
    qt = cm_rep * jnp.exp(bx)
    blast = bx[CB - 1:CB, :] if prompt else _seq_last_rows(bx)
    kd = kh * jnp.exp(blast - bx)
    kdt = kd.T
    mask = tril > 0

    def scores(h):
        sl = slice(HD * h, HD * (h + 1))
        bcol = bxw[h]
        decay = jnp.exp(jnp.where(mask, bcol - bcol.T, -1e30))
        return _bdot_nt(cm_rep[:, sl], kh[:, sl]) * decay

    parts = []
    if prompt:
        ri = lax.broadcasted_iota(jnp.int32, (2 * HD, 2 * HD), 0)
        ci = lax.broadcasted_iota(jnp.int32, (2 * HD, 2 * HD), 1)
        same_head = (ri // HD) == (ci // HD)
        for j in range(NH // 2):
            pr = slice(2 * HD * j, 2 * HD * (j + 1))
            xp = xs[:, pr]
            xdiag = jnp.concatenate([jnp.where(lane < HD, xp, 0.0), jnp.where(lane >= HD, xp, 0.0)], axis=0)
            st = st_scr[j]
            parts.append(_bdot(jnp.concatenate([scores(2 * j), scores(2 * j + 1)], axis=1), xdiag)
                         + _bdot(qt[:, pr], st))
            st_scr[j] = st * jnp.exp(blast[:, pr]) + jnp.where(same_head, _bdot(kdt[pr, :], xp), 0.0)
    else:
        for h in range(NH):
            sl = slice(HD * h, HD * (h + 1))
            sts = sss_in[seqs, h].reshape(SEQ_PER_CB * HD, HD)
            parts.append(_bdot(scores(h), xs[:, sl]) + _bdot(_seq_expand_lanes(qt[:, sl]), sts))
            dec3 = jnp.exp(blast[:, sl]).reshape(SEQ_PER_CB, L_S, HD)[:, L_S - 1:L_S, :]
            dec_rows = jnp.broadcast_to(dec3, (SEQ_PER_CB, HD, HD)).reshape(SEQ_PER_CB * HD, HD)
            upd = _bdot(_seq_expand_rows(kdt[sl, :]), xs[:, sl])
            sss_out[seqs, h] = (sts * dec_rows + upd).reshape(SEQ_PER_CB, HD, HD)
    if is_last is not None:
        @pl.when(is_last)
        def _():
            ssp_out[0] = st_scr[...]
            convp_out[0] = cbuf[0:8, :]
    y = (_concat_heads(parts) + dx_ref[...] * xs) * _silu(z)
    ms = _split_dot(y * y, g192_ref[...], 1) * (1.0 / (SSW // 2))
    om_ref[rows, :] = y * lax.rsqrt(ms + RMS_EPS) * sg_ref[...]


def _ssd_body(p_ref, cw_ref, cb_ref, dtb_ref, aneg_ref, dx_ref, sg_ref, g192_ref,
              trilp_ref, trils_ref, hist_ref, sss_in, *refs):
    om_ref, ssp_out, sss_out, convp_out, convs_out, cbuf, st_scr = refs[1:]
    i = pl.program_id(0)
    common = (p_ref, cw_ref, cb_ref, dtb_ref, aneg_ref, dx_ref, sg_ref, g192_ref)
    npc, steps_per_seq = NPC // SSD_SUB, CH_PER_SEQ // SSD_SUB

    @pl.when((i < npc) & (i % steps_per_seq == 0))
    def _():
        st_scr[...] = jnp.zeros_like(st_scr)
        cbuf[...] = jnp.zeros_like(cbuf)

    @pl.when(i < npc)
    def _():
        for sub in range(SSD_SUB):
            is_last = (i % steps_per_seq == steps_per_seq - 1) if sub == SSD_SUB - 1 else None
            _ssd_block(True, *common, trilp_ref, om_ref, cbuf, st_scr, sub, is_last=is_last,
                       ssp_out=ssp_out, convp_out=convp_out)

    @pl.when(i >= npc)
    def _():
        for sub in range(SSD_SUB):
            _ssd_block(False, *common, trils_ref, om_ref, cbuf, st_scr, sub, hist_ref=hist_ref,
                       sss_in=sss_in, sss_out=sss_out, convs_out=convs_out)


SSD_SUB = 2


def _ssd(pm, prm, consts, hist, st, layer, earlier):
    sample_rows = pl.BlockSpec((SSD_SUB * CB, CONVC),
                               lambda i: (jnp.clip(i - NPC // SSD_SUB, 0, NSC // SSD_SUB - 1), 0))
    in_specs = [
        _mixer_blk(P_SS, SSD_SUB), _const_spec((4, CONVC)), _const_spec((1, CONVC)),
        _const_spec((1, 128)), _const_spec((1, 128)), _const_spec((1, SSW)), _const_spec((1, SSW)),
        _const_spec((SSW, SSW)),
        _const_spec((CB, CB)), _const_spec((CB, CB)),
        sample_rows,
        _sstate_spec((NH, HD, HD), SSD_SUB),
    ]
    args = [pm, prm["cw"], prm["cb"], prm["dtb"], prm["aneg"], prm["dx"], prm["sg"],
            consts["g192"], consts["trilp"], consts["trils"], hist, st]
    in_specs.append(pl.BlockSpec(memory_space=pl.ANY))
    aliases = {len(args): 2}
    args.append(earlier)
    layer_slab = pl.BlockSpec((None, SSD_SUB * SEQ_PER_CB, NH, HD, HD),
                              lambda i: (layer, jnp.clip(i - NPC // SSD_SUB, 0, NSC // SSD_SUB - 1), 0, 0, 0))
    return pl.pallas_call(
        _ssd_body,
        grid=((NPC + NSC) // SSD_SUB,),
        in_specs=in_specs,
        out_specs=[_mixer_blk(SSW, SSD_SUB), _pstate_spec((NH // 2, 2 * HD, 2 * HD), SSD_SUB),
                   layer_slab, _pstate_spec((8, CONVC), SSD_SUB), sample_rows],
        out_shape=[
            jax.ShapeDtypeStruct((ROWS, SSW), F32),
            jax.ShapeDtypeStruct((NB_P, NH // 2, 2 * HD, 2 * HD), F32),
            jax.ShapeDtypeStruct((DEPTH, NB_S, NH, HD, HD), F32),
            jax.ShapeDtypeStruct((NB_P, 8, CONVC), F32),
            jax.ShapeDtypeStruct((ROWS_S, CONVC), F32),
        ],
        scratch_shapes=[pltpu.VMEM((CB + 8, CONVC), F32), pltpu.VMEM((NH // 2, 2 * HD, 2 * HD), F32)],
        input_output_aliases=aliases,
        compiler_params=_cparams(("arbitrary",)),
        name="ssd",
    )(*args)


def _o_core(split_x, i, oh_ref, osp_ref, oss_ref, om_ref, *refs):
    if split_x:
        x = jnp.where(i < NPT, refs[0][...], refs[1][...])
        refs = refs[2:]
    else:
        x = refs[0][...]
        refs = refs[1:]
    wo_ref, gp, gs, lng_ref, lnb_ref, scp, scs, shp, shs = refs[:9]
    os_ = jnp.where(i < NPT, osp_ref[...], oss_ref[...])
    mix = (_bdot(oh_ref[...], wo_ref[0:HW, :]) + _bdot(os_, wo_ref[HW:HW + S5W, :])
           + _bdot(om_ref[...], wo_ref[HW + S5W:D, :]))
    x1 = _layer_norm(ALPHA * x + _rowmod(i, gp, gs) * mix, lng_ref[...], lnb_ref[...])
    h2 = x1 * (1.0 + _rowmod(i, scp, scs)) + _rowmod(i, shp, shs)
    return x1, h2, refs[9:]


def _o_body(split_x, *refs):
    i = pl.program_id(0)
    x1, h2, (x1_ref, h2_ref) = _o_core(split_x, i, *refs)
    x1_ref[...] = x1
    h2_ref[...] = h2.astype(BF16)


def _o_router_body(split_x, *refs):
    i = pl.program_id(0)
    x1, h2, (wr_ref, br_ref, x1_ref, h2_ref, route_ref) = _o_core(split_x, i, *refs)
    x1_ref[...] = x1
    h2_ref[...] = h2
    h_hi = h2.astype(BF16)
    h_lo = (h2 - h_hi.astype(F32)).astype(BF16)
    logits = (jnp.dot(h_hi, wr_ref[0], preferred_element_type=F32)
              + jnp.dot(h_lo, wr_ref[0], preferred_element_type=F32)
              + jnp.dot(h_hi, wr_ref[1], preferred_element_type=F32)) + br_ref[...]
    lane = lax.broadcasted_iota(jnp.int32, (TM, 128), 1).astype(F32)
    neg = -jnp.inf
    lg = jnp.where(lane < NEXP, logits, neg)
    m1 = jnp.max(lg, axis=-1, keepdims=True)
    i1 = jnp.min(jnp.where(lg == m1, lane, 128.0), axis=-1, keepdims=True)
    lg2 = jnp.where(lane == i1, neg, lg)
    m2 = jnp.max(lg2, axis=-1, keepdims=True)
    i2 = jnp.min(jnp.where(lg2 == m2, lane, 128.0), axis=-1, keepdims=True)
    e2 = jnp.exp(m2 - m1)
    den = 1.0 + e2
    route_ref[...] = jnp.where(lane == 0.0, i1, jnp.where(lane == 1.0, i2,
                               jnp.where(lane == 2.0, 1.0 / den, jnp.where(lane == 3.0, e2 / den, 0.0))))


def _stage_o(layer, oh, os_p, os_s, om, x, wo_b, ada4, ln_g, ln_b, router=None):
    split_x = isinstance(x, tuple)
    x_specs = [_prompt_rows_spec(D), _sample_rows_spec(D)] if split_x else [_row_spec(D)]
    in_specs = [
        _row_spec(HW), _prompt_rows_spec(S5W), _sample_rows_spec(S5W), _row_spec(SSW), *x_specs,
        _const_spec((D, D)),
        *_mod_specs(layer, 2), _const_spec((1, D)), _const_spec((1, D)),
        *_mod_specs(layer, 4), *_mod_specs(layer, 3),
    ]
    args = [oh, os_p, os_s, om, *(x if split_x else (x,)), wo_b, ada4, ada4, ln_g, ln_b, ada4, ada4, ada4, ada4]
    out_specs = [_row_spec(D), _row_spec(D)]
    out_shape = [jax.ShapeDtypeStruct((ROWS, D), F32), jax.ShapeDtypeStruct((ROWS, D), BF16)]
    body = _o_body
    if router is not None:
        in_specs += [_const_spec((2, D, 128)), _const_spec((1, 128))]
        args += list(router)
        out_specs.append(_row_spec(128))
        out_shape[1] = jax.ShapeDtypeStruct((ROWS, D), F32)
        out_shape.append(jax.ShapeDtypeStruct((ROWS, 128), F32))
        body = _o_router_body
    return pl.pallas_call(
        functools.partial(body, split_x), grid=(NT,), in_specs=in_specs, out_specs=out_specs,
        out_shape=out_shape,
        compiler_params=_cparams(("parallel",)),
        name="stage_o_router" if router is not None else "stage_o",
    )(*args)


def _ffn_body(te_ref, nu_ref, h_ref, wg_ref, wu_ref, wd_ref, o_ref, acc_ref):
    i = pl.program_id(0)
    j = pl.program_id(1)

    @pl.when(j == 0)
    def _():
        acc_ref[...] = jnp.zeros_like(acc_ref)

    @pl.when(i < nu_ref[0])
    def _():
        h = h_ref[...].astype(BF16)
        g = jnp.dot(h, wg_ref[...], preferred_element_type=F32)
        u = jnp.dot(h, wu_ref[...], preferred_element_type=F32)
        act = (_silu(g) * u).astype(BF16)
        acc_ref[...] += jnp.dot(act, wd_ref[...], preferred_element_type=F32)

    @pl.when(j == pl.num_programs(1) - 1)
    def _():
        o_ref[...] = acc_ref[...]


def _ffn(tile_expert, n_used, h, wg, wu, wd):
    rows = h.shape[0]
    nj = DFF // TF

    def jblk(i, j, nu):
        return jnp.where(i < nu[0], j, nj - 1)

    grid_spec = pltpu.PrefetchScalarGridSpec(
        num_scalar_prefetch=2,
        grid=(rows // TMF, nj),
        in_specs=[
            pl.BlockSpec((TMF, D), lambda i, j, te, nu: (i, 0)),
            pl.BlockSpec((None, D, TF), lambda i, j, te, nu: (te[i], 0, jblk(i, j, nu))),
            pl.BlockSpec((None, D, TF), lambda i, j, te, nu: (te[i], 0, jblk(i, j, nu))),
            pl.BlockSpec((None, TF, D), lambda i, j, te, nu: (te[i], jblk(i, j, nu), 0)),
        ],
        out_specs=pl.BlockSpec((TMF, D), lambda i, j, te, nu: (i, 0)),
        scratch_shapes=[pltpu.VMEM((TMF, D), F32)],
    )
    return pl.pallas_call(
        _ffn_body, grid_spec=grid_spec,
        out_shape=jax.ShapeDtypeStruct((rows, D), F32),
        compiler_params=_cparams(("parallel", "arbitrary")),
        name="ffn",
    )(tile_expert, n_used, h, wg, wu, wd)


def _final_body(x1_ref, ya_ref, yb_ref, route_ref, gp, gs, lng_ref, lnb_ref, yp_ref, ys_ref):
    i = pl.program_id(0)
    f = route_ref[:, 2:3] * ya_ref[...] + route_ref[:, 3:4] * yb_ref[...]
    y = _layer_norm(ALPHA * x1_ref[...] + _rowmod(i, gp, gs) * f, lng_ref[...], lnb_ref[...])

    @pl.when(i < NPT)
    def _():
        yp_ref[...] = y

    @pl.when(i >= NPT)
    def _():
        ys_ref[...] = y


def _stage_final(layer, x1, ya, yb, route, ada4, ln_g, ln_b):
    return pl.pallas_call(
        _final_body,
        grid=(NT,),
        in_specs=[
            _row_spec(D), _row_spec(D), _row_spec(D), _row_spec(128),
            *_mod_specs(layer, 5), _const_spec((1, D)), _const_spec((1, D)),
        ],
        out_specs=[
            pl.BlockSpec((TM, D), lambda i: (jnp.minimum(i, NPT - 1), 0)),
            pl.BlockSpec((TM, D), lambda i: (jnp.clip(i - NPT, 0, NST - 1), 0)),
        ],
        out_shape=[jax.ShapeDtypeStruct((ROWS_P, D), F32), jax.ShapeDtypeStruct((ROWS_S, D), F32)],
        compiler_params=_cparams(("arbitrary",)),
        name="stage_final",
    )(x1, ya, yb, route, ada4, ada4, ln_g, ln_b)


def _block_ones(n, blk):
    r = np.arange(n) // blk
    return r[:, None] == r[None, :]


def _consts():
    t = np.arange(CB)
    causal = t[:, None] >= t[None, :]
    same_seq = (t[:, None] // L_S) == (t[None, :] // L_S)
    r = np.arange(S5_ROWS)
    perm = r[None, :] == ((r % NB_P) * S5_TB + r // NB_P)[:, None]
    mats = {
        "perm": perm,
        "permt": perm.T,
        "e64": _block_ones(HW, HD),
        "g192": _block_ones(SSW, SSW // 2),
        "trilp": causal,
        "trils": causal & same_seq,
    }
    return {k: jnp.asarray(v.astype(np.float32), dtype=BF16) for k, v in mats.items()}


def _s5_params(a_re, a_im, log_dt, b_re, b_im, c_re, c_im, d, w_glu, b_glu):
    dt = jnp.exp(log_dt)[:, None]
    mag = jnp.exp(a_re * dt)
    lam_re, lam_im = mag * jnp.cos(a_im * dt), mag * jnp.sin(a_im * dt)
    den = a_re * a_re + a_im * a_im
    nr, ni = lam_re - 1.0, lam_im
    zr = (nr * a_re + ni * a_im) / den
    zi = (ni * a_re - nr * a_im) / den
    bbar_re = zr[..., None] * b_re - zi[..., None] * b_im
    bbar_im = zr[..., None] * b_im + zi[..., None] * b_re
    eye = jnp.eye(16, dtype=F32)
    blk = lambda bb: jnp.einsum('gph,gk->ghkp', bb, eye).reshape(S5W, S5N)
    hu, hs = S5W // 2, S5N // 2
    bblk = jnp.stack([jnp.stack([blk(bb)[hu * j:hu * (j + 1), hs * j:hs * (j + 1)] for j in range(2)])
                      for bb in (bbar_re, bbar_im)]).astype(BF16)
    cblk = lambda cc: jnp.einsum('ghp,gk->gpkh', cc, eye).reshape(S5N, S5W)
    ccat = jnp.stack([jnp.stack([cblk(cc)[hs * j:hs * (j + 1), hu * j:hu * (j + 1)] for cc in (c_re, -c_im)])
                      for j in range(2)]).astype(BF16)
    lr, li = lam_re.reshape(-1), lam_im.reshape(-1)
    pows = [(jnp.ones_like(lr), jnp.zeros_like(li))]
    for _ in range(8):
        pr, pi = pows[-1]
        pows.append((pr * lr - pi * li, pr * li + pi * lr))
    rows = jnp.arange(8)[:, None]
    tabs = []
    for dsh in (1, 2, 4):
        pr, pi = pows[dsh]
        tabs.append(jnp.where(rows >= dsh, jnp.concatenate([pr, pi])[None, :], 0.0))
    tabs.append(jnp.stack([jnp.concatenate(pows[r + 1]) for r in range(8)]))
    return {
        "tab": jnp.stack(tabs), "bblk": bblk, "ccat": ccat,
        "lam8": jnp.broadcast_to(jnp.concatenate([lr, li])[None, :], (NB_P, 2 * S5N)),
        "d": d.reshape(1, S5W), "wglu": w_glu.astype(BF16), "bglu": b_glu.reshape(1, S5W),
    }


def _pair_diag(st):
    s6 = st.reshape(NB_P, NH // 2, 2, HD, 2, HD)
    return jnp.stack([s6[:, :, 0, :, 0, :], s6[:, :, 1, :, 1, :]], axis=2).reshape(NB_P, NH, HD, HD)


def _pad_lanes(v, n=128):
    return jnp.pad(v, (0, n - v.shape[0])).reshape(1, n)


def kernel(x_prompt, x_sample, c_prompt, c_sample, state_hgrn, state_s5, state_ssd, state_ssd_conv, w_ada, b_ada, ln_g, ln_b, w_in, w_out, hgrn_lb_logits, hgrn_norm_g, s5_a_re, s5_a_im, s5_log_dt, s5_b_re, s5_b_im, s5_c_re, s5_c_im, s5_d, s5_w_glu, s5_b_glu, ssd_conv_w, ssd_conv_b, ssd_dt_bias, ssd_a_log, ssd_d, ssd_norm_g, ffn_w_gate, ffn_w_up, ffn_w_down, moe_w_router, moe_b_router, moe_w_gate, moe_w_up, moe_w_down):
    consts = _consts()
    c_all = jnp.concatenate([c_sample, c_prompt], axis=0)
    ada4 = _ada(c_all, w_ada, b_ada).reshape(DEPTH, NB_S + NB_P, 1, 6 * D)

    lb_all = jnp.cumsum(jax.nn.softmax(hgrn_lb_logits, axis=0), axis=0)
    lb_all = lb_all - lb_all[0]

    xp = x_prompt.reshape(ROWS_P, D)
    xs = x_sample.reshape(ROWS_S, D)
    new_h, new_s5, new_m, new_c = [], [], [], []
    x1 = f = route = None
    ss_s = jnp.zeros((DEPTH, NB_S, NH, HD, HD), F32)
    for l in range(DEPTH):
        w_in_b = jnp.pad(w_in[l], ((0, 0), (0, N_IN_PAD - N_IN))).astype(BF16)
        if l == 0:
            x = (xp, xs)
            ph, ps_p, ps_s, pm = _stage_a0(xp, xs, ada4, w_in_b)
        else:
            x, ph, ps_p, ps_s, pm = _stage_a1(l, x1, f, ada4, ln_g[l - 1, 1].reshape(1, D),
                                              ln_b[l - 1, 1].reshape(1, D), w_in_b)
        oh, hg_p, hg_s = _hgrn(ph, lb_all[l].reshape(1, HW), hgrn_norm_g[l].reshape(1, HW), consts,
                               jnp.swapaxes(state_hgrn[l], -1, -2))
        s5p = _s5_params(s5_a_re[l], s5_a_im[l], s5_log_dt[l], s5_b_re[l], s5_b_im[l], s5_c_re[l], s5_c_im[l],
                         s5_d[l], s5_w_glu[l], s5_b_glu[l])
        os_p, s5_p = _s5_prompt(ps_p.reshape(NB_P, L_P, S5W), s5p, consts)
        os_s, s5_s = _s5_sample(ps_s, s5p, state_s5[l].reshape(NB_S, 1, 2 * S5N))
        os_p = os_p.reshape(ROWS_P, S5W)
        ssd_prm = {
            "cw": ssd_conv_w[l], "cb": ssd_conv_b[l].reshape(1, CONVC),
            "dtb": _pad_lanes(ssd_dt_bias[l]), "aneg": _pad_lanes(-jnp.exp(ssd_a_log[l])),
            "dx": jnp.repeat(ssd_d[l], HD).reshape(1, SSW), "sg": ssd_norm_g[l].reshape(1, SSW),
        }
        hist = jnp.pad(state_ssd_conv[l], ((0, 0), (0, L_S - 3), (0, 0))).reshape(ROWS_S, CONVC)
        om, ss_p, ss_s, tail_p, xbc_s = _ssd(pm, ssd_prm, consts, hist, state_ssd[l], l, ss_s)
        conv_p = tail_p[:, 8 - 3:]
        conv_s = xbc_s.reshape(NB_S, L_S, CONVC)[:, L_S - 3:]
        hg_p, ss_p = _pair_diag(hg_p), _pair_diag(ss_p)
        new_h.append((jnp.swapaxes(hg_p, -1, -2), jnp.swapaxes(hg_s, -1, -2)))
        new_s5.append((s5_p.reshape(NB_P, 2, 16, 64), s5_s.reshape(NB_S, 2, 16, 64)))
        new_m.append((ss_p,))
        new_c.append((conv_p, conv_s))

        wo_b = w_out[l].astype(BF16)
        lg, lbb = ln_g[l, 0].reshape(1, D), ln_b[l, 0].reshape(1, D)
        j = l // 2
        if l % 2 == 0:
            x1, h2 = _stage_o(l, oh, os_p, os_s, om, x, wo_b, ada4, lg, lbb)
            f = _ffn(jnp.zeros((ROWS // TMF,), jnp.int32), jnp.full((1,), ROWS // TMF, jnp.int32), h2,
                     ffn_w_gate[j:j + 1].astype(BF16),
                     ffn_w_up[j:j + 1].astype(BF16), ffn_w_down[j:j + 1].astype(BF16))
        else:
            wr = jnp.pad(moe_w_router[j], ((0, 0), (0, 128 - NEXP)))
            wr_hi = wr.astype(BF16)
            wr = jnp.stack([wr_hi, (wr - wr_hi.astype(F32)).astype(BF16)])
            br = _pad_lanes(moe_b_router[j])
            x1, h2, route = _stage_o(l, oh, os_p, os_s, om, x, wo_b, ada4, lg, lbb, router=(wr, br))
            wg_b, wu_b, wd_b = (moe_w_gate[j].astype(BF16), moe_w_up[j].astype(BF16),
                                moe_w_down[j].astype(BF16))
            flat_e = route[:, 0:2].astype(jnp.int32).reshape(-1)
            onehot = (flat_e[:, None] == jnp.arange(NEXP)[None, :]).astype(jnp.int32)
            csum = jnp.cumsum(onehot, axis=0)
            counts = csum[-1]
            rank = jnp.take_along_axis(csum, flat_e[:, None], axis=1)[:, 0] - 1
            padded = ((counts + TMF - 1) // TMF) * TMF
            pend = jnp.cumsum(padded)
            pstart = pend - padded
            dest = pstart[flat_e] + rank
            n_pad = 2 * ROWS + NEXP * TMF
            src_tok = jnp.zeros((n_pad,), jnp.int32).at[dest].set(jnp.arange(2 * ROWS, dtype=jnp.int32) // 2)
            tile_start = jnp.arange(n_pad // TMF, dtype=jnp.int32) * TMF
            tile_e = jnp.minimum(jnp.sum((pend[None, :] <= tile_start[:, None]).astype(jnp.int32), axis=1),
                                 NEXP - 1)
            n_used = (pend[NEXP - 1:NEXP] // TMF).astype(jnp.int32)
            y_sorted = _ffn(tile_e, n_used, h2[src_tok], wg_b, wu_b, wd_b)
            pos = dest.reshape(ROWS, 2)
            ya = y_sorted[pos[:, 0]]
            yb = y_sorted[pos[:, 1]]
    y_p, y_s = _stage_final(DEPTH - 1, x1, ya, yb, route, ada4, ln_g[DEPTH - 1, 1].reshape(1, D),
                            ln_b[DEPTH - 1, 1].reshape(1, D))
    stack = lambda lst, k: jnp.stack([t[k] for t in lst])
    return (y_p.reshape(NB_P, L_P, D), y_s.reshape(NB_S, L_S, D),
            stack(new_h, 0), stack(new_s5, 0), stack(new_m, 0), stack(new_c, 0),
            stack(new_h, 1), stack(new_s5, 1), ss_s, stack(new_c, 1))
```

```python
import functools
import math

import jax
import jax.numpy as jnp
import numpy as np
from jax import lax
from jax.experimental import pallas as pl
from jax.experimental.pallas import tpu as pltpu

F32 = jnp.float32
BF16 = jnp.bfloat16

D = 1024
NB_P, L_P = 8, 2048
NB_S, L_S = 128, 8
ROWS_P = NB_P * L_P
ROWS_S = NB_S * L_S
ROWS = ROWS_P + ROWS_S
DEPTH = 2
HW = 384
S5W = 256
SSW = 384
NH = 6
HD = 64
S5N = 1024
CONVC = 640
N_IN = 2822
N_IN_PAD = 2944
P_HG = 1536
P_SS = 1152
DFF = 2816
NEXP = 8
ALPHA = (2 * DEPTH) ** 0.25
LN_EPS = 1e-5
RMS_EPS = 1e-6

TM = 512
NPT = ROWS_P // TM
NST = ROWS_S // TM
NT = NPT + NST
SEQ_PER_TILE = TM // L_S
TILES_PER_SEQ = L_P // TM

CB = 128
NPC = ROWS_P // CB
NSC = ROWS_S // CB
CH_PER_SEQ = L_P // CB
SEQ_PER_CB = CB // L_S

TMF = 512
TF = 1408
MOE_PARTS = 4
VMEM_LIMIT = 56 * 1024 * 1024


def _cparams(sem):
    return pltpu.CompilerParams(dimension_semantics=sem, vmem_limit_bytes=VMEM_LIMIT)


def _bdot(a, b):
    return jnp.dot(a.astype(BF16), b.astype(BF16), preferred_element_type=F32)


def _bdot_nt(a, b):
    return lax.dot_general(a.astype(BF16), b.astype(BF16), (((1,), (1,)), ((), ())),
                           preferred_element_type=F32)


def _split_dot(x, e, passes):
    acc = None
    r = x
    for _ in range(passes):
        hi = r.astype(BF16)
        d = jnp.dot(hi, e, preferred_element_type=F32)
        acc = d if acc is None else acc + d
        r = r - hi.astype(F32)
    return acc


def _split_dot_l(e, x, passes):
    acc = None
    r = x
    for _ in range(passes):
        hi = r.astype(BF16)
        d = jnp.dot(e, hi, preferred_element_type=F32)
        acc = d if acc is None else acc + d
        r = r - hi.astype(F32)
    return acc


def _silu(x):
    return x * jax.nn.sigmoid(x)


def _layer_norm(x, g, b):
    mu = jnp.mean(x, -1, keepdims=True)
    xc = x - mu
    var = jnp.mean(xc * xc, -1, keepdims=True)
    return xc * lax.rsqrt(var + LN_EPS) * g + b


def _rowmod(i, p_ref, s_ref):
    s = jnp.broadcast_to(s_ref[...], (SEQ_PER_TILE, L_S, D)).reshape(TM, D)
    return jnp.where(i < NPT, p_ref[0], s)


ADA_TN = 1536


def _ada_body(c_ref, w_ref, b_ref, o_ref):
    o_ref[...] = _bdot(_silu(c_ref[...]), w_ref[...]) + b_ref[...]


def _ada(c_all, w_ada, b_ada):
    nc = c_all.shape[0]
    return pl.pallas_call(
        _ada_body,
        grid=(DEPTH, 6 * D // ADA_TN),
        in_specs=[
            pl.BlockSpec((nc, D), lambda l, j: (0, 0)),
            pl.BlockSpec((None, D, ADA_TN), lambda l, j: (l, 0, j)),
            pl.BlockSpec((None, 1, ADA_TN), lambda l, j: (l, 0, j)),
        ],
        out_specs=pl.BlockSpec((None, nc, ADA_TN), lambda l, j: (l, 0, j)),
        out_shape=jax.ShapeDtypeStruct((DEPTH, nc, 6 * D), F32),
        compiler_params=_cparams(("parallel", "parallel")),
        name="ada",
    )(c_all, w_ada, b_ada.reshape(DEPTH, 1, 6 * D))


def _mod_specs(layer, k):
    ps = pl.BlockSpec((None, 1, 1, D),
                      lambda i: (layer, NB_S + jnp.minimum(i // TILES_PER_SEQ, NB_P - 1), 0, k))
    ss = pl.BlockSpec((None, SEQ_PER_TILE, 1, D),
                      lambda i: (layer, jnp.clip(i - NPT, 0, NST - 1), 0, k))
    return [ps, ss]


def _row_spec(width):
    return pl.BlockSpec((TM, width), lambda i: (i, 0))


def _const_spec(shape):
    nd = len(shape)
    return pl.BlockSpec(shape, lambda *_: (0,) * nd)


def _prompt_rows_spec(width):
    return pl.BlockSpec((TM, width), lambda i: (jnp.minimum(i, NPT - 1), 0))


def _sample_rows_spec(width):
    return pl.BlockSpec((TM, width), lambda i: (jnp.clip(i - NPT, 0, NST - 1), 0))


def _proj_out(x, i, scp, scs, shp, shs, w_ref, ph_ref, psp_ref, pss_ref, pm_ref):
    h = x * (1.0 + _rowmod(i, scp, scs)) + _rowmod(i, shp, shs)
    proj = jnp.dot(h.astype(BF16), w_ref[...], preferred_element_type=F32)
    ph_ref[...] = proj[:, 0:P_HG]
    pm_ref[...] = proj[:, P_HG + S5W:N_IN_PAD]

    @pl.when(i < NPT)
    def _():
        psp_ref[...] = proj[:, P_HG:P_HG + S5W]

    @pl.when(i >= NPT)
    def _():
        pss_ref[...] = proj[:, P_HG:P_HG + S5W]


def _a0_body(xp_ref, xs_ref, scp, scs, shp, shs, w_ref, ph_ref, psp_ref, pss_ref, pm_ref):
    i = pl.program_id(0)
    x = jnp.where(i < NPT, xp_ref[...], xs_ref[...])
    _proj_out(x, i, scp, scs, shp, shs, w_ref, ph_ref, psp_ref, pss_ref, pm_ref)


def _a1_body(x1_ref, f_ref, gp, gs, lng_ref, lnb_ref, scp, scs, shp, shs, w_ref,
             x_ref, ph_ref, psp_ref, pss_ref, pm_ref):
    i = pl.program_id(0)
    x = _layer_norm(ALPHA * x1_ref[...] + _rowmod(i, gp, gs) * f_ref[...], lng_ref[...], lnb_ref[...])
    x_ref[...] = x
    _proj_out(x, i, scp, scs, shp, shs, w_ref, ph_ref, psp_ref, pss_ref, pm_ref)


def _a_out():
    specs = [_row_spec(P_HG), _prompt_rows_spec(S5W), _sample_rows_spec(S5W), _row_spec(P_SS)]
    shapes = [jax.ShapeDtypeStruct((ROWS, P_HG), F32), jax.ShapeDtypeStruct((ROWS_P, S5W), F32),
              jax.ShapeDtypeStruct((ROWS_S, S5W), F32), jax.ShapeDtypeStruct((ROWS, P_SS), F32)]
    return specs, shapes


def _stage_a0(xp, xs, ada4, w_in_b):
    out_specs, out_shape = _a_out()
    return pl.pallas_call(
        _a0_body,
        grid=(NT,),
        in_specs=[
            _prompt_rows_spec(D), _sample_rows_spec(D),
            *_mod_specs(0, 1), *_mod_specs(0, 0),
            _const_spec((D, N_IN_PAD)),
        ],
        out_specs=out_specs, out_shape=out_shape,
        compiler_params=_cparams(("arbitrary",)),
        name="stage_a0",
    )(xp, xs, ada4, ada4, ada4, ada4, w_in_b)


def _stage_a1(layer, x1, f, ada4, ln_g, ln_b, w_in_b):
    out_specs, out_shape = _a_out()
    return pl.pallas_call(
        _a1_body,
        grid=(NT,),
        in_specs=[
            _row_spec(D), _row_spec(D),
            *_mod_specs(layer - 1, 5),
            _const_spec((1, D)), _const_spec((1, D)),
            *_mod_specs(layer, 1), *_mod_specs(layer, 0),
            _const_spec((D, N_IN_PAD)),
        ],
        out_specs=[_row_spec(D)] + out_specs, out_shape=[jax.ShapeDtypeStruct((ROWS, D), F32)] + out_shape,
        compiler_params=_cparams(("arbitrary",)),
        name="stage_a1",
    )(x1, f, ada4, ada4, ln_g, ln_b, ada4, ada4, ada4, ada4, w_in_b)


def _mixer_blk(width, nsub=1):
    return pl.BlockSpec((nsub * CB, width), lambda i: (i, 0))


def _pstate_spec(shape, nsub=1):
    nd = len(shape)
    return pl.BlockSpec((1,) + shape,
                        lambda i: (jnp.minimum(i // (CH_PER_SEQ // nsub), NB_P - 1),) + (0,) * nd)


def _sstate_spec(shape, nsub=1):
    nd = len(shape)
    return pl.BlockSpec((nsub * SEQ_PER_CB,) + shape,
                        lambda i: (jnp.clip(i - NPC // nsub, 0, NSC // nsub - 1),) + (0,) * nd)


def _cumsum_rows(x, span):
    r = lax.broadcasted_iota(jnp.int32, x.shape, 0) & (span - 1)
    d = 1
    while d < span:
        x = x + jnp.where(r >= d, pltpu.roll(x, d, 0), 0.0)
        d *= 2
    return x


def _seq_last_rows(x):
    w = x.shape[-1]
    x3 = x.reshape(SEQ_PER_CB, L_S, w)
    return jnp.broadcast_to(x3[:, L_S - 1:L_S, :], (SEQ_PER_CB, L_S, w)).reshape(CB, w)


def _concat_heads(parts):
    return jnp.concatenate(parts, axis=1)


def _stack_select(shape, row_div, lane_div):
    r = lax.broadcasted_iota(jnp.int32, shape, 0) // row_div
    c = lax.broadcasted_iota(jnp.int32, shape, 1) // lane_div
    return r == c


def _seq_expand_lanes(qh):
    q2 = jnp.concatenate([qh, qh], axis=1)
    q16 = jnp.concatenate([q2] * (SEQ_PER_CB // 2), axis=1)
    return jnp.where(_stack_select((CB, SEQ_PER_CB * HD), L_S, HD), q16, 0.0)


def _seq_expand_rows(xt):
    t = jnp.broadcast_to(xt[None], (SEQ_PER_CB, HD, CB)).reshape(SEQ_PER_CB * HD, CB)
    return jnp.where(_stack_select((SEQ_PER_CB * HD, CB), HD, L_S), t, 0.0)


def _fold_seq_lanes(full):
    acc = full[:, 0:128]
    for j in range(1, SEQ_PER_CB * HD // 128):
        acc = acc + full[:, 128 * j:128 * (j + 1)]
    return acc[:, 0:HD] + acc[:, HD:2 * HD]


HGRN_BASE = 32
EXP_RANGE_MAX = 80.0


def _hgrn_block(prompt, p_ref, lb_ref, hg_ref, e64_ref, oh_ref, st_scr, o_scr, sub, is_last=None,
                stp_out=None, sts_in=None, sts_out=None):
    rows = slice(sub * CB, (sub + 1) * CB)
    seqs = slice(sub * SEQ_PER_CB, (sub + 1) * SEQ_PER_CB)
    lb = lb_ref[...]
    qr = p_ref[rows, 0:HW]
    fr = p_ref[rows, HW:2 * HW]
    v = p_ref[rows, 2 * HW:3 * HW]
    gr = p_ref[rows, 3 * HW:4 * HW]
    e = jnp.exp(-jnp.abs(fr))
    ope = 1.0 + e
    ls = jnp.minimum(fr, 0.0) - jnp.log(ope)
    a = jnp.log(lb)
    bb = jnp.log1p(-lb) + ls
    lf = jnp.maximum(a, bb) + jnp.log(1.0 + jnp.exp(-jnp.abs(a - bb)))
    rcp = 1.0 / ope
    kk = (1.0 - lb) * jnp.where(fr >= 0.0, e * rcp, rcp)
    q = _silu(qr)
    b = _cumsum_rows(lf, CB if prompt else L_S)

    e64 = e64_ref[...]
    ti = lax.broadcasted_iota(jnp.int32, (CB, CB), 0)
    si = lax.broadcasted_iota(jnp.int32, (CB, CB), 1)

    def diag8():
        nsub = CB // 8
        b3 = b.reshape(nsub, 8, HW)
        q3 = q.reshape(nsub, 8, HW)
        k3 = kk.reshape(nsub, 8, HW)
        v3 = v.reshape(nsub, 8, HW)
        r3 = lax.broadcasted_iota(jnp.int32, (nsub, 8, HW), 1)
        o = jnp.zeros((CB, HW), F32)
        for s in range(8):
            dlt = jnp.minimum(b3 - b3[:, s:s + 1, :], 0.0)
            w = jnp.where(r3 >= s, jnp.exp(dlt), 0.0) * q3 * k3[:, s:s + 1, :]
            hsum = jnp.dot(w.reshape(CB, HW).astype(BF16), e64, preferred_element_type=F32)
            o = o + hsum * jnp.broadcast_to(v3[:, s:s + 1, :], (nsub, 8, HW)).reshape(CB, HW)
        return o

    def level_terms(m):
        terms = []
        while m < CB:
            nb = CB // (2 * m)
            b4 = b.reshape(nb, 2 * m, HW)
            bmid = b4[:, m - 1:m, :]
            pos = lax.broadcasted_iota(jnp.int32, (nb, 2 * m, HW), 1)
            qq = jnp.where(pos >= m, q.reshape(nb, 2 * m, HW) * jnp.exp(jnp.minimum(b4 - bmid, 0.0)), 0.0)
            kq = jnp.where(pos < m, kk.reshape(nb, 2 * m, HW) * jnp.exp(jnp.minimum(bmid - b4, 0.0)), 0.0)
            terms.append((qq.reshape(CB, HW), kq.reshape(CB, HW), (ti // (2 * m)) == (si // (2 * m))))
            m *= 2
        return terms

    def scores_times_v(terms):
        lane = lax.broadcasted_iota(jnp.int32, (CB, 2 * HD), 1)
        parts = []
        for j in range(NH // 2):
            scs = []
            for h in (2 * j, 2 * j + 1):
                sl = slice(HD * h, HD * (h + 1))
                sc = None
                for qq, kq, keep in terms:
                    t = jnp.where(keep, _bdot_nt(qq[:, sl], kq[:, sl]), 0.0)
                    sc = t if sc is None else sc + t
                scs.append(sc)
            vp = v[:, 2 * HD * j:2 * HD * (j + 1)]
            vdiag = jnp.concatenate([jnp.where(lane < HD, vp, 0.0), jnp.where(lane >= HD, vp, 0.0)], axis=0)
            parts.append(_bdot(jnp.concatenate(scs, axis=1), vdiag))
        return _concat_heads(parts)

    base = HGRN_BASE if prompt else L_S
    nbase = CB // base
    bb3 = b.reshape(nbase, base, HW)
    top = bb3[:, 0:1, :] - lf.reshape(nbase, base, HW)[:, 0:1, :]
    decay_range = jnp.max(top - bb3[:, base - 1:base, :])
    in_range = decay_range <= EXP_RANGE_MAX

    def exact_path():
        o_scr[rows, :] = diag8() + (scores_times_v(level_terms(8)) if prompt else 0.0)

    def finish(fast):
        _hgrn_finish(prompt, fast, rows, seqs, q, kk, v, gr, b, bb3, top, ti, si, scores_times_v, level_terms,
                     hg_ref, e64, oh_ref, st_scr, o_scr, is_last, stp_out, sts_in, sts_out)

    return in_range, exact_path, finish


def _hgrn_finish(prompt, fast, rows, seqs, q, kk, v, gr, b, bb3, top, ti, si, scores_times_v, level_terms,
                 hg_ref, e64, oh_ref, st_scr, o_scr, is_last, stp_out, sts_in, sts_out):
    base = HGRN_BASE if prompt else L_S
    nbase = CB // base
    qf = (q.reshape(nbase, base, HW) * jnp.exp(bb3 - top)).reshape(CB, HW)
    kf = (kk.reshape(nbase, base, HW) * jnp.exp(top - bb3)).reshape(CB, HW)
    keep = ((ti // base) == (si // base)) & (si <= ti)
    o_fast = scores_times_v([(qf, kf, keep)] + (level_terms(base) if prompt else []))
    o = jnp.where(fast, o_fast, o_scr[rows, :])
    qt = q * jnp.exp(b)
    if prompt:
        blast = b[CB - 1:CB, :]
        kd = kk * jnp.exp(blast - b)
        vt = v.T
        same_head = (ti // HD) == (si // HD)
        parts = []
        for j in range(NH // 2):
            pr = slice(2 * HD * j, 2 * HD * (j + 1))
            st = st_scr[j]
            parts.append(o[:, pr] + _bdot_nt(qt[:, pr], st))
            st_scr[j] = st * jnp.exp(blast[:, pr]) + jnp.where(same_head, _bdot(vt[pr, :], kd[:, pr]), 0.0)

        if is_last is not None:
            @pl.when(is_last)
            def _():
                stp_out[0] = st_scr[...]
    else:
        blast = _seq_last_rows(b)
        kd = kk * jnp.exp(blast - b)
        dec = jnp.exp(blast)
        vt = v.T
        parts = []
        for h in range(NH):
            sl = slice(HD * h, HD * (h + 1))
            sts = sts_in[seqs, h].reshape(SEQ_PER_CB * HD, HD)
            full = _bdot_nt(qt[:, sl], sts)
            sel = jnp.where(_stack_select((CB, SEQ_PER_CB * HD), L_S, HD), full, 0.0)
            parts.append(o[:, sl] + _fold_seq_lanes(sel))
            dec3 = dec[:, sl].reshape(SEQ_PER_CB, L_S, HD)[:, L_S - 1:L_S, :]
            dec_rows = jnp.broadcast_to(dec3, (SEQ_PER_CB, HD, HD)).reshape(SEQ_PER_CB * HD, HD)
            upd = _bdot(_seq_expand_rows(vt[sl, :]), kd[:, sl])
            sts_out[seqs, h] = (sts * dec_rows + upd).reshape(SEQ_PER_CB, HD, HD)
    oall = _concat_heads(parts)
    ms = _split_dot(oall * oall, e64, 1) * (1.0 / HD)
    oh_ref[rows, :] = oall * lax.rsqrt(ms + RMS_EPS) * hg_ref[...] * _silu(gr)


HGRN_SUB = 2


def _hgrn_body(p_ref, lb_ref, hg_ref, e64_ref, sts_in,
               oh_ref, stp_out, sts_out, st_scr, o_scr):
    i = pl.program_id(0)
    npc, steps_per_seq = NPC // HGRN_SUB, CH_PER_SEQ // HGRN_SUB

    @pl.when(i == 0)
    def _():
        o_scr[...] = jnp.zeros_like(o_scr)

    @pl.when((i < npc) & (i % steps_per_seq == 0))
    def _():
        st_scr[...] = jnp.zeros_like(st_scr)

    def run(blocks):
        fast = blocks[0][0]
        for in_range, _, _ in blocks[1:]:
            fast = jnp.logical_and(fast, in_range)

        @pl.when(jnp.logical_not(fast))
        def _():
            for _, exact_path, _ in blocks:
                exact_path()

        for _, _, finish in blocks:
            finish(fast)

    @pl.when(i < npc)
    def _():
        run([_hgrn_block(True, p_ref, lb_ref, hg_ref, e64_ref, oh_ref, st_scr, o_scr, sub,
                         is_last=(i % steps_per_seq == steps_per_seq - 1) if sub == HGRN_SUB - 1 else None,
                         stp_out=stp_out) for sub in range(HGRN_SUB)])

    @pl.when(i >= npc)
    def _():
        run([_hgrn_block(False, p_ref, lb_ref, hg_ref, e64_ref, oh_ref, st_scr, o_scr, sub,
                         sts_in=sts_in, sts_out=sts_out) for sub in range(HGRN_SUB)])


def _hgrn(ph, lb, hg, consts, st_t):
    return pl.pallas_call(
        _hgrn_body,
        grid=((NPC + NSC) // HGRN_SUB,),
        in_specs=[
            _mixer_blk(P_HG, HGRN_SUB), _const_spec((1, HW)), _const_spec((1, HW)),
            _const_spec((HW, HW)),
            _sstate_spec((NH, HD, HD), HGRN_SUB),
        ],
        out_specs=[_mixer_blk(HW, HGRN_SUB), _pstate_spec((NH // 2, 2 * HD, 2 * HD), HGRN_SUB),
                   _sstate_spec((NH, HD, HD), HGRN_SUB)],
        out_shape=[
            jax.ShapeDtypeStruct((ROWS, HW), F32),
            jax.ShapeDtypeStruct((NB_P, NH // 2, 2 * HD, 2 * HD), F32),
            jax.ShapeDtypeStruct((NB_S, NH, HD, HD), F32),
        ],
        scratch_shapes=[pltpu.VMEM((NH // 2, 2 * HD, 2 * HD), F32), pltpu.VMEM((HGRN_SUB * CB, HW), F32)],
        compiler_params=_cparams(("arbitrary",)),
        name="hgrn",
    )(ph, lb, hg, consts["e64"], st_t)


def _cmul_add(hr, hi, lr, li, sr, si):
    return hr + lr * sr - li * si, hi + lr * si + li * sr


def _s5_project(ub, bblk_ref):
    halves = [ub[:, (S5W // 2) * j:(S5W // 2) * (j + 1)] for j in range(2)]
    hr = jnp.concatenate([jnp.dot(halves[j], bblk_ref[0, j], preferred_element_type=F32) for j in range(2)], axis=1)
    hi = jnp.concatenate([jnp.dot(halves[j], bblk_ref[1, j], preferred_element_type=F32) for j in range(2)], axis=1)
    return hr, hi


def _s5_readout(h_scr, u, ccat_ref, d_ref, wglu_ref, bglu_ref):
    hs = S5N // 2
    ch = [_bdot(h_scr[:, hs * j:hs * (j + 1)], ccat_ref[j, 0])
          + _bdot(h_scr[:, S5N + hs * j:S5N + hs * (j + 1)], ccat_ref[j, 1]) for j in range(2)]
    y = jnp.concatenate(ch, axis=1) + d_ref[...] * u
    c0 = math.sqrt(2.0 / math.pi)
    y = y * (0.5 * (1.0 + jnp.tanh(c0 * (y + 0.044715 * (y * y * y)))))
    return y * jax.nn.sigmoid(_bdot(y, wglu_ref[...]) + bglu_ref[...])


S5_TB = 32
S5_ROWS = NB_P * S5_TB


def _s5_prompt_body(p_ref, perm_ref, permt_ref, lam_ref, bblk_ref, ccat_ref, d_ref, wglu_ref, bglu_ref,
                    os_ref, st_out, carry_scr, h_scr):
    i = pl.program_id(0)

    @pl.when(i == 0)
    def _():
        carry_scr[...] = jnp.zeros_like(carry_scr)

    u = p_ref[...].reshape(S5_ROWS, S5W)
    u_hi = u.astype(BF16)
    u_lo = (u - u_hi.astype(F32)).astype(BF16)
    perm = perm_ref[...]
    up_hi = jnp.dot(perm, u_hi, preferred_element_type=F32)
    up = up_hi + jnp.dot(perm, u_lo, preferred_element_type=F32)
    hr, hi = _s5_project(up_hi.astype(BF16), bblk_ref)
    lr = lam_ref[:, 0:S5N]
    li = lam_ref[:, S5N:2 * S5N]
    cr = carry_scr[:, 0:S5N]
    ci = carry_scr[:, S5N:2 * S5N]
    for t in range(S5_TB):
        rows = slice(NB_P * t, NB_P * (t + 1))
        cr, ci = _cmul_add(hr[rows], hi[rows], lr, li, cr, ci)
        h_scr[rows, 0:S5N] = cr
        h_scr[rows, S5N:2 * S5N] = ci
    carry_scr[:, 0:S5N] = cr
    carry_scr[:, S5N:2 * S5N] = ci
    out = _s5_readout(h_scr, up, ccat_ref, d_ref, wglu_ref, bglu_ref)
    os_ref[...] = jnp.dot(permt_ref[...], out.astype(BF16), preferred_element_type=F32).reshape(NB_P, S5_TB, S5W)

    @pl.when(i == pl.num_programs(0) - 1)
    def _():
        st_out[...] = carry_scr[...]


def _s5_prompt(ps3, prm, consts):
    return pl.pallas_call(
        _s5_prompt_body,
        grid=(L_P // S5_TB,),
        in_specs=[
            pl.BlockSpec((NB_P, S5_TB, S5W), lambda i: (0, i, 0)),
            _const_spec((S5_ROWS, S5_ROWS)), _const_spec((S5_ROWS, S5_ROWS)), _const_spec((NB_P, 2 * S5N)),
            _const_spec((2, 2, S5W // 2, S5N // 2)), _const_spec((2, 2, S5N // 2, S5W // 2)),
            _const_spec((1, S5W)), _const_spec((S5W, S5W)), _const_spec((1, S5W)),
        ],
        out_specs=[pl.BlockSpec((NB_P, S5_TB, S5W), lambda i: (0, i, 0)), _const_spec((NB_P, 2 * S5N))],
        out_shape=[jax.ShapeDtypeStruct((NB_P, L_P, S5W), F32), jax.ShapeDtypeStruct((NB_P, 2 * S5N), F32)],
        scratch_shapes=[pltpu.VMEM((NB_P, 2 * S5N), F32), pltpu.VMEM((S5_ROWS, 2 * S5N), F32)],
        compiler_params=_cparams(("arbitrary",)),
        name="s5_prompt",
    )(ps3, consts["perm"], consts["permt"], prm["lam8"], prm["bblk"], prm["ccat"], prm["d"], prm["wglu"],
      prm["bglu"])


def _s5_sample_body(p_ref, tab_ref, bblk_ref, ccat_ref, d_ref, wglu_ref, bglu_ref, s5s_in,
                    os_ref, s5s_out, h_scr):
    u = p_ref[...]
    hr, hi = _s5_project(u.astype(BF16), bblk_ref)
    nsub = CB // 8
    for idx, dsh in enumerate((1, 2, 4)):
        sr = pltpu.roll(hr, dsh, 0).reshape(nsub, 8, S5N)
        si = pltpu.roll(hi, dsh, 0).reshape(nsub, 8, S5N)
        lr = tab_ref[idx, :, 0:S5N][None]
        li = tab_ref[idx, :, S5N:2 * S5N][None]
        nr, ni = _cmul_add(hr.reshape(nsub, 8, S5N), hi.reshape(nsub, 8, S5N), lr, li, sr, si)
        hr = nr.reshape(CB, S5N)
        hi = ni.reshape(CB, S5N)
    tcr = tab_ref[3, :, 0:S5N]
    tci = tab_ref[3, :, S5N:2 * S5N]
    cr = s5s_in[:, :, 0:S5N]
    ci = s5s_in[:, :, S5N:2 * S5N]
    tr, tim = _cmul_add(hr.reshape(nsub, 8, S5N), hi.reshape(nsub, 8, S5N), tcr[None], tci[None], cr, ci)
    h_scr[:, 0:S5N] = tr.reshape(CB, S5N)
    h_scr[:, S5N:2 * S5N] = tim.reshape(CB, S5N)
    sb = lax.broadcasted_iota(jnp.int32, (SEQ_PER_CB, CB), 0)
    st = lax.broadcasted_iota(jnp.int32, (SEQ_PER_CB, CB), 1)
    sel = (st == L_S * sb + (L_S - 1)).astype(BF16)
    s5s_out[...] = _split_dot_l(sel, h_scr[...], 3)
    os_ref[...] = _s5_readout(h_scr, u, ccat_ref, d_ref, wglu_ref, bglu_ref)


def _s5_sample(ps_s, prm, st):
    seqs = lambda shape: pl.BlockSpec((SEQ_PER_CB,) + shape, lambda i: (i,) + (0,) * len(shape))
    return pl.pallas_call(
        _s5_sample_body,
        grid=(NSC,),
        in_specs=[
            _mixer_blk(S5W), _const_spec((4, 8, 2 * S5N)), _const_spec((2, 2, S5W // 2, S5N // 2)),
            _const_spec((2, 2, S5N // 2, S5W // 2)), _const_spec((1, S5W)), _const_spec((S5W, S5W)),
            _const_spec((1, S5W)), seqs((1, 2 * S5N)),
        ],
        out_specs=[_mixer_blk(S5W), seqs((2 * S5N,))],
        out_shape=[jax.ShapeDtypeStruct((ROWS_S, S5W), F32), jax.ShapeDtypeStruct((NB_S, 2 * S5N), F32)],
        scratch_shapes=[pltpu.VMEM((CB, 2 * S5N), F32)],
        compiler_params=_cparams(("parallel",)),
        name="s5_sample",
    )(ps_s, prm["tab"], prm["bblk"], prm["ccat"], prm["d"], prm["wglu"], prm["bglu"], st)


def _ssd_block(prompt, p_ref, cw_ref, cb_ref, dtb_ref, aneg_ref, dx_ref, sg_ref,
               g192_ref, tril_ref, om_ref, cbuf, st_scr, sub, is_last=None, ssp_out=None, convp_out=None,
               hist_ref=None, sss_in=None, sss_out=None, convs_out=None):
    rows = slice(sub * CB, (sub + 1) * CB)
    seqs = slice(sub * SEQ_PER_CB, (sub + 1) * SEQ_PER_CB)
    z = p_ref[rows, 0:SSW]
    xbc = p_ref[rows, SSW:SSW + CONVC]
    dtr = p_ref[rows, SSW + CONVC:P_SS]
    acc = cb_ref[...] + cw_ref[3:4, :] * xbc
    if prompt:
        cbuf[8:8 + CB, :] = xbc
        for k in (1, 2, 3):
            acc = acc + cw_ref[3 - k:4 - k, :] * cbuf[8 - k:8 - k + CB, :]
        cbuf[0:8, :] = cbuf[CB:CB + 8, :]
    else:
        convs_out[rows, :] = xbc
        tl = lax.broadcasted_iota(jnp.int32, (CB, CONVC), 0) % L_S
        hist = hist_ref[rows, :]
        for k in (1, 2, 3):
            hk = hist if k == 3 else pltpu.roll(hist, CB - (3 - k), 0)
            sh = jnp.where(tl >= k, pltpu.roll(xbc, k, 0), hk)
            acc = acc + cw_ref[3 - k:4 - k, :] * sh
    xc = _silu(acc)
    xs = xc[:, 0:SSW]
    bm = xc[:, SSW:SSW + 2 * HD]
    cm = xc[:, SSW + 2 * HD:CONVC]
    xdt = dtr + dtb_ref[...]
    dt = jnp.maximum(xdt, 0.0) + jnp.log1p(jnp.exp(-jnp.abs(xdt)))
    la = dt * aneg_ref[...]
    tril = tril_ref[...]
    b6 = _cumsum_rows(la, CB if prompt else L_S)
    lane = lax.broadcasted_iota(jnp.int32, (CB, 128), 1)
    bxw = [jnp.broadcast_to(b6[:, h:h + 1], (CB, 128)) for h in range(NH)]
    dtw = [jnp.broadcast_to(dt[:, h:h + 1], (CB, 128)) for h in range(NH)]
    pair = lambda cols: _concat_heads([jnp.where(lane < HD, cols[2 * j], cols[2 * j + 1]) for j in range(NH // 2)])
    bx = pair(bxw)
    dtx = pair(dtw)
    bm_rep = _concat_heads([bm[:, 0:HD]] * 3 + [bm[:, HD:2 * HD]] * 3)
    cm_rep = _concat_heads([cm[:, 0:HD]] * 3 + [cm[:, HD:2 * HD]] * 3)
    kh = bm_rep * dtx
    qt = cm_rep * jnp.exp(bx)
    blast = bx[CB - 1:CB, :] if prompt else _seq_last_rows(bx)
    kd = kh * jnp.exp(blast - bx)
    kdt = kd.T
    mask = tril > 0

    def scores(h):
        sl = slice(HD * h, HD * (h + 1))
        bcol = bxw[h]
        decay = jnp.exp(jnp.where(mask, bcol - bcol.T, -1e30))
        return _bdot_nt(cm_rep[:, sl], kh[:, sl]) * decay

    parts = []
    if prompt:
        ri = lax.broadcasted_iota(jnp.int32, (2 * HD, 2 * HD), 0)
        ci = lax.broadcasted_iota(jnp.int32, (2 * HD, 2 * HD), 1)
        same_head = (ri // HD) == (ci // HD)
        for j in range(NH // 2):
            pr = slice(2 * HD * j, 2 * HD * (j + 1))
            xp = xs[:, pr]
            xdiag = jnp.concatenate([jnp.where(lane < HD, xp, 0.0), jnp.where(lane >= HD, xp, 0.0)], axis=0)
            st = st_scr[j]
            parts.append(_bdot(jnp.concatenate([scores(2 * j), scores(2 * j + 1)], axis=1), xdiag)
                         + _bdot(qt[:, pr], st))
            st_scr[j] = st * jnp.exp(blast[:, pr]) + jnp.where(same_head, _bdot(kdt[pr, :], xp), 0.0)
    else:
        for h in range(NH):
            sl = slice(HD * h, HD * (h + 1))
            sts = sss_in[seqs, h].reshape(SEQ_PER_CB * HD, HD)
            parts.append(_bdot(scores(h), xs[:, sl]) + _bdot(_seq_expand_lanes(qt[:, sl]), sts))
            dec3 = jnp.exp(blast[:, sl]).reshape(SEQ_PER_CB, L_S, HD)[:, L_S - 1:L_S, :]
            dec_rows = jnp.broadcast_to(dec3, (SEQ_PER_CB, HD, HD)).reshape(SEQ_PER_CB * HD, HD)
            upd = _bdot(_seq_expand_rows(kdt[sl, :]), xs[:, sl])
            sss_out[seqs, h] = (sts * dec_rows + upd).reshape(SEQ_PER_CB, HD, HD)
    if is_last is not None:
        @pl.when(is_last)
        def _():
            ssp_out[0] = st_scr[...]
            convp_out[0] = cbuf[0:8, :]
    y = (_concat_heads(parts) + dx_ref[...] * xs) * _silu(z)
    ms = _split_dot(y * y, g192_ref[...], 1) * (1.0 / (SSW // 2))
    om_ref[rows, :] = y * lax.rsqrt(ms + RMS_EPS) * sg_ref[...]


def _ssd_body(p_ref, cw_ref, cb_ref, dtb_ref, aneg_ref, dx_ref, sg_ref, g192_ref,
              trilp_ref, trils_ref, hist_ref, sss_in, *refs):
    om_ref, ssp_out, sss_out, convp_out, convs_out, cbuf, st_scr = refs[1:]
    i = pl.program_id(0)
    common = (p_ref, cw_ref, cb_ref, dtb_ref, aneg_ref, dx_ref, sg_ref, g192_ref)
    npc, steps_per_seq = NPC // SSD_SUB, CH_PER_SEQ // SSD_SUB

    @pl.when((i < npc) & (i % steps_per_seq == 0))
    def _():
        st_scr[...] = jnp.zeros_like(st_scr)
        cbuf[...] = jnp.zeros_like(cbuf)

    @pl.when(i < npc)
    def _():
        for sub in range(SSD_SUB):
            is_last = (i % steps_per_seq == steps_per_seq - 1) if sub == SSD_SUB - 1 else None
            _ssd_block(True, *common, trilp_ref, om_ref, cbuf, st_scr, sub, is_last=is_last,
                       ssp_out=ssp_out, convp_out=convp_out)

    @pl.when(i >= npc)
    def _():
        for sub in range(SSD_SUB):
            _ssd_block(False, *common, trils_ref, om_ref, cbuf, st_scr, sub, hist_ref=hist_ref,
                       sss_in=sss_in, sss_out=sss_out, convs_out=convs_out)


SSD_SUB = 2


def _ssd(pm, prm, consts, hist, st, layer, earlier):
    sample_rows = pl.BlockSpec((SSD_SUB * CB, CONVC),
                               lambda i: (jnp.clip(i - NPC // SSD_SUB, 0, NSC // SSD_SUB - 1), 0))
    in_specs = [
        _mixer_blk(P_SS, SSD_SUB), _const_spec((4, CONVC)), _const_spec((1, CONVC)),
        _const_spec((1, 128)), _const_spec((1, 128)), _const_spec((1, SSW)), _const_spec((1, SSW)),
        _const_spec((SSW, SSW)),
        _const_spec((CB, CB)), _const_spec((CB, CB)),
        sample_rows,
        _sstate_spec((NH, HD, HD), SSD_SUB),
    ]
    args = [pm, prm["cw"], prm["cb"], prm["dtb"], prm["aneg"], prm["dx"], prm["sg"],
            consts["g192"], consts["trilp"], consts["trils"], hist, st]
    in_specs.append(pl.BlockSpec(memory_space=pl.ANY))
    aliases = {len(args): 2}
    args.append(earlier)
    layer_slab = pl.BlockSpec((None, SSD_SUB * SEQ_PER_CB, NH, HD, HD),
                              lambda i: (layer, jnp.clip(i - NPC // SSD_SUB, 0, NSC // SSD_SUB - 1), 0, 0, 0))
    return pl.pallas_call(
        _ssd_body,
        grid=((NPC + NSC) // SSD_SUB,),
        in_specs=in_specs,
        out_specs=[_mixer_blk(SSW, SSD_SUB), _pstate_spec((NH // 2, 2 * HD, 2 * HD), SSD_SUB),
                   layer_slab, _pstate_spec((8, CONVC), SSD_SUB), sample_rows],
        out_shape=[
            jax.ShapeDtypeStruct((ROWS, SSW), F32),
            jax.ShapeDtypeStruct((NB_P, NH // 2, 2 * HD, 2 * HD), F32),
            jax.ShapeDtypeStruct((DEPTH, NB_S, NH, HD, HD), F32),
            jax.ShapeDtypeStruct((NB_P, 8, CONVC), F32),
            jax.ShapeDtypeStruct((ROWS_S, CONVC), F32),
        ],
        scratch_shapes=[pltpu.VMEM((CB + 8, CONVC), F32), pltpu.VMEM((NH // 2, 2 * HD, 2 * HD), F32)],
        input_output_aliases=aliases,
        compiler_params=_cparams(("arbitrary",)),
        name="ssd",
    )(*args)


def _o_core(split_x, i, oh_ref, osp_ref, oss_ref, om_ref, *refs):
    if split_x:
        x = jnp.where(i < NPT, refs[0][...], refs[1][...])
        refs = refs[2:]
    else:
        x = refs[0][...]
        refs = refs[1:]
    wo_ref, gp, gs, lng_ref, lnb_ref, scp, scs, shp, shs = refs[:9]
    os_ = jnp.where(i < NPT, osp_ref[...], oss_ref[...])
    mix = (_bdot(oh_ref[...], wo_ref[0:HW, :]) + _bdot(os_, wo_ref[HW:HW + S5W, :])
           + _bdot(om_ref[...], wo_ref[HW + S5W:D, :]))
    x1 = _layer_norm(ALPHA * x + _rowmod(i, gp, gs) * mix, lng_ref[...], lnb_ref[...])
    h2 = x1 * (1.0 + _rowmod(i, scp, scs)) + _rowmod(i, shp, shs)
    return x1, h2, refs[9:]


def _o_body(split_x, *refs):
    i = pl.program_id(0)
    x1, h2, (x1_ref, h2_ref) = _o_core(split_x, i, *refs)
    x1_ref[...] = x1
    h2_ref[...] = h2.astype(BF16)


def _o_router_body(split_x, *refs):
    i = pl.program_id(0)
    x1, h2, (wr_ref, br_ref, x1_ref, h2_ref, route_ref) = _o_core(split_x, i, *refs)
    x1_ref[...] = x1
    h2_ref[...] = h2
    h_hi = h2.astype(BF16)
    h_lo = (h2 - h_hi.astype(F32)).astype(BF16)
    logits = (jnp.dot(h_hi, wr_ref[0], preferred_element_type=F32)
              + jnp.dot(h_lo, wr_ref[0], preferred_element_type=F32)
              + jnp.dot(h_hi, wr_ref[1], preferred_element_type=F32)) + br_ref[...]
    lane = lax.broadcasted_iota(jnp.int32, (TM, 128), 1).astype(F32)
    neg = -jnp.inf
    lg = jnp.where(lane < NEXP, logits, neg)
    m1 = jnp.max(lg, axis=-1, keepdims=True)
    i1 = jnp.min(jnp.where(lg == m1, lane, 128.0), axis=-1, keepdims=True)
    lg2 = jnp.where(lane == i1, neg, lg)
    m2 = jnp.max(lg2, axis=-1, keepdims=True)
    i2 = jnp.min(jnp.where(lg2 == m2, lane, 128.0), axis=-1, keepdims=True)
    e2 = jnp.exp(m2 - m1)
    den = 1.0 + e2
    route_ref[...] = jnp.where(lane == 0.0, i1, jnp.where(lane == 1.0, i2,
                               jnp.where(lane == 2.0, 1.0 / den, jnp.where(lane == 3.0, e2 / den, 0.0))))


def _stage_o(layer, oh, os_p, os_s, om, x, wo_b, ada4, ln_g, ln_b, router=None):
    split_x = isinstance(x, tuple)
    x_specs = [_prompt_rows_spec(D), _sample_rows_spec(D)] if split_x else [_row_spec(D)]
    in_specs = [
        _row_spec(HW), _prompt_rows_spec(S5W), _sample_rows_spec(S5W), _row_spec(SSW), *x_specs,
        _const_spec((D, D)),
        *_mod_specs(layer, 2), _const_spec((1, D)), _const_spec((1, D)),
        *_mod_specs(layer, 4), *_mod_specs(layer, 3),
    ]
    args = [oh, os_p, os_s, om, *(x if split_x else (x,)), wo_b, ada4, ada4, ln_g, ln_b, ada4, ada4, ada4, ada4]
    out_specs = [_row_spec(D), _row_spec(D)]
    out_shape = [jax.ShapeDtypeStruct((ROWS, D), F32), jax.ShapeDtypeStruct((ROWS, D), BF16)]
    body = _o_body
    if router is not None:
        in_specs += [_const_spec((2, D, 128)), _const_spec((1, 128))]
        args += list(router)
        out_specs.append(_row_spec(128))
        out_shape[1] = jax.ShapeDtypeStruct((ROWS, D), F32)
        out_shape.append(jax.ShapeDtypeStruct((ROWS, 128), F32))
        body = _o_router_body
    return pl.pallas_call(
        functools.partial(body, split_x), grid=(NT,), in_specs=in_specs, out_specs=out_specs,
        out_shape=out_shape,
        compiler_params=_cparams(("parallel",)),
        name="stage_o_router" if router is not None else "stage_o",
    )(*args)


def _ffn_body(te_ref, nu_ref, h_ref, wg_ref, wu_ref, wd_ref, *refs):
    o_ref, acc_ref = refs[-2:]
    i = pl.program_id(0)
    j = pl.program_id(1)

    @pl.when(j == 0)
    def _():
        acc_ref[...] = jnp.zeros_like(acc_ref)

    @pl.when(i < nu_ref[0])
    def _():
        h = h_ref[...].astype(BF16)
        g = jnp.dot(h, wg_ref[...], preferred_element_type=F32)
        u = jnp.dot(h, wu_ref[...], preferred_element_type=F32)
        act = (_silu(g) * u).astype(BF16)
        acc_ref[...] += jnp.dot(act, wd_ref[...], preferred_element_type=F32)

    @pl.when(j == pl.num_programs(1) - 1)
    def _():
        o_ref[...] = acc_ref[...]


def _ffn(tile_expert, n_used, h, wg, wu, wd, into=None, tile0=0):
    rows = h.shape[0]
    nj = DFF // TF

    def jblk(i, j, nu):
        return jnp.where(i < nu[0], j, nj - 1)

    in_specs = [
        pl.BlockSpec((TMF, D), lambda i, j, te, nu: (i, 0)),
        pl.BlockSpec((None, D, TF), lambda i, j, te, nu: (te[i], 0, jblk(i, j, nu))),
        pl.BlockSpec((None, D, TF), lambda i, j, te, nu: (te[i], 0, jblk(i, j, nu))),
        pl.BlockSpec((None, TF, D), lambda i, j, te, nu: (te[i], jblk(i, j, nu), 0)),
    ]
    args = [tile_expert, n_used, h, wg, wu, wd]
    aliases = {}
    if into is not None:
        in_specs.append(pl.BlockSpec(memory_space=pl.ANY))
        aliases = {len(args): 0}
        args.append(into)
    grid_spec = pltpu.PrefetchScalarGridSpec(
        num_scalar_prefetch=2,
        grid=(rows // TMF, nj),
        in_specs=in_specs,
        out_specs=pl.BlockSpec((TMF, D), lambda i, j, te, nu: (i + tile0, 0)),
        scratch_shapes=[pltpu.VMEM((TMF, D), F32)],
    )
    return pl.pallas_call(
        _ffn_body, grid_spec=grid_spec,
        out_shape=jax.ShapeDtypeStruct((rows, D) if into is None else into.shape, F32),
        input_output_aliases=aliases,
        compiler_params=_cparams(("parallel", "arbitrary")),
        name="ffn",
    )(*args)


def _final_body(x1_ref, ya_ref, yb_ref, route_ref, gp, gs, lng_ref, lnb_ref, yp_ref, ys_ref):
    i = pl.program_id(0)
    f = route_ref[:, 2:3] * ya_ref[...] + route_ref[:, 3:4] * yb_ref[...]
    y = _layer_norm(ALPHA * x1_ref[...] + _rowmod(i, gp, gs) * f, lng_ref[...], lnb_ref[...])

    @pl.when(i < NPT)
    def _():
        yp_ref[...] = y

    @pl.when(i >= NPT)
    def _():
        ys_ref[...] = y


def _stage_final(layer, x1, ya, yb, route, ada4, ln_g, ln_b):
    return pl.pallas_call(
        _final_body,
        grid=(NT,),
        in_specs=[
            _row_spec(D), _row_spec(D), _row_spec(D), _row_spec(128),
            *_mod_specs(layer, 5), _const_spec((1, D)), _const_spec((1, D)),
        ],
        out_specs=[
            pl.BlockSpec((TM, D), lambda i: (jnp.minimum(i, NPT - 1), 0)),
            pl.BlockSpec((TM, D), lambda i: (jnp.clip(i - NPT, 0, NST - 1), 0)),
        ],
        out_shape=[jax.ShapeDtypeStruct((ROWS_P, D), F32), jax.ShapeDtypeStruct((ROWS_S, D), F32)],
        compiler_params=_cparams(("arbitrary",)),
        name="stage_final",
    )(x1, ya, yb, route, ada4, ada4, ln_g, ln_b)


def _block_ones(n, blk):
    r = np.arange(n) // blk
    return r[:, None] == r[None, :]


def _consts():
    t = np.arange(CB)
    causal = t[:, None] >= t[None, :]
    same_seq = (t[:, None] // L_S) == (t[None, :] // L_S)
    r = np.arange(S5_ROWS)
    perm = r[None, :] == ((r % NB_P) * S5_TB + r // NB_P)[:, None]
    mats = {
        "perm": perm,
        "permt": perm.T,
        "e64": _block_ones(HW, HD),
        "g192": _block_ones(SSW, SSW // 2),
        "trilp": causal,
        "trils": causal & same_seq,
    }
    return {k: jnp.asarray(v.astype(np.float32), dtype=BF16) for k, v in mats.items()}


def _s5_params(a_re, a_im, log_dt, b_re, b_im, c_re, c_im, d, w_glu, b_glu):
    dt = jnp.exp(log_dt)[:, None]
    mag = jnp.exp(a_re * dt)
    lam_re, lam_im = mag * jnp.cos(a_im * dt), mag * jnp.sin(a_im * dt)
    den = a_re * a_re + a_im * a_im
    nr, ni = lam_re - 1.0, lam_im
    zr = (nr * a_re + ni * a_im) / den
    zi = (ni * a_re - nr * a_im) / den
    bbar_re = zr[..., None] * b_re - zi[..., None] * b_im
    bbar_im = zr[..., None] * b_im + zi[..., None] * b_re
    eye = jnp.eye(16, dtype=F32)
    blk = lambda bb: jnp.einsum('gph,gk->ghkp', bb, eye).reshape(S5W, S5N)
    hu, hs = S5W // 2, S5N // 2
    bblk = jnp.stack([jnp.stack([blk(bb)[hu * j:hu * (j + 1), hs * j:hs * (j + 1)] for j in range(2)])
                      for bb in (bbar_re, bbar_im)]).astype(BF16)
    cblk = lambda cc: jnp.einsum('ghp,gk->gpkh', cc, eye).reshape(S5N, S5W)
    ccat = jnp.stack([jnp.stack([cblk(cc)[hs * j:hs * (j + 1), hu * j:hu * (j + 1)] for cc in (c_re, -c_im)])
                      for j in range(2)]).astype(BF16)
    lr, li = lam_re.reshape(-1), lam_im.reshape(-1)
    pows = [(jnp.ones_like(lr), jnp.zeros_like(li))]
    for _ in range(8):
        pr, pi = pows[-1]
        pows.append((pr * lr - pi * li, pr * li + pi * lr))
    rows = jnp.arange(8)[:, None]
    tabs = []
    for dsh in (1, 2, 4):
        pr, pi = pows[dsh]
        tabs.append(jnp.where(rows >= dsh, jnp.concatenate([pr, pi])[None, :], 0.0))
    tabs.append(jnp.stack([jnp.concatenate(pows[r + 1]) for r in range(8)]))
    return {
        "tab": jnp.stack(tabs), "bblk": bblk, "ccat": ccat,
        "lam8": jnp.broadcast_to(jnp.concatenate([lr, li])[None, :], (NB_P, 2 * S5N)),
        "d": d.reshape(1, S5W), "wglu": w_glu.astype(BF16), "bglu": b_glu.reshape(1, S5W),
    }


def _pair_diag(st):
    s6 = st.reshape(NB_P, NH // 2, 2, HD, 2, HD)
    return jnp.stack([s6[:, :, 0, :, 0, :], s6[:, :, 1, :, 1, :]], axis=2).reshape(NB_P, NH, HD, HD)


def _pad_lanes(v, n=128):
    return jnp.pad(v, (0, n - v.shape[0])).reshape(1, n)


def kernel(x_prompt, x_sample, c_prompt, c_sample, state_hgrn, state_s5, state_ssd, state_ssd_conv, w_ada, b_ada, ln_g, ln_b, w_in, w_out, hgrn_lb_logits, hgrn_norm_g, s5_a_re, s5_a_im, s5_log_dt, s5_b_re, s5_b_im, s5_c_re, s5_c_im, s5_d, s5_w_glu, s5_b_glu, ssd_conv_w, ssd_conv_b, ssd_dt_bias, ssd_a_log, ssd_d, ssd_norm_g, ffn_w_gate, ffn_w_up, ffn_w_down, moe_w_router, moe_b_router, moe_w_gate, moe_w_up, moe_w_down):
    consts = _consts()
    c_all = jnp.concatenate([c_sample, c_prompt], axis=0)
    ada4 = _ada(c_all, w_ada, b_ada).reshape(DEPTH, NB_S + NB_P, 1, 6 * D)

    lb_all = jnp.cumsum(jax.nn.softmax(hgrn_lb_logits, axis=0), axis=0)
    lb_all = lb_all - lb_all[0]

    xp = x_prompt.reshape(ROWS_P, D)
    xs = x_sample.reshape(ROWS_S, D)
    new_h, new_s5, new_m, new_c = [], [], [], []
    x1 = f = route = None
    ss_s = jnp.zeros((DEPTH, NB_S, NH, HD, HD), F32)
    for l in range(DEPTH):
        w_in_b = jnp.pad(w_in[l], ((0, 0), (0, N_IN_PAD - N_IN))).astype(BF16)
        if l == 0:
            x = (xp, xs)
            ph, ps_p, ps_s, pm = _stage_a0(xp, xs, ada4, w_in_b)
        else:
            x, ph, ps_p, ps_s, pm = _stage_a1(l, x1, f, ada4, ln_g[l - 1, 1].reshape(1, D),
                                              ln_b[l - 1, 1].reshape(1, D), w_in_b)
        oh, hg_p, hg_s = _hgrn(ph, lb_all[l].reshape(1, HW), hgrn_norm_g[l].reshape(1, HW), consts,
                               jnp.swapaxes(state_hgrn[l], -1, -2))
        s5p = _s5_params(s5_a_re[l], s5_a_im[l], s5_log_dt[l], s5_b_re[l], s5_b_im[l], s5_c_re[l], s5_c_im[l],
                         s5_d[l], s5_w_glu[l], s5_b_glu[l])
        os_p, s5_p = _s5_prompt(ps_p.reshape(NB_P, L_P, S5W), s5p, consts)
        os_s, s5_s = _s5_sample(ps_s, s5p, state_s5[l].reshape(NB_S, 1, 2 * S5N))
        os_p = os_p.reshape(ROWS_P, S5W)
        ssd_prm = {
            "cw": ssd_conv_w[l], "cb": ssd_conv_b[l].reshape(1, CONVC),
            "dtb": _pad_lanes(ssd_dt_bias[l]), "aneg": _pad_lanes(-jnp.exp(ssd_a_log[l])),
            "dx": jnp.repeat(ssd_d[l], HD).reshape(1, SSW), "sg": ssd_norm_g[l].reshape(1, SSW),
        }
        hist = jnp.pad(state_ssd_conv[l], ((0, 0), (0, L_S - 3), (0, 0))).reshape(ROWS_S, CONVC)
        om, ss_p, ss_s, tail_p, xbc_s = _ssd(pm, ssd_prm, consts, hist, state_ssd[l], l, ss_s)
        conv_p = tail_p[:, 8 - 3:]
        conv_s = xbc_s.reshape(NB_S, L_S, CONVC)[:, L_S - 3:]
        hg_p, ss_p = _pair_diag(hg_p), _pair_diag(ss_p)
        new_h.append((jnp.swapaxes(hg_p, -1, -2), jnp.swapaxes(hg_s, -1, -2)))
        new_s5.append((s5_p.reshape(NB_P, 2, 16, 64), s5_s.reshape(NB_S, 2, 16, 64)))
        new_m.append((ss_p,))
        new_c.append((conv_p, conv_s))

        wo_b = w_out[l].astype(BF16)
        lg, lbb = ln_g[l, 0].reshape(1, D), ln_b[l, 0].reshape(1, D)
        j = l // 2
        if l % 2 == 0:
            x1, h2 = _stage_o(l, oh, os_p, os_s, om, x, wo_b, ada4, lg, lbb)
            f = _ffn(jnp.zeros((ROWS // TMF,), jnp.int32), jnp.full((1,), ROWS // TMF, jnp.int32), h2,
                     ffn_w_gate[j:j + 1].astype(BF16),
                     ffn_w_up[j:j + 1].astype(BF16), ffn_w_down[j:j + 1].astype(BF16))
        else:
            wr = jnp.pad(moe_w_router[j], ((0, 0), (0, 128 - NEXP)))
            wr_hi = wr.astype(BF16)
            wr = jnp.stack([wr_hi, (wr - wr_hi.astype(F32)).astype(BF16)])
            br = _pad_lanes(moe_b_router[j])
            x1, h2, route = _stage_o(l, oh, os_p, os_s, om, x, wo_b, ada4, lg, lbb, router=(wr, br))
            wg_b, wu_b, wd_b = (moe_w_gate[j].astype(BF16), moe_w_up[j].astype(BF16),
                                moe_w_down[j].astype(BF16))
            flat_e = route[:, 0:2].astype(jnp.int32).reshape(-1)
            onehot = (flat_e[:, None] == jnp.arange(NEXP)[None, :]).astype(jnp.int32)
            csum = jnp.cumsum(onehot, axis=0)
            counts = csum[-1]
            rank = jnp.take_along_axis(csum, flat_e[:, None], axis=1)[:, 0] - 1
            padded = ((counts + TMF - 1) // TMF) * TMF
            pend = jnp.cumsum(padded)
            pstart = pend - padded
            dest = pstart[flat_e] + rank
            n_pad = 2 * ROWS + NEXP * TMF
            src_tok = jnp.zeros((n_pad,), jnp.int32).at[dest].set(jnp.arange(2 * ROWS, dtype=jnp.int32) // 2)
            tile_start = jnp.arange(n_pad // TMF, dtype=jnp.int32) * TMF
            tile_e = jnp.minimum(jnp.sum((pend[None, :] <= tile_start[:, None]).astype(jnp.int32), axis=1),
                                 NEXP - 1)
            n_used = (pend[NEXP - 1:NEXP] // TMF).astype(jnp.int32)
            part_tiles = n_pad // TMF // MOE_PARTS
            part_rows = part_tiles * TMF
            y_sorted = jnp.zeros((n_pad, D), F32)
            for c in range(MOE_PARTS):
                y_sorted = _ffn(tile_e[c * part_tiles:(c + 1) * part_tiles],
                                jnp.clip(n_used - c * part_tiles, 0, part_tiles),
                                h2[src_tok[c * part_rows:(c + 1) * part_rows]], wg_b, wu_b, wd_b,
                                into=y_sorted, tile0=c * part_tiles)
            pos = dest.reshape(ROWS, 2)
            ya = y_sorted[pos[:, 0]]
            yb = y_sorted[pos[:, 1]]
    y_p, y_s = _stage_final(DEPTH - 1, x1, ya, yb, route, ada4, ln_g[DEPTH - 1, 1].reshape(1, D),
                            ln_b[DEPTH - 1, 1].reshape(1, D))
    stack = lambda lst, k: jnp.stack([t[k] for t in lst])
    return (y_p.reshape(NB_P, L_P, D), y_s.reshape(NB_S, L_S, D),
            stack(new_h, 0), stack(new_s5, 0), stack(new_m, 0), stack(new_c, 0),
            stack(new_h, 1), stack(new_s5, 1), ss_s, stack(new_c, 1))
```

```python
import functools
import math

import jax
import jax.numpy as jnp
import numpy as np
from jax import lax
from jax.experimental import pallas as pl
from jax.experimental.pallas import tpu as pltpu

F32 = jnp.float32
BF16 = jnp.bfloat16

D = 1024
NB_P, L_P = 8, 2048
NB_S, L_S = 128, 8
ROWS_P = NB_P * L_P
ROWS_S = NB_S * L_S
ROWS = ROWS_P + ROWS_S
DEPTH = 2
HW = 384
S5W = 256
SSW = 384
NH = 6
HD = 64
S5N = 1024
CONVC = 640
N_IN = 2822
N_IN_PAD = 2944
P_HG = 1536
P_SS = 1152
DFF = 2816
NEXP = 8
ALPHA = (2 * DEPTH) ** 0.25
LN_EPS = 1e-5
RMS_EPS = 1e-6

TM = 512
NPT = ROWS_P // TM
NST = ROWS_S // TM
NT = NPT + NST
SEQ_PER_TILE = TM // L_S
TILES_PER_SEQ = L_P // TM

CB = 128
NPC = ROWS_P // CB
NSC = ROWS_S // CB
CH_PER_SEQ = L_P // CB
SEQ_PER_CB = CB // L_S

TMF = 512
TF = 1408
VMEM_LIMIT = 56 * 1024 * 1024


def _cparams(sem):
    return pltpu.CompilerParams(dimension_semantics=sem, vmem_limit_bytes=VMEM_LIMIT)


def _bdot(a, b):
    return jnp.dot(a.astype(BF16), b.astype(BF16), preferred_element_type=F32)


def _bdot_nt(a, b):
    return lax.dot_general(a.astype(BF16), b.astype(BF16), (((1,), (1,)), ((), ())),
                           preferred_element_type=F32)


def _split_dot(x, e, passes):
    acc = None
    r = x
    for _ in range(passes):
        hi = r.astype(BF16)
        d = jnp.dot(hi, e, preferred_element_type=F32)
        acc = d if acc is None else acc + d
        r = r - hi.astype(F32)
    return acc


def _split_dot_l(e, x, passes):
    acc = None
    r = x
    for _ in range(passes):
        hi = r.astype(BF16)
        d = jnp.dot(e, hi, preferred_element_type=F32)
        acc = d if acc is None else acc + d
        r = r - hi.astype(F32)
    return acc


def _silu(x):
    return x * jax.nn.sigmoid(x)


def _layer_norm(x, g, b):
    mu = jnp.mean(x, -1, keepdims=True)
    xc = x - mu
    var = jnp.mean(xc * xc, -1, keepdims=True)
    return xc * lax.rsqrt(var + LN_EPS) * g + b


def _rowmod(i, p_ref, s_ref):
    s = jnp.broadcast_to(s_ref[...], (SEQ_PER_TILE, L_S, D)).reshape(TM, D)
    return jnp.where(i < NPT, p_ref[0], s)


ADA_TN = 1536


def _ada_body(c_ref, w_ref, b_ref, o_ref):
    o_ref[...] = _bdot(_silu(c_ref[...]), w_ref[...]) + b_ref[...]


def _ada(c_all, w_ada, b_ada):
    nc = c_all.shape[0]
    return pl.pallas_call(
        _ada_body,
        grid=(DEPTH, 6 * D // ADA_TN),
        in_specs=[
            pl.BlockSpec((nc, D), lambda l, j: (0, 0)),
            pl.BlockSpec((None, D, ADA_TN), lambda l, j: (l, 0, j)),
            pl.BlockSpec((None, 1, ADA_TN), lambda l, j: (l, 0, j)),
        ],
        out_specs=pl.BlockSpec((None, nc, ADA_TN), lambda l, j: (l, 0, j)),
        out_shape=jax.ShapeDtypeStruct((DEPTH, nc, 6 * D), F32),
        compiler_params=_cparams(("parallel", "parallel")),
        name="ada",
    )(c_all, w_ada, b_ada.reshape(DEPTH, 1, 6 * D))


def _mod_specs(layer, k):
    ps = pl.BlockSpec((None, 1, 1, D),
                      lambda i: (layer, NB_S + jnp.minimum(i // TILES_PER_SEQ, NB_P - 1), 0, k))
    ss = pl.BlockSpec((None, SEQ_PER_TILE, 1, D),
                      lambda i: (layer, jnp.clip(i - NPT, 0, NST - 1), 0, k))
    return [ps, ss]


def _row_spec(width):
    return pl.BlockSpec((TM, width), lambda i: (i, 0))


def _const_spec(shape):
    nd = len(shape)
    return pl.BlockSpec(shape, lambda *_: (0,) * nd)


def _prompt_rows_spec(width):
    return pl.BlockSpec((TM, width), lambda i: (jnp.minimum(i, NPT - 1), 0))


def _sample_rows_spec(width):
    return pl.BlockSpec((TM, width), lambda i: (jnp.clip(i - NPT, 0, NST - 1), 0))


def _proj_out(x, i, scp, scs, shp, shs, w_ref, ph_ref, ps_ref, pm_ref):
    h = x * (1.0 + _rowmod(i, scp, scs)) + _rowmod(i, shp, shs)
    proj = jnp.dot(h.astype(BF16), w_ref[...], preferred_element_type=F32)
    ph_ref[...] = proj[:, 0:P_HG]
    ps_ref[...] = proj[:, P_HG:P_HG + S5W]
    pm_ref[...] = proj[:, P_HG + S5W:N_IN_PAD]


def _a0_body(xp_ref, xs_ref, scp, scs, shp, shs, w_ref, ph_ref, ps_ref, pm_ref):
    i = pl.program_id(0)
    x = jnp.where(i < NPT, xp_ref[...], xs_ref[...])
    _proj_out(x, i, scp, scs, shp, shs, w_ref, ph_ref, ps_ref, pm_ref)


def _a1_body(x1_ref, f_ref, gp, gs, lng_ref, lnb_ref, scp, scs, shp, shs, w_ref,
             x_ref, ph_ref, ps_ref, pm_ref):
    i = pl.program_id(0)
    x = _layer_norm(ALPHA * x1_ref[...] + _rowmod(i, gp, gs) * f_ref[...], lng_ref[...], lnb_ref[...])
    x_ref[...] = x
    _proj_out(x, i, scp, scs, shp, shs, w_ref, ph_ref, ps_ref, pm_ref)


def _a_out():
    specs = [_row_spec(P_HG), _row_spec(S5W), _row_spec(P_SS)]
    shapes = [jax.ShapeDtypeStruct((ROWS, w), F32) for w in (P_HG, S5W, P_SS)]
    return specs, shapes


def _stage_a0(xp, xs, ada4, w_in_b):
    out_specs, out_shape = _a_out()
    return pl.pallas_call(
        _a0_body,
        grid=(NT,),
        in_specs=[
            _prompt_rows_spec(D), _sample_rows_spec(D),
            *_mod_specs(0, 1), *_mod_specs(0, 0),
            _const_spec((D, N_IN_PAD)),
        ],
        out_specs=out_specs, out_shape=out_shape,
        compiler_params=_cparams(("parallel",)),
        name="stage_a0",
    )(xp, xs, ada4, ada4, ada4, ada4, w_in_b)


def _stage_a1(layer, x1, f, ada4, ln_g, ln_b, w_in_b):
    out_specs, out_shape = _a_out()
    return pl.pallas_call(
        _a1_body,
        grid=(NT,),
        in_specs=[
            _row_spec(D), _row_spec(D),
            *_mod_specs(layer - 1, 5),
            _const_spec((1, D)), _const_spec((1, D)),
            *_mod_specs(layer, 1), *_mod_specs(layer, 0),
            _const_spec((D, N_IN_PAD)),
        ],
        out_specs=[_row_spec(D)] + out_specs, out_shape=[jax.ShapeDtypeStruct((ROWS, D), F32)] + out_shape,
        compiler_params=_cparams(("parallel",)),
        name="stage_a1",
    )(x1, f, ada4, ada4, ln_g, ln_b, ada4, ada4, ada4, ada4, w_in_b)


def _mixer_blk(width, nsub=1):
    return pl.BlockSpec((nsub * CB, width), lambda i: (i, 0))


def _pstate_spec(shape, nsub=1):
    nd = len(shape)
    return pl.BlockSpec((1,) + shape,
                        lambda i: (jnp.minimum(i // (CH_PER_SEQ // nsub), NB_P - 1),) + (0,) * nd)


def _sstate_spec(shape, nsub=1):
    nd = len(shape)
    return pl.BlockSpec((nsub * SEQ_PER_CB,) + shape,
                        lambda i: (jnp.clip(i - NPC // nsub, 0, NSC // nsub - 1),) + (0,) * nd)


def _cumsum_rows(x, span):
    r = lax.broadcasted_iota(jnp.int32, x.shape, 0) & (span - 1)
    d = 1
    while d < span:
        x = x + jnp.where(r >= d, pltpu.roll(x, d, 0), 0.0)
        d *= 2
    return x


def _seq_last_rows(x):
    w = x.shape[-1]
    x3 = x.reshape(SEQ_PER_CB, L_S, w)
    return jnp.broadcast_to(x3[:, L_S - 1:L_S, :], (SEQ_PER_CB, L_S, w)).reshape(CB, w)


def _concat_heads(parts):
    return jnp.concatenate(parts, axis=1)


def _stack_select(shape, row_div, lane_div):
    r = lax.broadcasted_iota(jnp.int32, shape, 0) // row_div
    c = lax.broadcasted_iota(jnp.int32, shape, 1) // lane_div
    return r == c


def _seq_expand_lanes(qh):
    q2 = jnp.concatenate([qh, qh], axis=1)
    q16 = jnp.concatenate([q2] * (SEQ_PER_CB // 2), axis=1)
    return jnp.where(_stack_select((CB, SEQ_PER_CB * HD), L_S, HD), q16, 0.0)


def _seq_expand_rows(xt):
    t = jnp.broadcast_to(xt[None], (SEQ_PER_CB, HD, CB)).reshape(SEQ_PER_CB * HD, CB)
    return jnp.where(_stack_select((SEQ_PER_CB * HD, CB), HD, L_S), t, 0.0)


def _fold_seq_lanes(full):
    acc = full[:, 0:128]
    for j in range(1, SEQ_PER_CB * HD // 128):
        acc = acc + full[:, 128 * j:128 * (j + 1)]
    return acc[:, 0:HD] + acc[:, HD:2 * HD]


HGRN_BASE = 32
EXP_RANGE_MAX = 80.0


def _hgrn_block(prompt, p_ref, lb_ref, hg_ref, e64_ref, oh_ref, st_scr, o_scr, sub, is_last=None,
                stp_out=None, sts_in=None, sts_out=None):
    rows = slice(sub * CB, (sub + 1) * CB)
    seqs = slice(sub * SEQ_PER_CB, (sub + 1) * SEQ_PER_CB)
    lb = lb_ref[...]
    qr = p_ref[rows, 0:HW]
    fr = p_ref[rows, HW:2 * HW]
    v = p_ref[rows, 2 * HW:3 * HW]
    gr = p_ref[rows, 3 * HW:4 * HW]
    e = jnp.exp(-jnp.abs(fr))
    ope = 1.0 + e
    ls = jnp.minimum(fr, 0.0) - jnp.log(ope)
    a = jnp.log(lb)
    bb = jnp.log1p(-lb) + ls
    lf = jnp.maximum(a, bb) + jnp.log(1.0 + jnp.exp(-jnp.abs(a - bb)))
    rcp = 1.0 / ope
    kk = (1.0 - lb) * jnp.where(fr >= 0.0, e * rcp, rcp)
    q = _silu(qr)
    b = _cumsum_rows(lf, CB if prompt else L_S)

    e64 = e64_ref[...]
    ti = lax.broadcasted_iota(jnp.int32, (CB, CB), 0)
    si = lax.broadcasted_iota(jnp.int32, (CB, CB), 1)

    def diag8():
        nsub = CB // 8
        b3 = b.reshape(nsub, 8, HW)
        q3 = q.reshape(nsub, 8, HW)
        k3 = kk.reshape(nsub, 8, HW)
        v3 = v.reshape(nsub, 8, HW)
        r3 = lax.broadcasted_iota(jnp.int32, (nsub, 8, HW), 1)
        o = jnp.zeros((CB, HW), F32)
        for s in range(8):
            dlt = jnp.minimum(b3 - b3[:, s:s + 1, :], 0.0)
            w = jnp.where(r3 >= s, jnp.exp(dlt), 0.0) * q3 * k3[:, s:s + 1, :]
            hsum = jnp.dot(w.reshape(CB, HW).astype(BF16), e64, preferred_element_type=F32)
            o = o + hsum * jnp.broadcast_to(v3[:, s:s + 1, :], (nsub, 8, HW)).reshape(CB, HW)
        return o

    def level_terms(m):
        terms = []
        while m < CB:
            nb = CB // (2 * m)
            b4 = b.reshape(nb, 2 * m, HW)
            bmid = b4[:, m - 1:m, :]
            pos = lax.broadcasted_iota(jnp.int32, (nb, 2 * m, HW), 1)
            qq = jnp.where(pos >= m, q.reshape(nb, 2 * m, HW) * jnp.exp(jnp.minimum(b4 - bmid, 0.0)), 0.0)
            kq = jnp.where(pos < m, kk.reshape(nb, 2 * m, HW) * jnp.exp(jnp.minimum(bmid - b4, 0.0)), 0.0)
            terms.append((qq.reshape(CB, HW), kq.reshape(CB, HW), (ti // (2 * m)) == (si // (2 * m))))
            m *= 2
        return terms

    def scores_times_v(terms):
        lane = lax.broadcasted_iota(jnp.int32, (CB, 2 * HD), 1)
        parts = []
        for j in range(NH // 2):
            scs = []
            for h in (2 * j, 2 * j + 1):
                sl = slice(HD * h, HD * (h + 1))
                sc = None
                for qq, kq, keep in terms:
                    t = jnp.where(keep, _bdot_nt(qq[:, sl], kq[:, sl]), 0.0)
                    sc = t if sc is None else sc + t
                scs.append(sc)
            vp = v[:, 2 * HD * j:2 * HD * (j + 1)]
            vdiag = jnp.concatenate([jnp.where(lane < HD, vp, 0.0), jnp.where(lane >= HD, vp, 0.0)], axis=0)
            parts.append(_bdot(jnp.concatenate(scs, axis=1), vdiag))
        return _concat_heads(parts)

    base = HGRN_BASE if prompt else L_S
    nbase = CB // base
    bb3 = b.reshape(nbase, base, HW)
    top = bb3[:, 0:1, :] - lf.reshape(nbase, base, HW)[:, 0:1, :]
    decay_range = jnp.max(top - bb3[:, base - 1:base, :])
    in_range = decay_range <= EXP_RANGE_MAX

    def exact_path():
        o_scr[rows, :] = diag8() + (scores_times_v(level_terms(8)) if prompt else 0.0)

    def finish(fast):
        _hgrn_finish(prompt, fast, rows, seqs, q, kk, v, gr, b, bb3, top, ti, si, scores_times_v, level_terms,
                     hg_ref, e64, oh_ref, st_scr, o_scr, is_last, stp_out, sts_in, sts_out)

    return in_range, exact_path, finish


def _hgrn_finish(prompt, fast, rows, seqs, q, kk, v, gr, b, bb3, top, ti, si, scores_times_v, level_terms,
                 hg_ref, e64, oh_ref, st_scr, o_scr, is_last, stp_out, sts_in, sts_out):
    base = HGRN_BASE if prompt else L_S
    nbase = CB // base
    qf = (q.reshape(nbase, base, HW) * jnp.exp(bb3 - top)).reshape(CB, HW)
    kf = (kk.reshape(nbase, base, HW) * jnp.exp(top - bb3)).reshape(CB, HW)
    keep = ((ti // base) == (si // base)) & (si <= ti)
    o_fast = scores_times_v([(qf, kf, keep)] + (level_terms(base) if prompt else []))
    o = jnp.where(fast, o_fast, o_scr[rows, :])
    qt = q * jnp.exp(b)
    if prompt:
        blast = b[CB - 1:CB, :]
        kd = kk * jnp.exp(blast - b)
        vt = v.T
        same_head = (ti // HD) == (si // HD)
        parts = []
        for j in range(NH // 2):
            pr = slice(2 * HD * j, 2 * HD * (j + 1))
            st = st_scr[j]
            parts.append(o[:, pr] + _bdot_nt(qt[:, pr], st))
            st_scr[j] = st * jnp.exp(blast[:, pr]) + jnp.where(same_head, _bdot(vt[pr, :], kd[:, pr]), 0.0)

        if is_last is not None:
            @pl.when(is_last)
            def _():
                stp_out[0] = st_scr[...]
    else:
        blast = _seq_last_rows(b)
        kd = kk * jnp.exp(blast - b)
        dec = jnp.exp(blast)
        vt = v.T
        parts = []
        for h in range(NH):
            sl = slice(HD * h, HD * (h + 1))
            sts = sts_in[seqs, h].reshape(SEQ_PER_CB * HD, HD)
            full = _bdot_nt(qt[:, sl], sts)
            sel = jnp.where(_stack_select((CB, SEQ_PER_CB * HD), L_S, HD), full, 0.0)
            parts.append(o[:, sl] + _fold_seq_lanes(sel))
            dec3 = dec[:, sl].reshape(SEQ_PER_CB, L_S, HD)[:, L_S - 1:L_S, :]
            dec_rows = jnp.broadcast_to(dec3, (SEQ_PER_CB, HD, HD)).reshape(SEQ_PER_CB * HD, HD)
            upd = _bdot(_seq_expand_rows(vt[sl, :]), kd[:, sl])
            sts_out[seqs, h] = (sts * dec_rows + upd).reshape(SEQ_PER_CB, HD, HD)
    oall = _concat_heads(parts)
    ms = _split_dot(oall * oall, e64, 1) * (1.0 / HD)
    oh_ref[rows, :] = oall * lax.rsqrt(ms + RMS_EPS) * hg_ref[...] * _silu(gr)


HGRN_SUB = 2


def _hgrn_body(p_ref, lb_ref, hg_ref, e64_ref, sts_in,
               oh_ref, stp_out, sts_out, st_scr, o_scr):
    i = pl.program_id(0)
    npc, steps_per_seq = NPC // HGRN_SUB, CH_PER_SEQ // HGRN_SUB

    @pl.when(i == 0)
    def _():
        o_scr[...] = jnp.zeros_like(o_scr)

    @pl.when((i < npc) & (i % steps_per_seq == 0))
    def _():
        st_scr[...] = jnp.zeros_like(st_scr)

    def run(blocks):
        fast = blocks[0][0]
        for in_range, _, _ in blocks[1:]:
            fast = jnp.logical_and(fast, in_range)

        @pl.when(jnp.logical_not(fast))
        def _():
            for _, exact_path, _ in blocks:
                exact_path()

        for _, _, finish in blocks:
            finish(fast)

    @pl.when(i < npc)
    def _():
        run([_hgrn_block(True, p_ref, lb_ref, hg_ref, e64_ref, oh_ref, st_scr, o_scr, sub,
                         is_last=(i % steps_per_seq == steps_per_seq - 1) if sub == HGRN_SUB - 1 else None,
                         stp_out=stp_out) for sub in range(HGRN_SUB)])

    @pl.when(i >= npc)
    def _():
        run([_hgrn_block(False, p_ref, lb_ref, hg_ref, e64_ref, oh_ref, st_scr, o_scr, sub,
                         sts_in=sts_in, sts_out=sts_out) for sub in range(HGRN_SUB)])


def _hgrn(ph, lb, hg, consts, st_t):
    return pl.pallas_call(
        _hgrn_body,
        grid=((NPC + NSC) // HGRN_SUB,),
        in_specs=[
            _mixer_blk(P_HG, HGRN_SUB), _const_spec((1, HW)), _const_spec((1, HW)),
            _const_spec((HW, HW)),
            _sstate_spec((NH, HD, HD), HGRN_SUB),
        ],
        out_specs=[_mixer_blk(HW, HGRN_SUB), _pstate_spec((NH // 2, 2 * HD, 2 * HD), HGRN_SUB),
                   _sstate_spec((NH, HD, HD), HGRN_SUB)],
        out_shape=[
            jax.ShapeDtypeStruct((ROWS, HW), F32),
            jax.ShapeDtypeStruct((NB_P, NH // 2, 2 * HD, 2 * HD), F32),
            jax.ShapeDtypeStruct((NB_S, NH, HD, HD), F32),
        ],
        scratch_shapes=[pltpu.VMEM((NH // 2, 2 * HD, 2 * HD), F32), pltpu.VMEM((HGRN_SUB * CB, HW), F32)],
        compiler_params=_cparams(("arbitrary",)),
        name="hgrn",
    )(ph, lb, hg, consts["e64"], st_t)


def _cmul_add(hr, hi, lr, li, sr, si):
    return hr + lr * sr - li * si, hi + lr * si + li * sr


def _s5_project(ub, bblk_ref):
    halves = [ub[:, (S5W // 2) * j:(S5W // 2) * (j + 1)] for j in range(2)]
    hr = jnp.concatenate([jnp.dot(halves[j], bblk_ref[0, j], preferred_element_type=F32) for j in range(2)], axis=1)
    hi = jnp.concatenate([jnp.dot(halves[j], bblk_ref[1, j], preferred_element_type=F32) for j in range(2)], axis=1)
    return hr, hi


def _s5_readout(h_scr, u, ccat_ref, d_ref, wglu_ref, bglu_ref):
    hs = S5N // 2
    ch = [_bdot(h_scr[:, hs * j:hs * (j + 1)], ccat_ref[j, 0])
          + _bdot(h_scr[:, S5N + hs * j:S5N + hs * (j + 1)], ccat_ref[j, 1]) for j in range(2)]
    y = jnp.concatenate(ch, axis=1) + d_ref[...] * u
    c0 = math.sqrt(2.0 / math.pi)
    y = y * (0.5 * (1.0 + jnp.tanh(c0 * (y + 0.044715 * (y * y * y)))))
    return y * jax.nn.sigmoid(_bdot(y, wglu_ref[...]) + bglu_ref[...])


S5_TB = 64
S5_ROWS = NB_P * S5_TB


def _s5_prompt_body(*refs):
    p_refs = refs[:NB_P]
    (perm_ref, permt_ref, lam_ref, bblk_ref, ccat_ref, d_ref, wglu_ref, bglu_ref,
     os_ref, st_out, carry_scr, h_scr) = refs[NB_P:]
    i = pl.program_id(0)

    @pl.when(i == 0)
    def _():
        carry_scr[...] = jnp.zeros_like(carry_scr)

    u = jnp.concatenate([r[...] for r in p_refs], axis=0)
    u_hi = u.astype(BF16)
    u_lo = (u - u_hi.astype(F32)).astype(BF16)
    perm = perm_ref[...]
    up_hi = jnp.dot(perm, u_hi, preferred_element_type=F32)
    up = up_hi + jnp.dot(perm, u_lo, preferred_element_type=F32)
    hr, hi = _s5_project(up_hi.astype(BF16), bblk_ref)
    lr = lam_ref[:, 0:S5N]
    li = lam_ref[:, S5N:2 * S5N]
    cr = carry_scr[:, 0:S5N]
    ci = carry_scr[:, S5N:2 * S5N]
    for t in range(S5_TB):
        rows = slice(NB_P * t, NB_P * (t + 1))
        cr, ci = _cmul_add(hr[rows], hi[rows], lr, li, cr, ci)
        h_scr[rows, 0:S5N] = cr
        h_scr[rows, S5N:2 * S5N] = ci
    carry_scr[:, 0:S5N] = cr
    carry_scr[:, S5N:2 * S5N] = ci
    out = _s5_readout(h_scr, up, ccat_ref, d_ref, wglu_ref, bglu_ref)
    os_ref[...] = jnp.dot(permt_ref[...], out.astype(BF16), preferred_element_type=F32).reshape(NB_P, S5_TB, S5W)

    @pl.when(i == pl.num_programs(0) - 1)
    def _():
        st_out[...] = carry_scr[...]


def _s5_prompt(ps, prm, consts):
    steps = L_P // S5_TB
    windows = [pl.BlockSpec((S5_TB, S5W), lambda i, b=b: (b * steps + i, 0)) for b in range(NB_P)]
    return pl.pallas_call(
        _s5_prompt_body,
        grid=(steps,),
        in_specs=[
            *windows,
            _const_spec((S5_ROWS, S5_ROWS)), _const_spec((S5_ROWS, S5_ROWS)), _const_spec((NB_P, 2 * S5N)),
            _const_spec((2, 2, S5W // 2, S5N // 2)), _const_spec((2, 2, S5N // 2, S5W // 2)),
            _const_spec((1, S5W)), _const_spec((S5W, S5W)), _const_spec((1, S5W)),
        ],
        out_specs=[pl.BlockSpec((NB_P, S5_TB, S5W), lambda i: (0, i, 0)), _const_spec((NB_P, 2 * S5N))],
        out_shape=[jax.ShapeDtypeStruct((NB_P, L_P, S5W), F32), jax.ShapeDtypeStruct((NB_P, 2 * S5N), F32)],
        scratch_shapes=[pltpu.VMEM((NB_P, 2 * S5N), F32), pltpu.VMEM((S5_ROWS, 2 * S5N), F32)],
        compiler_params=_cparams(("arbitrary",)),
        name="s5_prompt",
    )(*([ps] * NB_P), consts["perm"], consts["permt"], prm["lam8"], prm["bblk"], prm["ccat"], prm["d"],
      prm["wglu"], prm["bglu"])


def _s5_sample_body(p_ref, tab_ref, bblk_ref, ccat_ref, d_ref, wglu_ref, bglu_ref, s5s_in,
                    os_ref, s5s_out, h_scr):
    u = p_ref[...]
    hr, hi = _s5_project(u.astype(BF16), bblk_ref)
    nsub = CB // 8
    for idx, dsh in enumerate((1, 2, 4)):
        sr = pltpu.roll(hr, dsh, 0).reshape(nsub, 8, S5N)
        si = pltpu.roll(hi, dsh, 0).reshape(nsub, 8, S5N)
        lr = tab_ref[idx, :, 0:S5N][None]
        li = tab_ref[idx, :, S5N:2 * S5N][None]
        nr, ni = _cmul_add(hr.reshape(nsub, 8, S5N), hi.reshape(nsub, 8, S5N), lr, li, sr, si)
        hr = nr.reshape(CB, S5N)
        hi = ni.reshape(CB, S5N)
    tcr = tab_ref[3, :, 0:S5N]
    tci = tab_ref[3, :, S5N:2 * S5N]
    cr = s5s_in[:, :, 0:S5N]
    ci = s5s_in[:, :, S5N:2 * S5N]
    tr, tim = _cmul_add(hr.reshape(nsub, 8, S5N), hi.reshape(nsub, 8, S5N), tcr[None], tci[None], cr, ci)
    h_scr[:, 0:S5N] = tr.reshape(CB, S5N)
    h_scr[:, S5N:2 * S5N] = tim.reshape(CB, S5N)
    sb = lax.broadcasted_iota(jnp.int32, (SEQ_PER_CB, CB), 0)
    st = lax.broadcasted_iota(jnp.int32, (SEQ_PER_CB, CB), 1)
    sel = (st == L_S * sb + (L_S - 1)).astype(BF16)
    s5s_out[...] = _split_dot_l(sel, h_scr[...], 3)
    os_ref[...] = _s5_readout(h_scr, u, ccat_ref, d_ref, wglu_ref, bglu_ref)


def _s5_sample(ps, prm, st):
    seqs = lambda shape: pl.BlockSpec((SEQ_PER_CB,) + shape, lambda i: (i,) + (0,) * len(shape))
    return pl.pallas_call(
        _s5_sample_body,
        grid=(NSC,),
        in_specs=[
            pl.BlockSpec((CB, S5W), lambda i: (NPC + i, 0)),
            _const_spec((4, 8, 2 * S5N)), _const_spec((2, 2, S5W // 2, S5N // 2)),
            _const_spec((2, 2, S5N // 2, S5W // 2)), _const_spec((1, S5W)), _const_spec((S5W, S5W)),
            _const_spec((1, S5W)), seqs((1, 2 * S5N)),
        ],
        out_specs=[_mixer_blk(S5W), seqs((2 * S5N,))],
        out_shape=[jax.ShapeDtypeStruct((ROWS_S, S5W), F32), jax.ShapeDtypeStruct((NB_S, 2 * S5N), F32)],
        scratch_shapes=[pltpu.VMEM((CB, 2 * S5N), F32)],
        compiler_params=_cparams(("parallel",)),
        name="s5_sample",
    )(ps, prm["tab"], prm["bblk"], prm["ccat"], prm["d"], prm["wglu"], prm["bglu"], st)


def _ssd_block(prompt, p_ref, cw_ref, cb_ref, dtb_ref, aneg_ref, dx_ref, sg_ref,
               g192_ref, tril_ref, om_ref, cbuf, st_scr, sub, is_last=None, ssp_out=None, convp_out=None,
               hist_ref=None, sss_in=None, sss_out=None, convs_out=None):
    rows = slice(sub * CB, (sub + 1) * CB)
    seqs = slice(sub * SEQ_PER_CB, (sub + 1) * SEQ_PER_CB)
    z = p_ref[rows, 0:SSW]
    xbc = p_ref[rows, SSW:SSW + CONVC]
    dtr = p_ref[rows, SSW + CONVC:P_SS]
    acc = cb_ref[...] + cw_ref[3:4, :] * xbc
    if prompt:
        cbuf[8:8 + CB, :] = xbc
        for k in (1, 2, 3):
            acc = acc + cw_ref[3 - k:4 - k, :] * cbuf[8 - k:8 - k + CB, :]
        cbuf[0:8, :] = cbuf[CB:CB + 8, :]
    else:
        convs_out[rows, :] = xbc
        tl = lax.broadcasted_iota(jnp.int32, (CB, CONVC), 0) % L_S
        hist = hist_ref[rows, :]
        for k in (1, 2, 3):
            hk = hist if k == 3 else pltpu.roll(hist, CB - (3 - k), 0)
            sh = jnp.where(tl >= k, pltpu.roll(xbc, k, 0), hk)
            acc = acc + cw_ref[3 - k:4 - k, :] * sh
    xc = _silu(acc)
    xs = xc[:, 0:SSW]
    bm = xc[:, SSW:SSW + 2 * HD]
    cm = xc[:, SSW + 2 * HD:CONVC]
    xdt = dtr + dtb_ref[...]
    dt = jnp.maximum(xdt, 0.0) + jnp.log1p(jnp.exp(-jnp.abs(xdt)))
    la = dt * aneg_ref[...]
    tril = tril_ref[...]
    b6 = _cumsum_rows(la, CB if prompt else L_S)
    lane = lax.broadcasted_iota(jnp.int32, (CB, 128), 1)
    bxw = [jnp.broadcast_to(b6[:, h:h + 1], (CB, 128)) for h in range(NH)]
    dtw = [jnp.broadcast_to(dt[:, h:h + 1], (CB, 128)) for h in range(NH)]
    pair = lambda cols: _concat_heads([jnp.where(lane < HD, cols[2 * j], cols[2 * j + 1]) for j in range(NH // 2)])
    bx = pair(bxw)
    dtx = pair(dtw)
    bm_rep = _concat_heads([bm[:, 0:HD]] * 3 + [bm[:, HD:2 * HD]] * 3)
    cm_rep = _concat_heads([cm[:, 0:HD]] * 3 + [cm[:, HD:2 * HD]] * 3)
    kh = bm_rep * dtx
    qt = cm_rep * jnp.exp(bx)
    blast = bx[CB - 1:CB, :] if prompt else _seq_last_rows(bx)
    kd = kh * jnp.exp(blast - bx)
    kdt = kd.T
    mask = tril > 0

    def scores(h):
        sl = slice(HD * h, HD * (h + 1))
        bcol = bxw[h]
        decay = jnp.exp(jnp.where(mask, bcol - bcol.T, -1e30))
        return _bdot_nt(cm_rep[:, sl], kh[:, sl]) * decay

    parts = []
    if prompt:
        ri = lax.broadcasted_iota(jnp.int32, (2 * HD, 2 * HD), 0)
        ci = lax.broadcasted_iota(jnp.int32, (2 * HD, 2 * HD), 1)
        same_head = (ri // HD) == (ci // HD)
        for j in range(NH // 2):
            pr = slice(2 * HD * j, 2 * HD * (j + 1))
            xp = xs[:, pr]
            xdiag = jnp.concatenate([jnp.where(lane < HD, xp, 0.0), jnp.where(lane >= HD, xp, 0.0)], axis=0)
            st = st_scr[j]
            parts.append(_bdot(jnp.concatenate([scores(2 * j), scores(2 * j + 1)], axis=1), xdiag)
                         + _bdot(qt[:, pr], st))
            st_scr[j] = st * jnp.exp(blast[:, pr]) + jnp.where(same_head, _bdot(kdt[pr, :], xp), 0.0)
    else:
        for h in range(NH):
            sl = slice(HD * h, HD * (h + 1))
            sts = sss_in[seqs, h].reshape(SEQ_PER_CB * HD, HD)
            parts.append(_bdot(scores(h), xs[:, sl]) + _bdot(_seq_expand_lanes(qt[:, sl]), sts))
            dec3 = jnp.exp(blast[:, sl]).reshape(SEQ_PER_CB, L_S, HD)[:, L_S - 1:L_S, :]
            dec_rows = jnp.broadcast_to(dec3, (SEQ_PER_CB, HD, HD)).reshape(SEQ_PER_CB * HD, HD)
            upd = _bdot(_seq_expand_rows(kdt[sl, :]), xs[:, sl])
            sss_out[seqs, h] = (sts * dec_rows + upd).reshape(SEQ_PER_CB, HD, HD)
    if is_last is not None:
        @pl.when(is_last)
        def _():
            ssp_out[0] = st_scr[...]
            convp_out[0] = cbuf[0:8, :]
    y = (_concat_heads(parts) + dx_ref[...] * xs) * _silu(z)
    ms = _split_dot(y * y, g192_ref[...], 1) * (1.0 / (SSW // 2))
    om_ref[rows, :] = y * lax.rsqrt(ms + RMS_EPS) * sg_ref[...]


def _ssd_body(p_ref, cw_ref, cb_ref, dtb_ref, aneg_ref, dx_ref, sg_ref, g192_ref,
              trilp_ref, trils_ref, hist_ref, sss_in, *refs):
    om_ref, ssp_out, sss_out, convp_out, convs_out, cbuf, st_scr = refs[1:]
    i = pl.program_id(0)
    common = (p_ref, cw_ref, cb_ref, dtb_ref, aneg_ref, dx_ref, sg_ref, g192_ref)
    npc, steps_per_seq = NPC // SSD_SUB, CH_PER_SEQ // SSD_SUB

    @pl.when((i < npc) & (i % steps_per_seq == 0))
    def _():
        st_scr[...] = jnp.zeros_like(st_scr)
        cbuf[...] = jnp.zeros_like(cbuf)

    @pl.when(i < npc)
    def _():
        for sub in range(SSD_SUB):
            is_last = (i % steps_per_seq == steps_per_seq - 1) if sub == SSD_SUB - 1 else None
            _ssd_block(True, *common, trilp_ref, om_ref, cbuf, st_scr, sub, is_last=is_last,
                       ssp_out=ssp_out, convp_out=convp_out)

    @pl.when(i >= npc)
    def _():
        for sub in range(SSD_SUB):
            _ssd_block(False, *common, trils_ref, om_ref, cbuf, st_scr, sub, hist_ref=hist_ref,
                       sss_in=sss_in, sss_out=sss_out, convs_out=convs_out)


SSD_SUB = 2


def _ssd(pm, prm, consts, hist, st, layer, earlier):
    sample_rows = pl.BlockSpec((SSD_SUB * CB, CONVC),
                               lambda i: (jnp.clip(i - NPC // SSD_SUB, 0, NSC // SSD_SUB - 1), 0))
    in_specs = [
        _mixer_blk(P_SS, SSD_SUB), _const_spec((4, CONVC)), _const_spec((1, CONVC)),
        _const_spec((1, 128)), _const_spec((1, 128)), _const_spec((1, SSW)), _const_spec((1, SSW)),
        _const_spec((SSW, SSW)),
        _const_spec((CB, CB)), _const_spec((CB, CB)),
        sample_rows,
        _sstate_spec((NH, HD, HD), SSD_SUB),
    ]
    args = [pm, prm["cw"], prm["cb"], prm["dtb"], prm["aneg"], prm["dx"], prm["sg"],
            consts["g192"], consts["trilp"], consts["trils"], hist, st]
    in_specs.append(pl.BlockSpec(memory_space=pl.ANY))
    aliases = {len(args): 2}
    args.append(earlier)
    layer_slab = pl.BlockSpec((None, SSD_SUB * SEQ_PER_CB, NH, HD, HD),
                              lambda i: (layer, jnp.clip(i - NPC // SSD_SUB, 0, NSC // SSD_SUB - 1), 0, 0, 0))
    return pl.pallas_call(
        _ssd_body,
        grid=((NPC + NSC) // SSD_SUB,),
        in_specs=in_specs,
        out_specs=[_mixer_blk(SSW, SSD_SUB), _pstate_spec((NH // 2, 2 * HD, 2 * HD), SSD_SUB),
                   layer_slab, _pstate_spec((8, CONVC), SSD_SUB), sample_rows],
        out_shape=[
            jax.ShapeDtypeStruct((ROWS, SSW), F32),
            jax.ShapeDtypeStruct((NB_P, NH // 2, 2 * HD, 2 * HD), F32),
            jax.ShapeDtypeStruct((DEPTH, NB_S, NH, HD, HD), F32),
            jax.ShapeDtypeStruct((NB_P, 8, CONVC), F32),
            jax.ShapeDtypeStruct((ROWS_S, CONVC), F32),
        ],
        scratch_shapes=[pltpu.VMEM((CB + 8, CONVC), F32), pltpu.VMEM((NH // 2, 2 * HD, 2 * HD), F32)],
        input_output_aliases=aliases,
        compiler_params=_cparams(("arbitrary",)),
        name="ssd",
    )(*args)


def _o_core(split_x, i, oh_ref, osp_ref, oss_ref, om_ref, *refs):
    if split_x:
        x = jnp.where(i < NPT, refs[0][...], refs[1][...])
        refs = refs[2:]
    else:
        x = refs[0][...]
        refs = refs[1:]
    wo_ref, gp, gs, lng_ref, lnb_ref, scp, scs, shp, shs = refs[:9]
    os_ = jnp.where(i < NPT, osp_ref[...], oss_ref[...])
    mix = (_bdot(oh_ref[...], wo_ref[0:HW, :]) + _bdot(os_, wo_ref[HW:HW + S5W, :])
           + _bdot(om_ref[...], wo_ref[HW + S5W:D, :]))
    x1 = _layer_norm(ALPHA * x + _rowmod(i, gp, gs) * mix, lng_ref[...], lnb_ref[...])
    h2 = x1 * (1.0 + _rowmod(i, scp, scs)) + _rowmod(i, shp, shs)
    return x1, h2, refs[9:]


def _o_body(split_x, *refs):
    i = pl.program_id(0)
    x1, h2, (x1_ref, h2_ref) = _o_core(split_x, i, *refs)
    x1_ref[...] = x1
    h2_ref[...] = h2.astype(BF16)


def _o_router_body(split_x, *refs):
    i = pl.program_id(0)
    x1, h2, (wr_ref, br_ref, x1_ref, h2_ref, route_ref) = _o_core(split_x, i, *refs)
    x1_ref[...] = x1
    h2_ref[...] = h2
    h_hi = h2.astype(BF16)
    h_lo = (h2 - h_hi.astype(F32)).astype(BF16)
    logits = (jnp.dot(h_hi, wr_ref[0], preferred_element_type=F32)
              + jnp.dot(h_lo, wr_ref[0], preferred_element_type=F32)
              + jnp.dot(h_hi, wr_ref[1], preferred_element_type=F32)) + br_ref[...]
    lane = lax.broadcasted_iota(jnp.int32, (TM, 128), 1).astype(F32)
    neg = -jnp.inf
    lg = jnp.where(lane < NEXP, logits, neg)
    m1 = jnp.max(lg, axis=-1, keepdims=True)
    i1 = jnp.min(jnp.where(lg == m1, lane, 128.0), axis=-1, keepdims=True)
    lg2 = jnp.where(lane == i1, neg, lg)
    m2 = jnp.max(lg2, axis=-1, keepdims=True)
    i2 = jnp.min(jnp.where(lg2 == m2, lane, 128.0), axis=-1, keepdims=True)
    e2 = jnp.exp(m2 - m1)
    den = 1.0 + e2
    route_ref[...] = jnp.where(lane == 0.0, i1, jnp.where(lane == 1.0, i2,
                               jnp.where(lane == 2.0, 1.0 / den, jnp.where(lane == 3.0, e2 / den, 0.0))))


def _stage_o(layer, oh, os_p, os_s, om, x, wo_b, ada4, ln_g, ln_b, router=None):
    split_x = isinstance(x, tuple)
    x_specs = [_prompt_rows_spec(D), _sample_rows_spec(D)] if split_x else [_row_spec(D)]
    in_specs = [
        _row_spec(HW), _prompt_rows_spec(S5W), _sample_rows_spec(S5W), _row_spec(SSW), *x_specs,
        _const_spec((D, D)),
        *_mod_specs(layer, 2), _const_spec((1, D)), _const_spec((1, D)),
        *_mod_specs(layer, 4), *_mod_specs(layer, 3),
    ]
    args = [oh, os_p, os_s, om, *(x if split_x else (x,)), wo_b, ada4, ada4, ln_g, ln_b, ada4, ada4, ada4, ada4]
    out_specs = [_row_spec(D), _row_spec(D)]
    out_shape = [jax.ShapeDtypeStruct((ROWS, D), F32), jax.ShapeDtypeStruct((ROWS, D), BF16)]
    body = _o_body
    if router is not None:
        in_specs += [_const_spec((2, D, 128)), _const_spec((1, 128))]
        args += list(router)
        out_specs.append(_row_spec(128))
        out_shape[1] = jax.ShapeDtypeStruct((ROWS, D), F32)
        out_shape.append(jax.ShapeDtypeStruct((ROWS, 128), F32))
        body = _o_router_body
    return pl.pallas_call(
        functools.partial(body, split_x), grid=(NT,), in_specs=in_specs, out_specs=out_specs,
        out_shape=out_shape,
        compiler_params=_cparams(("parallel",)),
        name="stage_o_router" if router is not None else "stage_o",
    )(*args)


def _ffn_body(te_ref, nu_ref, h_ref, wg_ref, wu_ref, wd_ref, o_ref, acc_ref):
    i = pl.program_id(0)
    j = pl.program_id(1)

    @pl.when(j == 0)
    def _():
        acc_ref[...] = jnp.zeros_like(acc_ref)

    @pl.when(i < nu_ref[0])
    def _():
        h = h_ref[...].astype(BF16)
        g = jnp.dot(h, wg_ref[...], preferred_element_type=F32)
        u = jnp.dot(h, wu_ref[...], preferred_element_type=F32)
        act = (_silu(g) * u).astype(BF16)
        acc_ref[...] += jnp.dot(act, wd_ref[...], preferred_element_type=F32)

    @pl.when(j == pl.num_programs(1) - 1)
    def _():
        o_ref[...] = acc_ref[...]


def _ffn(tile_expert, n_used, h, wg, wu, wd):
    rows = h.shape[0]
    nj = DFF // TF

    def jblk(i, j, nu):
        return jnp.where(i < nu[0], j, nj - 1)

    grid_spec = pltpu.PrefetchScalarGridSpec(
        num_scalar_prefetch=2,
        grid=(rows // TMF, nj),
        in_specs=[
            pl.BlockSpec((TMF, D), lambda i, j, te, nu: (i, 0)),
            pl.BlockSpec((None, D, TF), lambda i, j, te, nu: (te[i], 0, jblk(i, j, nu))),
            pl.BlockSpec((None, D, TF), lambda i, j, te, nu: (te[i], 0, jblk(i, j, nu))),
            pl.BlockSpec((None, TF, D), lambda i, j, te, nu: (te[i], jblk(i, j, nu), 0)),
        ],
        out_specs=pl.BlockSpec((TMF, D), lambda i, j, te, nu: (i, 0)),
        scratch_shapes=[pltpu.VMEM((TMF, D), F32)],
    )
    return pl.pallas_call(
        _ffn_body, grid_spec=grid_spec,
        out_shape=jax.ShapeDtypeStruct((rows, D), F32),
        compiler_params=_cparams(("parallel", "arbitrary")),
        name="ffn",
    )(tile_expert, n_used, h, wg, wu, wd)


def _final_body(x1_ref, ya_ref, yb_ref, route_ref, gp, gs, lng_ref, lnb_ref, yp_ref, ys_ref):
    i = pl.program_id(0)
    f = route_ref[:, 2:3] * ya_ref[...] + route_ref[:, 3:4] * yb_ref[...]
    y = _layer_norm(ALPHA * x1_ref[...] + _rowmod(i, gp, gs) * f, lng_ref[...], lnb_ref[...])

    @pl.when(i < NPT)
    def _():
        yp_ref[...] = y

    @pl.when(i >= NPT)
    def _():
        ys_ref[...] = y


def _stage_final(layer, x1, ya, yb, route, ada4, ln_g, ln_b):
    return pl.pallas_call(
        _final_body,
        grid=(NT,),
        in_specs=[
            _row_spec(D), _row_spec(D), _row_spec(D), _row_spec(128),
            *_mod_specs(layer, 5), _const_spec((1, D)), _const_spec((1, D)),
        ],
        out_specs=[
            pl.BlockSpec((TM, D), lambda i: (jnp.minimum(i, NPT - 1), 0)),
            pl.BlockSpec((TM, D), lambda i: (jnp.clip(i - NPT, 0, NST - 1), 0)),
        ],
        out_shape=[jax.ShapeDtypeStruct((ROWS_P, D), F32), jax.ShapeDtypeStruct((ROWS_S, D), F32)],
        compiler_params=_cparams(("arbitrary",)),
        name="stage_final",
    )(x1, ya, yb, route, ada4, ada4, ln_g, ln_b)


def _block_ones(n, blk):
    r = np.arange(n) // blk
    return r[:, None] == r[None, :]


def _consts():
    t = np.arange(CB)
    causal = t[:, None] >= t[None, :]
    same_seq = (t[:, None] // L_S) == (t[None, :] // L_S)
    r = np.arange(S5_ROWS)
    perm = r[None, :] == ((r % NB_P) * S5_TB + r // NB_P)[:, None]
    mats = {
        "perm": perm,
        "permt": perm.T,
        "e64": _block_ones(HW, HD),
        "g192": _block_ones(SSW, SSW // 2),
        "trilp": causal,
        "trils": causal & same_seq,
    }
    return {k: jnp.asarray(v.astype(np.float32), dtype=BF16) for k, v in mats.items()}


def _s5_params(a_re, a_im, log_dt, b_re, b_im, c_re, c_im, d, w_glu, b_glu):
    dt = jnp.exp(log_dt)[:, None]
    mag = jnp.exp(a_re * dt)
    lam_re, lam_im = mag * jnp.cos(a_im * dt), mag * jnp.sin(a_im * dt)
    den = a_re * a_re + a_im * a_im
    nr, ni = lam_re - 1.0, lam_im
    zr = (nr * a_re + ni * a_im) / den
    zi = (ni * a_re - nr * a_im) / den
    bbar_re = zr[..., None] * b_re - zi[..., None] * b_im
    bbar_im = zr[..., None] * b_im + zi[..., None] * b_re
    eye = jnp.eye(16, dtype=F32)
    blk = lambda bb: jnp.einsum('gph,gk->ghkp', bb, eye).reshape(S5W, S5N)
    hu, hs = S5W // 2, S5N // 2
    bblk = jnp.stack([jnp.stack([blk(bb)[hu * j:hu * (j + 1), hs * j:hs * (j + 1)] for j in range(2)])
                      for bb in (bbar_re, bbar_im)]).astype(BF16)
    cblk = lambda cc: jnp.einsum('ghp,gk->gpkh', cc, eye).reshape(S5N, S5W)
    ccat = jnp.stack([jnp.stack([cblk(cc)[hs * j:hs * (j + 1), hu * j:hu * (j + 1)] for cc in (c_re, -c_im)])
                      for j in range(2)]).astype(BF16)
    lr, li = lam_re.reshape(-1), lam_im.reshape(-1)
    pows = [(jnp.ones_like(lr), jnp.zeros_like(li))]
    for _ in range(8):
        pr, pi = pows[-1]
        pows.append((pr * lr - pi * li, pr * li + pi * lr))
    rows = jnp.arange(8)[:, None]
    tabs = []
    for dsh in (1, 2, 4):
        pr, pi = pows[dsh]
        tabs.append(jnp.where(rows >= dsh, jnp.concatenate([pr, pi])[None, :], 0.0))
    tabs.append(jnp.stack([jnp.concatenate(pows[r + 1]) for r in range(8)]))
    return {
        "tab": jnp.stack(tabs), "bblk": bblk, "ccat": ccat,
        "lam8": jnp.broadcast_to(jnp.concatenate([lr, li])[None, :], (NB_P, 2 * S5N)),
        "d": d.reshape(1, S5W), "wglu": w_glu.astype(BF16), "bglu": b_glu.reshape(1, S5W),
    }


def _pair_diag(st):
    s6 = st.reshape(NB_P, NH // 2, 2, HD, 2, HD)
    return jnp.stack([s6[:, :, 0, :, 0, :], s6[:, :, 1, :, 1, :]], axis=2).reshape(NB_P, NH, HD, HD)


def _pad_lanes(v, n=128):
    return jnp.pad(v, (0, n - v.shape[0])).reshape(1, n)


def kernel(x_prompt, x_sample, c_prompt, c_sample, state_hgrn, state_s5, state_ssd, state_ssd_conv, w_ada, b_ada, ln_g, ln_b, w_in, w_out, hgrn_lb_logits, hgrn_norm_g, s5_a_re, s5_a_im, s5_log_dt, s5_b_re, s5_b_im, s5_c_re, s5_c_im, s5_d, s5_w_glu, s5_b_glu, ssd_conv_w, ssd_conv_b, ssd_dt_bias, ssd_a_log, ssd_d, ssd_norm_g, ffn_w_gate, ffn_w_up, ffn_w_down, moe_w_router, moe_b_router, moe_w_gate, moe_w_up, moe_w_down):
    consts = _consts()
    c_all = jnp.concatenate([c_sample, c_prompt], axis=0)
    ada4 = _ada(c_all, w_ada, b_ada).reshape(DEPTH, NB_S + NB_P, 1, 6 * D)

    lb_all = jnp.cumsum(jax.nn.softmax(hgrn_lb_logits, axis=0), axis=0)
    lb_all = lb_all - lb_all[0]

    xp = x_prompt.reshape(ROWS_P, D)
    xs = x_sample.reshape(ROWS_S, D)
    new_h, new_s5, new_m, new_c = [], [], [], []
    x1 = f = route = None
    ss_s = jnp.zeros((DEPTH, NB_S, NH, HD, HD), F32)
    for l in range(DEPTH):
        w_in_b = jnp.pad(w_in[l], ((0, 0), (0, N_IN_PAD - N_IN))).astype(BF16)
        if l == 0:
            x = (xp, xs)
            ph, ps, pm = _stage_a0(xp, xs, ada4, w_in_b)
        else:
            x, ph, ps, pm = _stage_a1(l, x1, f, ada4, ln_g[l - 1, 1].reshape(1, D),
                                      ln_b[l - 1, 1].reshape(1, D), w_in_b)
        oh, hg_p, hg_s = _hgrn(ph, lb_all[l].reshape(1, HW), hgrn_norm_g[l].reshape(1, HW), consts,
                               jnp.swapaxes(state_hgrn[l], -1, -2))
        s5p = _s5_params(s5_a_re[l], s5_a_im[l], s5_log_dt[l], s5_b_re[l], s5_b_im[l], s5_c_re[l], s5_c_im[l],
                         s5_d[l], s5_w_glu[l], s5_b_glu[l])
        os_p, s5_p = _s5_prompt(ps, s5p, consts)
        os_s, s5_s = _s5_sample(ps, s5p, state_s5[l].reshape(NB_S, 1, 2 * S5N))
        os_p = os_p.reshape(ROWS_P, S5W)
        ssd_prm = {
            "cw": ssd_conv_w[l], "cb": ssd_conv_b[l].reshape(1, CONVC),
            "dtb": _pad_lanes(ssd_dt_bias[l]), "aneg": _pad_lanes(-jnp.exp(ssd_a_log[l])),
            "dx": jnp.repeat(ssd_d[l], HD).reshape(1, SSW), "sg": ssd_norm_g[l].reshape(1, SSW),
        }
        hist = jnp.pad(state_ssd_conv[l], ((0, 0), (0, L_S - 3), (0, 0))).reshape(ROWS_S, CONVC)
        om, ss_p, ss_s, tail_p, xbc_s = _ssd(pm, ssd_prm, consts, hist, state_ssd[l], l, ss_s)
        conv_p = tail_p[:, 8 - 3:]
        conv_s = xbc_s.reshape(NB_S, L_S, CONVC)[:, L_S - 3:]
        hg_p, ss_p = _pair_diag(hg_p), _pair_diag(ss_p)
        new_h.append((jnp.swapaxes(hg_p, -1, -2), jnp.swapaxes(hg_s, -1, -2)))
        new_s5.append((s5_p.reshape(NB_P, 2, 16, 64), s5_s.reshape(NB_S, 2, 16, 64)))
        new_m.append((ss_p,))
        new_c.append((conv_p, conv_s))

        wo_b = w_out[l].astype(BF16)
        lg, lbb = ln_g[l, 0].reshape(1, D), ln_b[l, 0].reshape(1, D)
        j = l // 2
        if l % 2 == 0:
            x1, h2 = _stage_o(l, oh, os_p, os_s, om, x, wo_b, ada4, lg, lbb)
            f = _ffn(jnp.zeros((ROWS // TMF,), jnp.int32), jnp.full((1,), ROWS // TMF, jnp.int32), h2,
                     ffn_w_gate[j:j + 1].astype(BF16),
                     ffn_w_up[j:j + 1].astype(BF16), ffn_w_down[j:j + 1].astype(BF16))
        else:
            wr = jnp.pad(moe_w_router[j], ((0, 0), (0, 128 - NEXP)))
            wr_hi = wr.astype(BF16)
            wr = jnp.stack([wr_hi, (wr - wr_hi.astype(F32)).astype(BF16)])
            br = _pad_lanes(moe_b_router[j])
            x1, h2, route = _stage_o(l, oh, os_p, os_s, om, x, wo_b, ada4, lg, lbb, router=(wr, br))
            wg_b, wu_b, wd_b = (moe_w_gate[j].astype(BF16), moe_w_up[j].astype(BF16),
                                moe_w_down[j].astype(BF16))
            flat_e = route[:, 0:2].astype(jnp.int32).reshape(-1)
            onehot = (flat_e[:, None] == jnp.arange(NEXP)[None, :]).astype(jnp.int32)
            csum = jnp.cumsum(onehot, axis=0)
            counts = csum[-1]
            rank = jnp.take_along_axis(csum, flat_e[:, None], axis=1)[:, 0] - 1
            padded = ((counts + TMF - 1) // TMF) * TMF
            pend = jnp.cumsum(padded)
            pstart = pend - padded
            dest = pstart[flat_e] + rank
            n_pad = 2 * ROWS + NEXP * TMF
            src_tok = jnp.zeros((n_pad,), jnp.int32).at[dest].set(jnp.arange(2 * ROWS, dtype=jnp.int32) // 2)
            tile_start = jnp.arange(n_pad // TMF, dtype=jnp.int32) * TMF
            tile_e = jnp.minimum(jnp.sum((pend[None, :] <= tile_start[:, None]).astype(jnp.int32), axis=1),
                                 NEXP - 1)
            n_used = (pend[NEXP - 1:NEXP] // TMF).astype(jnp.int32)
            y_sorted = _ffn(tile_e, n_used, h2[src_tok], wg_b, wu_b, wd_b)
            pos = dest.reshape(ROWS, 2)
            ya = y_sorted[pos[:, 0]]
            yb = y_sorted[pos[:, 1]]
    y_p, y_s = _stage_final(DEPTH - 1, x1, ya, yb, route, ada4, ln_g[DEPTH - 1, 1].reshape(1, D),
                            ln_b[DEPTH - 1, 1].reshape(1, D))
    stack = lambda lst, k: jnp.stack([t[k] for t in lst])
    return (y_p.reshape(NB_P, L_P, D), y_s.reshape(NB_S, L_S, D),
            stack(new_h, 0), stack(new_s5, 0), stack(new_m, 0), stack(new_c, 0),
            stack(new_h, 1), stack(new_s5, 1), ss_s, stack(new_c, 1))
```

```python
import functools
import math

import jax
import jax.numpy as jnp
import numpy as np
from jax import lax
from jax.experimental import pallas as pl
from jax.experimental.pallas import tpu as pltpu

F32 = jnp.float32
BF16 = jnp.bfloat16

D = 1024
NB_P, L_P = 8, 2048
NB_S, L_S = 128, 8
ROWS_P = NB_P * L_P
ROWS_S = NB_S * L_S
ROWS = ROWS_P + ROWS_S
DEPTH = 2
HW = 384
S5W = 256
SSW = 384
NH = 6
HD = 64
S5N = 1024
CONVC = 640
N_IN = 2822
N_IN_PAD = 2944
P_HG = 1536
P_SS = 1152
DFF = 2816
NEXP = 8
ALPHA = (2 * DEPTH) ** 0.25
LN_EPS = 1e-5
RMS_EPS = 1e-6

TM = 512
NPT = ROWS_P // TM
NST = ROWS_S // TM
NT = NPT + NST
SEQ_PER_TILE = TM // L_S
TILES_PER_SEQ = L_P // TM

CB = 128
NPC = ROWS_P // CB
NSC = ROWS_S // CB
CH_PER_SEQ = L_P // CB
SEQ_PER_CB = CB // L_S

TMF = 512
TF = 1408
MOE_PARTS = 4
VMEM_LIMIT = 56 * 1024 * 1024


def _cparams(sem):
    return pltpu.CompilerParams(dimension_semantics=sem, vmem_limit_bytes=VMEM_LIMIT)


def _bdot(a, b):
    return jnp.dot(a.astype(BF16), b.astype(BF16), preferred_element_type=F32)


def _bdot_nt(a, b):
    return lax.dot_general(a.astype(BF16), b.astype(BF16), (((1,), (1,)), ((), ())),
                           preferred_element_type=F32)


def _split_dot(x, e, passes):
    acc = None
    r = x
    for _ in range(passes):
        hi = r.astype(BF16)
        d = jnp.dot(hi, e, preferred_element_type=F32)
        acc = d if acc is None else acc + d
        r = r - hi.astype(F32)
    return acc


def _split_dot_l(e, x, passes):
    acc = None
    r = x
    for _ in range(passes):
        hi = r.astype(BF16)
        d = jnp.dot(e, hi, preferred_element_type=F32)
        acc = d if acc is None else acc + d
        r = r - hi.astype(F32)
    return acc


def _silu(x):
    return x * jax.nn.sigmoid(x)


def _layer_norm(x, g, b):
    mu = jnp.mean(x, -1, keepdims=True)
    xc = x - mu
    var = jnp.mean(xc * xc, -1, keepdims=True)
    return xc * lax.rsqrt(var + LN_EPS) * g + b


def _rowmod(i, p_ref, s_ref):
    s = jnp.broadcast_to(s_ref[...], (SEQ_PER_TILE, L_S, D)).reshape(TM, D)
    return jnp.where(i < NPT, p_ref[0], s)


ADA_TN = 1536


def _ada_body(c_ref, w_ref, b_ref, o_ref):
    o_ref[...] = _bdot(_silu(c_ref[...]), w_ref[...]) + b_ref[...]


def _ada(c_all, w_ada, b_ada):
    nc = c_all.shape[0]
    return pl.pallas_call(
        _ada_body,
        grid=(DEPTH, 6 * D // ADA_TN),
        in_specs=[
            pl.BlockSpec((nc, D), lambda l, j: (0, 0)),
            pl.BlockSpec((None, D, ADA_TN), lambda l, j: (l, 0, j)),
            pl.BlockSpec((None, 1, ADA_TN), lambda l, j: (l, 0, j)),
        ],
        out_specs=pl.BlockSpec((None, nc, ADA_TN), lambda l, j: (l, 0, j)),
        out_shape=jax.ShapeDtypeStruct((DEPTH, nc, 6 * D), F32),
        compiler_params=_cparams(("parallel", "parallel")),
        name="ada",
    )(c_all, w_ada, b_ada.reshape(DEPTH, 1, 6 * D))


def _mod_specs(layer, k):
    ps = pl.BlockSpec((None, 1, 1, D),
                      lambda i: (layer, NB_S + jnp.minimum(i // TILES_PER_SEQ, NB_P - 1), 0, k))
    ss = pl.BlockSpec((None, SEQ_PER_TILE, 1, D),
                      lambda i: (layer, jnp.clip(i - NPT, 0, NST - 1), 0, k))
    return [ps, ss]


def _row_spec(width):
    return pl.BlockSpec((TM, width), lambda i: (i, 0))


def _const_spec(shape):
    nd = len(shape)
    return pl.BlockSpec(shape, lambda *_: (0,) * nd)


def _prompt_rows_spec(width):
    return pl.BlockSpec((TM, width), lambda i: (jnp.minimum(i, NPT - 1), 0))


def _sample_rows_spec(width):
    return pl.BlockSpec((TM, width), lambda i: (jnp.clip(i - NPT, 0, NST - 1), 0))


def _proj_out(x, i, scp, scs, shp, shs, w_ref, ph_ref, ps_ref, pm_ref):
    h = x * (1.0 + _rowmod(i, scp, scs)) + _rowmod(i, shp, shs)
    proj = jnp.dot(h.astype(BF16), w_ref[...], preferred_element_type=F32)
    ph_ref[...] = proj[:, 0:P_HG]
    ps_ref[...] = proj[:, P_HG:P_HG + S5W]
    pm_ref[...] = proj[:, P_HG + S5W:N_IN_PAD]


def _a0_body(xp_ref, xs_ref, scp, scs, shp, shs, w_ref, ph_ref, ps_ref, pm_ref):
    i = pl.program_id(0)
    x = jnp.where(i < NPT, xp_ref[...], xs_ref[...])
    _proj_out(x, i, scp, scs, shp, shs, w_ref, ph_ref, ps_ref, pm_ref)


def _a1_body(x1_ref, f_ref, gp, gs, lng_ref, lnb_ref, scp, scs, shp, shs, w_ref,
             x_ref, ph_ref, ps_ref, pm_ref):
    i = pl.program_id(0)
    x = _layer_norm(ALPHA * x1_ref[...] + _rowmod(i, gp, gs) * f_ref[...], lng_ref[...], lnb_ref[...])
    x_ref[...] = x
    _proj_out(x, i, scp, scs, shp, shs, w_ref, ph_ref, ps_ref, pm_ref)


def _a_out():
    specs = [_row_spec(P_HG), _row_spec(S5W), _row_spec(P_SS)]
    shapes = [jax.ShapeDtypeStruct((ROWS, w), F32) for w in (P_HG, S5W, P_SS)]
    return specs, shapes


def _stage_a0(xp, xs, ada4, w_in_b):
    out_specs, out_shape = _a_out()
    return pl.pallas_call(
        _a0_body,
        grid=(NT,),
        in_specs=[
            _prompt_rows_spec(D), _sample_rows_spec(D),
            *_mod_specs(0, 1), *_mod_specs(0, 0),
            _const_spec((D, N_IN_PAD)),
        ],
        out_specs=out_specs, out_shape=out_shape,
        compiler_params=_cparams(("parallel",)),
        name="stage_a0",
    )(xp, xs, ada4, ada4, ada4, ada4, w_in_b)


def _stage_a1(layer, x1, f, ada4, ln_g, ln_b, w_in_b):
    out_specs, out_shape = _a_out()
    return pl.pallas_call(
        _a1_body,
        grid=(NT,),
        in_specs=[
            _row_spec(D), _row_spec(D),
            *_mod_specs(layer - 1, 5),
            _const_spec((1, D)), _const_spec((1, D)),
            *_mod_specs(layer, 1), *_mod_specs(layer, 0),
            _const_spec((D, N_IN_PAD)),
        ],
        out_specs=[_row_spec(D)] + out_specs, out_shape=[jax.ShapeDtypeStruct((ROWS, D), F32)] + out_shape,
        compiler_params=_cparams(("parallel",)),
        name="stage_a1",
    )(x1, f, ada4, ada4, ln_g, ln_b, ada4, ada4, ada4, ada4, w_in_b)


def _mixer_blk(width, nsub=1):
    return pl.BlockSpec((nsub * CB, width), lambda i: (i, 0))


def _pstate_spec(shape, nsub=1):
    nd = len(shape)
    return pl.BlockSpec((1,) + shape,
                        lambda i: (jnp.minimum(i // (CH_PER_SEQ // nsub), NB_P - 1),) + (0,) * nd)


def _sstate_spec(shape, nsub=1):
    nd = len(shape)
    return pl.BlockSpec((nsub * SEQ_PER_CB,) + shape,
                        lambda i: (jnp.clip(i - NPC // nsub, 0, NSC // nsub - 1),) + (0,) * nd)


def _cumsum_rows(x, span):
    r = lax.broadcasted_iota(jnp.int32, x.shape, 0) & (span - 1)
    d = 1
    while d < span:
        x = x + jnp.where(r >= d, pltpu.roll(x, d, 0), 0.0)
        d *= 2
    return x


def _seq_last_rows(x):
    w = x.shape[-1]
    x3 = x.reshape(SEQ_PER_CB, L_S, w)
    return jnp.broadcast_to(x3[:, L_S - 1:L_S, :], (SEQ_PER_CB, L_S, w)).reshape(CB, w)


def _concat_heads(parts):
    return jnp.concatenate(parts, axis=1)


def _stack_select(shape, row_div, lane_div):
    r = lax.broadcasted_iota(jnp.int32, shape, 0) // row_div
    c = lax.broadcasted_iota(jnp.int32, shape, 1) // lane_div
    return r == c


def _seq_expand_lanes(qh):
    q2 = jnp.concatenate([qh, qh], axis=1)
    q16 = jnp.concatenate([q2] * (SEQ_PER_CB // 2), axis=1)
    return jnp.where(_stack_select((CB, SEQ_PER_CB * HD), L_S, HD), q16, 0.0)


def _seq_expand_rows(xt):
    t = jnp.broadcast_to(xt[None], (SEQ_PER_CB, HD, CB)).reshape(SEQ_PER_CB * HD, CB)
    return jnp.where(_stack_select((SEQ_PER_CB * HD, CB), HD, L_S), t, 0.0)


def _fold_seq_lanes(full):
    acc = full[:, 0:128]
    for j in range(1, SEQ_PER_CB * HD // 128):
        acc = acc + full[:, 128 * j:128 * (j + 1)]
    return acc[:, 0:HD] + acc[:, HD:2 * HD]


HGRN_BASE = 32
EXP_RANGE_MAX = 80.0


def _hgrn_block(prompt, p_ref, lb_ref, hg_ref, e64_ref, oh_ref, st_scr, o_scr, sub, is_last=None,
                stp_out=None, sts_in=None, sts_out=None):
    rows = slice(sub * CB, (sub + 1) * CB)
    seqs = slice(sub * SEQ_PER_CB, (sub + 1) * SEQ_PER_CB)
    lb = lb_ref[...]
    qr = p_ref[rows, 0:HW]
    fr = p_ref[rows, HW:2 * HW]
    v = p_ref[rows, 2 * HW:3 * HW]
    gr = p_ref[rows, 3 * HW:4 * HW]
    e = jnp.exp(-jnp.abs(fr))
    ope = 1.0 + e
    ls = jnp.minimum(fr, 0.0) - jnp.log(ope)
    a = jnp.log(lb)
    bb = jnp.log1p(-lb) + ls
    lf = jnp.maximum(a, bb) + jnp.log(1.0 + jnp.exp(-jnp.abs(a - bb)))
    rcp = 1.0 / ope
    kk = (1.0 - lb) * jnp.where(fr >= 0.0, e * rcp, rcp)
    q = _silu(qr)
    b = _cumsum_rows(lf, CB if prompt else L_S)

    e64 = e64_ref[...]
    ti = lax.broadcasted_iota(jnp.int32, (CB, CB), 0)
    si = lax.broadcasted_iota(jnp.int32, (CB, CB), 1)

    def diag8():
        nsub = CB // 8
        b3 = b.reshape(nsub, 8, HW)
        q3 = q.reshape(nsub, 8, HW)
        k3 = kk.reshape(nsub, 8, HW)
        v3 = v.reshape(nsub, 8, HW)
        r3 = lax.broadcasted_iota(jnp.int32, (nsub, 8, HW), 1)
        o = jnp.zeros((CB, HW), F32)
        for s in range(8):
            dlt = jnp.minimum(b3 - b3[:, s:s + 1, :], 0.0)
            w = jnp.where(r3 >= s, jnp.exp(dlt), 0.0) * q3 * k3[:, s:s + 1, :]
            hsum = jnp.dot(w.reshape(CB, HW).astype(BF16), e64, preferred_element_type=F32)
            o = o + hsum * jnp.broadcast_to(v3[:, s:s + 1, :], (nsub, 8, HW)).reshape(CB, HW)
        return o

    def level_terms(m):
        terms = []
        while m < CB:
            nb = CB // (2 * m)
            b4 = b.reshape(nb, 2 * m, HW)
            bmid = b4[:, m - 1:m, :]
            pos = lax.broadcasted_iota(jnp.int32, (nb, 2 * m, HW), 1)
            qq = jnp.where(pos >= m, q.reshape(nb, 2 * m, HW) * jnp.exp(jnp.minimum(b4 - bmid, 0.0)), 0.0)
            kq = jnp.where(pos < m, kk.reshape(nb, 2 * m, HW) * jnp.exp(jnp.minimum(bmid - b4, 0.0)), 0.0)
            terms.append((qq.reshape(CB, HW), kq.reshape(CB, HW), (ti // (2 * m)) == (si // (2 * m))))
            m *= 2
        return terms

    def scores_times_v(terms):
        lane = lax.broadcasted_iota(jnp.int32, (CB, 2 * HD), 1)
        parts = []
        for j in range(NH // 2):
            scs = []
            for h in (2 * j, 2 * j + 1):
                sl = slice(HD * h, HD * (h + 1))
                sc = None
                for qq, kq, keep in terms:
                    t = jnp.where(keep, _bdot_nt(qq[:, sl], kq[:, sl]), 0.0)
                    sc = t if sc is None else sc + t
                scs.append(sc)
            vp = v[:, 2 * HD * j:2 * HD * (j + 1)]
            vdiag = jnp.concatenate([jnp.where(lane < HD, vp, 0.0), jnp.where(lane >= HD, vp, 0.0)], axis=0)
            parts.append(_bdot(jnp.concatenate(scs, axis=1), vdiag))
        return _concat_heads(parts)

    base = HGRN_BASE if prompt else L_S
    nbase = CB // base
    bb3 = b.reshape(nbase, base, HW)
    top = bb3[:, 0:1, :] - lf.reshape(nbase, base, HW)[:, 0:1, :]
    decay_range = jnp.max(top - bb3[:, base - 1:base, :])
    in_range = decay_range <= EXP_RANGE_MAX

    def exact_path():
        o_scr[rows, :] = diag8() + (scores_times_v(level_terms(8)) if prompt else 0.0)

    def finish(fast):
        _hgrn_finish(prompt, fast, rows, seqs, q, kk, v, gr, b, bb3, top, ti, si, scores_times_v, level_terms,
                     hg_ref, e64, oh_ref, st_scr, o_scr, is_last, stp_out, sts_in, sts_out)

    return in_range, exact_path, finish


def _hgrn_finish(prompt, fast, rows, seqs, q, kk, v, gr, b, bb3, top, ti, si, scores_times_v, level_terms,
                 hg_ref, e64, oh_ref, st_scr, o_scr, is_last, stp_out, sts_in, sts_out):
    base = HGRN_BASE if prompt else L_S
    nbase = CB // base
    qf = (q.reshape(nbase, base, HW) * jnp.exp(bb3 - top)).reshape(CB, HW)
    kf = (kk.reshape(nbase, base, HW) * jnp.exp(top - bb3)).reshape(CB, HW)
    keep = ((ti // base) == (si // base)) & (si <= ti)
    o_fast = scores_times_v([(qf, kf, keep)] + (level_terms(base) if prompt else []))
    o = jnp.where(fast, o_fast, o_scr[rows, :])
    qt = q * jnp.exp(b)
    if prompt:
        blast = b[CB - 1:CB, :]
        kd = kk * jnp.exp(blast - b)
        vt = v.T
        same_head = (ti // HD) == (si // HD)
        parts = []
        for j in range(NH // 2):
            pr = slice(2 * HD * j, 2 * HD * (j + 1))
            st = st_scr[j]
            parts.append(o[:, pr] + _bdot_nt(qt[:, pr], st))
            st_scr[j] = st * jnp.exp(blast[:, pr]) + jnp.where(same_head, _bdot(vt[pr, :], kd[:, pr]), 0.0)

        if is_last is not None:
            @pl.when(is_last)
            def _():
                stp_out[0] = st_scr[...]
    else:
        blast = _seq_last_rows(b)
        kd = kk * jnp.exp(blast - b)
        dec = jnp.exp(blast)
        vt = v.T
        parts = []
        for h in range(NH):
            sl = slice(HD * h, HD * (h + 1))
            sts = sts_in[seqs, h].reshape(SEQ_PER_CB * HD, HD)
            full = _bdot_nt(qt[:, sl], sts)
            sel = jnp.where(_stack_select((CB, SEQ_PER_CB * HD), L_S, HD), full, 0.0)
            parts.append(o[:, sl] + _fold_seq_lanes(sel))
            dec3 = dec[:, sl].reshape(SEQ_PER_CB, L_S, HD)[:, L_S - 1:L_S, :]
            dec_rows = jnp.broadcast_to(dec3, (SEQ_PER_CB, HD, HD)).reshape(SEQ_PER_CB * HD, HD)
            upd = _bdot(_seq_expand_rows(vt[sl, :]), kd[:, sl])
            sts_out[seqs, h] = (sts * dec_rows + upd).reshape(SEQ_PER_CB, HD, HD)
    oall = _concat_heads(parts)
    ms = _split_dot(oall * oall, e64, 1) * (1.0 / HD)
    oh_ref[rows, :] = oall * lax.rsqrt(ms + RMS_EPS) * hg_ref[...] * _silu(gr)


HGRN_SUB = 2


def _hgrn_body(p_ref, lb_ref, hg_ref, e64_ref, sts_in,
               oh_ref, stp_out, sts_out, st_scr, o_scr):
    i = pl.program_id(0)
    npc, steps_per_seq = NPC // HGRN_SUB, CH_PER_SEQ // HGRN_SUB

    @pl.when(i == 0)
    def _():
        o_scr[...] = jnp.zeros_like(o_scr)

    @pl.when((i < npc) & (i % steps_per_seq == 0))
    def _():
        st_scr[...] = jnp.zeros_like(st_scr)

    def run(blocks):
        fast = blocks[0][0]
        for in_range, _, _ in blocks[1:]:
            fast = jnp.logical_and(fast, in_range)

        @pl.when(jnp.logical_not(fast))
        def _():
            for _, exact_path, _ in blocks:
                exact_path()

        for _, _, finish in blocks:
            finish(fast)

    @pl.when(i < npc)
    def _():
        run([_hgrn_block(True, p_ref, lb_ref, hg_ref, e64_ref, oh_ref, st_scr, o_scr, sub,
                         is_last=(i % steps_per_seq == steps_per_seq - 1) if sub == HGRN_SUB - 1 else None,
                         stp_out=stp_out) for sub in range(HGRN_SUB)])

    @pl.when(i >= npc)
    def _():
        run([_hgrn_block(False, p_ref, lb_ref, hg_ref, e64_ref, oh_ref, st_scr, o_scr, sub,
                         sts_in=sts_in, sts_out=sts_out) for sub in range(HGRN_SUB)])


def _hgrn(ph, lb, hg, consts, st_t):
    return pl.pallas_call(
        _hgrn_body,
        grid=((NPC + NSC) // HGRN_SUB,),
        in_specs=[
            _mixer_blk(P_HG, HGRN_SUB), _const_spec((1, HW)), _const_spec((1, HW)),
            _const_spec((HW, HW)),
            _sstate_spec((NH, HD, HD), HGRN_SUB),
        ],
        out_specs=[_mixer_blk(HW, HGRN_SUB), _pstate_spec((NH // 2, 2 * HD, 2 * HD), HGRN_SUB),
                   _sstate_spec((NH, HD, HD), HGRN_SUB)],
        out_shape=[
            jax.ShapeDtypeStruct((ROWS, HW), F32),
            jax.ShapeDtypeStruct((NB_P, NH // 2, 2 * HD, 2 * HD), F32),
            jax.ShapeDtypeStruct((NB_S, NH, HD, HD), F32),
        ],
        scratch_shapes=[pltpu.VMEM((NH // 2, 2 * HD, 2 * HD), F32), pltpu.VMEM((HGRN_SUB * CB, HW), F32)],
        compiler_params=_cparams(("arbitrary",)),
        name="hgrn",
    )(ph, lb, hg, consts["e64"], st_t)


def _cmul_add(hr, hi, lr, li, sr, si):
    return hr + lr * sr - li * si, hi + lr * si + li * sr


def _s5_project(ub, bblk_ref):
    halves = [ub[:, (S5W // 2) * j:(S5W // 2) * (j + 1)] for j in range(2)]
    hr = jnp.concatenate([jnp.dot(halves[j], bblk_ref[0, j], preferred_element_type=F32) for j in range(2)], axis=1)
    hi = jnp.concatenate([jnp.dot(halves[j], bblk_ref[1, j], preferred_element_type=F32) for j in range(2)], axis=1)
    return hr, hi


def _s5_readout(h_scr, u, ccat_ref, d_ref, wglu_ref, bglu_ref):
    hs = S5N // 2
    ch = [_bdot(h_scr[:, hs * j:hs * (j + 1)], ccat_ref[j, 0])
          + _bdot(h_scr[:, S5N + hs * j:S5N + hs * (j + 1)], ccat_ref[j, 1]) for j in range(2)]
    y = jnp.concatenate(ch, axis=1) + d_ref[...] * u
    c0 = math.sqrt(2.0 / math.pi)
    y = y * (0.5 * (1.0 + jnp.tanh(c0 * (y + 0.044715 * (y * y * y)))))
    return y * jax.nn.sigmoid(_bdot(y, wglu_ref[...]) + bglu_ref[...])


S5_TB = 64
S5_ROWS = NB_P * S5_TB


def _s5_prompt_body(*refs):
    p_refs = refs[:NB_P]
    (perm_ref, permt_ref, lam_ref, bblk_ref, ccat_ref, d_ref, wglu_ref, bglu_ref,
     os_ref, st_out, carry_scr, h_scr) = refs[NB_P:]
    i = pl.program_id(0)

    @pl.when(i == 0)
    def _():
        carry_scr[...] = jnp.zeros_like(carry_scr)

    u = jnp.concatenate([r[...] for r in p_refs], axis=0)
    u_hi = u.astype(BF16)
    u_lo = (u - u_hi.astype(F32)).astype(BF16)
    perm = perm_ref[...]
    up_hi = jnp.dot(perm, u_hi, preferred_element_type=F32)
    up = up_hi + jnp.dot(perm, u_lo, preferred_element_type=F32)
    hr, hi = _s5_project(up_hi.astype(BF16), bblk_ref)
    lr = lam_ref[:, 0:S5N]
    li = lam_ref[:, S5N:2 * S5N]
    cr = carry_scr[:, 0:S5N]
    ci = carry_scr[:, S5N:2 * S5N]
    for t in range(S5_TB):
        rows = slice(NB_P * t, NB_P * (t + 1))
        cr, ci = _cmul_add(hr[rows], hi[rows], lr, li, cr, ci)
        h_scr[rows, 0:S5N] = cr
        h_scr[rows, S5N:2 * S5N] = ci
    carry_scr[:, 0:S5N] = cr
    carry_scr[:, S5N:2 * S5N] = ci
    out = _s5_readout(h_scr, up, ccat_ref, d_ref, wglu_ref, bglu_ref)
    os_ref[...] = jnp.dot(permt_ref[...], out.astype(BF16), preferred_element_type=F32).reshape(NB_P, S5_TB, S5W)

    @pl.when(i == pl.num_programs(0) - 1)
    def _():
        st_out[...] = carry_scr[...]


def _s5_prompt(ps, prm, consts):
    steps = L_P // S5_TB
    windows = [pl.BlockSpec((S5_TB, S5W), lambda i, b=b: (b * steps + i, 0)) for b in range(NB_P)]
    return pl.pallas_call(
        _s5_prompt_body,
        grid=(steps,),
        in_specs=[
            *windows,
            _const_spec((S5_ROWS, S5_ROWS)), _const_spec((S5_ROWS, S5_ROWS)), _const_spec((NB_P, 2 * S5N)),
            _const_spec((2, 2, S5W // 2, S5N // 2)), _const_spec((2, 2, S5N // 2, S5W // 2)),
            _const_spec((1, S5W)), _const_spec((S5W, S5W)), _const_spec((1, S5W)),
        ],
        out_specs=[pl.BlockSpec((NB_P, S5_TB, S5W), lambda i: (0, i, 0)), _const_spec((NB_P, 2 * S5N))],
        out_shape=[jax.ShapeDtypeStruct((NB_P, L_P, S5W), F32), jax.ShapeDtypeStruct((NB_P, 2 * S5N), F32)],
        scratch_shapes=[pltpu.VMEM((NB_P, 2 * S5N), F32), pltpu.VMEM((S5_ROWS, 2 * S5N), F32)],
        compiler_params=_cparams(("arbitrary",)),
        name="s5_prompt",
    )(*([ps] * NB_P), consts["perm"], consts["permt"], prm["lam8"], prm["bblk"], prm["ccat"], prm["d"],
      prm["wglu"], prm["bglu"])


def _s5_sample_body(p_ref, tab_ref, bblk_ref, ccat_ref, d_ref, wglu_ref, bglu_ref, s5s_in,
                    os_ref, s5s_out, h_scr):
    u = p_ref[...]
    hr, hi = _s5_project(u.astype(BF16), bblk_ref)
    nsub = CB // 8
    for idx, dsh in enumerate((1, 2, 4)):
        sr = pltpu.roll(hr, dsh, 0).reshape(nsub, 8, S5N)
        si = pltpu.roll(hi, dsh, 0).reshape(nsub, 8, S5N)
        lr = tab_ref[idx, :, 0:S5N][None]
        li = tab_ref[idx, :, S5N:2 * S5N][None]
        nr, ni = _cmul_add(hr.reshape(nsub, 8, S5N), hi.reshape(nsub, 8, S5N), lr, li, sr, si)
        hr = nr.reshape(CB, S5N)
        hi = ni.reshape(CB, S5N)
    tcr = tab_ref[3, :, 0:S5N]
    tci = tab_ref[3, :, S5N:2 * S5N]
    cr = s5s_in[:, :, 0:S5N]
    ci = s5s_in[:, :, S5N:2 * S5N]
    tr, tim = _cmul_add(hr.reshape(nsub, 8, S5N), hi.reshape(nsub, 8, S5N), tcr[None], tci[None], cr, ci)
    h_scr[:, 0:S5N] = tr.reshape(CB, S5N)
    h_scr[:, S5N:2 * S5N] = tim.reshape(CB, S5N)
    sb = lax.broadcasted_iota(jnp.int32, (SEQ_PER_CB, CB), 0)
    st = lax.broadcasted_iota(jnp.int32, (SEQ_PER_CB, CB), 1)
    sel = (st == L_S * sb + (L_S - 1)).astype(BF16)
    s5s_out[...] = _split_dot_l(sel, h_scr[...], 3)
    os_ref[...] = _s5_readout(h_scr, u, ccat_ref, d_ref, wglu_ref, bglu_ref)


def _s5_sample(ps, prm, st):
    seqs = lambda shape: pl.BlockSpec((SEQ_PER_CB,) + shape, lambda i: (i,) + (0,) * len(shape))
    return pl.pallas_call(
        _s5_sample_body,
        grid=(NSC,),
        in_specs=[
            pl.BlockSpec((CB, S5W), lambda i: (NPC + i, 0)),
            _const_spec((4, 8, 2 * S5N)), _const_spec((2, 2, S5W // 2, S5N // 2)),
            _const_spec((2, 2, S5N // 2, S5W // 2)), _const_spec((1, S5W)), _const_spec((S5W, S5W)),
            _const_spec((1, S5W)), seqs((1, 2 * S5N)),
        ],
        out_specs=[_mixer_blk(S5W), seqs((2 * S5N,))],
        out_shape=[jax.ShapeDtypeStruct((ROWS_S, S5W), F32), jax.ShapeDtypeStruct((NB_S, 2 * S5N), F32)],
        scratch_shapes=[pltpu.VMEM((CB, 2 * S5N), F32)],
        compiler_params=_cparams(("parallel",)),
        name="s5_sample",
    )(ps, prm["tab"], prm["bblk"], prm["ccat"], prm["d"], prm["wglu"], prm["bglu"], st)


def _ssd_block(prompt, p_ref, cw_ref, cb_ref, dtb_ref, aneg_ref, dx_ref, sg_ref,
               g192_ref, tril_ref, om_ref, cbuf, st_scr, sub, is_last=None, ssp_out=None, convp_out=None,
               hist_ref=None, sss_in=None, sss_out=None, convs_out=None):
    rows = slice(sub * CB, (sub + 1) * CB)
    seqs = slice(sub * SEQ_PER_CB, (sub + 1) * SEQ_PER_CB)
    z = p_ref[rows, 0:SSW]
    xbc = p_ref[rows, SSW:SSW + CONVC]
    dtr = p_ref[rows, SSW + CONVC:P_SS]
    acc = cb_ref[...] + cw_ref[3:4, :] * xbc
    if prompt:
        cbuf[8:8 + CB, :] = xbc
        for k in (1, 2, 3):
            acc = acc + cw_ref[3 - k:4 - k, :] * cbuf[8 - k:8 - k + CB, :]
        cbuf[0:8, :] = cbuf[CB:CB + 8, :]
    else:
        convs_out[rows, :] = xbc
        tl = lax.broadcasted_iota(jnp.int32, (CB, CONVC), 0) % L_S
        hist = hist_ref[rows, :]
        for k in (1, 2, 3):
            hk = hist if k == 3 else pltpu.roll(hist, CB - (3 - k), 0)
            sh = jnp.where(tl >= k, pltpu.roll(xbc, k, 0), hk)
            acc = acc + cw_ref[3 - k:4 - k, :] * sh
    xc = _silu(acc)
    xs = xc[:, 0:SSW]
    bm = xc[:, SSW:SSW + 2 * HD]
    cm = xc[:, SSW + 2 * HD:CONVC]
    xdt = dtr + dtb_ref[...]
    dt = jnp.maximum(xdt, 0.0) + jnp.log1p(jnp.exp(-jnp.abs(xdt)))
    la = dt * aneg_ref[...]
    tril = tril_ref[...]
    b6 = _cumsum_rows(la, CB if prompt else L_S)
    lane = lax.broadcasted_iota(jnp.int32, (CB, 128), 1)
    bxw = [jnp.broadcast_to(b6[:, h:h + 1], (CB, 128)) for h in range(NH)]
    dtw = [jnp.broadcast_to(dt[:, h:h + 1], (CB, 128)) for h in range(NH)]
    pair = lambda cols: _concat_heads([jnp.where(lane < HD, cols[2 * j], cols[2 * j + 1]) for j in range(NH // 2)])
    bx = pair(bxw)
    dtx = pair(dtw)
    bm_rep = _concat_heads([bm[:, 0:HD]] * 3 + [bm[:, HD:2 * HD]] * 3)
    cm_rep = _concat_heads([cm[:, 0:HD]] * 3 + [cm[:, HD:2 * HD]] * 3)
    kh = bm_rep * dtx
    qt = cm_rep * jnp.exp(bx)
    blast = bx[CB - 1:CB, :] if prompt else _seq_last_rows(bx)
    kd = kh * jnp.exp(blast - bx)
    kdt = kd.T
    mask = tril > 0

    def scores(h):
        sl = slice(HD * h, HD * (h + 1))
        bcol = bxw[h]
        decay = jnp.exp(jnp.where(mask, bcol - bcol.T, -1e30))
        return _bdot_nt(cm_rep[:, sl], kh[:, sl]) * decay

    parts = []
    if prompt:
        ri = lax.broadcasted_iota(jnp.int32, (2 * HD, 2 * HD), 0)
        ci = lax.broadcasted_iota(jnp.int32, (2 * HD, 2 * HD), 1)
        same_head = (ri // HD) == (ci // HD)
        for j in range(NH // 2):
            pr = slice(2 * HD * j, 2 * HD * (j + 1))
            xp = xs[:, pr]
            xdiag = jnp.concatenate([jnp.where(lane < HD, xp, 0.0), jnp.where(lane >= HD, xp, 0.0)], axis=0)
            st = st_scr[j]
            parts.append(_bdot(jnp.concatenate([scores(2 * j), scores(2 * j + 1)], axis=1), xdiag)
                         + _bdot(qt[:, pr], st))
            st_scr[j] = st * jnp.exp(blast[:, pr]) + jnp.where(same_head, _bdot(kdt[pr, :], xp), 0.0)
    else:
        for h in range(NH):
            sl = slice(HD * h, HD * (h + 1))
            sts = sss_in[seqs, h].reshape(SEQ_PER_CB * HD, HD)
            parts.append(_bdot(scores(h), xs[:, sl]) + _bdot(_seq_expand_lanes(qt[:, sl]), sts))
            dec3 = jnp.exp(blast[:, sl]).reshape(SEQ_PER_CB, L_S, HD)[:, L_S - 1:L_S, :]
            dec_rows = jnp.broadcast_to(dec3, (SEQ_PER_CB, HD, HD)).reshape(SEQ_PER_CB * HD, HD)
            upd = _bdot(_seq_expand_rows(kdt[sl, :]), xs[:, sl])
            sss_out[seqs, h] = (sts * dec_rows + upd).reshape(SEQ_PER_CB, HD, HD)
    if is_last is not None:
        @pl.when(is_last)
        def _():
            ssp_out[0] = st_scr[...]
            convp_out[0] = cbuf[0:8, :]
    y = (_concat_heads(parts) + dx_ref[...] * xs) * _silu(z)
    ms = _split_dot(y * y, g192_ref[...], 1) * (1.0 / (SSW // 2))
    om_ref[rows, :] = y * lax.rsqrt(ms + RMS_EPS) * sg_ref[...]


def _ssd_body(p_ref, cw_ref, cb_ref, dtb_ref, aneg_ref, dx_ref, sg_ref, g192_ref,
              trilp_ref, trils_ref, hist_ref, sss_in, *refs):
    om_ref, ssp_out, sss_out, convp_out, convs_out, cbuf, st_scr = refs[1:]
    i = pl.program_id(0)
    common = (p_ref, cw_ref, cb_ref, dtb_ref, aneg_ref, dx_ref, sg_ref, g192_ref)
    npc, steps_per_seq = NPC // SSD_SUB, CH_PER_SEQ // SSD_SUB

    @pl.when((i < npc) & (i % steps_per_seq == 0))
    def _():
        st_scr[...] = jnp.zeros_like(st_scr)
        cbuf[...] = jnp.zeros_like(cbuf)

    @pl.when(i < npc)
    def _():
        for sub in range(SSD_SUB):
            is_last = (i % steps_per_seq == steps_per_seq - 1) if sub == SSD_SUB - 1 else None
            _ssd_block(True, *common, trilp_ref, om_ref, cbuf, st_scr, sub, is_last=is_last,
                       ssp_out=ssp_out, convp_out=convp_out)

    @pl.when(i >= npc)
    def _():
        for sub in range(SSD_SUB):
            _ssd_block(False, *common, trils_ref, om_ref, cbuf, st_scr, sub, hist_ref=hist_ref,
                       sss_in=sss_in, sss_out=sss_out, convs_out=convs_out)


SSD_SUB = 2


def _ssd(pm, prm, consts, hist, st, layer, earlier):
    sample_rows = pl.BlockSpec((SSD_SUB * CB, CONVC),
                               lambda i: (jnp.clip(i - NPC // SSD_SUB, 0, NSC // SSD_SUB - 1), 0))
    in_specs = [
        _mixer_blk(P_SS, SSD_SUB), _const_spec((4, CONVC)), _const_spec((1, CONVC)),
        _const_spec((1, 128)), _const_spec((1, 128)), _const_spec((1, SSW)), _const_spec((1, SSW)),
        _const_spec((SSW, SSW)),
        _const_spec((CB, CB)), _const_spec((CB, CB)),
        sample_rows,
        _sstate_spec((NH, HD, HD), SSD_SUB),
    ]
    args = [pm, prm["cw"], prm["cb"], prm["dtb"], prm["aneg"], prm["dx"], prm["sg"],
            consts["g192"], consts["trilp"], consts["trils"], hist, st]
    in_specs.append(pl.BlockSpec(memory_space=pl.ANY))
    aliases = {len(args): 2}
    args.append(earlier)
    layer_slab = pl.BlockSpec((None, SSD_SUB * SEQ_PER_CB, NH, HD, HD),
                              lambda i: (layer, jnp.clip(i - NPC // SSD_SUB, 0, NSC // SSD_SUB - 1), 0, 0, 0))
    return pl.pallas_call(
        _ssd_body,
        grid=((NPC + NSC) // SSD_SUB,),
        in_specs=in_specs,
        out_specs=[_mixer_blk(SSW, SSD_SUB), _pstate_spec((NH // 2, 2 * HD, 2 * HD), SSD_SUB),
                   layer_slab, _pstate_spec((8, CONVC), SSD_SUB), sample_rows],
        out_shape=[
            jax.ShapeDtypeStruct((ROWS, SSW), F32),
            jax.ShapeDtypeStruct((NB_P, NH // 2, 2 * HD, 2 * HD), F32),
            jax.ShapeDtypeStruct((DEPTH, NB_S, NH, HD, HD), F32),
            jax.ShapeDtypeStruct((NB_P, 8, CONVC), F32),
            jax.ShapeDtypeStruct((ROWS_S, CONVC), F32),
        ],
        scratch_shapes=[pltpu.VMEM((CB + 8, CONVC), F32), pltpu.VMEM((NH // 2, 2 * HD, 2 * HD), F32)],
        input_output_aliases=aliases,
        compiler_params=_cparams(("arbitrary",)),
        name="ssd",
    )(*args)


def _o_core(split_x, i, oh_ref, osp_ref, oss_ref, om_ref, *refs):
    if split_x:
        x = jnp.where(i < NPT, refs[0][...], refs[1][...])
        refs = refs[2:]
    else:
        x = refs[0][...]
        refs = refs[1:]
    wo_ref, gp, gs, lng_ref, lnb_ref, scp, scs, shp, shs = refs[:9]
    os_ = jnp.where(i < NPT, osp_ref[...], oss_ref[...])
    mix = (_bdot(oh_ref[...], wo_ref[0:HW, :]) + _bdot(os_, wo_ref[HW:HW + S5W, :])
           + _bdot(om_ref[...], wo_ref[HW + S5W:D, :]))
    x1 = _layer_norm(ALPHA * x + _rowmod(i, gp, gs) * mix, lng_ref[...], lnb_ref[...])
    h2 = x1 * (1.0 + _rowmod(i, scp, scs)) + _rowmod(i, shp, shs)
    return x1, h2, refs[9:]


def _o_body(split_x, *refs):
    i = pl.program_id(0)
    x1, h2, (x1_ref, h2_ref) = _o_core(split_x, i, *refs)
    x1_ref[...] = x1
    h2_ref[...] = h2.astype(BF16)


def _o_router_body(split_x, *refs):
    i = pl.program_id(0)
    x1, h2, (wr_ref, br_ref, x1_ref, h2_ref, route_ref) = _o_core(split_x, i, *refs)
    x1_ref[...] = x1
    h2_ref[...] = h2
    h_hi = h2.astype(BF16)
    h_lo = (h2 - h_hi.astype(F32)).astype(BF16)
    logits = (jnp.dot(h_hi, wr_ref[0], preferred_element_type=F32)
              + jnp.dot(h_lo, wr_ref[0], preferred_element_type=F32)
              + jnp.dot(h_hi, wr_ref[1], preferred_element_type=F32)) + br_ref[...]
    lane = lax.broadcasted_iota(jnp.int32, (TM, 128), 1).astype(F32)
    neg = -jnp.inf
    lg = jnp.where(lane < NEXP, logits, neg)
    m1 = jnp.max(lg, axis=-1, keepdims=True)
    i1 = jnp.min(jnp.where(lg == m1, lane, 128.0), axis=-1, keepdims=True)
    lg2 = jnp.where(lane == i1, neg, lg)
    m2 = jnp.max(lg2, axis=-1, keepdims=True)
    i2 = jnp.min(jnp.where(lg2 == m2, lane, 128.0), axis=-1, keepdims=True)
    e2 = jnp.exp(m2 - m1)
    den = 1.0 + e2
    route_ref[...] = jnp.where(lane == 0.0, i1, jnp.where(lane == 1.0, i2,
                               jnp.where(lane == 2.0, 1.0 / den, jnp.where(lane == 3.0, e2 / den, 0.0))))


def _stage_o(layer, oh, os_p, os_s, om, x, wo_b, ada4, ln_g, ln_b, router=None):
    split_x = isinstance(x, tuple)
    x_specs = [_prompt_rows_spec(D), _sample_rows_spec(D)] if split_x else [_row_spec(D)]
    in_specs = [
        _row_spec(HW), _prompt_rows_spec(S5W), _sample_rows_spec(S5W), _row_spec(SSW), *x_specs,
        _const_spec((D, D)),
        *_mod_specs(layer, 2), _const_spec((1, D)), _const_spec((1, D)),
        *_mod_specs(layer, 4), *_mod_specs(layer, 3),
    ]
    args = [oh, os_p, os_s, om, *(x if split_x else (x,)), wo_b, ada4, ada4, ln_g, ln_b, ada4, ada4, ada4, ada4]
    out_specs = [_row_spec(D), _row_spec(D)]
    out_shape = [jax.ShapeDtypeStruct((ROWS, D), F32), jax.ShapeDtypeStruct((ROWS, D), BF16)]
    body = _o_body
    if router is not None:
        in_specs += [_const_spec((2, D, 128)), _const_spec((1, 128))]
        args += list(router)
        out_specs.append(_row_spec(128))
        out_shape[1] = jax.ShapeDtypeStruct((ROWS, D), F32)
        out_shape.append(jax.ShapeDtypeStruct((ROWS, 128), F32))
        body = _o_router_body
    return pl.pallas_call(
        functools.partial(body, split_x), grid=(NT,), in_specs=in_specs, out_specs=out_specs,
        out_shape=out_shape,
        compiler_params=_cparams(("parallel",)),
        name="stage_o_router" if router is not None else "stage_o",
    )(*args)


def _ffn_body(te_ref, nu_ref, h_ref, wg_ref, wu_ref, wd_ref, *refs):
    o_ref, acc_ref = refs[-2:]
    i = pl.program_id(0)
    j = pl.program_id(1)

    @pl.when(j == 0)
    def _():
        acc_ref[...] = jnp.zeros_like(acc_ref)

    @pl.when(i < nu_ref[0])
    def _():
        h = h_ref[...].astype(BF16)
        g = jnp.dot(h, wg_ref[...], preferred_element_type=F32)
        u = jnp.dot(h, wu_ref[...], preferred_element_type=F32)
        act = (_silu(g) * u).astype(BF16)
        acc_ref[...] += jnp.dot(act, wd_ref[...], preferred_element_type=F32)

    @pl.when(j == pl.num_programs(1) - 1)
    def _():
        o_ref[...] = acc_ref[...]


def _ffn(tile_expert, n_used, h, wg, wu, wd, out_tiles=None, into=None, tile0=0):
    h_tiles = h.shape[0] // TMF
    nj = DFF // TF

    def jblk(i, j, nu):
        return jnp.where(i < nu[0], j, nj - 1)

    in_specs = [
        pl.BlockSpec((TMF, D), lambda i, j, te, nu: (jnp.minimum(i, h_tiles - 1), 0)),
        pl.BlockSpec((None, D, TF), lambda i, j, te, nu: (te[i], 0, jblk(i, j, nu))),
        pl.BlockSpec((None, D, TF), lambda i, j, te, nu: (te[i], 0, jblk(i, j, nu))),
        pl.BlockSpec((None, TF, D), lambda i, j, te, nu: (te[i], jblk(i, j, nu), 0)),
    ]
    args = [tile_expert, n_used, h, wg, wu, wd]
    aliases = {}
    grid_tiles = h_tiles if out_tiles is None else out_tiles
    out_rows = grid_tiles * TMF
    if into is not None:
        in_specs.append(pl.BlockSpec(memory_space=pl.ANY))
        aliases = {len(args): 0}
        args.append(into)
        out_rows = into.shape[0]
    grid_spec = pltpu.PrefetchScalarGridSpec(
        num_scalar_prefetch=2,
        grid=(grid_tiles, nj),
        in_specs=in_specs,
        out_specs=pl.BlockSpec((TMF, D), lambda i, j, te, nu: (i + tile0, 0)),
        scratch_shapes=[pltpu.VMEM((TMF, D), F32)],
    )
    return pl.pallas_call(
        _ffn_body, grid_spec=grid_spec,
        out_shape=jax.ShapeDtypeStruct((out_rows, D), F32),
        input_output_aliases=aliases,
        compiler_params=_cparams(("parallel", "arbitrary")),
        name="ffn",
    )(*args)


def _final_body(x1_ref, ya_ref, yb_ref, route_ref, gp, gs, lng_ref, lnb_ref, yp_ref, ys_ref):
    i = pl.program_id(0)
    f = route_ref[:, 2:3] * ya_ref[...] + route_ref[:, 3:4] * yb_ref[...]
    y = _layer_norm(ALPHA * x1_ref[...] + _rowmod(i, gp, gs) * f, lng_ref[...], lnb_ref[...])

    @pl.when(i < NPT)
    def _():
        yp_ref[...] = y

    @pl.when(i >= NPT)
    def _():
        ys_ref[...] = y


def _stage_final(layer, x1, ya, yb, route, ada4, ln_g, ln_b):
    return pl.pallas_call(
        _final_body,
        grid=(NT,),
        in_specs=[
            _row_spec(D), _row_spec(D), _row_spec(D), _row_spec(128),
            *_mod_specs(layer, 5), _const_spec((1, D)), _const_spec((1, D)),
        ],
        out_specs=[
            pl.BlockSpec((TM, D), lambda i: (jnp.minimum(i, NPT - 1), 0)),
            pl.BlockSpec((TM, D), lambda i: (jnp.clip(i - NPT, 0, NST - 1), 0)),
        ],
        out_shape=[jax.ShapeDtypeStruct((ROWS_P, D), F32), jax.ShapeDtypeStruct((ROWS_S, D), F32)],
        compiler_params=_cparams(("arbitrary",)),
        name="stage_final",
    )(x1, ya, yb, route, ada4, ada4, ln_g, ln_b)


def _block_ones(n, blk):
    r = np.arange(n) // blk
    return r[:, None] == r[None, :]


def _consts():
    t = np.arange(CB)
    causal = t[:, None] >= t[None, :]
    same_seq = (t[:, None] // L_S) == (t[None, :] // L_S)
    r = np.arange(S5_ROWS)
    perm = r[None, :] == ((r % NB_P) * S5_TB + r // NB_P)[:, None]
    mats = {
        "perm": perm,
        "permt": perm.T,
        "e64": _block_ones(HW, HD),
        "g192": _block_ones(SSW, SSW // 2),
        "trilp": causal,
        "trils": causal & same_seq,
    }
    return {k: jnp.asarray(v.astype(np.float32), dtype=BF16) for k, v in mats.items()}


def _s5_params(a_re, a_im, log_dt, b_re, b_im, c_re, c_im, d, w_glu, b_glu):
    dt = jnp.exp(log_dt)[:, None]
    mag = jnp.exp(a_re * dt)
    lam_re, lam_im = mag * jnp.cos(a_im * dt), mag * jnp.sin(a_im * dt)
    den = a_re * a_re + a_im * a_im
    nr, ni = lam_re - 1.0, lam_im
    zr = (nr * a_re + ni * a_im) / den
    zi = (ni * a_re - nr * a_im) / den
    bbar_re = zr[..., None] * b_re - zi[..., None] * b_im
    bbar_im = zr[..., None] * b_im + zi[..., None] * b_re
    eye = jnp.eye(16, dtype=F32)
    blk = lambda bb: jnp.einsum('gph,gk->ghkp', bb, eye).reshape(S5W, S5N)
    hu, hs = S5W // 2, S5N // 2
    bblk = jnp.stack([jnp.stack([blk(bb)[hu * j:hu * (j + 1), hs * j:hs * (j + 1)] for j in range(2)])
                      for bb in (bbar_re, bbar_im)]).astype(BF16)
    cblk = lambda cc: jnp.einsum('ghp,gk->gpkh', cc, eye).reshape(S5N, S5W)
    ccat = jnp.stack([jnp.stack([cblk(cc)[hs * j:hs * (j + 1), hu * j:hu * (j + 1)] for cc in (c_re, -c_im)])
                      for j in range(2)]).astype(BF16)
    lr, li = lam_re.reshape(-1), lam_im.reshape(-1)
    pows = [(jnp.ones_like(lr), jnp.zeros_like(li))]
    for _ in range(8):
        pr, pi = pows[-1]
        pows.append((pr * lr - pi * li, pr * li + pi * lr))
    rows = jnp.arange(8)[:, None]
    tabs = []
    for dsh in (1, 2, 4):
        pr, pi = pows[dsh]
        tabs.append(jnp.where(rows >= dsh, jnp.concatenate([pr, pi])[None, :], 0.0))
    tabs.append(jnp.stack([jnp.concatenate(pows[r + 1]) for r in range(8)]))
    return {
        "tab": jnp.stack(tabs), "bblk": bblk, "ccat": ccat,
        "lam8": jnp.broadcast_to(jnp.concatenate([lr, li])[None, :], (NB_P, 2 * S5N)),
        "d": d.reshape(1, S5W), "wglu": w_glu.astype(BF16), "bglu": b_glu.reshape(1, S5W),
    }


def _pair_diag(st):
    s6 = st.reshape(NB_P, NH // 2, 2, HD, 2, HD)
    return jnp.stack([s6[:, :, 0, :, 0, :], s6[:, :, 1, :, 1, :]], axis=2).reshape(NB_P, NH, HD, HD)


def _pad_lanes(v, n=128):
    return jnp.pad(v, (0, n - v.shape[0])).reshape(1, n)


def kernel(x_prompt, x_sample, c_prompt, c_sample, state_hgrn, state_s5, state_ssd, state_ssd_conv, w_ada, b_ada, ln_g, ln_b, w_in, w_out, hgrn_lb_logits, hgrn_norm_g, s5_a_re, s5_a_im, s5_log_dt, s5_b_re, s5_b_im, s5_c_re, s5_c_im, s5_d, s5_w_glu, s5_b_glu, ssd_conv_w, ssd_conv_b, ssd_dt_bias, ssd_a_log, ssd_d, ssd_norm_g, ffn_w_gate, ffn_w_up, ffn_w_down, moe_w_router, moe_b_router, moe_w_gate, moe_w_up, moe_w_down):
    consts = _consts()
    c_all = jnp.concatenate([c_sample, c_prompt], axis=0)
    ada4 = _ada(c_all, w_ada, b_ada).reshape(DEPTH, NB_S + NB_P, 1, 6 * D)

    lb_all = jnp.cumsum(jax.nn.softmax(hgrn_lb_logits, axis=0), axis=0)
    lb_all = lb_all - lb_all[0]

    xp = x_prompt.reshape(ROWS_P, D)
    xs = x_sample.reshape(ROWS_S, D)
    new_h, new_s5, new_m, new_c = [], [], [], []
    x1 = f = route = None
    ss_s = jnp.zeros((DEPTH, NB_S, NH, HD, HD), F32)
    for l in range(DEPTH):
        w_in_b = jnp.pad(w_in[l], ((0, 0), (0, N_IN_PAD - N_IN))).astype(BF16)
        if l == 0:
            x = (xp, xs)
            ph, ps, pm = _stage_a0(xp, xs, ada4, w_in_b)
        else:
            x, ph, ps, pm = _stage_a1(l, x1, f, ada4, ln_g[l - 1, 1].reshape(1, D),
                                      ln_b[l - 1, 1].reshape(1, D), w_in_b)
        oh, hg_p, hg_s = _hgrn(ph, lb_all[l].reshape(1, HW), hgrn_norm_g[l].reshape(1, HW), consts,
                               jnp.swapaxes(state_hgrn[l], -1, -2))
        s5p = _s5_params(s5_a_re[l], s5_a_im[l], s5_log_dt[l], s5_b_re[l], s5_b_im[l], s5_c_re[l], s5_c_im[l],
                         s5_d[l], s5_w_glu[l], s5_b_glu[l])
        os_p, s5_p = _s5_prompt(ps, s5p, consts)
        os_s, s5_s = _s5_sample(ps, s5p, state_s5[l].reshape(NB_S, 1, 2 * S5N))
        os_p = os_p.reshape(ROWS_P, S5W)
        ssd_prm = {
            "cw": ssd_conv_w[l], "cb": ssd_conv_b[l].reshape(1, CONVC),
            "dtb": _pad_lanes(ssd_dt_bias[l]), "aneg": _pad_lanes(-jnp.exp(ssd_a_log[l])),
            "dx": jnp.repeat(ssd_d[l], HD).reshape(1, SSW), "sg": ssd_norm_g[l].reshape(1, SSW),
        }
        hist = jnp.pad(state_ssd_conv[l], ((0, 0), (0, L_S - 3), (0, 0))).reshape(ROWS_S, CONVC)
        om, ss_p, ss_s, tail_p, xbc_s = _ssd(pm, ssd_prm, consts, hist, state_ssd[l], l, ss_s)
        conv_p = tail_p[:, 8 - 3:]
        conv_s = xbc_s.reshape(NB_S, L_S, CONVC)[:, L_S - 3:]
        hg_p, ss_p = _pair_diag(hg_p), _pair_diag(ss_p)
        new_h.append((jnp.swapaxes(hg_p, -1, -2), jnp.swapaxes(hg_s, -1, -2)))
        new_s5.append((s5_p.reshape(NB_P, 2, 16, 64), s5_s.reshape(NB_S, 2, 16, 64)))
        new_m.append((ss_p,))
        new_c.append((conv_p, conv_s))

        wo_b = w_out[l].astype(BF16)
        lg, lbb = ln_g[l, 0].reshape(1, D), ln_b[l, 0].reshape(1, D)
        j = l // 2
        if l % 2 == 0:
            x1, h2 = _stage_o(l, oh, os_p, os_s, om, x, wo_b, ada4, lg, lbb)
            f = _ffn(jnp.zeros((ROWS // TMF,), jnp.int32), jnp.full((1,), ROWS // TMF, jnp.int32), h2,
                     ffn_w_gate[j:j + 1].astype(BF16),
                     ffn_w_up[j:j + 1].astype(BF16), ffn_w_down[j:j + 1].astype(BF16))
        else:
            wr = jnp.pad(moe_w_router[j], ((0, 0), (0, 128 - NEXP)))
            wr_hi = wr.astype(BF16)
            wr = jnp.stack([wr_hi, (wr - wr_hi.astype(F32)).astype(BF16)])
            br = _pad_lanes(moe_b_router[j])
            x1, h2, route = _stage_o(l, oh, os_p, os_s, om, x, wo_b, ada4, lg, lbb, router=(wr, br))
            wg_b, wu_b, wd_b = (moe_w_gate[j].astype(BF16), moe_w_up[j].astype(BF16),
                                moe_w_down[j].astype(BF16))
            flat_e = route[:, 0:2].astype(jnp.int32).reshape(-1)
            onehot = (flat_e[:, None] == jnp.arange(NEXP)[None, :]).astype(jnp.int32)
            csum = jnp.cumsum(onehot, axis=0)
            counts = csum[-1]
            rank = jnp.take_along_axis(csum, flat_e[:, None], axis=1)[:, 0] - 1
            padded = ((counts + TMF - 1) // TMF) * TMF
            pend = jnp.cumsum(padded)
            pstart = pend - padded
            dest = pstart[flat_e] + rank
            n_pad = 2 * ROWS + NEXP * TMF
            src_tok = jnp.zeros((n_pad,), jnp.int32).at[dest].set(jnp.arange(2 * ROWS, dtype=jnp.int32) // 2)
            tile_start = jnp.arange(n_pad // TMF, dtype=jnp.int32) * TMF
            tile_e = jnp.minimum(jnp.sum((pend[None, :] <= tile_start[:, None]).astype(jnp.int32), axis=1),
                                 NEXP - 1)
            n_used = (pend[NEXP - 1:NEXP] // TMF).astype(jnp.int32)
            n_tiles = n_pad // TMF
            part_tiles = n_tiles // MOE_PARTS
            part_rows = part_tiles * TMF
            te0 = jnp.where(jnp.arange(n_tiles) < part_tiles, tile_e, tile_e[part_tiles - 1])
            y_sorted = _ffn(te0, jnp.minimum(n_used, part_tiles), h2[src_tok[:part_rows]], wg_b, wu_b, wd_b,
                            out_tiles=n_tiles)
            for c in range(1, MOE_PARTS):
                y_sorted = _ffn(tile_e[c * part_tiles:(c + 1) * part_tiles],
                                jnp.clip(n_used - c * part_tiles, 0, part_tiles),
                                h2[src_tok[c * part_rows:(c + 1) * part_rows]], wg_b, wu_b, wd_b,
                                into=y_sorted, tile0=c * part_tiles)
            pos = dest.reshape(ROWS, 2)
            ya = y_sorted[pos[:, 0]]
            yb = y_sorted[pos[:, 1]]
    y_p, y_s = _stage_final(DEPTH - 1, x1, ya, yb, route, ada4, ln_g[DEPTH - 1, 1].reshape(1, D),
                            ln_b[DEPTH - 1, 1].reshape(1, D))
    stack = lambda lst, k: jnp.stack([t[k] for t in lst])
    return (y_p.reshape(NB_P, L_P, D), y_s.reshape(NB_S, L_S, D),
            stack(new_h, 0), stack(new_s5, 0), stack(new_m, 0), stack(new_c, 0),
            stack(new_h, 1), stack(new_s5, 1), ss_s, stack(new_c, 1))
```

```python
import functools
import math

import jax
import jax.numpy as jnp
import numpy as np
from jax import lax
from jax.experimental import pallas as pl
from jax.experimental.pallas import tpu as pltpu

F32 = jnp.float32
BF16 = jnp.bfloat16

D = 1024
NB_P, L_P = 8, 2048
NB_S, L_S = 128, 8
ROWS_P = NB_P * L_P
ROWS_S = NB_S * L_S
ROWS = ROWS_P + ROWS_S
DEPTH = 2
HW = 384
S5W = 256
SSW = 384
NH = 6
HD = 64
S5N = 1024
CONVC = 640
N_IN = 2822
N_IN_PAD = 2944
P_HG = 1536
P_SS = 1152
DFF = 2816
NEXP = 8
ALPHA = (2 * DEPTH) ** 0.25
LN_EPS = 1e-5
RMS_EPS = 1e-6

TM = 512
NPT = ROWS_P // TM
NST = ROWS_S // TM
NT = NPT + NST
SEQ_PER_TILE = TM // L_S
TILES_PER_SEQ = L_P // TM

CB = 128
NPC = ROWS_P // CB
NSC = ROWS_S // CB
CH_PER_SEQ = L_P // CB
SEQ_PER_CB = CB // L_S

TMF = 512
TF = 1408
VMEM_LIMIT = 56 * 1024 * 1024


def _cparams(sem):
    return pltpu.CompilerParams(dimension_semantics=sem, vmem_limit_bytes=VMEM_LIMIT)


def _bdot(a, b):
    return jnp.dot(a.astype(BF16), b.astype(BF16), preferred_element_type=F32)


def _bdot_nt(a, b):
    return lax.dot_general(a.astype(BF16), b.astype(BF16), (((1,), (1,)), ((), ())),
                           preferred_element_type=F32)


def _split_dot(x, e, passes):
    acc = None
    r = x
    for _ in range(passes):
        hi = r.astype(BF16)
        d = jnp.dot(hi, e, preferred_element_type=F32)
        acc = d if acc is None else acc + d
        r = r - hi.astype(F32)
    return acc


def _split_dot_l(e, x, passes):
    acc = None
    r = x
    for _ in range(passes):
        hi = r.astype(BF16)
        d = jnp.dot(e, hi, preferred_element_type=F32)
        acc = d if acc is None else acc + d
        r = r - hi.astype(F32)
    return acc


def _silu(x):
    return x * jax.nn.sigmoid(x)


def _layer_norm(x, g, b):
    mu = jnp.mean(x, -1, keepdims=True)
    xc = x - mu
    var = jnp.mean(xc * xc, -1, keepdims=True)
    return xc * lax.rsqrt(var + LN_EPS) * g + b


def _rowmod(i, p_ref, s_ref):
    s = jnp.broadcast_to(s_ref[...], (SEQ_PER_TILE, L_S, D)).reshape(TM, D)
    return jnp.where(i < NPT, p_ref[0], s)


ADA_TN = 1536


def _ada_body(c_ref, w_ref, b_ref, o_ref):
    o_ref[...] = _bdot(_silu(c_ref[...]), w_ref[...]) + b_ref[...]


def _ada(c_all, w_ada, b_ada):
    nc = c_all.shape[0]
    return pl.pallas_call(
        _ada_body,
        grid=(DEPTH, 6 * D // ADA_TN),
        in_specs=[
            pl.BlockSpec((nc, D), lambda l, j: (0, 0)),
            pl.BlockSpec((None, D, ADA_TN), lambda l, j: (l, 0, j)),
            pl.BlockSpec((None, 1, ADA_TN), lambda l, j: (l, 0, j)),
        ],
        out_specs=pl.BlockSpec((None, nc, ADA_TN), lambda l, j: (l, 0, j)),
        out_shape=jax.ShapeDtypeStruct((DEPTH, nc, 6 * D), F32),
        compiler_params=_cparams(("parallel", "parallel")),
        name="ada",
    )(c_all, w_ada, b_ada.reshape(DEPTH, 1, 6 * D))


def _mod_specs(layer, k):
    ps = pl.BlockSpec((None, 1, 1, D),
                      lambda i: (layer, NB_S + jnp.minimum(i // TILES_PER_SEQ, NB_P - 1), 0, k))
    ss = pl.BlockSpec((None, SEQ_PER_TILE, 1, D),
                      lambda i: (layer, jnp.clip(i - NPT, 0, NST - 1), 0, k))
    return [ps, ss]


def _row_spec(width):
    return pl.BlockSpec((TM, width), lambda i: (i, 0))


def _const_spec(shape):
    nd = len(shape)
    return pl.BlockSpec(shape, lambda *_: (0,) * nd)


def _prompt_rows_spec(width):
    return pl.BlockSpec((TM, width), lambda i: (jnp.minimum(i, NPT - 1), 0))


def _sample_rows_spec(width):
    return pl.BlockSpec((TM, width), lambda i: (jnp.clip(i - NPT, 0, NST - 1), 0))


def _proj_out(x, i, scp, scs, shp, shs, w_ref, ph_ref, ps_ref, pm_ref):
    h = x * (1.0 + _rowmod(i, scp, scs)) + _rowmod(i, shp, shs)
    proj = jnp.dot(h.astype(BF16), w_ref[...], preferred_element_type=F32)
    ph_ref[...] = proj[:, 0:P_HG]
    ps_ref[...] = proj[:, P_HG:P_HG + S5W]
    pm_ref[...] = proj[:, P_HG + S5W:N_IN_PAD]


def _a0_body(xp_ref, xs_ref, scp, scs, shp, shs, w_ref, ph_ref, ps_ref, pm_ref):
    i = pl.program_id(0)
    x = jnp.where(i < NPT, xp_ref[...], xs_ref[...])
    _proj_out(x, i, scp, scs, shp, shs, w_ref, ph_ref, ps_ref, pm_ref)


def _a1_body(x1_ref, f_ref, gp, gs, lng_ref, lnb_ref, scp, scs, shp, shs, w_ref,
             x_ref, ph_ref, ps_ref, pm_ref):
    i = pl.program_id(0)
    x = _layer_norm(ALPHA * x1_ref[...] + _rowmod(i, gp, gs) * f_ref[...], lng_ref[...], lnb_ref[...])
    x_ref[...] = x
    _proj_out(x, i, scp, scs, shp, shs, w_ref, ph_ref, ps_ref, pm_ref)


def _a_out():
    specs = [_row_spec(P_HG), _row_spec(S5W), _row_spec(P_SS)]
    shapes = [jax.ShapeDtypeStruct((ROWS, w), F32) for w in (P_HG, S5W, P_SS)]
    return specs, shapes


def _stage_a0(xp, xs, ada4, w_in_b):
    out_specs, out_shape = _a_out()
    return pl.pallas_call(
        _a0_body,
        grid=(NT,),
        in_specs=[
            _prompt_rows_spec(D), _sample_rows_spec(D),
            *_mod_specs(0, 1), *_mod_specs(0, 0),
            _const_spec((D, N_IN_PAD)),
        ],
        out_specs=out_specs, out_shape=out_shape,
        compiler_params=_cparams(("parallel",)),
        name="stage_a0",
    )(xp, xs, ada4, ada4, ada4, ada4, w_in_b)


def _stage_a1(layer, x1, f, ada4, ln_g, ln_b, w_in_b):
    out_specs, out_shape = _a_out()
    return pl.pallas_call(
        _a1_body,
        grid=(NT,),
        in_specs=[
            _row_spec(D), _row_spec(D),
            *_mod_specs(layer - 1, 5),
            _const_spec((1, D)), _const_spec((1, D)),
            *_mod_specs(layer, 1), *_mod_specs(layer, 0),
            _const_spec((D, N_IN_PAD)),
        ],
        out_specs=[_row_spec(D)] + out_specs, out_shape=[jax.ShapeDtypeStruct((ROWS, D), F32)] + out_shape,
        compiler_params=_cparams(("parallel",)),
        name="stage_a1",
    )(x1, f, ada4, ada4, ln_g, ln_b, ada4, ada4, ada4, ada4, w_in_b)


def _mixer_blk(width, nsub=1):
    return pl.BlockSpec((nsub * CB, width), lambda i: (i, 0))


def _pstate_spec(shape, nsub=1):
    nd = len(shape)
    return pl.BlockSpec((1,) + shape,
                        lambda i: (jnp.minimum(i // (CH_PER_SEQ // nsub), NB_P - 1),) + (0,) * nd)


def _sstate_spec(shape, nsub=1):
    nd = len(shape)
    return pl.BlockSpec((nsub * SEQ_PER_CB,) + shape,
                        lambda i: (jnp.clip(i - NPC // nsub, 0, NSC // nsub - 1),) + (0,) * nd)


def _cumsum_rows(x, span):
    r = lax.broadcasted_iota(jnp.int32, x.shape, 0) & (span - 1)
    d = 1
    while d < span:
        x = x + jnp.where(r >= d, pltpu.roll(x, d, 0), 0.0)
        d *= 2
    return x


def _seq_last_rows(x):
    w = x.shape[-1]
    x3 = x.reshape(SEQ_PER_CB, L_S, w)
    return jnp.broadcast_to(x3[:, L_S - 1:L_S, :], (SEQ_PER_CB, L_S, w)).reshape(CB, w)


def _concat_heads(parts):
    return jnp.concatenate(parts, axis=1)


def _stack_select(shape, row_div, lane_div):
    r = lax.broadcasted_iota(jnp.int32, shape, 0) // row_div
    c = lax.broadcasted_iota(jnp.int32, shape, 1) // lane_div
    return r == c


def _seq_expand_lanes(qh):
    q2 = jnp.concatenate([qh, qh], axis=1)
    q16 = jnp.concatenate([q2] * (SEQ_PER_CB // 2), axis=1)
    return jnp.where(_stack_select((CB, SEQ_PER_CB * HD), L_S, HD), q16, 0.0)


def _seq_expand_rows(xt):
    t = jnp.broadcast_to(xt[None], (SEQ_PER_CB, HD, CB)).reshape(SEQ_PER_CB * HD, CB)
    return jnp.where(_stack_select((SEQ_PER_CB * HD, CB), HD, L_S), t, 0.0)


def _fold_seq_lanes(full):
    acc = full[:, 0:128]
    for j in range(1, SEQ_PER_CB * HD // 128):
        acc = acc + full[:, 128 * j:128 * (j + 1)]
    return acc[:, 0:HD] + acc[:, HD:2 * HD]


HGRN_BASE = 32
EXP_RANGE_MAX = 80.0


def _hgrn_block(prompt, p_ref, lb_ref, hg_ref, e64_ref, oh_ref, st_scr, o_scr, sub, is_last=None,
                stp_out=None, sts_in=None, sts_out=None):
    rows = slice(sub * CB, (sub + 1) * CB)
    seqs = slice(sub * SEQ_PER_CB, (sub + 1) * SEQ_PER_CB)
    lb = lb_ref[...]
    qr = p_ref[rows, 0:HW]
    fr = p_ref[rows, HW:2 * HW]
    v = p_ref[rows, 2 * HW:3 * HW]
    gr = p_ref[rows, 3 * HW:4 * HW]
    e = jnp.exp(-jnp.abs(fr))
    ope = 1.0 + e
    ls = jnp.minimum(fr, 0.0) - jnp.log(ope)
    a = jnp.log(lb)
    bb = jnp.log1p(-lb) + ls
    lf = jnp.maximum(a, bb) + jnp.log(1.0 + jnp.exp(-jnp.abs(a - bb)))
    rcp = 1.0 / ope
    kk = (1.0 - lb) * jnp.where(fr >= 0.0, e * rcp, rcp)
    q = _silu(qr)
    b = _cumsum_rows(lf, CB if prompt else L_S)

    e64 = e64_ref[...]
    ti = lax.broadcasted_iota(jnp.int32, (CB, CB), 0)
    si = lax.broadcasted_iota(jnp.int32, (CB, CB), 1)

    def diag8():
        nsub = CB // 8
        b3 = b.reshape(nsub, 8, HW)
        q3 = q.reshape(nsub, 8, HW)
        k3 = kk.reshape(nsub, 8, HW)
        v3 = v.reshape(nsub, 8, HW)
        r3 = lax.broadcasted_iota(jnp.int32, (nsub, 8, HW), 1)
        o = jnp.zeros((CB, HW), F32)
        for s in range(8):
            dlt = jnp.minimum(b3 - b3[:, s:s + 1, :], 0.0)
            w = jnp.where(r3 >= s, jnp.exp(dlt), 0.0) * q3 * k3[:, s:s + 1, :]
            hsum = jnp.dot(w.reshape(CB, HW).astype(BF16), e64, preferred_element_type=F32)
            o = o + hsum * jnp.broadcast_to(v3[:, s:s + 1, :], (nsub, 8, HW)).reshape(CB, HW)
        return o

    def level_terms(m):
        terms = []
        while m < CB:
            nb = CB // (2 * m)
            b4 = b.reshape(nb, 2 * m, HW)
            bmid = b4[:, m - 1:m, :]
            pos = lax.broadcasted_iota(jnp.int32, (nb, 2 * m, HW), 1)
            qq = jnp.where(pos >= m, q.reshape(nb, 2 * m, HW) * jnp.exp(jnp.minimum(b4 - bmid, 0.0)), 0.0)
            kq = jnp.where(pos < m, kk.reshape(nb, 2 * m, HW) * jnp.exp(jnp.minimum(bmid - b4, 0.0)), 0.0)
            terms.append((qq.reshape(CB, HW), kq.reshape(CB, HW), (ti // (2 * m)) == (si // (2 * m))))
            m *= 2
        return terms

    def scores_times_v(terms):
        lane = lax.broadcasted_iota(jnp.int32, (CB, 2 * HD), 1)
        parts = []
        for j in range(NH // 2):
            scs = []
            for h in (2 * j, 2 * j + 1):
                sl = slice(HD * h, HD * (h + 1))
                sc = None
                for qq, kq, keep in terms:
                    t = jnp.where(keep, _bdot_nt(qq[:, sl], kq[:, sl]), 0.0)
                    sc = t if sc is None else sc + t
                scs.append(sc)
            vp = v[:, 2 * HD * j:2 * HD * (j + 1)]
            vdiag = jnp.concatenate([jnp.where(lane < HD, vp, 0.0), jnp.where(lane >= HD, vp, 0.0)], axis=0)
            parts.append(_bdot(jnp.concatenate(scs, axis=1), vdiag))
        return _concat_heads(parts)

    base = HGRN_BASE if prompt else L_S
    nbase = CB // base
    bb3 = b.reshape(nbase, base, HW)
    top = bb3[:, 0:1, :] - lf.reshape(nbase, base, HW)[:, 0:1, :]
    decay_range = jnp.max(top - bb3[:, base - 1:base, :])
    in_range = decay_range <= EXP_RANGE_MAX

    def exact_path():
        o_scr[rows, :] = diag8() + (scores_times_v(level_terms(8)) if prompt else 0.0)

    def finish(fast):
        _hgrn_finish(prompt, fast, rows, seqs, q, kk, v, gr, b, bb3, top, ti, si, scores_times_v, level_terms,
                     hg_ref, e64, oh_ref, st_scr, o_scr, is_last, stp_out, sts_in, sts_out)

    return in_range, exact_path, finish


def _hgrn_finish(prompt, fast, rows, seqs, q, kk, v, gr, b, bb3, top, ti, si, scores_times_v, level_terms,
                 hg_ref, e64, oh_ref, st_scr, o_scr, is_last, stp_out, sts_in, sts_out):
    base = HGRN_BASE if prompt else L_S
    nbase = CB // base
    qf = (q.reshape(nbase, base, HW) * jnp.exp(bb3 - top)).reshape(CB, HW)
    kf = (kk.reshape(nbase, base, HW) * jnp.exp(top - bb3)).reshape(CB, HW)
    keep = ((ti // base) == (si // base)) & (si <= ti)
    o_fast = scores_times_v([(qf, kf, keep)] + (level_terms(base) if prompt else []))
    o = jnp.where(fast, o_fast, o_scr[rows, :])
    qt = q * jnp.exp(b)
    if prompt:
        blast = b[CB - 1:CB, :]
        kd = kk * jnp.exp(blast - b)
        vt = v.T
        same_head = (ti // HD) == (si // HD)
        parts = []
        for j in range(NH // 2):
            pr = slice(2 * HD * j, 2 * HD * (j + 1))
            st = st_scr[j]
            parts.append(o[:, pr] + _bdot_nt(qt[:, pr], st))
            st_scr[j] = st * jnp.exp(blast[:, pr]) + jnp.where(same_head, _bdot(vt[pr, :], kd[:, pr]), 0.0)

        if is_last is not None:
            @pl.when(is_last)
            def _():
                stp_out[0] = st_scr[...]
    else:
        blast = _seq_last_rows(b)
        kd = kk * jnp.exp(blast - b)
        dec = jnp.exp(blast)
        vt = v.T
        parts = []
        for h in range(NH):
            sl = slice(HD * h, HD * (h + 1))
            sts = sts_in[seqs, h].reshape(SEQ_PER_CB * HD, HD)
            full = _bdot_nt(qt[:, sl], sts)
            sel = jnp.where(_stack_select((CB, SEQ_PER_CB * HD), L_S, HD), full, 0.0)
            parts.append(o[:, sl] + _fold_seq_lanes(sel))
            dec3 = dec[:, sl].reshape(SEQ_PER_CB, L_S, HD)[:, L_S - 1:L_S, :]
            dec_rows = jnp.broadcast_to(dec3, (SEQ_PER_CB, HD, HD)).reshape(SEQ_PER_CB * HD, HD)
            upd = _bdot(_seq_expand_rows(vt[sl, :]), kd[:, sl])
            sts_out[seqs, h] = (sts * dec_rows + upd).reshape(SEQ_PER_CB, HD, HD)
    oall = _concat_heads(parts)
    ms = _split_dot(oall * oall, e64, 1) * (1.0 / HD)
    oh_ref[rows, :] = oall * lax.rsqrt(ms + RMS_EPS) * hg_ref[...] * _silu(gr)


HGRN_SUB = 2


def _hgrn_body(p_ref, lb_ref, hg_ref, e64_ref, sts_in,
               oh_ref, stp_out, sts_out, st_scr, o_scr):
    i = pl.program_id(0)
    npc, steps_per_seq = NPC // HGRN_SUB, CH_PER_SEQ // HGRN_SUB

    @pl.when(i == 0)
    def _():
        o_scr[...] = jnp.zeros_like(o_scr)

    @pl.when((i < npc) & (i % steps_per_seq == 0))
    def _():
        st_scr[...] = jnp.zeros_like(st_scr)

    def run(blocks):
        fast = blocks[0][0]
        for in_range, _, _ in blocks[1:]:
            fast = jnp.logical_and(fast, in_range)

        @pl.when(jnp.logical_not(fast))
        def _():
            for _, exact_path, _ in blocks:
                exact_path()

        for _, _, finish in blocks:
            finish(fast)

    @pl.when(i < npc)
    def _():
        run([_hgrn_block(True, p_ref, lb_ref, hg_ref, e64_ref, oh_ref, st_scr, o_scr, sub,
                         is_last=(i % steps_per_seq == steps_per_seq - 1) if sub == HGRN_SUB - 1 else None,
                         stp_out=stp_out) for sub in range(HGRN_SUB)])

    @pl.when(i >= npc)
    def _():
        run([_hgrn_block(False, p_ref, lb_ref, hg_ref, e64_ref, oh_ref, st_scr, o_scr, sub,
                         sts_in=sts_in, sts_out=sts_out) for sub in range(HGRN_SUB)])


def _hgrn(ph, lb, hg, consts, st_t):
    return pl.pallas_call(
        _hgrn_body,
        grid=((NPC + NSC) // HGRN_SUB,),
        in_specs=[
            _mixer_blk(P_HG, HGRN_SUB), _const_spec((1, HW)), _const_spec((1, HW)),
            _const_spec((HW, HW)),
            _sstate_spec((NH, HD, HD), HGRN_SUB),
        ],
        out_specs=[_mixer_blk(HW, HGRN_SUB), _pstate_spec((NH // 2, 2 * HD, 2 * HD), HGRN_SUB),
                   _sstate_spec((NH, HD, HD), HGRN_SUB)],
        out_shape=[
            jax.ShapeDtypeStruct((ROWS, HW), F32),
            jax.ShapeDtypeStruct((NB_P, NH // 2, 2 * HD, 2 * HD), F32),
            jax.ShapeDtypeStruct((NB_S, NH, HD, HD), F32),
        ],
        scratch_shapes=[pltpu.VMEM((NH // 2, 2 * HD, 2 * HD), F32), pltpu.VMEM((HGRN_SUB * CB, HW), F32)],
        compiler_params=_cparams(("arbitrary",)),
        name="hgrn",
    )(ph, lb, hg, consts["e64"], st_t)


def _cmul_add(hr, hi, lr, li, sr, si):
    return hr + lr * sr - li * si, hi + lr * si + li * sr


def _s5_project(ub, bblk_ref):
    halves = [ub[:, (S5W // 2) * j:(S5W // 2) * (j + 1)] for j in range(2)]
    hr = jnp.concatenate([jnp.dot(halves[j], bblk_ref[0, j], preferred_element_type=F32) for j in range(2)], axis=1)
    hi = jnp.concatenate([jnp.dot(halves[j], bblk_ref[1, j], preferred_element_type=F32) for j in range(2)], axis=1)
    return hr, hi


def _s5_readout(h_scr, u, ccat_ref, d_ref, wglu_ref, bglu_ref):
    hs = S5N // 2
    ch = [_bdot(h_scr[:, hs * j:hs * (j + 1)], ccat_ref[j, 0])
          + _bdot(h_scr[:, S5N + hs * j:S5N + hs * (j + 1)], ccat_ref[j, 1]) for j in range(2)]
    y = jnp.concatenate(ch, axis=1) + d_ref[...] * u
    c0 = math.sqrt(2.0 / math.pi)
    y = y * (0.5 * (1.0 + jnp.tanh(c0 * (y + 0.044715 * (y * y * y)))))
    return y * jax.nn.sigmoid(_bdot(y, wglu_ref[...]) + bglu_ref[...])


S5_TB = 64
S5_ROWS = NB_P * S5_TB


def _s5_prompt_body(*refs):
    p_refs = refs[:NB_P]
    (perm_ref, permt_ref, lam_ref, bblk_ref, ccat_ref, d_ref, wglu_ref, bglu_ref,
     os_ref, st_out, carry_scr, h_scr) = refs[NB_P:]
    i = pl.program_id(0)

    @pl.when(i == 0)
    def _():
        carry_scr[...] = jnp.zeros_like(carry_scr)

    u = jnp.concatenate([r[...] for r in p_refs], axis=0)
    u_hi = u.astype(BF16)
    u_lo = (u - u_hi.astype(F32)).astype(BF16)
    perm = perm_ref[...]
    up_hi = jnp.dot(perm, u_hi, preferred_element_type=F32)
    up = up_hi + jnp.dot(perm, u_lo, preferred_element_type=F32)
    hr, hi = _s5_project(up_hi.astype(BF16), bblk_ref)
    lr = lam_ref[:, 0:S5N]
    li = lam_ref[:, S5N:2 * S5N]
    cr = carry_scr[:, 0:S5N]
    ci = carry_scr[:, S5N:2 * S5N]
    for t in range(S5_TB):
        rows = slice(NB_P * t, NB_P * (t + 1))
        cr, ci = _cmul_add(hr[rows], hi[rows], lr, li, cr, ci)
        h_scr[rows, 0:S5N] = cr
        h_scr[rows, S5N:2 * S5N] = ci
    carry_scr[:, 0:S5N] = cr
    carry_scr[:, S5N:2 * S5N] = ci
    out = _s5_readout(h_scr, up, ccat_ref, d_ref, wglu_ref, bglu_ref)
    os_ref[...] = jnp.dot(permt_ref[...], out.astype(BF16), preferred_element_type=F32).reshape(NB_P, S5_TB, S5W)

    @pl.when(i == pl.num_programs(0) - 1)
    def _():
        st_out[...] = carry_scr[...]


def _s5_prompt(ps, prm, consts):
    steps = L_P // S5_TB
    windows = [pl.BlockSpec((S5_TB, S5W), lambda i, b=b: (b * steps + i, 0)) for b in range(NB_P)]
    return pl.pallas_call(
        _s5_prompt_body,
        grid=(steps,),
        in_specs=[
            *windows,
            _const_spec((S5_ROWS, S5_ROWS)), _const_spec((S5_ROWS, S5_ROWS)), _const_spec((NB_P, 2 * S5N)),
            _const_spec((2, 2, S5W // 2, S5N // 2)), _const_spec((2, 2, S5N // 2, S5W // 2)),
            _const_spec((1, S5W)), _const_spec((S5W, S5W)), _const_spec((1, S5W)),
        ],
        out_specs=[pl.BlockSpec((NB_P, S5_TB, S5W), lambda i: (0, i, 0)), _const_spec((NB_P, 2 * S5N))],
        out_shape=[jax.ShapeDtypeStruct((NB_P, L_P, S5W), F32), jax.ShapeDtypeStruct((NB_P, 2 * S5N), F32)],
        scratch_shapes=[pltpu.VMEM((NB_P, 2 * S5N), F32), pltpu.VMEM((S5_ROWS, 2 * S5N), F32)],
        compiler_params=_cparams(("arbitrary",)),
        name="s5_prompt",
    )(*([ps] * NB_P), consts["perm"], consts["permt"], prm["lam8"], prm["bblk"], prm["ccat"], prm["d"],
      prm["wglu"], prm["bglu"])


def _s5_sample_body(p_ref, tab_ref, bblk_ref, ccat_ref, d_ref, wglu_ref, bglu_ref, s5s_in,
                    os_ref, s5s_out, h_scr):
    u = p_ref[...]
    hr, hi = _s5_project(u.astype(BF16), bblk_ref)
    nsub = CB // 8
    for idx, dsh in enumerate((1, 2, 4)):
        sr = pltpu.roll(hr, dsh, 0).reshape(nsub, 8, S5N)
        si = pltpu.roll(hi, dsh, 0).reshape(nsub, 8, S5N)
        lr = tab_ref[idx, :, 0:S5N][None]
        li = tab_ref[idx, :, S5N:2 * S5N][None]
        nr, ni = _cmul_add(hr.reshape(nsub, 8, S5N), hi.reshape(nsub, 8, S5N), lr, li, sr, si)
        hr = nr.reshape(CB, S5N)
        hi = ni.reshape(CB, S5N)
    tcr = tab_ref[3, :, 0:S5N]
    tci = tab_ref[3, :, S5N:2 * S5N]
    cr = s5s_in[:, :, 0:S5N]
    ci = s5s_in[:, :, S5N:2 * S5N]
    tr, tim = _cmul_add(hr.reshape(nsub, 8, S5N), hi.reshape(nsub, 8, S5N), tcr[None], tci[None], cr, ci)
    h_scr[:, 0:S5N] = tr.reshape(CB, S5N)
    h_scr[:, S5N:2 * S5N] = tim.reshape(CB, S5N)
    sb = lax.broadcasted_iota(jnp.int32, (SEQ_PER_CB, CB), 0)
    st = lax.broadcasted_iota(jnp.int32, (SEQ_PER_CB, CB), 1)
    sel = (st == L_S * sb + (L_S - 1)).astype(BF16)
    s5s_out[...] = _split_dot_l(sel, h_scr[...], 3)
    os_ref[...] = _s5_readout(h_scr, u, ccat_ref, d_ref, wglu_ref, bglu_ref)


def _s5_sample(ps, prm, st):
    seqs = lambda shape: pl.BlockSpec((SEQ_PER_CB,) + shape, lambda i: (i,) + (0,) * len(shape))
    return pl.pallas_call(
        _s5_sample_body,
        grid=(NSC,),
        in_specs=[
            pl.BlockSpec((CB, S5W), lambda i: (NPC + i, 0)),
            _const_spec((4, 8, 2 * S5N)), _const_spec((2, 2, S5W // 2, S5N // 2)),
            _const_spec((2, 2, S5N // 2, S5W // 2)), _const_spec((1, S5W)), _const_spec((S5W, S5W)),
            _const_spec((1, S5W)), seqs((1, 2 * S5N)),
        ],
        out_specs=[_mixer_blk(S5W), seqs((2 * S5N,))],
        out_shape=[jax.ShapeDtypeStruct((ROWS_S, S5W), F32), jax.ShapeDtypeStruct((NB_S, 2 * S5N), F32)],
        scratch_shapes=[pltpu.VMEM((CB, 2 * S5N), F32)],
        compiler_params=_cparams(("parallel",)),
        name="s5_sample",
    )(ps, prm["tab"], prm["bblk"], prm["ccat"], prm["d"], prm["wglu"], prm["bglu"], st)


def _ssd_block(prompt, p_ref, cw_ref, cb_ref, dtb_ref, aneg_ref, dx_ref, sg_ref,
               g192_ref, tril_ref, om_ref, cbuf, st_scr, sub, is_last=None, ssp_out=None, convp_out=None,
               hist_ref=None, sss_in=None, sss_out=None, convs_out=None):
    rows = slice(sub * CB, (sub + 1) * CB)
    seqs = slice(sub * SEQ_PER_CB, (sub + 1) * SEQ_PER_CB)
    z = p_ref[rows, 0:SSW]
    xbc = p_ref[rows, SSW:SSW + CONVC]
    dtr = p_ref[rows, SSW + CONVC:P_SS]
    acc = cb_ref[...] + cw_ref[3:4, :] * xbc
    if prompt:
        cbuf[8:8 + CB, :] = xbc
        for k in (1, 2, 3):
            acc = acc + cw_ref[3 - k:4 - k, :] * cbuf[8 - k:8 - k + CB, :]
        cbuf[0:8, :] = cbuf[CB:CB + 8, :]
    else:
        convs_out[rows, :] = xbc
        tl = lax.broadcasted_iota(jnp.int32, (CB, CONVC), 0) % L_S
        hist = hist_ref[rows, :]
        for k in (1, 2, 3):
            hk = hist if k == 3 else pltpu.roll(hist, CB - (3 - k), 0)
            sh = jnp.where(tl >= k, pltpu.roll(xbc, k, 0), hk)
            acc = acc + cw_ref[3 - k:4 - k, :] * sh
    xc = _silu(acc)
    xs = xc[:, 0:SSW]
    bm = xc[:, SSW:SSW + 2 * HD]
    cm = xc[:, SSW + 2 * HD:CONVC]
    xdt = dtr + dtb_ref[...]
    dt = jnp.maximum(xdt, 0.0) + jnp.log1p(jnp.exp(-jnp.abs(xdt)))
    la = dt * aneg_ref[...]
    tril = tril_ref[...]
    b6 = _cumsum_rows(la, CB if prompt else L_S)
    lane = lax.broadcasted_iota(jnp.int32, (CB, 128), 1)
    bxw = [jnp.broadcast_to(b6[:, h:h + 1], (CB, 128)) for h in range(NH)]
    dtw = [jnp.broadcast_to(dt[:, h:h + 1], (CB, 128)) for h in range(NH)]
    pair = lambda cols: _concat_heads([jnp.where(lane < HD, cols[2 * j], cols[2 * j + 1]) for j in range(NH // 2)])
    bx = pair(bxw)
    dtx = pair(dtw)
    bm_rep = _concat_heads([bm[:, 0:HD]] * 3 + [bm[:, HD:2 * HD]] * 3)
    cm_rep = _concat_heads([cm[:, 0:HD]] * 3 + [cm[:, HD:2 * HD]] * 3)
    kh = bm_rep * dtx
    qt = cm_rep * jnp.exp(bx)
    blast = bx[CB - 1:CB, :] if prompt else _seq_last_rows(bx)
    kd = kh * jnp.exp(blast - bx)
    kdt = kd.T
    mask = tril > 0

    def scores(h):
        sl = slice(HD * h, HD * (h + 1))
        bcol = bxw[h]
        decay = jnp.exp(jnp.where(mask, bcol - bcol.T, -1e30))
        return _bdot_nt(cm_rep[:, sl], kh[:, sl]) * decay

    parts = []
    if prompt:
        ri = lax.broadcasted_iota(jnp.int32, (2 * HD, 2 * HD), 0)
        ci = lax.broadcasted_iota(jnp.int32, (2 * HD, 2 * HD), 1)
        same_head = (ri // HD) == (ci // HD)
        for j in range(NH // 2):
            pr = slice(2 * HD * j, 2 * HD * (j + 1))
            xp = xs[:, pr]
            xdiag = jnp.concatenate([jnp.where(lane < HD, xp, 0.0), jnp.where(lane >= HD, xp, 0.0)], axis=0)
            st = st_scr[j]
            parts.append(_bdot(jnp.concatenate([scores(2 * j), scores(2 * j + 1)], axis=1), xdiag)
                         + _bdot(qt[:, pr], st))
            st_scr[j] = st * jnp.exp(blast[:, pr]) + jnp.where(same_head, _bdot(kdt[pr, :], xp), 0.0)
    else:
        for h in range(NH):
            sl = slice(HD * h, HD * (h + 1))
            sts = sss_in[seqs, h].reshape(SEQ_PER_CB * HD, HD)
            parts.append(_bdot(scores(h), xs[:, sl]) + _bdot(_seq_expand_lanes(qt[:, sl]), sts))
            dec3 = jnp.exp(blast[:, sl]).reshape(SEQ_PER_CB, L_S, HD)[:, L_S - 1:L_S, :]
            dec_rows = jnp.broadcast_to(dec3, (SEQ_PER_CB, HD, HD)).reshape(SEQ_PER_CB * HD, HD)
            upd = _bdot(_seq_expand_rows(kdt[sl, :]), xs[:, sl])
            sss_out[seqs, h] = (sts * dec_rows + upd).reshape(SEQ_PER_CB, HD, HD)
    if is_last is not None:
        @pl.when(is_last)
        def _():
            ssp_out[0] = st_scr[...]
            convp_out[0] = cbuf[0:8, :]
    y = (_concat_heads(parts) + dx_ref[...] * xs) * _silu(z)
    ms = _split_dot(y * y, g192_ref[...], 1) * (1.0 / (SSW // 2))
    om_ref[rows, :] = y * lax.rsqrt(ms + RMS_EPS) * sg_ref[...]


def _ssd_body(p_ref, cw_ref, cb_ref, dtb_ref, aneg_ref, dx_ref, sg_ref, g192_ref,
              trilp_ref, trils_ref, hist_ref, sss_in, *refs):
    om_ref, ssp_out, sss_out, convp_out, convs_out, cbuf, st_scr = refs[1:]
    i = pl.program_id(0)
    common = (p_ref, cw_ref, cb_ref, dtb_ref, aneg_ref, dx_ref, sg_ref, g192_ref)
    npc, steps_per_seq = NPC // SSD_SUB, CH_PER_SEQ // SSD_SUB

    @pl.when((i < npc) & (i % steps_per_seq == 0))
    def _():
        st_scr[...] = jnp.zeros_like(st_scr)
        cbuf[...] = jnp.zeros_like(cbuf)

    @pl.when(i < npc)
    def _():
        for sub in range(SSD_SUB):
            is_last = (i % steps_per_seq == steps_per_seq - 1) if sub == SSD_SUB - 1 else None
            _ssd_block(True, *common, trilp_ref, om_ref, cbuf, st_scr, sub, is_last=is_last,
                       ssp_out=ssp_out, convp_out=convp_out)

    @pl.when(i >= npc)
    def _():
        for sub in range(SSD_SUB):
            _ssd_block(False, *common, trils_ref, om_ref, cbuf, st_scr, sub, hist_ref=hist_ref,
                       sss_in=sss_in, sss_out=sss_out, convs_out=convs_out)


SSD_SUB = 2


def _ssd(pm, prm, consts, hist, st, layer, earlier):
    sample_rows = pl.BlockSpec((SSD_SUB * CB, CONVC),
                               lambda i: (jnp.clip(i - NPC // SSD_SUB, 0, NSC // SSD_SUB - 1), 0))
    in_specs = [
        _mixer_blk(P_SS, SSD_SUB), _const_spec((4, CONVC)), _const_spec((1, CONVC)),
        _const_spec((1, 128)), _const_spec((1, 128)), _const_spec((1, SSW)), _const_spec((1, SSW)),
        _const_spec((SSW, SSW)),
        _const_spec((CB, CB)), _const_spec((CB, CB)),
        sample_rows,
        _sstate_spec((NH, HD, HD), SSD_SUB),
    ]
    args = [pm, prm["cw"], prm["cb"], prm["dtb"], prm["aneg"], prm["dx"], prm["sg"],
            consts["g192"], consts["trilp"], consts["trils"], hist, st]
    in_specs.append(pl.BlockSpec(memory_space=pl.ANY))
    aliases = {len(args): 2}
    args.append(earlier)
    layer_slab = pl.BlockSpec((None, SSD_SUB * SEQ_PER_CB, NH, HD, HD),
                              lambda i: (layer, jnp.clip(i - NPC // SSD_SUB, 0, NSC // SSD_SUB - 1), 0, 0, 0))
    return pl.pallas_call(
        _ssd_body,
        grid=((NPC + NSC) // SSD_SUB,),
        in_specs=in_specs,
        out_specs=[_mixer_blk(SSW, SSD_SUB), _pstate_spec((NH // 2, 2 * HD, 2 * HD), SSD_SUB),
                   layer_slab, _pstate_spec((8, CONVC), SSD_SUB), sample_rows],
        out_shape=[
            jax.ShapeDtypeStruct((ROWS, SSW), F32),
            jax.ShapeDtypeStruct((NB_P, NH // 2, 2 * HD, 2 * HD), F32),
            jax.ShapeDtypeStruct((DEPTH, NB_S, NH, HD, HD), F32),
            jax.ShapeDtypeStruct((NB_P, 8, CONVC), F32),
            jax.ShapeDtypeStruct((ROWS_S, CONVC), F32),
        ],
        scratch_shapes=[pltpu.VMEM((CB + 8, CONVC), F32), pltpu.VMEM((NH // 2, 2 * HD, 2 * HD), F32)],
        input_output_aliases=aliases,
        compiler_params=_cparams(("arbitrary",)),
        name="ssd",
    )(*args)


def _o_core(split_x, i, oh_ref, osp_ref, oss_ref, om_ref, *refs):
    if split_x:
        x = jnp.where(i < NPT, refs[0][...], refs[1][...])
        refs = refs[2:]
    else:
        x = refs[0][...]
        refs = refs[1:]
    wo_ref, gp, gs, lng_ref, lnb_ref, scp, scs, shp, shs = refs[:9]
    os_ = jnp.where(i < NPT, osp_ref[...], oss_ref[...])
    mix = (_bdot(oh_ref[...], wo_ref[0:HW, :]) + _bdot(os_, wo_ref[HW:HW + S5W, :])
           + _bdot(om_ref[...], wo_ref[HW + S5W:D, :]))
    x1 = _layer_norm(ALPHA * x + _rowmod(i, gp, gs) * mix, lng_ref[...], lnb_ref[...])
    h2 = x1 * (1.0 + _rowmod(i, scp, scs)) + _rowmod(i, shp, shs)
    return x1, h2, refs[9:]


def _o_body(split_x, *refs):
    i = pl.program_id(0)
    x1, h2, (x1_ref, h2_ref) = _o_core(split_x, i, *refs)
    x1_ref[...] = x1
    h2_ref[...] = h2.astype(BF16)


def _o_router_body(split_x, *refs):
    i = pl.program_id(0)
    x1, h2, (wr_ref, br_ref, x1_ref, h2_ref, route_ref) = _o_core(split_x, i, *refs)
    x1_ref[...] = x1
    h2_ref[...] = h2
    h_hi = h2.astype(BF16)
    h_lo = (h2 - h_hi.astype(F32)).astype(BF16)
    logits = (jnp.dot(h_hi, wr_ref[0], preferred_element_type=F32)
              + jnp.dot(h_lo, wr_ref[0], preferred_element_type=F32)
              + jnp.dot(h_hi, wr_ref[1], preferred_element_type=F32)) + br_ref[...]
    lane = lax.broadcasted_iota(jnp.int32, (TM, 128), 1).astype(F32)
    neg = -jnp.inf
    lg = jnp.where(lane < NEXP, logits, neg)
    m1 = jnp.max(lg, axis=-1, keepdims=True)
    i1 = jnp.min(jnp.where(lg == m1, lane, 128.0), axis=-1, keepdims=True)
    lg2 = jnp.where(lane == i1, neg, lg)
    m2 = jnp.max(lg2, axis=-1, keepdims=True)
    i2 = jnp.min(jnp.where(lg2 == m2, lane, 128.0), axis=-1, keepdims=True)
    e2 = jnp.exp(m2 - m1)
    den = 1.0 + e2
    route_ref[...] = jnp.where(lane == 0.0, i1, jnp.where(lane == 1.0, i2,
                               jnp.where(lane == 2.0, 1.0 / den, jnp.where(lane == 3.0, e2 / den, 0.0))))


def _stage_o(layer, oh, os_p, os_s, om, x, wo_b, ada4, ln_g, ln_b, router=None):
    split_x = isinstance(x, tuple)
    x_specs = [_prompt_rows_spec(D), _sample_rows_spec(D)] if split_x else [_row_spec(D)]
    in_specs = [
        _row_spec(HW), _prompt_rows_spec(S5W), _sample_rows_spec(S5W), _row_spec(SSW), *x_specs,
        _const_spec((D, D)),
        *_mod_specs(layer, 2), _const_spec((1, D)), _const_spec((1, D)),
        *_mod_specs(layer, 4), *_mod_specs(layer, 3),
    ]
    args = [oh, os_p, os_s, om, *(x if split_x else (x,)), wo_b, ada4, ada4, ln_g, ln_b, ada4, ada4, ada4, ada4]
    out_specs = [_row_spec(D), _row_spec(D)]
    out_shape = [jax.ShapeDtypeStruct((ROWS, D), F32), jax.ShapeDtypeStruct((ROWS, D), BF16)]
    body = _o_body
    if router is not None:
        in_specs += [_const_spec((2, D, 128)), _const_spec((1, 128))]
        args += list(router)
        out_specs.append(_row_spec(128))
        out_shape[1] = jax.ShapeDtypeStruct((ROWS, D), F32)
        out_shape.append(jax.ShapeDtypeStruct((ROWS, 128), F32))
        body = _o_router_body
    return pl.pallas_call(
        functools.partial(body, split_x), grid=(NT,), in_specs=in_specs, out_specs=out_specs,
        out_shape=out_shape,
        compiler_params=_cparams(("parallel",)),
        name="stage_o_router" if router is not None else "stage_o",
    )(*args)


def _ffn_body(te_ref, nu_ref, h_ref, wg_ref, wu_ref, wd_ref, o_ref, acc_ref):
    i = pl.program_id(0)
    j = pl.program_id(1)

    @pl.when(j == 0)
    def _():
        acc_ref[...] = jnp.zeros_like(acc_ref)

    @pl.when(i < nu_ref[0])
    def _():
        h = h_ref[...].astype(BF16)
        g = jnp.dot(h, wg_ref[...], preferred_element_type=F32)
        u = jnp.dot(h, wu_ref[...], preferred_element_type=F32)
        act = (_silu(g) * u).astype(BF16)
        acc_ref[...] += jnp.dot(act, wd_ref[...], preferred_element_type=F32)

    @pl.when(j == pl.num_programs(1) - 1)
    def _():
        o_ref[...] = acc_ref[...]


def _ffn(tile_expert, n_used, h, wg, wu, wd):
    rows = h.shape[0]
    nj = DFF // TF

    def jblk(i, j, nu):
        return jnp.where(i < nu[0], j, nj - 1)

    grid_spec = pltpu.PrefetchScalarGridSpec(
        num_scalar_prefetch=2,
        grid=(rows // TMF, nj),
        in_specs=[
            pl.BlockSpec((TMF, D), lambda i, j, te, nu: (i, 0)),
            pl.BlockSpec((None, D, TF), lambda i, j, te, nu: (te[i], 0, jblk(i, j, nu))),
            pl.BlockSpec((None, D, TF), lambda i, j, te, nu: (te[i], 0, jblk(i, j, nu))),
            pl.BlockSpec((None, TF, D), lambda i, j, te, nu: (te[i], jblk(i, j, nu), 0)),
        ],
        out_specs=pl.BlockSpec((TMF, D), lambda i, j, te, nu: (i, 0)),
        scratch_shapes=[pltpu.VMEM((TMF, D), F32)],
    )
    return pl.pallas_call(
        _ffn_body, grid_spec=grid_spec,
        out_shape=jax.ShapeDtypeStruct((rows, D), F32),
        compiler_params=_cparams(("parallel", "arbitrary")),
        name="ffn",
    )(tile_expert, n_used, h, wg, wu, wd)


RING = 3


def _ring_copy(src_hbm, buf, sem, step):
    slot = step % RING
    return pltpu.make_async_copy(src_hbm.at[pl.ds(step * TM, TM)], buf.at[slot], sem.at[slot])


def _final_body(x1_ref, ya_hbm, yb_hbm, route_ref, gp, gs, lng_ref, lnb_ref, yp_ref, ys_ref,
                ya_buf, yb_buf, ya_sem, yb_sem):
    i = pl.program_id(0)
    streams = ((ya_hbm, ya_buf, ya_sem), (yb_hbm, yb_buf, yb_sem))

    @pl.when(i == 0)
    def _():
        for s in streams:
            for step in range(RING - 1):
                _ring_copy(*s, step).start()

    @pl.when(i + RING - 1 < NT)
    def _():
        for s in streams:
            _ring_copy(*s, i + RING - 1).start()

    for s in streams:
        _ring_copy(*s, i).wait()
    slot = i % RING
    f = route_ref[:, 2:3] * ya_buf[slot] + route_ref[:, 3:4] * yb_buf[slot]
    y = _layer_norm(ALPHA * x1_ref[...] + _rowmod(i, gp, gs) * f, lng_ref[...], lnb_ref[...])

    @pl.when(i < NPT)
    def _():
        yp_ref[...] = y

    @pl.when(i >= NPT)
    def _():
        ys_ref[...] = y


def _stage_final(layer, x1, ya, yb, route, ada4, ln_g, ln_b):
    return pl.pallas_call(
        _final_body,
        grid=(NT,),
        in_specs=[
            _row_spec(D), pl.BlockSpec(memory_space=pl.ANY), pl.BlockSpec(memory_space=pl.ANY), _row_spec(128),
            *_mod_specs(layer, 5), _const_spec((1, D)), _const_spec((1, D)),
        ],
        out_specs=[
            pl.BlockSpec((TM, D), lambda i: (jnp.minimum(i, NPT - 1), 0)),
            pl.BlockSpec((TM, D), lambda i: (jnp.clip(i - NPT, 0, NST - 1), 0)),
        ],
        out_shape=[jax.ShapeDtypeStruct((ROWS_P, D), F32), jax.ShapeDtypeStruct((ROWS_S, D), F32)],
        scratch_shapes=[pltpu.VMEM((RING, TM, D), F32), pltpu.VMEM((RING, TM, D), F32),
                        pltpu.SemaphoreType.DMA((RING,)), pltpu.SemaphoreType.DMA((RING,))],
        compiler_params=_cparams(("arbitrary",)),
        name="stage_final",
    )(x1, ya, yb, route, ada4, ada4, ln_g, ln_b)


def _block_ones(n, blk):
    r = np.arange(n) // blk
    return r[:, None] == r[None, :]


def _consts():
    t = np.arange(CB)
    causal = t[:, None] >= t[None, :]
    same_seq = (t[:, None] // L_S) == (t[None, :] // L_S)
    r = np.arange(S5_ROWS)
    perm = r[None, :] == ((r % NB_P) * S5_TB + r // NB_P)[:, None]
    mats = {
        "perm": perm,
        "permt": perm.T,
        "e64": _block_ones(HW, HD),
        "g192": _block_ones(SSW, SSW // 2),
        "trilp": causal,
        "trils": causal & same_seq,
    }
    return {k: jnp.asarray(v.astype(np.float32), dtype=BF16) for k, v in mats.items()}


def _s5_params(a_re, a_im, log_dt, b_re, b_im, c_re, c_im, d, w_glu, b_glu):
    dt = jnp.exp(log_dt)[:, None]
    mag = jnp.exp(a_re * dt)
    lam_re, lam_im = mag * jnp.cos(a_im * dt), mag * jnp.sin(a_im * dt)
    den = a_re * a_re + a_im * a_im
    nr, ni = lam_re - 1.0, lam_im
    zr = (nr * a_re + ni * a_im) / den
    zi = (ni * a_re - nr * a_im) / den
    bbar_re = zr[..., None] * b_re - zi[..., None] * b_im
    bbar_im = zr[..., None] * b_im + zi[..., None] * b_re
    eye = jnp.eye(16, dtype=F32)
    blk = lambda bb: jnp.einsum('gph,gk->ghkp', bb, eye).reshape(S5W, S5N)
    hu, hs = S5W // 2, S5N // 2
    bblk = jnp.stack([jnp.stack([blk(bb)[hu * j:hu * (j + 1), hs * j:hs * (j + 1)] for j in range(2)])
                      for bb in (bbar_re, bbar_im)]).astype(BF16)
    cblk = lambda cc: jnp.einsum('ghp,gk->gpkh', cc, eye).reshape(S5N, S5W)
    ccat = jnp.stack([jnp.stack([cblk(cc)[hs * j:hs * (j + 1), hu * j:hu * (j + 1)] for cc in (c_re, -c_im)])
                      for j in range(2)]).astype(BF16)
    lr, li = lam_re.reshape(-1), lam_im.reshape(-1)
    pows = [(jnp.ones_like(lr), jnp.zeros_like(li))]
    for _ in range(8):
        pr, pi = pows[-1]
        pows.append((pr * lr - pi * li, pr * li + pi * lr))
    rows = jnp.arange(8)[:, None]
    tabs = []
    for dsh in (1, 2, 4):
        pr, pi = pows[dsh]
        tabs.append(jnp.where(rows >= dsh, jnp.concatenate([pr, pi])[None, :], 0.0))
    tabs.append(jnp.stack([jnp.concatenate(pows[r + 1]) for r in range(8)]))
    return {
        "tab": jnp.stack(tabs), "bblk": bblk, "ccat": ccat,
        "lam8": jnp.broadcast_to(jnp.concatenate([lr, li])[None, :], (NB_P, 2 * S5N)),
        "d": d.reshape(1, S5W), "wglu": w_glu.astype(BF16), "bglu": b_glu.reshape(1, S5W),
    }


def _pair_diag(st):
    s6 = st.reshape(NB_P, NH // 2, 2, HD, 2, HD)
    return jnp.stack([s6[:, :, 0, :, 0, :], s6[:, :, 1, :, 1, :]], axis=2).reshape(NB_P, NH, HD, HD)


def _pad_lanes(v, n=128):
    return jnp.pad(v, (0, n - v.shape[0])).reshape(1, n)


def kernel(x_prompt, x_sample, c_prompt, c_sample, state_hgrn, state_s5, state_ssd, state_ssd_conv, w_ada, b_ada, ln_g, ln_b, w_in, w_out, hgrn_lb_logits, hgrn_norm_g, s5_a_re, s5_a_im, s5_log_dt, s5_b_re, s5_b_im, s5_c_re, s5_c_im, s5_d, s5_w_glu, s5_b_glu, ssd_conv_w, ssd_conv_b, ssd_dt_bias, ssd_a_log, ssd_d, ssd_norm_g, ffn_w_gate, ffn_w_up, ffn_w_down, moe_w_router, moe_b_router, moe_w_gate, moe_w_up, moe_w_down):
    consts = _consts()
    c_all = jnp.concatenate([c_sample, c_prompt], axis=0)
    ada4 = _ada(c_all, w_ada, b_ada).reshape(DEPTH, NB_S + NB_P, 1, 6 * D)

    lb_all = jnp.cumsum(jax.nn.softmax(hgrn_lb_logits, axis=0), axis=0)
    lb_all = lb_all - lb_all[0]

    xp = x_prompt.reshape(ROWS_P, D)
    xs = x_sample.reshape(ROWS_S, D)
    new_h, new_s5, new_m, new_c = [], [], [], []
    x1 = f = route = None
    ss_s = jnp.zeros((DEPTH, NB_S, NH, HD, HD), F32)
    for l in range(DEPTH):
        w_in_b = jnp.pad(w_in[l], ((0, 0), (0, N_IN_PAD - N_IN))).astype(BF16)
        if l == 0:
            x = (xp, xs)
            ph, ps, pm = _stage_a0(xp, xs, ada4, w_in_b)
        else:
            x, ph, ps, pm = _stage_a1(l, x1, f, ada4, ln_g[l - 1, 1].reshape(1, D),
                                      ln_b[l - 1, 1].reshape(1, D), w_in_b)
        oh, hg_p, hg_s = _hgrn(ph, lb_all[l].reshape(1, HW), hgrn_norm_g[l].reshape(1, HW), consts,
                               jnp.swapaxes(state_hgrn[l], -1, -2))
        s5p = _s5_params(s5_a_re[l], s5_a_im[l], s5_log_dt[l], s5_b_re[l], s5_b_im[l], s5_c_re[l], s5_c_im[l],
                         s5_d[l], s5_w_glu[l], s5_b_glu[l])
        os_p, s5_p = _s5_prompt(ps, s5p, consts)
        os_s, s5_s = _s5_sample(ps, s5p, state_s5[l].reshape(NB_S, 1, 2 * S5N))
        os_p = os_p.reshape(ROWS_P, S5W)
        ssd_prm = {
            "cw": ssd_conv_w[l], "cb": ssd_conv_b[l].reshape(1, CONVC),
            "dtb": _pad_lanes(ssd_dt_bias[l]), "aneg": _pad_lanes(-jnp.exp(ssd_a_log[l])),
            "dx": jnp.repeat(ssd_d[l], HD).reshape(1, SSW), "sg": ssd_norm_g[l].reshape(1, SSW),
        }
        hist = jnp.pad(state_ssd_conv[l], ((0, 0), (0, L_S - 3), (0, 0))).reshape(ROWS_S, CONVC)
        om, ss_p, ss_s, tail_p, xbc_s = _ssd(pm, ssd_prm, consts, hist, state_ssd[l], l, ss_s)
        conv_p = tail_p[:, 8 - 3:]
        conv_s = xbc_s.reshape(NB_S, L_S, CONVC)[:, L_S - 3:]
        hg_p, ss_p = _pair_diag(hg_p), _pair_diag(ss_p)
        new_h.append((jnp.swapaxes(hg_p, -1, -2), jnp.swapaxes(hg_s, -1, -2)))
        new_s5.append((s5_p.reshape(NB_P, 2, 16, 64), s5_s.reshape(NB_S, 2, 16, 64)))
        new_m.append((ss_p,))
        new_c.append((conv_p, conv_s))

        wo_b = w_out[l].astype(BF16)
        lg, lbb = ln_g[l, 0].reshape(1, D), ln_b[l, 0].reshape(1, D)
        j = l // 2
        if l % 2 == 0:
            x1, h2 = _stage_o(l, oh, os_p, os_s, om, x, wo_b, ada4, lg, lbb)
            f = _ffn(jnp.zeros((ROWS // TMF,), jnp.int32), jnp.full((1,), ROWS // TMF, jnp.int32), h2,
                     ffn_w_gate[j:j + 1].astype(BF16),
                     ffn_w_up[j:j + 1].astype(BF16), ffn_w_down[j:j + 1].astype(BF16))
        else:
            wr = jnp.pad(moe_w_router[j], ((0, 0), (0, 128 - NEXP)))
            wr_hi = wr.astype(BF16)
            wr = jnp.stack([wr_hi, (wr - wr_hi.astype(F32)).astype(BF16)])
            br = _pad_lanes(moe_b_router[j])
            x1, h2, route = _stage_o(l, oh, os_p, os_s, om, x, wo_b, ada4, lg, lbb, router=(wr, br))
            wg_b, wu_b, wd_b = (moe_w_gate[j].astype(BF16), moe_w_up[j].astype(BF16),
                                moe_w_down[j].astype(BF16))
            flat_e = route[:, 0:2].astype(jnp.int32).reshape(-1)
            onehot = (flat_e[:, None] == jnp.arange(NEXP)[None, :]).astype(jnp.int32)
            csum = jnp.cumsum(onehot, axis=0)
            counts = csum[-1]
            rank = jnp.take_along_axis(csum, flat_e[:, None], axis=1)[:, 0] - 1
            padded = ((counts + TMF - 1) // TMF) * TMF
            pend = jnp.cumsum(padded)
            pstart = pend - padded
            dest = pstart[flat_e] + rank
            n_pad = 2 * ROWS + NEXP * TMF
            src_tok = jnp.zeros((n_pad,), jnp.int32).at[dest].set(jnp.arange(2 * ROWS, dtype=jnp.int32) // 2)
            tile_start = jnp.arange(n_pad // TMF, dtype=jnp.int32) * TMF
            tile_e = jnp.minimum(jnp.sum((pend[None, :] <= tile_start[:, None]).astype(jnp.int32), axis=1),
                                 NEXP - 1)
            n_used = (pend[NEXP - 1:NEXP] // TMF).astype(jnp.int32)
            y_sorted = _ffn(tile_e, n_used, h2[src_tok], wg_b, wu_b, wd_b)
            pos = dest.reshape(ROWS, 2)
            ya = y_sorted[pos[:, 0]]
            yb = y_sorted[pos[:, 1]]
    y_p, y_s = _stage_final(DEPTH - 1, x1, ya, yb, route, ada4, ln_g[DEPTH - 1, 1].reshape(1, D),
                            ln_b[DEPTH - 1, 1].reshape(1, D))
    stack = lambda lst, k: jnp.stack([t[k] for t in lst])
    return (y_p.reshape(NB_P, L_P, D), y_s.reshape(NB_S, L_S, D),
            stack(new_h, 0), stack(new_s5, 0), stack(new_m, 0), stack(new_c, 0),
            stack(new_h, 1), stack(new_s5, 1), ss_s, stack(new_c, 1))
```
